```python
import jax, jax.numpy as jnp
from jax import lax
import numpy as np

D_MODEL = 1024
BATCH = 2
SEQ = 8192
DEPTH = 1
DEC_BATCH = 16
DEC_SEQ = 32
PAST_LEN = 1024

CHUNK = 64
N_META = 16
Q_BLOCK = 128
GLA_HEADS = 4
GLA_DK = 128
GLA_DV = 256
GLA_GATE_RANK = 16
GLA_TAU = 16.0
GLA_QK = GLA_HEADS * GLA_DK
GLA_V = GLA_HEADS * GLA_DV
GLA_SCALE = GLA_DK ** -0.5
MLA_HEADS = 16
MLA_Q_RANK = 384
MLA_KV_RANK = 128
MLA_NOPE = 64
MLA_ROPE = 32
MLA_DV = 64
MLA_QK_DIM = MLA_NOPE + MLA_ROPE
MLA_V = MLA_HEADS * MLA_DV
MLA_SCALE = MLA_QK_DIM ** -0.5
ROPE_THETA = 10000.0
KEY_SPLITS = (GLA_QK, GLA_QK + GLA_V, GLA_QK + GLA_V + GLA_GATE_RANK, GLA_QK + GLA_V + GLA_GATE_RANK + MLA_KV_RANK)
KEY_COLS = GLA_QK + GLA_V + GLA_GATE_RANK + MLA_KV_RANK + MLA_ROPE
QUERY_SPLITS = (GLA_QK, GLA_QK + GLA_V)
QUERY_COLS = GLA_QK + GLA_V + MLA_Q_RANK
D_IN = KEY_COLS + QUERY_COLS
N_GROUPS = 4
EXPERTS_PER_GROUP = 8
N_EXPERTS = N_GROUPS * EXPERTS_PER_GROUP
TOP_K_IN_GROUP = 2
D_EXPERT = 256
LN_EPS = 1e-5
RMS_EPS = 1e-6
DEEPNORM_ALPHA = (2.0 * DEPTH) ** 0.25
DEEPNORM_BETA = (8.0 * DEPTH) ** -0.25

kernel_name = "gla_mla_hier_moe_streaming_encoder"


def layer_norm(x, g, b):
    xf = x.astype(jnp.float32)
    mu = xf.mean(-1, keepdims=True)
    var = jnp.square(xf - mu).mean(-1, keepdims=True)
    return ((xf - mu) * lax.rsqrt(var + LN_EPS) * g + b).astype(x.dtype)


def rms_norm(x, g):
    xf = x.astype(jnp.float32)
    return (xf * lax.rsqrt(jnp.square(xf).mean(-1, keepdims=True) + RMS_EPS) * g).astype(x.dtype)


def rope_tables(pos):
    inv = ROPE_THETA ** (-jnp.arange(0, MLA_ROPE, 2, dtype=jnp.float32) / MLA_ROPE)
    ang = pos.astype(jnp.float32)[:, None] * inv[None, :]
    return jnp.cos(ang), jnp.sin(ang)


def apply_rope(x, cos, sin):
    x1, x2 = jnp.split(x.astype(jnp.float32), 2, axis=-1)
    return jnp.concatenate([x1 * cos - x2 * sin, x2 * cos + x1 * sin], axis=-1).astype(x.dtype)


def to_heads(t, d):
    b, s, _ = t.shape
    return t.reshape(b, s, -1, d).transpose(0, 2, 1, 3)


def project_keys(h, cos, sin, lp):
    z = h @ lp["w_in"][:, :KEY_COLS]
    k, v, gk_lr, c_kv, k_r = jnp.split(z, KEY_SPLITS, axis=-1)
    gk = jax.nn.log_sigmoid((gk_lr @ lp["w_gk2"] + lp["b_gk"]).astype(jnp.float32)) / GLA_TAU
    ckv = rms_norm(c_kv, lp["kv_norm_g"])
    krope = apply_rope(k_r, cos, sin)
    return to_heads(k, GLA_DK), to_heads(v, GLA_DV), to_heads(gk, GLA_DK), ckv, krope


def project_queries(h, cos, sin, lp):
    b, s, _ = h.shape
    z = h @ lp["w_in"][:, KEY_COLS:]
    q, g, c_q = jnp.split(z, QUERY_SPLITS, axis=-1)
    qm = (rms_norm(c_q, lp["q_norm_g"]) @ lp["w_uq"]).reshape(b, s, MLA_HEADS, MLA_QK_DIM)
    q_rope = apply_rope(qm[..., MLA_NOPE:], cos[:, None, :], sin[:, None, :])
    q_lat = jnp.einsum("bshn,chn->bshc", qm[..., :MLA_NOPE], lp["w_uk"])
    return to_heads(q, GLA_DK), g, q_lat, q_rope


def gla_state_update(state, k, v, b):
    b_last = b[:, :, -1, :]
    k_tail = k.astype(jnp.float32) * jnp.exp(b_last[:, :, None, :] - b)
    return jnp.exp(b_last)[..., None] * state + jnp.einsum("bhsk,bhsv->bhkv", k_tail, v.astype(jnp.float32))


def gla_block(state, q, k, v, gk):
    c = q.shape[2]
    b = jnp.cumsum(gk, axis=2)
    q_dec = q.astype(jnp.float32) * GLA_SCALE * jnp.exp(b)
    k_inv = k.astype(jnp.float32) * jnp.exp(-b)
    causal = jnp.tril(jnp.ones((c, c), dtype=bool))
    a = jnp.where(causal, jnp.einsum("bhtk,bhsk->bhts", q_dec, k_inv), 0.0)
    o = jnp.einsum("bhts,bhsv->bhtv", a, v.astype(jnp.float32)) + jnp.einsum("bhtk,bhkv->bhtv", q_dec, state)
    return o, gla_state_update(state, k, v, b)


def gla_scan(state, q, k, v, gk):
    bsz, nh, s, _ = q.shape
    c = min(s, CHUNK)
    n = s // c

    def blocks(t):
        return t.reshape(bsz, nh, n, c, t.shape[-1]).transpose(2, 0, 1, 3, 4)

    def step(st, xs):
        o, st = gla_block(st, *xs)
        return st, o

    state, o = lax.scan(step, state, (blocks(q), blocks(k), blocks(v), blocks(gk)))
    return o.transpose(1, 2, 0, 3, 4).reshape(bsz, nh, s, GLA_DV), state


def mla_attend(q_lat, q_rope, ckv, krope, mask):
    s = jnp.einsum("bqhc,bkc->bhqk", q_lat, ckv) + jnp.einsum("bqhr,bkr->bhqk", q_rope, krope)
    s = s.astype(jnp.float32) * MLA_SCALE
    if mask is not None:
        s = jnp.where(mask, s, -jnp.inf)
    p = jax.nn.softmax(s, axis=-1).astype(ckv.dtype)
    return jnp.einsum("bhqk,bkc->bqhc", p, ckv)


def mla_prompt(q_lat, q_rope, ckv, krope, key_chunk):
    bsz, s, nh, c = q_lat.shape
    n = s // Q_BLOCK

    def blocks(t):
        return t.reshape(bsz, n, Q_BLOCK, *t.shape[2:]).swapaxes(0, 1)

    q_chunk = (jnp.arange(s, dtype=jnp.int32) // CHUNK).reshape(n, Q_BLOCK)

    def one(args):
        ql, qr, qc = args
        return mla_attend(ql, qr, ckv, krope, key_chunk[None, :] <= qc[:, None])

    o = lax.map(one, (blocks(q_lat), blocks(q_rope), q_chunk))
    return o.swapaxes(0, 1).reshape(bsz, s, nh, c)


def hier_moe(x, lp):
    t = x.shape[0]
    grp_logits = (x @ lp["w_rg"] + lp["b_rg"]).astype(jnp.float32)
    grp_prob = jax.nn.softmax(grp_logits, axis=-1)
    g_sel = jnp.argmax(grp_logits, axis=-1)
    p_grp = jnp.take_along_axis(grp_prob, g_sel[:, None], axis=-1)
    exp_logits = (jnp.einsum("td,gde->tge", x, lp["w_re"]) + lp["b_re"]).astype(jnp.float32)
    idx = jnp.broadcast_to(g_sel[:, None, None], (t, 1, EXPERTS_PER_GROUP))
    sel = jnp.take_along_axis(exp_logits, idx, axis=1)[:, 0]
    top_val, top_idx = lax.top_k(sel, TOP_K_IN_GROUP)
    w_top = jax.nn.softmax(top_val, axis=-1) * p_grp
    eid = g_sel[:, None] * EXPERTS_PER_GROUP + top_idx
    combine = jnp.einsum("tk,tke->te", w_top, jax.nn.one_hot(eid, N_EXPERTS, dtype=jnp.float32)).astype(x.dtype)
    y = jnp.zeros_like(x)
    for e in range(N_EXPERTS):
        hid = jax.nn.silu(x @ lp["w_gate"][e]) * (x @ lp["w_up"][e])
        y = y + combine[:, e:e + 1] * (hid @ lp["w_down"][e])
    return y


def merge_and_ffn(h, o_gla, o_lat, g_out, lp):
    bsz, s, d = h.shape
    og = o_gla.astype(h.dtype).transpose(0, 2, 1, 3)
    og = rms_norm(og, lp["gla_norm_g"]) * jax.nn.silu(g_out.reshape(bsz, s, GLA_HEADS, GLA_DV))
    y_a = og.reshape(bsz, s, GLA_V) @ lp["w_br_gla"]
    y_b = jnp.einsum("bshc,chv->bshv", o_lat, lp["w_uv"]).reshape(bsz, s, MLA_V) @ lp["w_br_mla"]
    g_a, g_b = jnp.split(jax.nn.sigmoid(h @ lp["w_mg"] + lp["b_mg"]), 2, axis=-1)
    mix = (g_a * y_a + g_b * y_b) @ lp["w_out"]
    h1 = layer_norm(DEEPNORM_ALPHA * h + mix, lp["ln1_g"], lp["ln1_b"])
    f = hier_moe(h1.reshape(bsz * s, d), lp).reshape(bsz, s, d)
    return layer_norm(DEEPNORM_ALPHA * h1 + f, lp["ln2_g"], lp["ln2_b"])


def setup_inputs(seed: int = 0) -> dict:
    key = jax.random.key(seed)
    ks = iter(jax.random.split(key, 40))

    def nrm(shape, scale):
        return jax.random.normal(next(ks), shape, jnp.float32) * scale

    L, D = DEPTH, D_MODEL
    beta = DEEPNORM_BETA
    col_scale = jnp.concatenate([jnp.ones((GLA_QK,), jnp.float32),
                                 jnp.full((GLA_V,), beta, jnp.float32),
                                 jnp.ones((D_IN - GLA_QK - GLA_V,), jnp.float32)])
    return {
        "x_prompt": nrm((BATCH, SEQ, D), 1.0),
        "x_sample": nrm((DEC_BATCH, DEC_SEQ, D), 1.0),
        "cache_mla_latent": nrm((L, DEC_BATCH, PAST_LEN, MLA_KV_RANK), 1.0),
        "cache_mla_krope": nrm((L, DEC_BATCH, PAST_LEN, MLA_ROPE), 1.0),
        "state_gla": nrm((L, DEC_BATCH, GLA_HEADS, GLA_DK, GLA_DV), 0.3),
        "meta_tokens": nrm((N_META, D), 1.0),
        "ln_in_g": 1.0 + nrm((D,), 0.02),
        "ln_in_b": nrm((D,), 0.02),
        "w_in": nrm((L, D, D_IN), D ** -0.5) * col_scale,
        "w_gk2": nrm((L, GLA_GATE_RANK, GLA_QK), GLA_GATE_RANK ** -0.5),
        "b_gk": nrm((L, GLA_QK), 0.1),
        "gla_norm_g": 1.0 + nrm((L, GLA_DV), 0.02),
        "q_norm_g": 1.0 + nrm((L, MLA_Q_RANK), 0.02),
        "kv_norm_g": 1.0 + nrm((L, MLA_KV_RANK), 0.02),
        "w_uq": nrm((L, MLA_Q_RANK, MLA_HEADS * MLA_QK_DIM), MLA_Q_RANK ** -0.5),
        "w_uk": nrm((L, MLA_KV_RANK, MLA_HEADS, MLA_NOPE), MLA_KV_RANK ** -0.5),
        "w_uv": nrm((L, MLA_KV_RANK, MLA_HEADS, MLA_DV), beta * MLA_KV_RANK ** -0.5),
        "w_br_gla": nrm((L, GLA_V, D), beta * GLA_V ** -0.5),
        "w_br_mla": nrm((L, MLA_V, D), beta * MLA_V ** -0.5),
        "w_mg": nrm((L, D, 2 * D), D ** -0.5),
        "b_mg": nrm((L, 2 * D), 0.02),
        "w_out": nrm((L, D, D), beta * D ** -0.5),
        "ln1_g": 1.0 + nrm((L, D), 0.02),
        "ln1_b": nrm((L, D), 0.02),
        "w_rg": nrm((L, D, N_GROUPS), D ** -0.5),
        "b_rg": nrm((L, N_GROUPS), 0.01),
        "w_re": nrm((L, N_GROUPS, D, EXPERTS_PER_GROUP), D ** -0.5),
        "b_re": nrm((L, N_GROUPS, EXPERTS_PER_GROUP), 0.01),
        "w_gate": nrm((L, N_EXPERTS, D, D_EXPERT), D ** -0.5),
        "w_up": nrm((L, N_EXPERTS, D, D_EXPERT), beta * D ** -0.5),
        "w_down": nrm((L, N_EXPERTS, D_EXPERT, D), beta * D_EXPERT ** -0.5),
        "ln2_g": 1.0 + nrm((L, D), 0.02),
        "ln2_b": nrm((L, D), 0.02),
    }


def reference(x_prompt, x_sample, cache_mla_latent, cache_mla_krope, state_gla,
              meta_tokens, ln_in_g, ln_in_b, w_in, w_gk2, b_gk, gla_norm_g, q_norm_g, kv_norm_g,
              w_uq, w_uk, w_uv, w_br_gla, w_br_mla, w_mg, b_mg, w_out, ln1_g, ln1_b,
              w_rg, b_rg, w_re, b_re, w_gate, w_up, w_down, ln2_g, ln2_b):
    bp, sp, _ = x_prompt.shape
    bs, ss, _ = x_sample.shape
    cos_m, sin_m = rope_tables(jnp.arange(-N_META, 0, dtype=jnp.int32))
    cos_p, sin_p = rope_tables(jnp.arange(sp, dtype=jnp.int32))
    cos_s, sin_s = rope_tables(PAST_LEN + jnp.arange(ss, dtype=jnp.int32))
    key_chunk = jnp.concatenate([jnp.full((N_META,), -1, jnp.int32),
                                 jnp.arange(sp, dtype=jnp.int32) // CHUNK])

    hm = layer_norm(meta_tokens, ln_in_g, ln_in_b)[None]
    hp = layer_norm(x_prompt, ln_in_g, ln_in_b)
    hs = layer_norm(x_sample, ln_in_g, ln_in_b)
    lat_p_l, kr_p_l, st_p_l, lat_s_l, kr_s_l, st_s_l = [], [], [], [], [], []

    for l in range(DEPTH):
        lp = {"w_in": w_in[l], "w_gk2": w_gk2[l], "b_gk": b_gk[l], "gla_norm_g": gla_norm_g[l],
              "q_norm_g": q_norm_g[l], "kv_norm_g": kv_norm_g[l], "w_uq": w_uq[l], "w_uk": w_uk[l],
              "w_uv": w_uv[l], "w_br_gla": w_br_gla[l], "w_br_mla": w_br_mla[l], "w_mg": w_mg[l],
              "b_mg": b_mg[l], "w_out": w_out[l], "ln1_g": ln1_g[l], "ln1_b": ln1_b[l],
              "w_rg": w_rg[l], "b_rg": b_rg[l], "w_re": w_re[l], "b_re": b_re[l],
              "w_gate": w_gate[l], "w_up": w_up[l], "w_down": w_down[l],
              "ln2_g": ln2_g[l], "ln2_b": ln2_b[l]}

        mk, mv, mgk, mckv, mkr = project_keys(hm, cos_m, sin_m, lp)
        zero_state = jnp.zeros((1, GLA_HEADS, GLA_DK, GLA_DV), jnp.float32)
        if l + 1 < DEPTH:
            mq, mg, mql, mqr = project_queries(hm, cos_m, sin_m, lp)
            mo, m_state = gla_scan(zero_state, mq, mk, mv, mgk)
            m_lat = mla_attend(mql, mqr, mckv, mkr, None)
            hm = merge_and_ffn(hm, mo, m_lat, mg, lp)
        else:
            m_state = gla_state_update(zero_state, mk, mv, jnp.cumsum(mgk, axis=2))

        pk, pv, pgk, pckv, pkr = project_keys(hp, cos_p, sin_p, lp)
        pq, pg, pql, pqr = project_queries(hp, cos_p, sin_p, lp)
        po, p_state = gla_scan(jnp.broadcast_to(m_state, (bp,) + m_state.shape[1:]), pq, pk, pv, pgk)
        lat_p = jnp.concatenate([jnp.broadcast_to(mckv, (bp,) + mckv.shape[1:]), pckv], axis=1)
        kr_p = jnp.concatenate([jnp.broadcast_to(mkr, (bp,) + mkr.shape[1:]), pkr], axis=1)
        p_lat = mla_prompt(pql, pqr, lat_p, kr_p, key_chunk)
        hp = merge_and_ffn(hp, po, p_lat, pg, lp)

        sk, sv, sgk, sckv, skr = project_keys(hs, cos_s, sin_s, lp)
        sq, sg, sql, sqr = project_queries(hs, cos_s, sin_s, lp)
        so, s_state = gla_scan(state_gla[l].astype(jnp.float32), sq, sk, sv, sgk)
        lat_s = jnp.concatenate([jnp.broadcast_to(mckv, (bs,) + mckv.shape[1:]), cache_mla_latent[l], sckv], axis=1)
        kr_s = jnp.concatenate([jnp.broadcast_to(mkr, (bs,) + mkr.shape[1:]), cache_mla_krope[l], skr], axis=1)
        s_lat = mla_attend(sql, sqr, lat_s, kr_s, None)
        hs = merge_and_ffn(hs, so, s_lat, sg, lp)

        lat_p_l.append(lat_p)
        kr_p_l.append(kr_p)
        st_p_l.append(p_state.astype(state_gla.dtype))
        lat_s_l.append(sckv)
        kr_s_l.append(skr)
        st_s_l.append(s_state.astype(state_gla.dtype))

    y_prompt = hp
    y_sample = hs
    new_latent_prompt = jnp.stack(lat_p_l)
    new_krope_prompt = jnp.stack(kr_p_l)
    new_gla_prompt = jnp.stack(st_p_l)
    new_latent_sample = jnp.stack(lat_s_l)
    new_krope_sample = jnp.stack(kr_s_l)
    new_gla_sample = jnp.stack(st_s_l)
    return (y_prompt, y_sample, new_latent_prompt, new_krope_prompt, new_gla_prompt,
            new_latent_sample, new_krope_sample, new_gla_sample)
```

```python
import functools

import jax
import jax.numpy as jnp
from jax import lax
from jax.experimental import pallas as pl
from jax.experimental.pallas import tpu as pltpu

F32 = jnp.float32
BF16 = jnp.bfloat16

CHUNK = 64
N_META = 16
GLA_HEADS = 4
GLA_DK = 128
GLA_DV = 256
GLA_RANK = 16
GLA_TAU = 16.0
GLA_QK = GLA_HEADS * GLA_DK
GLA_V = GLA_HEADS * GLA_DV
GLA_SCALE = GLA_DK ** -0.5
MLA_HEADS = 16
MLA_Q_RANK = 384
MLA_KV_RANK = 128
MLA_NOPE = 64
MLA_ROPE = 32
MLA_HALF = MLA_ROPE // 2
MLA_DV = 64
MLA_QK_DIM = MLA_NOPE + MLA_ROPE
MLA_CAT = MLA_KV_RANK + MLA_ROPE
MLA_SCALE = MLA_QK_DIM ** -0.5
ROPE_THETA = 10000.0
N_GROUPS = 4
EXPERTS_PER_GROUP = 8
N_EXPERTS = N_GROUPS * EXPERTS_PER_GROUP
D_EXPERT = 256
LN_EPS = 1e-5
RMS_EPS = 1e-6
DEEPNORM_ALPHA = 2.0 ** 0.25

LANES = 128
MXU_DIM = 256
VMEM_LIMIT_BYTES = 56 * 1024 * 1024

TOKEN_TILE = 256
KEY_TILE = MXU_DIM
MOE_TOKEN_TILE = 1024
ROUTER_LANES = LANES
NEG_BIG = -1e30

NT_DIMS = (((1,), (1,)), ((), ()))
TN_DIMS = (((0,), (0,)), ((), ()))


def _cparams(*sem):
    return pltpu.CompilerParams(dimension_semantics=sem, vmem_limit_bytes=VMEM_LIMIT_BYTES)


def _const_spec(shape):
    nd = len(shape)
    return pl.BlockSpec(shape, lambda *_: (0,) * nd, pipeline_mode=pl.Buffered(1))


def _layer_norm(x, g, b):
    mu = jnp.mean(x, axis=-1, keepdims=True)
    xc = x - mu
    var = jnp.mean(xc * xc, axis=-1, keepdims=True)
    return xc * lax.rsqrt(var + LN_EPS) * g + b


def _dot(a, b):
    return jnp.dot(a, b, preferred_element_type=F32)


TOK_COLS = (GLA_QK, GLA_V, GLA_QK, MLA_KV_RANK, MLA_ROPE, GLA_RANK)
TOK_USED = sum(TOK_COLS)
TOK_PAD = -(-TOK_USED // MXU_DIM) * MXU_DIM
TR_ROWS = MLA_KV_RANK + MLA_Q_RANK


def _proj_kernel(x_ref, cs_ref, cst_ref, lng_ref, lnb_ref, wtok_ref, wtr_ref, wgk2_ref, bgk_ref,
                 kvg_ref, kvgt_ref, qngt_ref, wuqt_ref, wuk_ref,
                 qd_ref, ki_ref, kt_ref, v_ref, ebl_ref, ckv_ref, kr_ref, kcat_ref, vt_ref, qt_ref, *, cl):
    tt = x_ref.shape[1]
    n_chunks = tt // cl
    h = _layer_norm(x_ref[0], lng_ref[...], lnb_ref[...])
    hb = h.astype(BF16)

    z = _dot(hb, wtok_ref[...])
    o0 = 0
    k = z[:, o0:o0 + GLA_QK]; o0 += GLA_QK
    v = z[:, o0:o0 + GLA_V]; o0 += GLA_V
    q = z[:, o0:o0 + GLA_QK]; o0 += GLA_QK
    ckv_raw = z[:, o0:o0 + MLA_KV_RANK]; o0 += MLA_KV_RANK
    kr = z[:, o0:o0 + MLA_ROPE]; o0 += MLA_ROPE
    gklr = z[:, o0:o0 + GLA_RANK]

    gz = _dot(gklr.astype(BF16), wgk2_ref[...]) + bgk_ref[...]
    gk = (jnp.minimum(gz, 0.0) - jnp.log(1.0 + jnp.exp(-jnp.abs(gz)))) * (1.0 / GLA_TAU)
    row_in_chunk = lax.broadcasted_iota(jnp.int32, gk.shape, 0) & (cl - 1)
    b = gk
    shift = 1
    while shift < cl:
        b = b + jnp.where(row_in_chunk >= shift, pltpu.roll(b, shift, 0), 0.0)
        shift *= 2
    b3 = b.reshape(n_chunks, cl, GLA_QK)
    bl = b3[:, cl - 1:cl, :]
    qd_ref[0] = (q * GLA_SCALE * jnp.exp(b)).astype(BF16)
    ki_ref[0] = (k * jnp.exp(-b)).astype(BF16)
    kt_ref[0] = (k.reshape(n_chunks, cl, GLA_QK) * jnp.exp(bl - b3)).reshape(tt, GLA_QK).astype(BF16)
    v_ref[0] = v.astype(BF16)
    ebl_ref[0] = jnp.exp(bl)

    ckv = ckv_raw * lax.rsqrt(jnp.mean(ckv_raw * ckv_raw, axis=-1, keepdims=True) + RMS_EPS) * kvg_ref[...]
    cs = cs_ref[0]
    cos, sin = cs[:, :MLA_HALF], cs[:, MLA_HALF:]
    x1, x2 = kr[:, :MLA_HALF], kr[:, MLA_HALF:]
    kr_rot = jnp.concatenate([x1 * cos - x2 * sin, x2 * cos + x1 * sin], axis=-1)
    ckv_ref[0] = ckv
    kr_ref[0] = kr_rot
    kcat_ref[0, :, :MLA_KV_RANK] = ckv.astype(BF16)
    kcat_ref[0, :, MLA_KV_RANK:] = kr_rot.astype(BF16)

    zt = lax.dot_general(wtr_ref[...], hb, NT_DIMS, preferred_element_type=F32)
    ckvt = zt[:MLA_KV_RANK]
    ckvt = ckvt * lax.rsqrt(jnp.mean(ckvt * ckvt, axis=0, keepdims=True) + RMS_EPS) * kvgt_ref[...]
    vt_ref[0] = ckvt.astype(BF16)
    cqt = zt[MLA_KV_RANK:]
    cqt = cqt * lax.rsqrt(jnp.mean(cqt * cqt, axis=0, keepdims=True) + RMS_EPS) * qngt_ref[...]
    qmt = _dot(wuqt_ref[...], cqt.astype(BF16))
    n_nope = MLA_HEADS * MLA_NOPE
    n_half = MLA_HEADS * MLA_HALF
    cst = cst_ref[0]
    cos_t = jnp.concatenate([cst[:MLA_HALF]] * MLA_HEADS, axis=0)
    sin_t = jnp.concatenate([cst[MLA_HALF:]] * MLA_HEADS, axis=0)
    r1 = qmt[n_nope:n_nope + n_half]
    r2 = qmt[n_nope + n_half:]
    rot1 = ((r1 * cos_t - r2 * sin_t) * MLA_SCALE).astype(BF16)
    rot2 = ((r2 * cos_t + r1 * sin_t) * MLA_SCALE).astype(BF16)
    for hd in range(MLA_HEADS):
        nope = qmt[hd * MLA_NOPE:(hd + 1) * MLA_NOPE].astype(BF16)
        qlat = _dot(wuk_ref[hd], nope) * MLA_SCALE
        qt_ref[0, 0, hd, :MLA_KV_RANK, :] = qlat.astype(BF16)
        qt_ref[0, 0, hd, MLA_KV_RANK:MLA_KV_RANK + MLA_HALF, :] = rot1[hd * MLA_HALF:(hd + 1) * MLA_HALF]
        qt_ref[0, 0, hd, MLA_KV_RANK + MLA_HALF:, :] = rot2[hd * MLA_HALF:(hd + 1) * MLA_HALF]


def _proj(x, cs, cst, wp, *, cl, tt):
    bsz, s, d = x.shape
    nt = s // tt
    n_chunks = tt // cl
    tok = lambda w: pl.BlockSpec((1, tt, w), lambda b, t: (b, t, 0))
    in_specs = [
        tok(d),
        pl.BlockSpec((1, tt, MLA_ROPE), lambda b, t: (0, t, 0)),
        pl.BlockSpec((1, MLA_ROPE, tt), lambda b, t: (0, 0, t)),
    ] + [_const_spec(wp[n].shape) for n in _PROJ_WEIGHTS]
    out_shape = [
        jax.ShapeDtypeStruct((bsz, s, GLA_QK), BF16),
        jax.ShapeDtypeStruct((bsz, s, GLA_QK), BF16),
        jax.ShapeDtypeStruct((bsz, s, GLA_QK), BF16),
        jax.ShapeDtypeStruct((bsz, s, GLA_V), BF16),
        jax.ShapeDtypeStruct((bsz, s // cl, 1, GLA_QK), F32),
        jax.ShapeDtypeStruct((bsz, s, MLA_KV_RANK), F32),
        jax.ShapeDtypeStruct((bsz, s, MLA_ROPE), F32),
        jax.ShapeDtypeStruct((bsz, s, MLA_CAT), BF16),
        jax.ShapeDtypeStruct((bsz, MLA_KV_RANK, s), BF16),
        jax.ShapeDtypeStruct((bsz, nt, MLA_HEADS, MLA_CAT, tt), BF16),
    ]
    out_specs = [
        tok(GLA_QK), tok(GLA_QK), tok(GLA_QK), tok(GLA_V),
        pl.BlockSpec((1, n_chunks, 1, GLA_QK), lambda b, t: (b, t, 0, 0)),
        tok(MLA_KV_RANK), tok(MLA_ROPE), tok(MLA_CAT),
        pl.BlockSpec((1, MLA_KV_RANK, tt), lambda b, t: (b, 0, t)),
        pl.BlockSpec((1, 1, MLA_HEADS, MLA_CAT, tt), lambda b, t: (b, t, 0, 0, 0)),
    ]
    return pl.pallas_call(
        functools.partial(_proj_kernel, cl=cl),
        grid=(bsz, nt),
        in_specs=in_specs,
        out_specs=out_specs,
        out_shape=out_shape,
        compiler_params=_cparams("parallel", "parallel"),
        name="proj",
    )(x, cs, cst, *[wp[n] for n in _PROJ_WEIGHTS])


_PROJ_WEIGHTS = ("ln_g", "ln_b", "w_tok", "w_tr", "w_gk2", "b_gk", "kv_g", "kv_gt", "q_gt", "w_uqt", "w_uk")


def _gla_kernel(qd_ref, ki_ref, kt_ref, v_ref, ebl_ref, s0_ref, o_ref, sfin_ref, st_ref, *, cl):
    t = pl.program_id(2)
    ts = qd_ref.shape[1]

    @pl.when(t == 0)
    def _():
        st_ref[...] = s0_ref[0, 0].T

    qd, ki, kt, v = qd_ref[0], ki_ref[0], kt_ref[0], v_ref[0]
    a = lax.dot_general(qd, ki, NT_DIMS, preferred_element_type=F32)
    row = lax.broadcasted_iota(jnp.int32, (ts, ts), 0)
    col = lax.broadcasted_iota(jnp.int32, (ts, ts), 1)
    keep = (row >= col) & ((row & -cl) == (col & -cl))
    a = jnp.where(keep, a, 0.0).astype(BF16)
    o_ref[0] = _dot(a, v)
    for c in range(ts // cl):
        rows = slice(c * cl, (c + 1) * cl)
        st = st_ref[...]
        o_ref[0, rows, :] += lax.dot_general(qd[rows], st.astype(BF16), NT_DIMS, preferred_element_type=F32)
        st_ref[...] = st * ebl_ref[0, c] + lax.dot_general(v[rows], kt[rows], TN_DIMS, preferred_element_type=F32)

    @pl.when(t == pl.num_programs(2) - 1)
    def _():
        sfin_ref[0, 0] = st_ref[...].T


def _gla(qd, ki, kt, v, ebl, s0, *, cl, ts):
    bsz, s, _ = qd.shape
    n_chunks = ts // cl
    s0_b = s0.shape[0]
    qk_spec = pl.BlockSpec((1, ts, GLA_DK), lambda b, h, t: (b, t, h))
    v_spec = pl.BlockSpec((1, ts, GLA_DV), lambda b, h, t: (b, t, h))
    st_spec = pl.BlockSpec((1, 1, GLA_DK, GLA_DV), lambda b, h, t: (b, h, 0, 0))
    s0_spec = st_spec if s0_b == bsz else pl.BlockSpec((1, 1, GLA_DK, GLA_DV), lambda b, h, t: (0, h, 0, 0))
    return pl.pallas_call(
        functools.partial(_gla_kernel, cl=cl),
        grid=(bsz, GLA_HEADS, s // ts),
        in_specs=[qk_spec, qk_spec, qk_spec, v_spec,
                  pl.BlockSpec((1, n_chunks, 1, GLA_DK), lambda b, h, t: (b, t, 0, h)),
                  s0_spec],
        out_specs=[v_spec, st_spec],
        out_shape=[jax.ShapeDtypeStruct((bsz, s, GLA_V), F32),
                   jax.ShapeDtypeStruct((bsz, GLA_HEADS, GLA_DK, GLA_DV), F32)],
        scratch_shapes=[pltpu.VMEM((GLA_DV, GLA_DK), F32)],
        compiler_params=_cparams("parallel", "parallel", "arbitrary"),
        name="gla",
    )(qd, ki, kt, v, ebl, s0)


def _attn_kernel(qt_ref, kcat_ref, vt_ref, o_ref, m_ref, l_ref, acc_ref, *, bq, n_valid, causal):
    i = pl.program_id(1)
    n_cc = qt_ref.shape[2]
    n_kt = kcat_ref.shape[1]
    m_ref[...] = jnp.full(m_ref.shape, NEG_BIG, F32)
    l_ref[...] = jnp.zeros(l_ref.shape, F32)
    acc_ref[...] = jnp.zeros(acc_ref.shape, F32)
    key_in_tile = lax.broadcasted_iota(jnp.int32, (KEY_TILE, 1), 0)
    col = lax.broadcasted_iota(jnp.int32, (1, KEY_TILE), 1)
    if causal:
        q_pos = i * bq + (col & (bq - 1))
        q_code = (q_pos >> 6) + 1
        n_steps = jnp.minimum(i * (bq // KEY_TILE) + (bq // KEY_TILE) + 1, n_kt)
    else:
        n_steps = n_kt

    def key_step(j, carry):
        kt = kcat_ref[0, j]
        vt = vt_ref[0, j]
        key_idx = j * KEY_TILE + key_in_tile
        if causal:
            k_code = (key_idx + (CHUNK - N_META)) >> 6
            visible = (k_code <= q_code) & (key_idx < n_valid)
        else:
            visible = jnp.broadcast_to(key_idx < n_valid, (KEY_TILE, KEY_TILE))
        for c in range(n_cc):
            s = _dot(kt, qt_ref[0, 0, c])
            s = jnp.where(visible, s, NEG_BIG)
            m_prev = m_ref[c]
            m_new = jnp.maximum(m_prev, jnp.max(s, axis=0, keepdims=True))
            alpha = jnp.exp(m_prev - m_new)
            p = jnp.exp(s - m_new)
            l_ref[c] = alpha * l_ref[c] + jnp.sum(p, axis=0, keepdims=True)
            acc_ref[c] = alpha * acc_ref[c] + _dot(vt, p.astype(BF16))
            m_ref[c] = m_new
        return carry

    lax.fori_loop(0, n_steps, key_step, 0)

    groups_per_cc = KEY_TILE // LANES
    for c in range(n_cc):
        o_t = acc_ref[c] * (1.0 / l_ref[c])
        for g in range(groups_per_cc):
            blk = o_t[:, g * LANES:(g + 1) * LANES].T.astype(BF16)
            col0 = c * KEY_TILE + g * LANES
            if bq >= LANES:
                hd, q0 = col0 // bq, col0 % bq
                o_ref[0, q0:q0 + LANES, hd * MLA_KV_RANK:(hd + 1) * MLA_KV_RANK] = blk
            else:
                for hl in range(LANES // bq):
                    hd = col0 // bq + hl
                    o_ref[0, :, hd * MLA_KV_RANK:(hd + 1) * MLA_KV_RANK] = blk[hl * bq:(hl + 1) * bq]


def _attn(qt, kcat, vt, *, bq, n_valid, causal):
    bsz, nq, n_cc = qt.shape[:3]
    n_kt = kcat.shape[1]
    return pl.pallas_call(
        functools.partial(_attn_kernel, bq=bq, n_valid=n_valid, causal=causal),
        grid=(bsz, nq),
        in_specs=[pl.BlockSpec((1, 1, n_cc, MLA_CAT, KEY_TILE), lambda b, i: (b, i, 0, 0, 0)),
                  pl.BlockSpec((1, n_kt, KEY_TILE, MLA_CAT), lambda b, i: (b, 0, 0, 0)),
                  pl.BlockSpec((1, n_kt, MLA_KV_RANK, KEY_TILE), lambda b, i: (b, 0, 0, 0))],
        out_specs=pl.BlockSpec((1, bq, MLA_HEADS * MLA_KV_RANK), lambda b, i: (b, i, 0)),
        out_shape=jax.ShapeDtypeStruct((bsz, nq * bq, MLA_HEADS * MLA_KV_RANK), BF16),
        scratch_shapes=[pltpu.VMEM((n_cc, 1, KEY_TILE), F32),
                        pltpu.VMEM((n_cc, 1, KEY_TILE), F32),
                        pltpu.VMEM((n_cc, MLA_KV_RANK, KEY_TILE), F32)],
        compiler_params=_cparams("parallel", "arbitrary"),
        name="attn",
    )(qt, kcat, vt)


_MERGE_WEIGHTS = ("ln_g", "ln_b", "w_g", "gla_g", "w_br_gla", "w_uv_bd", "w_br_mla", "w_mg", "b_mg", "w_out",
                  "ln1_g", "ln1_b", "w_r_hi", "w_r_lo", "b_r")


def _merge_kernel(x_ref, og_ref, ol_ref, lng_ref, lnb_ref, wg_ref, glag_ref, wbg_ref, wuv_ref, wbm_ref,
                  wmg_ref, bmg_ref, wout_ref, l1g_ref, l1b_ref, wrh_ref, wrl_ref, br_ref,
                  h1_ref, comb_ref):
    d = x_ref.shape[1]
    h = _layer_norm(x_ref[...], lng_ref[...], lnb_ref[...])
    hb = h.astype(BF16)
    g_out = _dot(hb, wg_ref[...])
    og = og_ref[...]
    parts = []
    for hd in range(GLA_HEADS):
        cols = slice(hd * GLA_DV, (hd + 1) * GLA_DV)
        o_h = og[:, cols]
        g_h = g_out[:, cols]
        o_n = o_h * lax.rsqrt(jnp.mean(o_h * o_h, axis=-1, keepdims=True) + RMS_EPS) * glag_ref[...]
        parts.append(o_n * (g_h * jax.nn.sigmoid(g_h)))
    y_a = _dot(jnp.concatenate(parts, axis=-1).astype(BF16), wbg_ref[...])
    y_heads = _dot(ol_ref[...], wuv_ref[...])
    y_b = _dot(y_heads.astype(BF16), wbm_ref[...])
    gates = jax.nn.sigmoid(_dot(hb, wmg_ref[...]) + bmg_ref[...])
    mix_in = gates[:, :d] * y_a + gates[:, d:] * y_b
    mix = _dot(mix_in.astype(BF16), wout_ref[...])
    h1 = _layer_norm(DEEPNORM_ALPHA * h + mix, l1g_ref[...], l1b_ref[...])
    h1_ref[...] = h1

    h1_hi = h1.astype(BF16)
    h1_lo = (h1 - h1_hi.astype(F32)).astype(BF16)
    logits = (_dot(h1_hi, wrh_ref[...]) + (_dot(h1_hi, wrl_ref[...]) + _dot(h1_lo, wrh_ref[...]))) + br_ref[...]
    lane = lax.broadcasted_iota(jnp.int32, logits.shape, 1)
    is_grp = lane < N_GROUPS
    gl = jnp.where(is_grp, logits, NEG_BIG)
    g_max = jnp.max(gl, axis=-1, keepdims=True)
    g_sel = jnp.min(jnp.where(gl == g_max, lane, ROUTER_LANES), axis=-1, keepdims=True)
    p_grp = 1.0 / jnp.sum(jnp.where(is_grp, jnp.exp(gl - g_max), 0.0), axis=-1, keepdims=True)
    e_lo = N_GROUPS + g_sel * EXPERTS_PER_GROUP
    in_grp = (lane >= e_lo) & (lane < e_lo + EXPERTS_PER_GROUP)
    el = jnp.where(in_grp, logits, NEG_BIG)
    v1 = jnp.max(el, axis=-1, keepdims=True)
    i1 = jnp.min(jnp.where(el == v1, lane, ROUTER_LANES), axis=-1, keepdims=True)
    el2 = jnp.where(lane == i1, NEG_BIG, el)
    v2 = jnp.max(el2, axis=-1, keepdims=True)
    i2 = jnp.min(jnp.where(el2 == v2, lane, ROUTER_LANES), axis=-1, keepdims=True)
    e2 = jnp.exp(v2 - v1)
    w1 = p_grp / (1.0 + e2)
    w2 = p_grp * e2 / (1.0 + e2)
    comb_ref[...] = jnp.where(lane == i1 - N_GROUPS, w1, 0.0) + jnp.where(lane == i2 - N_GROUPS, w2, 0.0)


def _merge(x2, og2, ol2, wp, *, tt):
    t, d = x2.shape
    row = lambda w: pl.BlockSpec((tt, w), lambda i: (i, 0))
    return pl.pallas_call(
        _merge_kernel,
        grid=(t // tt,),
        in_specs=[row(d), row(GLA_V), row(MLA_HEADS * MLA_KV_RANK)] + [_const_spec(wp[n].shape) for n in _MERGE_WEIGHTS],
        out_specs=[row(d), row(ROUTER_LANES)],
        out_shape=[jax.ShapeDtypeStruct((t, d), F32), jax.ShapeDtypeStruct((t, ROUTER_LANES), F32)],
        compiler_params=_cparams("parallel"),
        name="merge",
    )(x2, og2, ol2, *[wp[n] for n in _MERGE_WEIGHTS])


def _moe_kernel(h1_ref, comb_ref, wg_ref, wu_ref, wd_ref, l2g_ref, l2b_ref, y_ref, xb_ref):
    e = pl.program_id(1)

    @pl.when(e == 0)
    def _():
        xb_ref[...] = h1_ref[...].astype(BF16)
        y_ref[...] = jnp.zeros(y_ref.shape, F32)

    xb = xb_ref[...]
    gate = _dot(xb, wg_ref[0])
    up = _dot(xb, wu_ref[0])
    hid = (gate * jax.nn.sigmoid(gate)) * up
    lane = lax.broadcasted_iota(jnp.int32, comb_ref.shape, 1)
    w_e = jnp.sum(jnp.where(lane == e, comb_ref[...], 0.0), axis=-1, keepdims=True)
    y_ref[...] += w_e * _dot(hid.astype(BF16), wd_ref[0])

    @pl.when(e == pl.num_programs(1) - 1)
    def _():
        y_ref[...] = _layer_norm(DEEPNORM_ALPHA * h1_ref[...] + y_ref[...], l2g_ref[...], l2b_ref[...])


def _moe(h1, comb, wp, *, tt):
    t, d = h1.shape
    n_e = wp["w_gate"].shape[0]
    return pl.pallas_call(
        _moe_kernel,
        grid=(t // tt, n_e),
        in_specs=[pl.BlockSpec((tt, d), lambda i, e: (i, 0)),
                  pl.BlockSpec((tt, ROUTER_LANES), lambda i, e: (i, 0)),
                  pl.BlockSpec((1, d, D_EXPERT), lambda i, e: (e, 0, 0)),
                  pl.BlockSpec((1, d, D_EXPERT), lambda i, e: (e, 0, 0)),
                  pl.BlockSpec((1, D_EXPERT, d), lambda i, e: (e, 0, 0)),
                  _const_spec((1, d)), _const_spec((1, d))],
        out_specs=pl.BlockSpec((tt, d), lambda i, e: (i, 0)),
        out_shape=jax.ShapeDtypeStruct((t, d), F32),
        scratch_shapes=[pltpu.VMEM((tt, d), BF16)],
        compiler_params=_cparams("parallel", "arbitrary"),
        name="moe",
    )(h1, comb, wp["w_gate"], wp["w_up"], wp["w_down"], wp["ln2_g"], wp["ln2_b"])


def _rope_tables(pos):
    inv = ROPE_THETA ** (-jnp.arange(0, MLA_ROPE, 2, dtype=F32) / MLA_ROPE)
    ang = pos.astype(F32)[:, None] * inv[None, :]
    cs = jnp.concatenate([jnp.cos(ang), jnp.sin(ang)], axis=-1)
    return cs[None], cs.T[None]


def _prep_weights(ln_in_g, ln_in_b, w_in, w_gk2, b_gk, gla_norm_g, q_norm_g, kv_norm_g, w_uq, w_uk, w_uv,
                  w_br_gla, w_br_mla, w_mg, b_mg, w_out, ln1_g, ln1_b, w_rg, b_rg, w_re, b_re,
                  w_gate, w_up, w_down, ln2_g, ln2_b):
    d = w_in.shape[1]
    w = w_in[0]
    c0 = 0
    wk = w[:, c0:c0 + GLA_QK]; c0 += GLA_QK
    wv = w[:, c0:c0 + GLA_V]; c0 += GLA_V
    wgr = w[:, c0:c0 + GLA_RANK]; c0 += GLA_RANK
    wckv = w[:, c0:c0 + MLA_KV_RANK]; c0 += MLA_KV_RANK
    wkr = w[:, c0:c0 + MLA_ROPE]; c0 += MLA_ROPE
    wq = w[:, c0:c0 + GLA_QK]; c0 += GLA_QK
    wg = w[:, c0:c0 + GLA_V]; c0 += GLA_V
    wcq = w[:, c0:c0 + MLA_Q_RANK]
    w_tok = jnp.concatenate([wk, wv, wq, wckv, wkr, wgr, jnp.zeros((d, TOK_PAD - TOK_USED), F32)], axis=1)
    w_tr = jnp.concatenate([wckv, wcq], axis=1).T
    uq = w_uq[0].reshape(MLA_Q_RANK, MLA_HEADS, MLA_QK_DIM)
    uq_perm = jnp.concatenate([
        uq[:, :, :MLA_NOPE].reshape(MLA_Q_RANK, -1),
        uq[:, :, MLA_NOPE:MLA_NOPE + MLA_HALF].reshape(MLA_Q_RANK, -1),
        uq[:, :, MLA_NOPE + MLA_HALF:].reshape(MLA_Q_RANK, -1)], axis=1)
    uv = w_uv[0]
    eye = jnp.eye(MLA_HEADS, dtype=F32)
    w_uv_bd = (uv.transpose(1, 0, 2)[:, :, None, :] * eye[:, None, :, None]).reshape(
        MLA_HEADS * MLA_KV_RANK, MLA_HEADS * MLA_DV)
    w_r = jnp.concatenate([w_rg[0], w_re[0].transpose(1, 0, 2).reshape(d, N_EXPERTS),
                           jnp.zeros((d, ROUTER_LANES - N_GROUPS - N_EXPERTS), F32)], axis=1)
    w_r_hi = w_r.astype(BF16)
    b_r = jnp.concatenate([b_rg[0], b_re[0].reshape(-1), jnp.zeros((ROUTER_LANES - N_GROUPS - N_EXPERTS,), F32)])
    row = lambda a: a.reshape(1, -1)
    return {
        "ln_g": row(ln_in_g), "ln_b": row(ln_in_b),
        "w_tok": w_tok.astype(BF16), "w_tr": w_tr.astype(BF16),
        "w_gk2": w_gk2[0].astype(BF16), "b_gk": row(b_gk[0]),
        "kv_g": row(kv_norm_g[0]), "kv_gt": kv_norm_g[0].reshape(-1, 1), "q_gt": q_norm_g[0].reshape(-1, 1),
        "w_uqt": uq_perm.T.astype(BF16), "w_uk": w_uk[0].transpose(1, 0, 2).astype(BF16),
        "w_g": wg.astype(BF16), "gla_g": row(gla_norm_g[0]),
        "w_br_gla": w_br_gla[0].astype(BF16), "w_uv_bd": w_uv_bd.astype(BF16), "w_br_mla": w_br_mla[0].astype(BF16),
        "w_mg": w_mg[0].astype(BF16), "b_mg": row(b_mg[0]), "w_out": w_out[0].astype(BF16),
        "ln1_g": row(ln1_g[0]), "ln1_b": row(ln1_b[0]),
        "w_r_hi": w_r_hi, "w_r_lo": (w_r - w_r_hi.astype(F32)).astype(BF16), "b_r": row(b_r),
        "w_gate": w_gate[0].astype(BF16), "w_up": w_up[0].astype(BF16), "w_down": w_down[0].astype(BF16),
        "ln2_g": row(ln2_g[0]), "ln2_b": row(ln2_b[0]),
    }


def _key_tiles(kcat, vt):
    bsz, length, _ = kcat.shape
    n = -(-length // KEY_TILE)
    pad = n * KEY_TILE - length
    kcat = jnp.pad(kcat, ((0, 0), (0, pad), (0, 0)))
    vt = jnp.pad(vt, ((0, 0), (0, 0), (0, pad)))
    return (kcat.reshape(bsz, n, KEY_TILE, MLA_CAT),
            vt.reshape(bsz, MLA_KV_RANK, n, KEY_TILE).transpose(0, 2, 1, 3))


def _ffn(x, og, ol, wp):
    bsz, s, d = x.shape
    t = bsz * s
    h1, comb = _merge(x.reshape(t, d), og.reshape(t, -1), ol.reshape(t, -1), wp, tt=min(TOKEN_TILE, t))
    return _moe(h1, comb, wp, tt=min(MOE_TOKEN_TILE, t)).reshape(bsz, s, d)


def kernel(x_prompt, x_sample, cache_mla_latent, cache_mla_krope, state_gla, meta_tokens, ln_in_g, ln_in_b, w_in, w_gk2, b_gk, gla_norm_g, q_norm_g, kv_norm_g, w_uq, w_uk, w_uv, w_br_gla, w_br_mla, w_mg, b_mg, w_out, ln1_g, ln1_b, w_rg, b_rg, w_re, b_re, w_gate, w_up, w_down, ln2_g, ln2_b):
    bp, sp, d = x_prompt.shape
    bs, ss, _ = x_sample.shape
    past = cache_mla_latent.shape[2]
    wp = _prep_weights(ln_in_g, ln_in_b, w_in, w_gk2, b_gk, gla_norm_g, q_norm_g, kv_norm_g, w_uq, w_uk, w_uv,
                       w_br_gla, w_br_mla, w_mg, b_mg, w_out, ln1_g, ln1_b, w_rg, b_rg, w_re, b_re,
                       w_gate, w_up, w_down, ln2_g, ln2_b)

    cs_m, cst_m = _rope_tables(jnp.arange(-N_META, 0, dtype=jnp.int32))
    m = _proj(meta_tokens[None], cs_m, cst_m, wp, cl=N_META, tt=N_META)
    _, _, m_kt, m_v, m_ebl, m_ckv, m_kr, m_kcat, m_vt, _ = m
    zero_state = jnp.zeros((1, GLA_HEADS, GLA_DK, GLA_DV), F32)
    _, m_state = _gla(m[0], m[1], m_kt, m_v, m_ebl, zero_state, cl=N_META, ts=N_META)

    cs_p, cst_p = _rope_tables(jnp.arange(sp, dtype=jnp.int32))
    p_qd, p_ki, p_kt, p_v, p_ebl, p_ckv, p_kr, p_kcat, p_vt, p_qt = _proj(
        x_prompt, cs_p, cst_p, wp, cl=CHUNK, tt=TOKEN_TILE)
    p_o, p_state = _gla(p_qd, p_ki, p_kt, p_v, p_ebl, m_state, cl=CHUNK, ts=TOKEN_TILE)
    rep = lambda a, n: jnp.broadcast_to(a, (n,) + a.shape[1:])
    lat_p = jnp.concatenate([rep(m_ckv, bp), p_ckv], axis=1)
    kr_p = jnp.concatenate([rep(m_kr, bp), p_kr], axis=1)
    kcat_p, vt_p = _key_tiles(jnp.concatenate([rep(m_kcat, bp), p_kcat], axis=1),
                              jnp.concatenate([rep(m_vt, bp), p_vt], axis=2))
    p_ol = _attn(p_qt, kcat_p, vt_p, bq=TOKEN_TILE, n_valid=N_META + sp, causal=True)
    y_prompt = _ffn(x_prompt, p_o, p_ol, wp)

    ts_all = bs * ss
    cs_s, cst_s = _rope_tables(past + (jnp.arange(ts_all, dtype=jnp.int32) % ss))
    s_qd, s_ki, s_kt, s_v, s_ebl, s_ckv, s_kr, s_kcat, s_vt, s_qt = _proj(
        x_sample.reshape(1, ts_all, d), cs_s, cst_s, wp, cl=ss, tt=min(TOKEN_TILE, ts_all))
    per_stream = lambda a: a.reshape(bs, ss, a.shape[-1])
    s_o, s_state = _gla(per_stream(s_qd), per_stream(s_ki), per_stream(s_kt), per_stream(s_v),
                        s_ebl.reshape(bs, 1, 1, GLA_QK), state_gla[0].astype(F32), cl=ss, ts=ss)
    s_ckv, s_kr = per_stream(s_ckv), per_stream(s_kr)
    cache_kcat = jnp.concatenate([cache_mla_latent[0], cache_mla_krope[0]], axis=-1).astype(BF16)
    cache_vt = cache_mla_latent[0].astype(BF16).transpose(0, 2, 1)
    new_vt = s_vt.reshape(MLA_KV_RANK, bs, ss).transpose(1, 0, 2)
    kcat_s, vt_s = _key_tiles(jnp.concatenate([rep(m_kcat, bs), cache_kcat, per_stream(s_kcat)], axis=1),
                              jnp.concatenate([rep(m_vt, bs), cache_vt, new_vt], axis=2))
    qt = s_qt.transpose(0, 2, 3, 1, 4).reshape(MLA_HEADS, MLA_CAT, bs, ss)
    qt = qt.transpose(2, 1, 0, 3).reshape(bs, MLA_CAT, MLA_HEADS * ss // KEY_TILE, KEY_TILE)
    qt = qt.transpose(0, 2, 1, 3)[:, None]
    s_ol = _attn(qt, kcat_s, vt_s, bq=ss, n_valid=N_META + past + ss, causal=False)
    y_sample = _ffn(x_sample, s_o, s_ol, wp)

    return (y_prompt, y_sample, lat_p[None], kr_p[None], p_state[None].astype(state_gla.dtype),
            s_ckv[None], s_kr[None], s_state[None].astype(state_gla.dtype))
```

```python
import functools

import jax
import jax.numpy as jnp
from jax import lax
from jax.experimental import pallas as pl
from jax.experimental.pallas import tpu as pltpu

F32 = jnp.float32
BF16 = jnp.bfloat16

CHUNK = 64
N_META = 16
GLA_HEADS = 4
GLA_DK = 128
GLA_DV = 256
GLA_RANK = 16
GLA_TAU = 16.0
GLA_QK = GLA_HEADS * GLA_DK
GLA_V = GLA_HEADS * GLA_DV
GLA_SCALE = GLA_DK ** -0.5
MLA_HEADS = 16
MLA_Q_RANK = 384
MLA_KV_RANK = 128
MLA_NOPE = 64
MLA_ROPE = 32
MLA_HALF = MLA_ROPE // 2
MLA_DV = 64
MLA_QK_DIM = MLA_NOPE + MLA_ROPE
MLA_CAT = MLA_KV_RANK + MLA_ROPE
MLA_SCALE = MLA_QK_DIM ** -0.5
LOG2_E = 1.4426950408889634
Q_SCALE = MLA_SCALE * LOG2_E
V_ROWS = MLA_KV_RANK + 16
ROPE_THETA = 10000.0
N_GROUPS = 4
EXPERTS_PER_GROUP = 8
N_EXPERTS = N_GROUPS * EXPERTS_PER_GROUP
D_EXPERT = 256
LN_EPS = 1e-5
RMS_EPS = 1e-6
DEEPNORM_ALPHA = 2.0 ** 0.25

LANES = 128
MXU_DIM = 256
VMEM_LIMIT_BYTES = 56 * 1024 * 1024

TOKEN_TILE = 256
KEY_TILE = MXU_DIM
MOE_TOKEN_TILE = 1024
ROUTER_LANES = LANES
NEG_BIG = -1e30

NT_DIMS = (((1,), (1,)), ((), ()))
TN_DIMS = (((0,), (0,)), ((), ()))


def _cparams(*sem):
    return pltpu.CompilerParams(dimension_semantics=sem, vmem_limit_bytes=VMEM_LIMIT_BYTES)


def _const_spec(shape):
    nd = len(shape)
    return pl.BlockSpec(shape, lambda *_: (0,) * nd, pipeline_mode=pl.Buffered(1))


def _layer_norm(x, g, b):
    mu = jnp.mean(x, axis=-1, keepdims=True)
    xc = x - mu
    var = jnp.mean(xc * xc, axis=-1, keepdims=True)
    return xc * lax.rsqrt(var + LN_EPS) * g + b


def _dot(a, b):
    return jnp.dot(a, b, preferred_element_type=F32)


TOK_COLS = (GLA_QK, GLA_V, GLA_QK, MLA_KV_RANK, MLA_ROPE, GLA_RANK)
TOK_USED = sum(TOK_COLS)
TOK_PAD = -(-TOK_USED // MXU_DIM) * MXU_DIM
TR_ROWS = MLA_KV_RANK + MLA_Q_RANK


def _proj_kernel(x_ref, cs_ref, cst_ref, lng_ref, lnb_ref, wtok_ref, wtr_ref, wgk2_ref, bgk_ref,
                 kvg_ref, kvgt_ref, qngt_ref, wuqt_ref, wuk_ref,
                 qd_ref, ki_ref, kt_ref, v_ref, ebl_ref, ckv_ref, kr_ref, kcat_ref, vt_ref, qt_ref, *, cl):
    tt = x_ref.shape[1]
    n_chunks = tt // cl
    h = _layer_norm(x_ref[0], lng_ref[...], lnb_ref[...])
    hb = h.astype(BF16)

    z = _dot(hb, wtok_ref[...])
    o0 = 0
    k = z[:, o0:o0 + GLA_QK]; o0 += GLA_QK
    v = z[:, o0:o0 + GLA_V]; o0 += GLA_V
    q = z[:, o0:o0 + GLA_QK]; o0 += GLA_QK
    ckv_raw = z[:, o0:o0 + MLA_KV_RANK]; o0 += MLA_KV_RANK
    kr = z[:, o0:o0 + MLA_ROPE]; o0 += MLA_ROPE
    gklr = z[:, o0:o0 + GLA_RANK]

    gz = _dot(gklr.astype(BF16), wgk2_ref[...]) + bgk_ref[...]
    gk = (jnp.minimum(gz, 0.0) - jnp.log(1.0 + jnp.exp(-jnp.abs(gz)))) * (1.0 / GLA_TAU)
    row_in_chunk = lax.broadcasted_iota(jnp.int32, gk.shape, 0) & (cl - 1)
    b = gk
    shift = 1
    while shift < cl:
        b = b + jnp.where(row_in_chunk >= shift, pltpu.roll(b, shift, 0), 0.0)
        shift *= 2
    b3 = b.reshape(n_chunks, cl, GLA_QK)
    bl = b3[:, cl - 1:cl, :]
    qd_ref[0] = (q * GLA_SCALE * jnp.exp(b)).astype(BF16)
    ki_ref[0] = (k * jnp.exp(-b)).astype(BF16)
    kt_ref[0] = (k.reshape(n_chunks, cl, GLA_QK) * jnp.exp(bl - b3)).reshape(tt, GLA_QK).astype(BF16)
    v_ref[0] = v.astype(BF16)
    ebl_ref[0] = jnp.exp(bl)

    ckv = ckv_raw * lax.rsqrt(jnp.mean(ckv_raw * ckv_raw, axis=-1, keepdims=True) + RMS_EPS) * kvg_ref[...]
    cs = cs_ref[0]
    cos, sin = cs[:, :MLA_HALF], cs[:, MLA_HALF:]
    x1, x2 = kr[:, :MLA_HALF], kr[:, MLA_HALF:]
    kr_rot = jnp.concatenate([x1 * cos - x2 * sin, x2 * cos + x1 * sin], axis=-1)
    ckv_ref[0] = ckv
    kr_ref[0] = kr_rot
    kcat_ref[0, :, :MLA_KV_RANK] = ckv.astype(BF16)
    kcat_ref[0, :, MLA_KV_RANK:] = kr_rot.astype(BF16)

    zt = lax.dot_general(wtr_ref[...], hb, NT_DIMS, preferred_element_type=F32)
    ckvt = zt[:MLA_KV_RANK]
    ckvt = ckvt * lax.rsqrt(jnp.mean(ckvt * ckvt, axis=0, keepdims=True) + RMS_EPS) * kvgt_ref[...]
    vt_ref[0] = ckvt.astype(BF16)
    cqt = zt[MLA_KV_RANK:]
    cqt = cqt * lax.rsqrt(jnp.mean(cqt * cqt, axis=0, keepdims=True) + RMS_EPS) * qngt_ref[...]
    qmt = _dot(wuqt_ref[...], cqt.astype(BF16))
    n_nope = MLA_HEADS * MLA_NOPE
    n_half = MLA_HEADS * MLA_HALF
    cst = cst_ref[0]
    cos_t = jnp.concatenate([cst[:MLA_HALF]] * MLA_HEADS, axis=0)
    sin_t = jnp.concatenate([cst[MLA_HALF:]] * MLA_HEADS, axis=0)
    r1 = qmt[n_nope:n_nope + n_half]
    r2 = qmt[n_nope + n_half:]
    rot1 = ((r1 * cos_t - r2 * sin_t) * Q_SCALE).astype(BF16)
    rot2 = ((r2 * cos_t + r1 * sin_t) * Q_SCALE).astype(BF16)
    for hd in range(MLA_HEADS):
        nope = qmt[hd * MLA_NOPE:(hd + 1) * MLA_NOPE].astype(BF16)
        qlat = _dot(wuk_ref[hd], nope) * Q_SCALE
        qt_ref[0, 0, hd, :MLA_KV_RANK, :] = qlat.astype(BF16)
        qt_ref[0, 0, hd, MLA_KV_RANK:MLA_KV_RANK + MLA_HALF, :] = rot1[hd * MLA_HALF:(hd + 1) * MLA_HALF]
        qt_ref[0, 0, hd, MLA_KV_RANK + MLA_HALF:, :] = rot2[hd * MLA_HALF:(hd + 1) * MLA_HALF]


def _proj(x, cs, cst, wp, *, cl, tt):
    bsz, s, d = x.shape
    nt = s // tt
    n_chunks = tt // cl
    tok = lambda w: pl.BlockSpec((1, tt, w), lambda b, t: (b, t, 0))
    in_specs = [
        tok(d),
        pl.BlockSpec((1, tt, MLA_ROPE), lambda b, t: (0, t, 0)),
        pl.BlockSpec((1, MLA_ROPE, tt), lambda b, t: (0, 0, t)),
    ] + [_const_spec(wp[n].shape) for n in _PROJ_WEIGHTS]
    out_shape = [
        jax.ShapeDtypeStruct((bsz, s, GLA_QK), BF16),
        jax.ShapeDtypeStruct((bsz, s, GLA_QK), BF16),
        jax.ShapeDtypeStruct((bsz, s, GLA_QK), BF16),
        jax.ShapeDtypeStruct((bsz, s, GLA_V), BF16),
        jax.ShapeDtypeStruct((bsz, s // cl, 1, GLA_QK), F32),
        jax.ShapeDtypeStruct((bsz, s, MLA_KV_RANK), F32),
        jax.ShapeDtypeStruct((bsz, s, MLA_ROPE), F32),
        jax.ShapeDtypeStruct((bsz, s, MLA_CAT), BF16),
        jax.ShapeDtypeStruct((bsz, MLA_KV_RANK, s), BF16),
        jax.ShapeDtypeStruct((bsz, nt, MLA_HEADS, MLA_CAT, tt), BF16),
    ]
    out_specs = [
        tok(GLA_QK), tok(GLA_QK), tok(GLA_QK), tok(GLA_V),
        pl.BlockSpec((1, n_chunks, 1, GLA_QK), lambda b, t: (b, t, 0, 0)),
        tok(MLA_KV_RANK), tok(MLA_ROPE), tok(MLA_CAT),
        pl.BlockSpec((1, MLA_KV_RANK, tt), lambda b, t: (b, 0, t)),
        pl.BlockSpec((1, 1, MLA_HEADS, MLA_CAT, tt), lambda b, t: (b, t, 0, 0, 0)),
    ]
    return pl.pallas_call(
        functools.partial(_proj_kernel, cl=cl),
        grid=(bsz, nt),
        in_specs=in_specs,
        out_specs=out_specs,
        out_shape=out_shape,
        compiler_params=_cparams("parallel", "parallel"),
        name="proj",
    )(x, cs, cst, *[wp[n] for n in _PROJ_WEIGHTS])


_PROJ_WEIGHTS = ("ln_g", "ln_b", "w_tok", "w_tr", "w_gk2", "b_gk", "kv_g", "kv_gt", "q_gt", "w_uqt", "w_uk")


def _gla_kernel(qd_ref, ki_ref, kt_ref, v_ref, ebl_ref, s0_ref, o_ref, sfin_ref, st_ref, *, cl):
    t = pl.program_id(2)
    ts = qd_ref.shape[1]

    @pl.when(t == 0)
    def _():
        st_ref[...] = s0_ref[0, 0].T

    qd, ki, kt, v = qd_ref[0], ki_ref[0], kt_ref[0], v_ref[0]
    a = lax.dot_general(qd, ki, NT_DIMS, preferred_element_type=F32)
    row = lax.broadcasted_iota(jnp.int32, (ts, ts), 0)
    col = lax.broadcasted_iota(jnp.int32, (ts, ts), 1)
    keep = (row >= col) & ((row & -cl) == (col & -cl))
    a = jnp.where(keep, a, 0.0).astype(BF16)
    o_ref[0] = _dot(a, v)
    for c in range(ts // cl):
        rows = slice(c * cl, (c + 1) * cl)
        st = st_ref[...]
        o_ref[0, rows, :] += lax.dot_general(qd[rows], st.astype(BF16), NT_DIMS, preferred_element_type=F32)
        st_ref[...] = st * ebl_ref[0, c] + lax.dot_general(v[rows], kt[rows], TN_DIMS, preferred_element_type=F32)

    @pl.when(t == pl.num_programs(2) - 1)
    def _():
        sfin_ref[0, 0] = st_ref[...].T


def _gla(qd, ki, kt, v, ebl, s0, *, cl, ts):
    bsz, s, _ = qd.shape
    n_chunks = ts // cl
    s0_b = s0.shape[0]
    qk_spec = pl.BlockSpec((1, ts, GLA_DK), lambda b, h, t: (b, t, h))
    v_spec = pl.BlockSpec((1, ts, GLA_DV), lambda b, h, t: (b, t, h))
    st_spec = pl.BlockSpec((1, 1, GLA_DK, GLA_DV), lambda b, h, t: (b, h, 0, 0))
    s0_spec = st_spec if s0_b == bsz else pl.BlockSpec((1, 1, GLA_DK, GLA_DV), lambda b, h, t: (0, h, 0, 0))
    return pl.pallas_call(
        functools.partial(_gla_kernel, cl=cl),
        grid=(bsz, GLA_HEADS, s // ts),
        in_specs=[qk_spec, qk_spec, qk_spec, v_spec,
                  pl.BlockSpec((1, n_chunks, 1, GLA_DK), lambda b, h, t: (b, t, 0, h)),
                  s0_spec],
        out_specs=[v_spec, st_spec],
        out_shape=[jax.ShapeDtypeStruct((bsz, s, GLA_V), F32),
                   jax.ShapeDtypeStruct((bsz, GLA_HEADS, GLA_DK, GLA_DV), F32)],
        scratch_shapes=[pltpu.VMEM((GLA_DV, GLA_DK), F32)],
        compiler_params=_cparams("parallel", "parallel", "arbitrary"),
        name="gla",
    )(qd, ki, kt, v, ebl, s0)


def _attn_kernel(qt_ref, kcat_ref, vt_ref, km_ref, vm_ref, o_ref, s_ref, p_ref, m_ref, a_ref, acc_ref,
                 *, bq, causal, n_valid_last):
    i = pl.program_id(1)
    n_cc = qt_ref.shape[2]
    n_kt = kcat_ref.shape[1]
    n_int = i if causal else n_kt - 1

    def scores(j):
        kt = kcat_ref[0, j]
        for c in range(n_cc):
            s_ref[c] = _dot(kt, qt_ref[0, 0, c])

    def softmax(c, mask):
        s = s_ref[c]
        if mask is not None:
            s = jnp.where(mask, s, NEG_BIG)
        m_prev = m_ref[c]
        m_new = jnp.maximum(m_prev, jnp.max(s, axis=0, keepdims=True))
        a_ref[c] = jnp.exp2(m_prev - m_new)
        m_ref[c] = m_new
        p_ref[c] = jnp.exp2(s - m_new).astype(BF16)

    def values(c, vt):
        acc_ref[c] = a_ref[c] * acc_ref[c] + _dot(vt, p_ref[c])

    km, vm = km_ref[...], vm_ref[...]
    sm = [_dot(km, qt_ref[0, 0, c]) for c in range(n_cc)]
    pm = []
    for c in range(n_cc):
        m0 = jnp.max(sm[c], axis=0, keepdims=True)
        m_ref[c] = m0
        pm.append(jnp.exp2(sm[c] - m0).astype(BF16))
    for c in range(n_cc):
        acc_ref[c] = _dot(vm, pm[c])
    a_ref[...] = jnp.ones(a_ref.shape, F32)
    p_ref[...] = jnp.zeros(p_ref.shape, BF16)
    scores(0)

    def key_step(j, carry):
        kt_next = kcat_ref[0, j + 1]
        vt_prev = vt_ref[0, jnp.maximum(j - 1, 0)]
        for c in range(n_cc):
            values(c, vt_prev)
            softmax(c, None)
            s_ref[c] = _dot(kt_next, qt_ref[0, 0, c])
        return carry

    lax.fori_loop(0, n_int, key_step, 0)

    row = lax.broadcasted_iota(jnp.int32, (KEY_TILE, KEY_TILE), 0)
    if causal:
        col = lax.broadcasted_iota(jnp.int32, (KEY_TILE, KEY_TILE), 1)
        mask = (row >> 6) <= (col >> 6)
    else:
        mask = row < n_valid_last
    vt_prev = vt_ref[0, jnp.maximum(n_int - 1, 0)]
    for c in range(n_cc):
        values(c, vt_prev)
    for c in range(n_cc):
        softmax(c, mask)
    vt_last = vt_ref[0, n_int]
    for c in range(n_cc):
        values(c, vt_last)

    groups_per_cc = KEY_TILE // LANES
    for c in range(n_cc):
        acc = acc_ref[c]
        o_t = acc[:MLA_KV_RANK] * (1.0 / acc[MLA_KV_RANK:MLA_KV_RANK + 1])
        for g in range(groups_per_cc):
            blk = o_t[:, g * LANES:(g + 1) * LANES].T.astype(BF16)
            col0 = c * KEY_TILE + g * LANES
            if bq >= LANES:
                hd, q0 = col0 // bq, col0 % bq
                o_ref[0, q0:q0 + LANES, hd * MLA_KV_RANK:(hd + 1) * MLA_KV_RANK] = blk
            else:
                for hl in range(LANES // bq):
                    hd = col0 // bq + hl
                    o_ref[0, :, hd * MLA_KV_RANK:(hd + 1) * MLA_KV_RANK] = blk[hl * bq:(hl + 1) * bq]


def _attn(qt, kcat, vt, km, vm, *, bq, causal, n_valid_last):
    bsz, nq, n_cc = qt.shape[:3]
    n_kt = kcat.shape[1]
    assert not causal or bq == KEY_TILE
    return pl.pallas_call(
        functools.partial(_attn_kernel, bq=bq, causal=causal, n_valid_last=n_valid_last),
        grid=(bsz, nq),
        in_specs=[pl.BlockSpec((1, 1, n_cc, MLA_CAT, KEY_TILE), lambda b, i: (b, i, 0, 0, 0)),
                  pl.BlockSpec((1, n_kt, KEY_TILE, MLA_CAT), lambda b, i: (b, 0, 0, 0)),
                  pl.BlockSpec((1, n_kt, V_ROWS, KEY_TILE), lambda b, i: (b, 0, 0, 0)),
                  _const_spec(km.shape), _const_spec(vm.shape)],
        out_specs=pl.BlockSpec((1, bq, MLA_HEADS * MLA_KV_RANK), lambda b, i: (b, i, 0)),
        out_shape=jax.ShapeDtypeStruct((bsz, nq * bq, MLA_HEADS * MLA_KV_RANK), BF16),
        scratch_shapes=[pltpu.VMEM((n_cc, KEY_TILE, KEY_TILE), F32),
                        pltpu.VMEM((n_cc, KEY_TILE, KEY_TILE), BF16),
                        pltpu.VMEM((n_cc, 1, KEY_TILE), F32),
                        pltpu.VMEM((n_cc, 1, KEY_TILE), F32),
                        pltpu.VMEM((n_cc, V_ROWS, KEY_TILE), F32)],
        compiler_params=_cparams("parallel", "arbitrary"),
        name="attn",
    )(qt, kcat, vt, km, vm)


_MERGE_WEIGHTS = ("ln_g", "ln_b", "w_g", "gla_g", "w_br_gla", "w_uv_bd", "w_br_mla", "w_mg", "b_mg", "w_out",
                  "ln1_g", "ln1_b", "w_r_hi", "w_r_lo", "b_r")


def _merge_kernel(x_ref, og_ref, ol_ref, lng_ref, lnb_ref, wg_ref, glag_ref, wbg_ref, wuv_ref, wbm_ref,
                  wmg_ref, bmg_ref, wout_ref, l1g_ref, l1b_ref, wrh_ref, wrl_ref, br_ref,
                  h1_ref, comb_ref):
    d = x_ref.shape[1]
    h = _layer_norm(x_ref[...], lng_ref[...], lnb_ref[...])
    hb = h.astype(BF16)
    g_out = _dot(hb, wg_ref[...])
    og = og_ref[...]
    parts = []
    for hd in range(GLA_HEADS):
        cols = slice(hd * GLA_DV, (hd + 1) * GLA_DV)
        o_h = og[:, cols]
        g_h = g_out[:, cols]
        o_n = o_h * lax.rsqrt(jnp.mean(o_h * o_h, axis=-1, keepdims=True) + RMS_EPS) * glag_ref[...]
        parts.append(o_n * (g_h * jax.nn.sigmoid(g_h)))
    y_a = _dot(jnp.concatenate(parts, axis=-1).astype(BF16), wbg_ref[...])
    y_heads = _dot(ol_ref[...], wuv_ref[...])
    y_b = _dot(y_heads.astype(BF16), wbm_ref[...])
    gates = jax.nn.sigmoid(_dot(hb, wmg_ref[...]) + bmg_ref[...])
    mix_in = gates[:, :d] * y_a + gates[:, d:] * y_b
    mix = _dot(mix_in.astype(BF16), wout_ref[...])
    h1 = _layer_norm(DEEPNORM_ALPHA * h + mix, l1g_ref[...], l1b_ref[...])
    h1_ref[...] = h1

    h1_hi = h1.astype(BF16)
    h1_lo = (h1 - h1_hi.astype(F32)).astype(BF16)
    logits = (_dot(h1_hi, wrh_ref[...]) + (_dot(h1_hi, wrl_ref[...]) + _dot(h1_lo, wrh_ref[...]))) + br_ref[...]
    lane = lax.broadcasted_iota(jnp.int32, logits.shape, 1)
    is_grp = lane < N_GROUPS
    gl = jnp.where(is_grp, logits, NEG_BIG)
    g_max = jnp.max(gl, axis=-1, keepdims=True)
    g_sel = jnp.min(jnp.where(gl == g_max, lane, ROUTER_LANES), axis=-1, keepdims=True)
    p_grp = 1.0 / jnp.sum(jnp.where(is_grp, jnp.exp(gl - g_max), 0.0), axis=-1, keepdims=True)
    e_lo = N_GROUPS + g_sel * EXPERTS_PER_GROUP
    in_grp = (lane >= e_lo) & (lane < e_lo + EXPERTS_PER_GROUP)
    el = jnp.where(in_grp, logits, NEG_BIG)
    v1 = jnp.max(el, axis=-1, keepdims=True)
    i1 = jnp.min(jnp.where(el == v1, lane, ROUTER_LANES), axis=-1, keepdims=True)
    el2 = jnp.where(lane == i1, NEG_BIG, el)
    v2 = jnp.max(el2, axis=-1, keepdims=True)
    i2 = jnp.min(jnp.where(el2 == v2, lane, ROUTER_LANES), axis=-1, keepdims=True)
    e2 = jnp.exp(v2 - v1)
    w1 = p_grp / (1.0 + e2)
    w2 = p_grp * e2 / (1.0 + e2)
    comb_ref[...] = jnp.where(lane == i1 - N_GROUPS, w1, 0.0) + jnp.where(lane == i2 - N_GROUPS, w2, 0.0)


def _merge(x2, og2, ol2, wp, *, tt):
    t, d = x2.shape
    row = lambda w: pl.BlockSpec((tt, w), lambda i: (i, 0))
    return pl.pallas_call(
        _merge_kernel,
        grid=(t // tt,),
        in_specs=[row(d), row(GLA_V), row(MLA_HEADS * MLA_KV_RANK)] + [_const_spec(wp[n].shape) for n in _MERGE_WEIGHTS],
        out_specs=[row(d), row(ROUTER_LANES)],
        out_shape=[jax.ShapeDtypeStruct((t, d), F32), jax.ShapeDtypeStruct((t, ROUTER_LANES), F32)],
        compiler_params=_cparams("parallel"),
        name="merge",
    )(x2, og2, ol2, *[wp[n] for n in _MERGE_WEIGHTS])


def _moe_kernel(h1_ref, comb_ref, wg_ref, wu_ref, wd_ref, l2g_ref, l2b_ref, y_ref, xb_ref):
    e = pl.program_id(1)

    @pl.when(e == 0)
    def _():
        xb_ref[...] = h1_ref[...].astype(BF16)
        y_ref[...] = jnp.zeros(y_ref.shape, F32)

    xb = xb_ref[...]
    gate = _dot(xb, wg_ref[0])
    up = _dot(xb, wu_ref[0])
    hid = (gate * jax.nn.sigmoid(gate)) * up
    lane = lax.broadcasted_iota(jnp.int32, comb_ref.shape, 1)
    w_e = jnp.sum(jnp.where(lane == e, comb_ref[...], 0.0), axis=-1, keepdims=True)
    y_ref[...] += w_e * _dot(hid.astype(BF16), wd_ref[0])

    @pl.when(e == pl.num_programs(1) - 1)
    def _():
        y_ref[...] = _layer_norm(DEEPNORM_ALPHA * h1_ref[...] + y_ref[...], l2g_ref[...], l2b_ref[...])


def _moe(h1, comb, wp, *, tt):
    t, d = h1.shape
    n_e = wp["w_gate"].shape[0]
    return pl.pallas_call(
        _moe_kernel,
        grid=(t // tt, n_e),
        in_specs=[pl.BlockSpec((tt, d), lambda i, e: (i, 0)),
                  pl.BlockSpec((tt, ROUTER_LANES), lambda i, e: (i, 0)),
                  pl.BlockSpec((1, d, D_EXPERT), lambda i, e: (e, 0, 0)),
                  pl.BlockSpec((1, d, D_EXPERT), lambda i, e: (e, 0, 0)),
                  pl.BlockSpec((1, D_EXPERT, d), lambda i, e: (e, 0, 0)),
                  _const_spec((1, d)), _const_spec((1, d))],
        out_specs=pl.BlockSpec((tt, d), lambda i, e: (i, 0)),
        out_shape=jax.ShapeDtypeStruct((t, d), F32),
        scratch_shapes=[pltpu.VMEM((tt, d), BF16)],
        compiler_params=_cparams("parallel", "arbitrary"),
        name="moe",
    )(h1, comb, wp["w_gate"], wp["w_up"], wp["w_down"], wp["ln2_g"], wp["ln2_b"])


def _rope_tables(pos):
    inv = ROPE_THETA ** (-jnp.arange(0, MLA_ROPE, 2, dtype=F32) / MLA_ROPE)
    ang = pos.astype(F32)[:, None] * inv[None, :]
    cs = jnp.concatenate([jnp.cos(ang), jnp.sin(ang)], axis=-1)
    return cs[None], cs.T[None]


def _prep_weights(ln_in_g, ln_in_b, w_in, w_gk2, b_gk, gla_norm_g, q_norm_g, kv_norm_g, w_uq, w_uk, w_uv,
                  w_br_gla, w_br_mla, w_mg, b_mg, w_out, ln1_g, ln1_b, w_rg, b_rg, w_re, b_re,
                  w_gate, w_up, w_down, ln2_g, ln2_b):
    d = w_in.shape[1]
    w = w_in[0]
    c0 = 0
    wk = w[:, c0:c0 + GLA_QK]; c0 += GLA_QK
    wv = w[:, c0:c0 + GLA_V]; c0 += GLA_V
    wgr = w[:, c0:c0 + GLA_RANK]; c0 += GLA_RANK
    wckv = w[:, c0:c0 + MLA_KV_RANK]; c0 += MLA_KV_RANK
    wkr = w[:, c0:c0 + MLA_ROPE]; c0 += MLA_ROPE
    wq = w[:, c0:c0 + GLA_QK]; c0 += GLA_QK
    wg = w[:, c0:c0 + GLA_V]; c0 += GLA_V
    wcq = w[:, c0:c0 + MLA_Q_RANK]
    w_tok = jnp.concatenate([wk, wv, wq, wckv, wkr, wgr, jnp.zeros((d, TOK_PAD - TOK_USED), F32)], axis=1)
    w_tr = jnp.concatenate([wckv, wcq], axis=1).T
    uq = w_uq[0].reshape(MLA_Q_RANK, MLA_HEADS, MLA_QK_DIM)
    uq_perm = jnp.concatenate([
        uq[:, :, :MLA_NOPE].reshape(MLA_Q_RANK, -1),
        uq[:, :, MLA_NOPE:MLA_NOPE + MLA_HALF].reshape(MLA_Q_RANK, -1),
        uq[:, :, MLA_NOPE + MLA_HALF:].reshape(MLA_Q_RANK, -1)], axis=1)
    uv = w_uv[0]
    eye = jnp.eye(MLA_HEADS, dtype=F32)
    w_uv_bd = (uv.transpose(1, 0, 2)[:, :, None, :] * eye[:, None, :, None]).reshape(
        MLA_HEADS * MLA_KV_RANK, MLA_HEADS * MLA_DV)
    w_r = jnp.concatenate([w_rg[0], w_re[0].transpose(1, 0, 2).reshape(d, N_EXPERTS),
                           jnp.zeros((d, ROUTER_LANES - N_GROUPS - N_EXPERTS), F32)], axis=1)
    w_r_hi = w_r.astype(BF16)
    b_r = jnp.concatenate([b_rg[0], b_re[0].reshape(-1), jnp.zeros((ROUTER_LANES - N_GROUPS - N_EXPERTS,), F32)])
    row = lambda a: a.reshape(1, -1)
    return {
        "ln_g": row(ln_in_g), "ln_b": row(ln_in_b),
        "w_tok": w_tok.astype(BF16), "w_tr": w_tr.astype(BF16),
        "w_gk2": w_gk2[0].astype(BF16), "b_gk": row(b_gk[0]),
        "kv_g": row(kv_norm_g[0]), "kv_gt": kv_norm_g[0].reshape(-1, 1), "q_gt": q_norm_g[0].reshape(-1, 1),
        "w_uqt": uq_perm.T.astype(BF16), "w_uk": w_uk[0].transpose(1, 0, 2).astype(BF16),
        "w_g": wg.astype(BF16), "gla_g": row(gla_norm_g[0]),
        "w_br_gla": w_br_gla[0].astype(BF16), "w_uv_bd": w_uv_bd.astype(BF16), "w_br_mla": w_br_mla[0].astype(BF16),
        "w_mg": w_mg[0].astype(BF16), "b_mg": row(b_mg[0]), "w_out": w_out[0].astype(BF16),
        "ln1_g": row(ln1_g[0]), "ln1_b": row(ln1_b[0]),
        "w_r_hi": w_r_hi, "w_r_lo": (w_r - w_r_hi.astype(F32)).astype(BF16), "b_r": row(b_r),
        "w_gate": w_gate[0].astype(BF16), "w_up": w_up[0].astype(BF16), "w_down": w_down[0].astype(BF16),
        "ln2_g": row(ln2_g[0]), "ln2_b": row(ln2_b[0]),
    }


def _value_rows(vt):
    lead, length = vt.shape[:-2], vt.shape[-1]
    return jnp.concatenate([vt, jnp.ones(lead + (1, length), vt.dtype),
                            jnp.zeros(lead + (V_ROWS - MLA_KV_RANK - 1, length), vt.dtype)], axis=-2)


def _key_tiles(kcat, vt):
    bsz, length, _ = kcat.shape
    n = -(-length // KEY_TILE)
    pad = n * KEY_TILE - length
    kcat = jnp.pad(kcat, ((0, 0), (0, pad), (0, 0)))
    vt = jnp.pad(_value_rows(vt), ((0, 0), (0, 0), (0, pad)))
    return (kcat.reshape(bsz, n, KEY_TILE, MLA_CAT),
            vt.reshape(bsz, V_ROWS, n, KEY_TILE).transpose(0, 2, 1, 3))


def _ffn(x, og, ol, wp):
    bsz, s, d = x.shape
    t = bsz * s
    h1, comb = _merge(x.reshape(t, d), og.reshape(t, -1), ol.reshape(t, -1), wp, tt=min(TOKEN_TILE, t))
    return _moe(h1, comb, wp, tt=min(MOE_TOKEN_TILE, t)).reshape(bsz, s, d)


def kernel(x_prompt, x_sample, cache_mla_latent, cache_mla_krope, state_gla, meta_tokens, ln_in_g, ln_in_b, w_in, w_gk2, b_gk, gla_norm_g, q_norm_g, kv_norm_g, w_uq, w_uk, w_uv, w_br_gla, w_br_mla, w_mg, b_mg, w_out, ln1_g, ln1_b, w_rg, b_rg, w_re, b_re, w_gate, w_up, w_down, ln2_g, ln2_b):
    bp, sp, d = x_prompt.shape
    bs, ss, _ = x_sample.shape
    past = cache_mla_latent.shape[2]
    wp = _prep_weights(ln_in_g, ln_in_b, w_in, w_gk2, b_gk, gla_norm_g, q_norm_g, kv_norm_g, w_uq, w_uk, w_uv,
                       w_br_gla, w_br_mla, w_mg, b_mg, w_out, ln1_g, ln1_b, w_rg, b_rg, w_re, b_re,
                       w_gate, w_up, w_down, ln2_g, ln2_b)

    cs_m, cst_m = _rope_tables(jnp.arange(-N_META, 0, dtype=jnp.int32))
    m = _proj(meta_tokens[None], cs_m, cst_m, wp, cl=N_META, tt=N_META)
    _, _, m_kt, m_v, m_ebl, m_ckv, m_kr, m_kcat, m_vt, _ = m
    zero_state = jnp.zeros((1, GLA_HEADS, GLA_DK, GLA_DV), F32)
    _, m_state = _gla(m[0], m[1], m_kt, m_v, m_ebl, zero_state, cl=N_META, ts=N_META)

    cs_p, cst_p = _rope_tables(jnp.arange(sp, dtype=jnp.int32))
    p_qd, p_ki, p_kt, p_v, p_ebl, p_ckv, p_kr, p_kcat, p_vt, p_qt = _proj(
        x_prompt, cs_p, cst_p, wp, cl=CHUNK, tt=TOKEN_TILE)
    p_o, p_state = _gla(p_qd, p_ki, p_kt, p_v, p_ebl, m_state, cl=CHUNK, ts=TOKEN_TILE)
    rep = lambda a, n: jnp.broadcast_to(a, (n,) + a.shape[1:])
    lat_p = jnp.concatenate([rep(m_ckv, bp), p_ckv], axis=1)
    kr_p = jnp.concatenate([rep(m_kr, bp), p_kr], axis=1)
    kcat_p, vt_p = _key_tiles(p_kcat, p_vt)
    km, vm = m_kcat[0], _value_rows(m_vt[0])
    p_ol = _attn(p_qt, kcat_p, vt_p, km, vm, bq=TOKEN_TILE, causal=True, n_valid_last=KEY_TILE)
    y_prompt = _ffn(x_prompt, p_o, p_ol, wp)

    ts_all = bs * ss
    cs_s, cst_s = _rope_tables(past + (jnp.arange(ts_all, dtype=jnp.int32) % ss))
    s_qd, s_ki, s_kt, s_v, s_ebl, s_ckv, s_kr, s_kcat, s_vt, s_qt = _proj(
        x_sample.reshape(1, ts_all, d), cs_s, cst_s, wp, cl=ss, tt=min(TOKEN_TILE, ts_all))
    per_stream = lambda a: a.reshape(bs, ss, a.shape[-1])
    s_o, s_state = _gla(per_stream(s_qd), per_stream(s_ki), per_stream(s_kt), per_stream(s_v),
                        s_ebl.reshape(bs, 1, 1, GLA_QK), state_gla[0].astype(F32), cl=ss, ts=ss)
    s_ckv, s_kr = per_stream(s_ckv), per_stream(s_kr)
    cache_kcat = jnp.concatenate([cache_mla_latent[0], cache_mla_krope[0]], axis=-1).astype(BF16)
    cache_vt = cache_mla_latent[0].astype(BF16).transpose(0, 2, 1)
    new_vt = s_vt.reshape(MLA_KV_RANK, bs, ss).transpose(1, 0, 2)
    kcat_s, vt_s = _key_tiles(jnp.concatenate([cache_kcat, per_stream(s_kcat)], axis=1),
                              jnp.concatenate([cache_vt, new_vt], axis=2))
    qt = s_qt.transpose(0, 2, 3, 1, 4).reshape(MLA_HEADS, MLA_CAT, bs, ss)
    qt = qt.transpose(2, 1, 0, 3).reshape(bs, MLA_CAT, MLA_HEADS * ss // KEY_TILE, KEY_TILE)
    qt = qt.transpose(0, 2, 1, 3)[:, None]
    s_ol = _attn(qt, kcat_s, vt_s, km, vm, bq=ss, causal=False, n_valid_last=(past + ss - 1) % KEY_TILE + 1)
    y_sample = _ffn(x_sample, s_o, s_ol, wp)

    return (y_prompt, y_sample, lat_p[None], kr_p[None], p_state[None].astype(state_gla.dtype),
            s_ckv[None], s_kr[None], s_state[None].astype(state_gla.dtype))
```

```python
import functools

import jax
import jax.numpy as jnp
from jax import lax
from jax.experimental import pallas as pl
from jax.experimental.pallas import tpu as pltpu

F32 = jnp.float32
BF16 = jnp.bfloat16

CHUNK = 64
N_META = 16
GLA_HEADS = 4
GLA_DK = 128
GLA_DV = 256
GLA_RANK = 16
GLA_TAU = 16.0
GLA_QK = GLA_HEADS * GLA_DK
GLA_V = GLA_HEADS * GLA_DV
GLA_SCALE = GLA_DK ** -0.5
MLA_HEADS = 16
MLA_Q_RANK = 384
MLA_KV_RANK = 128
MLA_NOPE = 64
MLA_ROPE = 32
MLA_HALF = MLA_ROPE // 2
MLA_DV = 64
MLA_QK_DIM = MLA_NOPE + MLA_ROPE
MLA_CAT = MLA_KV_RANK + MLA_ROPE
MLA_SCALE = MLA_QK_DIM ** -0.5
LOG2_E = 1.4426950408889634
Q_SCALE = MLA_SCALE * LOG2_E
V_ROWS = MLA_KV_RANK + 16
ROPE_THETA = 10000.0
N_GROUPS = 4
EXPERTS_PER_GROUP = 8
N_EXPERTS = N_GROUPS * EXPERTS_PER_GROUP
D_EXPERT = 256
LN_EPS = 1e-5
RMS_EPS = 1e-6
DEEPNORM_ALPHA = 2.0 ** 0.25

LANES = 128
MXU_DIM = 256
VMEM_LIMIT_BYTES = 56 * 1024 * 1024

TOKEN_TILE = 256
KEY_TILE = MXU_DIM
MOE_TOKEN_TILE = 256
EXPERT_ROWS = 256
ROW_TILE = 8
ROUTER_LANES = LANES
RT_E, RT_W, RT_RANK = 0, 2, 4
NEG_BIG = -1e30

NT_DIMS = (((1,), (1,)), ((), ()))
TN_DIMS = (((0,), (0,)), ((), ()))


def _cparams(*sem):
    return pltpu.CompilerParams(dimension_semantics=sem, vmem_limit_bytes=VMEM_LIMIT_BYTES)


def _const_spec(shape):
    nd = len(shape)
    return pl.BlockSpec(shape, lambda *_: (0,) * nd, pipeline_mode=pl.Buffered(1))


def _layer_norm(x, g, b):
    mu = jnp.mean(x, axis=-1, keepdims=True)
    xc = x - mu
    var = jnp.mean(xc * xc, axis=-1, keepdims=True)
    return xc * lax.rsqrt(var + LN_EPS) * g + b


def _dot(a, b):
    return jnp.dot(a, b, preferred_element_type=F32)


TOK_COLS = (GLA_QK, GLA_V, GLA_QK, MLA_KV_RANK, MLA_ROPE, GLA_RANK)
TOK_USED = sum(TOK_COLS)
TOK_PAD = -(-TOK_USED // MXU_DIM) * MXU_DIM
TR_ROWS = MLA_KV_RANK + MLA_Q_RANK


def _proj_kernel(x_ref, cs_ref, cst_ref, lng_ref, lnb_ref, wtok_ref, wtr_ref, wgk2_ref, bgk_ref,
                 kvg_ref, kvgt_ref, qngt_ref, wuqt_ref, wuk_ref,
                 qd_ref, ki_ref, kt_ref, v_ref, ebl_ref, ckv_ref, kr_ref, kcat_ref, vt_ref, qt_ref, *, cl):
    tt = x_ref.shape[1]
    n_chunks = tt // cl
    h = _layer_norm(x_ref[0], lng_ref[...], lnb_ref[...])
    hb = h.astype(BF16)

    z = _dot(hb, wtok_ref[...])
    o0 = 0
    k = z[:, o0:o0 + GLA_QK]; o0 += GLA_QK
    v = z[:, o0:o0 + GLA_V]; o0 += GLA_V
    q = z[:, o0:o0 + GLA_QK]; o0 += GLA_QK
    ckv_raw = z[:, o0:o0 + MLA_KV_RANK]; o0 += MLA_KV_RANK
    kr = z[:, o0:o0 + MLA_ROPE]; o0 += MLA_ROPE
    gklr = z[:, o0:o0 + GLA_RANK]

    gz = _dot(gklr.astype(BF16), wgk2_ref[...]) + bgk_ref[...]
    gk = (jnp.minimum(gz, 0.0) - jnp.log(1.0 + jnp.exp(-jnp.abs(gz)))) * (1.0 / GLA_TAU)
    row_in_chunk = lax.broadcasted_iota(jnp.int32, gk.shape, 0) & (cl - 1)
    b = gk
    shift = 1
    while shift < cl:
        b = b + jnp.where(row_in_chunk >= shift, pltpu.roll(b, shift, 0), 0.0)
        shift *= 2
    b3 = b.reshape(n_chunks, cl, GLA_QK)
    bl = b3[:, cl - 1:cl, :]
    qd_ref[0] = (q * GLA_SCALE * jnp.exp(b)).astype(BF16)
    ki_ref[0] = (k * jnp.exp(-b)).astype(BF16)
    kt_ref[0] = (k.reshape(n_chunks, cl, GLA_QK) * jnp.exp(bl - b3)).reshape(tt, GLA_QK).astype(BF16)
    v_ref[0] = v.astype(BF16)
    ebl_ref[0] = jnp.exp(bl)

    ckv = ckv_raw * lax.rsqrt(jnp.mean(ckv_raw * ckv_raw, axis=-1, keepdims=True) + RMS_EPS) * kvg_ref[...]
    cs = cs_ref[0]
    cos, sin = cs[:, :MLA_HALF], cs[:, MLA_HALF:]
    x1, x2 = kr[:, :MLA_HALF], kr[:, MLA_HALF:]
    kr_rot = jnp.concatenate([x1 * cos - x2 * sin, x2 * cos + x1 * sin], axis=-1)
    ckv_ref[0] = ckv
    kr_ref[0] = kr_rot
    kcat_ref[0, :, :MLA_KV_RANK] = ckv.astype(BF16)
    kcat_ref[0, :, MLA_KV_RANK:] = kr_rot.astype(BF16)

    zt = lax.dot_general(wtr_ref[...], hb, NT_DIMS, preferred_element_type=F32)
    ckvt = zt[:MLA_KV_RANK]
    ckvt = ckvt * lax.rsqrt(jnp.mean(ckvt * ckvt, axis=0, keepdims=True) + RMS_EPS) * kvgt_ref[...]
    vt_ref[0] = ckvt.astype(BF16)
    cqt = zt[MLA_KV_RANK:]
    cqt = cqt * lax.rsqrt(jnp.mean(cqt * cqt, axis=0, keepdims=True) + RMS_EPS) * qngt_ref[...]
    qmt = _dot(wuqt_ref[...], cqt.astype(BF16))
    n_nope = MLA_HEADS * MLA_NOPE
    n_half = MLA_HEADS * MLA_HALF
    cst = cst_ref[0]
    cos_t = jnp.concatenate([cst[:MLA_HALF]] * MLA_HEADS, axis=0)
    sin_t = jnp.concatenate([cst[MLA_HALF:]] * MLA_HEADS, axis=0)
    r1 = qmt[n_nope:n_nope + n_half]
    r2 = qmt[n_nope + n_half:]
    rot1 = ((r1 * cos_t - r2 * sin_t) * Q_SCALE).astype(BF16)
    rot2 = ((r2 * cos_t + r1 * sin_t) * Q_SCALE).astype(BF16)
    for hd in range(MLA_HEADS):
        nope = qmt[hd * MLA_NOPE:(hd + 1) * MLA_NOPE].astype(BF16)
        qlat = _dot(wuk_ref[hd], nope) * Q_SCALE
        qt_ref[0, 0, hd, :MLA_KV_RANK, :] = qlat.astype(BF16)
        qt_ref[0, 0, hd, MLA_KV_RANK:MLA_KV_RANK + MLA_HALF, :] = rot1[hd * MLA_HALF:(hd + 1) * MLA_HALF]
        qt_ref[0, 0, hd, MLA_KV_RANK + MLA_HALF:, :] = rot2[hd * MLA_HALF:(hd + 1) * MLA_HALF]


def _proj(x, cs, cst, wp, *, cl, tt):
    bsz, s, d = x.shape
    nt = s // tt
    n_chunks = tt // cl
    tok = lambda w: pl.BlockSpec((1, tt, w), lambda b, t: (b, t, 0))
    in_specs = [
        tok(d),
        pl.BlockSpec((1, tt, MLA_ROPE), lambda b, t: (0, t, 0)),
        pl.BlockSpec((1, MLA_ROPE, tt), lambda b, t: (0, 0, t)),
    ] + [_const_spec(wp[n].shape) for n in _PROJ_WEIGHTS]
    out_shape = [
        jax.ShapeDtypeStruct((bsz, s, GLA_QK), BF16),
        jax.ShapeDtypeStruct((bsz, s, GLA_QK), BF16),
        jax.ShapeDtypeStruct((bsz, s, GLA_QK), BF16),
        jax.ShapeDtypeStruct((bsz, s, GLA_V), BF16),
        jax.ShapeDtypeStruct((bsz, s // cl, 1, GLA_QK), F32),
        jax.ShapeDtypeStruct((bsz, s, MLA_KV_RANK), F32),
        jax.ShapeDtypeStruct((bsz, s, MLA_ROPE), F32),
        jax.ShapeDtypeStruct((bsz, s, MLA_CAT), BF16),
        jax.ShapeDtypeStruct((bsz, MLA_KV_RANK, s), BF16),
        jax.ShapeDtypeStruct((bsz, nt, MLA_HEADS, MLA_CAT, tt), BF16),
    ]
    out_specs = [
        tok(GLA_QK), tok(GLA_QK), tok(GLA_QK), tok(GLA_V),
        pl.BlockSpec((1, n_chunks, 1, GLA_QK), lambda b, t: (b, t, 0, 0)),
        tok(MLA_KV_RANK), tok(MLA_ROPE), tok(MLA_CAT),
        pl.BlockSpec((1, MLA_KV_RANK, tt), lambda b, t: (b, 0, t)),
        pl.BlockSpec((1, 1, MLA_HEADS, MLA_CAT, tt), lambda b, t: (b, t, 0, 0, 0)),
    ]
    return pl.pallas_call(
        functools.partial(_proj_kernel, cl=cl),
        grid=(bsz, nt),
        in_specs=in_specs,
        out_specs=out_specs,
        out_shape=out_shape,
        compiler_params=_cparams("parallel", "parallel"),
        name="proj",
    )(x, cs, cst, *[wp[n] for n in _PROJ_WEIGHTS])


_PROJ_WEIGHTS = ("ln_g", "ln_b", "w_tok", "w_tr", "w_gk2", "b_gk", "kv_g", "kv_gt", "q_gt", "w_uqt", "w_uk")


def _gla_kernel(qd_ref, ki_ref, kt_ref, v_ref, ebl_ref, s0_ref, o_ref, sfin_ref, st_ref, *, cl):
    t = pl.program_id(1)
    ts = qd_ref.shape[1]

    @pl.when(t == 0)
    def _():
        for hd in range(GLA_HEADS):
            st_ref[hd] = s0_ref[0, hd].T

    row = lax.broadcasted_iota(jnp.int32, (ts, ts), 0)
    col = lax.broadcasted_iota(jnp.int32, (ts, ts), 1)
    keep = (row >= col) & ((row & -cl) == (col & -cl))
    qk = lambda ref, hd: ref[0, :, hd * GLA_DK:(hd + 1) * GLA_DK]
    val = lambda hd: v_ref[0, :, hd * GLA_DV:(hd + 1) * GLA_DV]
    for hd in range(GLA_HEADS):
        a = lax.dot_general(qk(qd_ref, hd), qk(ki_ref, hd), NT_DIMS, preferred_element_type=F32)
        a = jnp.where(keep, a, 0.0).astype(BF16)
        o_ref[0, :, hd * GLA_DV:(hd + 1) * GLA_DV] = _dot(a, val(hd))
    for c in range(ts // cl):
        rows = slice(c * cl, (c + 1) * cl)
        for hd in range(GLA_HEADS):
            st = st_ref[hd]
            o_ref[0, rows, hd * GLA_DV:(hd + 1) * GLA_DV] += lax.dot_general(
                qk(qd_ref, hd)[rows], st.astype(BF16), NT_DIMS, preferred_element_type=F32)
            st_ref[hd] = (st * ebl_ref[0, c, :, hd * GLA_DK:(hd + 1) * GLA_DK]
                          + lax.dot_general(val(hd)[rows], qk(kt_ref, hd)[rows], TN_DIMS, preferred_element_type=F32))

    @pl.when(t == pl.num_programs(1) - 1)
    def _():
        for hd in range(GLA_HEADS):
            sfin_ref[0, hd] = st_ref[hd].T


def _gla(qd, ki, kt, v, ebl, s0, *, cl, ts):
    bsz, s, _ = qd.shape
    n_chunks = ts // cl
    s0_b = s0.shape[0]
    qk_spec = pl.BlockSpec((1, ts, GLA_QK), lambda b, t: (b, t, 0))
    v_spec = pl.BlockSpec((1, ts, GLA_V), lambda b, t: (b, t, 0))
    st_spec = pl.BlockSpec((1, GLA_HEADS, GLA_DK, GLA_DV), lambda b, t: (b, 0, 0, 0))
    s0_spec = st_spec if s0_b == bsz else pl.BlockSpec((1, GLA_HEADS, GLA_DK, GLA_DV), lambda b, t: (0, 0, 0, 0))
    return pl.pallas_call(
        functools.partial(_gla_kernel, cl=cl),
        grid=(bsz, s // ts),
        in_specs=[qk_spec, qk_spec, qk_spec, v_spec,
                  pl.BlockSpec((1, n_chunks, 1, GLA_QK), lambda b, t: (b, t, 0, 0)),
                  s0_spec],
        out_specs=[v_spec, st_spec],
        out_shape=[jax.ShapeDtypeStruct((bsz, s, GLA_V), F32),
                   jax.ShapeDtypeStruct((bsz, GLA_HEADS, GLA_DK, GLA_DV), F32)],
        scratch_shapes=[pltpu.VMEM((GLA_HEADS, GLA_DV, GLA_DK), F32)],
        compiler_params=_cparams("parallel", "arbitrary"),
        name="gla",
    )(qd, ki, kt, v, ebl, s0)


def _attn_kernel(qt_ref, kcat_ref, vt_ref, km_ref, vm_ref, o_ref, s_ref, p_ref, m_ref, a_ref, acc_ref,
                 *, bq, causal, n_valid_last):
    i = pl.program_id(1)
    n_cc = qt_ref.shape[2]
    n_kt = kcat_ref.shape[1]
    n_int = i if causal else n_kt - 1

    def scores(j):
        kt = kcat_ref[0, j]
        for c in range(n_cc):
            s_ref[c] = _dot(kt, qt_ref[0, 0, c])

    def softmax(c, mask):
        s = s_ref[c]
        if mask is not None:
            s = jnp.where(mask, s, NEG_BIG)
        m_prev = m_ref[c]
        m_new = jnp.maximum(m_prev, jnp.max(s, axis=0, keepdims=True))
        a_ref[c] = jnp.exp2(m_prev - m_new)
        m_ref[c] = m_new
        p_ref[c] = jnp.exp2(s - m_new).astype(BF16)

    def values(c, vt):
        acc_ref[c] = a_ref[c] * acc_ref[c] + _dot(vt, p_ref[c])

    km, vm = km_ref[...], vm_ref[...]
    sm = [_dot(km, qt_ref[0, 0, c]) for c in range(n_cc)]
    pm = []
    for c in range(n_cc):
        m0 = jnp.max(sm[c], axis=0, keepdims=True)
        m_ref[c] = m0
        pm.append(jnp.exp2(sm[c] - m0).astype(BF16))
    for c in range(n_cc):
        acc_ref[c] = _dot(vm, pm[c])
    a_ref[...] = jnp.ones(a_ref.shape, F32)
    p_ref[...] = jnp.zeros(p_ref.shape, BF16)
    scores(0)

    def key_step(j, carry):
        kt_next = kcat_ref[0, j + 1]
        vt_prev = vt_ref[0, jnp.maximum(j - 1, 0)]
        for c in range(n_cc):
            values(c, vt_prev)
            softmax(c, None)
            s_ref[c] = _dot(kt_next, qt_ref[0, 0, c])
        return carry

    lax.fori_loop(0, n_int, key_step, 0)

    row = lax.broadcasted_iota(jnp.int32, (KEY_TILE, KEY_TILE), 0)
    if causal:
        col = lax.broadcasted_iota(jnp.int32, (KEY_TILE, KEY_TILE), 1)
        mask = (row >> 6) <= (col >> 6)
    else:
        mask = row < n_valid_last
    vt_prev = vt_ref[0, jnp.maximum(n_int - 1, 0)]
    for c in range(n_cc):
        values(c, vt_prev)
    for c in range(n_cc):
        softmax(c, mask)
    vt_last = vt_ref[0, n_int]
    for c in range(n_cc):
        values(c, vt_last)

    groups_per_cc = KEY_TILE // LANES
    for c in range(n_cc):
        acc = acc_ref[c]
        o_t = acc[:MLA_KV_RANK] * (1.0 / acc[MLA_KV_RANK:MLA_KV_RANK + 1])
        for g in range(groups_per_cc):
            blk = o_t[:, g * LANES:(g + 1) * LANES].T.astype(BF16)
            col0 = c * KEY_TILE + g * LANES
            if bq >= LANES:
                hd, q0 = col0 // bq, col0 % bq
                o_ref[0, q0:q0 + LANES, hd * MLA_KV_RANK:(hd + 1) * MLA_KV_RANK] = blk
            else:
                for hl in range(LANES // bq):
                    hd = col0 // bq + hl
                    o_ref[0, :, hd * MLA_KV_RANK:(hd + 1) * MLA_KV_RANK] = blk[hl * bq:(hl + 1) * bq]


def _attn(qt, kcat, vt, km, vm, *, bq, causal, n_valid_last):
    bsz, nq, n_cc = qt.shape[:3]
    n_kt = kcat.shape[1]
    assert not causal or bq == KEY_TILE
    return pl.pallas_call(
        functools.partial(_attn_kernel, bq=bq, causal=causal, n_valid_last=n_valid_last),
        grid=(bsz, nq),
        in_specs=[pl.BlockSpec((1, 1, n_cc, MLA_CAT, KEY_TILE), lambda b, i: (b, i, 0, 0, 0)),
                  pl.BlockSpec((1, n_kt, KEY_TILE, MLA_CAT), lambda b, i: (b, 0, 0, 0)),
                  pl.BlockSpec((1, n_kt, V_ROWS, KEY_TILE), lambda b, i: (b, 0, 0, 0)),
                  _const_spec(km.shape), _const_spec(vm.shape)],
        out_specs=pl.BlockSpec((1, bq, MLA_HEADS * MLA_KV_RANK), lambda b, i: (b, i, 0)),
        out_shape=jax.ShapeDtypeStruct((bsz, nq * bq, MLA_HEADS * MLA_KV_RANK), BF16),
        scratch_shapes=[pltpu.VMEM((n_cc, KEY_TILE, KEY_TILE), F32),
                        pltpu.VMEM((n_cc, KEY_TILE, KEY_TILE), BF16),
                        pltpu.VMEM((n_cc, 1, KEY_TILE), F32),
                        pltpu.VMEM((n_cc, 1, KEY_TILE), F32),
                        pltpu.VMEM((n_cc, V_ROWS, KEY_TILE), F32)],
        compiler_params=_cparams("parallel", "arbitrary"),
        name="attn",
    )(qt, kcat, vt, km, vm)


_MERGE_WEIGHTS = ("ln_g", "ln_b", "w_g", "gla_g", "w_br_gla", "w_uv_bd", "w_br_mla", "w_mg", "b_mg", "w_out",
                  "ln1_g", "ln1_b", "w_r_hi", "w_r_lo", "b_r")


def _merge_kernel(x_ref, og_ref, ol_ref, lng_ref, lnb_ref, wg_ref, glag_ref, wbg_ref, wuv_ref, wbm_ref,
                  wmg_ref, bmg_ref, wout_ref, l1g_ref, l1b_ref, wrh_ref, wrl_ref, br_ref,
                  h1t_ref, rt_ref, cnt_ref, carry_ref):
    tt, d = x_ref.shape
    h = _layer_norm(x_ref[...], lng_ref[...], lnb_ref[...])
    hb = h.astype(BF16)
    g_out = _dot(hb, wg_ref[...])
    og = og_ref[...]
    parts = []
    for hd in range(GLA_HEADS):
        cols = slice(hd * GLA_DV, (hd + 1) * GLA_DV)
        o_h = og[:, cols]
        g_h = g_out[:, cols]
        o_n = o_h * lax.rsqrt(jnp.mean(o_h * o_h, axis=-1, keepdims=True) + RMS_EPS) * glag_ref[...]
        parts.append(o_n * (g_h * jax.nn.sigmoid(g_h)))
    y_a = _dot(jnp.concatenate(parts, axis=-1).astype(BF16), wbg_ref[...])
    pair_in = 2 * MLA_KV_RANK
    y_heads = jnp.concatenate([_dot(ol_ref[:, p * pair_in:(p + 1) * pair_in], wuv_ref[p])
                               for p in range(MLA_HEADS // 2)], axis=-1)
    y_b = _dot(y_heads.astype(BF16), wbm_ref[...])
    gates = jax.nn.sigmoid(_dot(hb, wmg_ref[...]) + bmg_ref[...])
    mix_in = gates[:, :d] * y_a + gates[:, d:] * y_b
    mix = _dot(mix_in.astype(BF16), wout_ref[...])
    h1 = _layer_norm(DEEPNORM_ALPHA * h + mix, l1g_ref[...], l1b_ref[...])
    for s in range(ROW_TILE):
        h1t_ref[pl.ds(s, tt, stride=ROW_TILE), :] = h1[:, s * LANES:(s + 1) * LANES]

    h1_hi = h1.astype(BF16)
    h1_lo = (h1 - h1_hi.astype(F32)).astype(BF16)
    logits = (_dot(h1_hi, wrh_ref[...]) + (_dot(h1_hi, wrl_ref[...]) + _dot(h1_lo, wrh_ref[...]))) + br_ref[...]
    lane = lax.broadcasted_iota(jnp.int32, logits.shape, 1)
    is_grp = lane < N_GROUPS
    gl = jnp.where(is_grp, logits, NEG_BIG)
    g_max = jnp.max(gl, axis=-1, keepdims=True)
    g_sel = jnp.min(jnp.where(gl == g_max, lane, ROUTER_LANES), axis=-1, keepdims=True)
    p_grp = 1.0 / jnp.sum(jnp.where(is_grp, jnp.exp(gl - g_max), 0.0), axis=-1, keepdims=True)
    e_lo = N_GROUPS + g_sel * EXPERTS_PER_GROUP
    in_grp = (lane >= e_lo) & (lane < e_lo + EXPERTS_PER_GROUP)
    el = jnp.where(in_grp, logits, NEG_BIG)
    v1 = jnp.max(el, axis=-1, keepdims=True)
    i1 = jnp.min(jnp.where(el == v1, lane, ROUTER_LANES), axis=-1, keepdims=True)
    el2 = jnp.where(lane == i1, NEG_BIG, el)
    v2 = jnp.max(el2, axis=-1, keepdims=True)
    i2 = jnp.min(jnp.where(el2 == v2, lane, ROUTER_LANES), axis=-1, keepdims=True)
    e2 = jnp.exp(v2 - v1)
    w1 = p_grp / (1.0 + e2)
    w2 = p_grp * e2 / (1.0 + e2)

    @pl.when(pl.program_id(0) == 0)
    def _():
        carry_ref[...] = jnp.zeros(carry_ref.shape, F32)

    e1, e2i = i1 - N_GROUPS, i2 - N_GROUPS
    hot = ((lane == e1) | (lane == e2i)).astype(F32)
    earlier = (lax.broadcasted_iota(jnp.int32, (tt, tt), 0) > lax.broadcasted_iota(jnp.int32, (tt, tt), 1))
    before = carry_ref[...] + _dot(earlier.astype(BF16), hot.astype(BF16))
    r1 = jnp.sum(jnp.where(lane == e1, before, 0.0), axis=-1, keepdims=True)
    r2 = jnp.sum(jnp.where(lane == e2i, before, 0.0), axis=-1, keepdims=True)
    carry_ref[...] += jnp.sum(hot, axis=0, keepdims=True)
    cnt_ref[...] = carry_ref[...]
    fields = (e1.astype(F32), e2i.astype(F32), w1, w2, r1, r2)
    rt = jnp.zeros(logits.shape, F32)
    for k, val in enumerate(fields):
        rt = jnp.where(lane == k, val, rt)
    rt_ref[...] = rt


def _merge(x2, og2, ol2, wp, *, tt):
    t, d = x2.shape
    assert d == ROW_TILE * LANES
    row = lambda w: pl.BlockSpec((tt, w), lambda i: (i, 0))
    return pl.pallas_call(
        _merge_kernel,
        grid=(t // tt,),
        in_specs=[row(d), row(GLA_V), row(MLA_HEADS * MLA_KV_RANK)] + [_const_spec(wp[n].shape) for n in _MERGE_WEIGHTS],
        out_specs=[pl.BlockSpec((tt * ROW_TILE, LANES), lambda i: (i, 0)), row(ROUTER_LANES),
                   pl.BlockSpec((1, ROUTER_LANES), lambda i: (0, 0))],
        out_shape=[jax.ShapeDtypeStruct((t * ROW_TILE, LANES), F32), jax.ShapeDtypeStruct((t, ROUTER_LANES), F32),
                   jax.ShapeDtypeStruct((1, ROUTER_LANES), F32)],
        scratch_shapes=[pltpu.VMEM((1, ROUTER_LANES), F32)],
        compiler_params=_cparams("arbitrary"),
        name="merge",
    )(x2, og2, ol2, *[wp[n] for n in _MERGE_WEIGHTS])


def _row_copy(src_ref, src_row, dst_ref, dst_row, sem):
    return pltpu.make_async_copy(src_ref.at[pl.ds(src_row * ROW_TILE, ROW_TILE)],
                                 dst_ref.at[pl.ds(dst_row * ROW_TILE, ROW_TILE)], sem)


def _dispatch_kernel(pos_ref, zrow_ref, nu_ref, h1t_ref, xs_ref, zero_ref, sem, zsem, *, td):
    i = pl.program_id(0)

    @pl.when(i == 0)
    def _():
        zero_ref[...] = jnp.zeros(zero_ref.shape, F32)
        tile_rows = EXPERT_ROWS * ROW_TILE
        fill = lambda row: pltpu.make_async_copy(
            zero_ref, xs_ref.at[pl.ds(pl.multiple_of(row * ROW_TILE, ROW_TILE), tile_rows)], zsem)
        for e in range(N_EXPERTS):
            @pl.when(zrow_ref[e] >= 0)
            def _():
                fill(zrow_ref[e]).start()
        for e in range(N_EXPERTS):
            @pl.when(zrow_ref[e] >= 0)
            def _():
                fill(zrow_ref[e]).wait()

        def fill_tail(r, carry):
            fill(r * EXPERT_ROWS).start()
            fill(r * EXPERT_ROWS).wait()
            return carry

        lax.fori_loop(nu_ref[0], xs_ref.shape[0] // tile_rows, fill_tail, 0)

    base = i * (2 * td)
    for t in range(td):
        for k in range(2):
            dst = pl.multiple_of(pos_ref[base + 2 * t + k] * ROW_TILE, ROW_TILE)
            pltpu.make_async_copy(h1t_ref.at[pl.ds(t * ROW_TILE, ROW_TILE)],
                                  xs_ref.at[pl.ds(dst, ROW_TILE)], sem).start()
    for _ in range(2 * td):
        _row_copy(h1t_ref, 0, xs_ref, 0, sem).wait()


def _dispatch(h1t, pos, zrow, n_used, *, n_rows, td):
    t = h1t.shape[0] // ROW_TILE
    return pl.pallas_call(
        functools.partial(_dispatch_kernel, td=td),
        grid_spec=pltpu.PrefetchScalarGridSpec(
            num_scalar_prefetch=3,
            grid=(t // td,),
            in_specs=[pl.BlockSpec((td * ROW_TILE, LANES), lambda i, *_: (i, 0))],
            out_specs=pl.BlockSpec(memory_space=pl.ANY),
            scratch_shapes=[pltpu.VMEM((EXPERT_ROWS * ROW_TILE, LANES), F32),
                            pltpu.SemaphoreType.DMA(()), pltpu.SemaphoreType.DMA(())]),
        out_shape=jax.ShapeDtypeStruct((n_rows * ROW_TILE, LANES), F32),
        compiler_params=_cparams("arbitrary"),
        name="dispatch",
    )(pos, zrow, n_used, h1t)


def _experts_kernel(te_ref, tb_ref, nu_ref, xs_ref, wg_ref, wu_ref, wd_ref, out_ref):
    del te_ref, tb_ref
    rows = xs_ref.shape[0] // ROW_TILE

    @pl.when(pl.program_id(0) < nu_ref[0])
    def _():
        x = jnp.concatenate([xs_ref[pl.ds(s, rows, stride=ROW_TILE), :] for s in range(ROW_TILE)], axis=1).astype(BF16)
        gate = _dot(x, wg_ref[0])
        up = _dot(x, wu_ref[0])
        hid = (gate * jax.nn.sigmoid(gate)) * up
        out = _dot(hid.astype(BF16), wd_ref[0])
        for s in range(ROW_TILE):
            out_ref[pl.ds(s, rows, stride=ROW_TILE), :] = out[:, s * LANES:(s + 1) * LANES]

    @pl.when(pl.program_id(0) >= nu_ref[0])
    def _():
        out_ref[...] = jnp.zeros(out_ref.shape, F32)


def _experts(xs, tile_e, tile_blk, n_used, wp):
    n_rows = xs.shape[0] // ROW_TILE
    d = ROW_TILE * LANES
    blk = pl.BlockSpec((EXPERT_ROWS * ROW_TILE, LANES), lambda r, te, tb, nu: (tb[r], 0))
    return pl.pallas_call(
        _experts_kernel,
        grid_spec=pltpu.PrefetchScalarGridSpec(
            num_scalar_prefetch=3,
            grid=(n_rows // EXPERT_ROWS,),
            in_specs=[blk,
                      pl.BlockSpec((1, d, D_EXPERT), lambda r, te, tb, nu: (te[r], 0, 0)),
                      pl.BlockSpec((1, d, D_EXPERT), lambda r, te, tb, nu: (te[r], 0, 0)),
                      pl.BlockSpec((1, D_EXPERT, d), lambda r, te, tb, nu: (te[r], 0, 0))],
            out_specs=pl.BlockSpec((EXPERT_ROWS * ROW_TILE, LANES), lambda r, te, tb, nu: (r, 0))),
        out_shape=jax.ShapeDtypeStruct(xs.shape, F32),
        compiler_params=_cparams("arbitrary"),
        name="experts",
    )(tile_e, tile_blk, n_used, xs, wp["w_gate"], wp["w_up"], wp["w_down"])


def _combine_kernel(pos_ref, h1t_ref, rt_ref, outs_ref, l2g_ref, l2b_ref, y_ref, g0_ref, g1_ref, sem, *, tc):
    i = pl.program_id(0)
    base = i * (2 * tc)
    for t in range(tc):
        for k, g_ref in enumerate((g0_ref, g1_ref)):
            src = pl.multiple_of(pos_ref[base + 2 * t + k] * ROW_TILE, ROW_TILE)
            pltpu.make_async_copy(outs_ref.at[pl.ds(src, ROW_TILE)],
                                  g_ref.at[pl.ds(t * ROW_TILE, ROW_TILE)], sem).start()
    for _ in range(2 * tc):
        _row_copy(outs_ref, 0, g0_ref, 0, sem).wait()
    rt = rt_ref[...]
    w0, w1 = rt[:, RT_W:RT_W + 1], rt[:, RT_W + 1:RT_W + 2]
    cols = []
    for s in range(ROW_TILE):
        rows = pl.ds(s, tc, stride=ROW_TILE)
        cols.append(DEEPNORM_ALPHA * h1t_ref[rows, :] + (w0 * g0_ref[rows, :] + w1 * g1_ref[rows, :]))
    y_ref[...] = _layer_norm(jnp.concatenate(cols, axis=1), l2g_ref[...], l2b_ref[...])


def _combine(h1t, rt, outs, pos, wp, *, tc):
    t = rt.shape[0]
    d = ROW_TILE * LANES
    return pl.pallas_call(
        functools.partial(_combine_kernel, tc=tc),
        grid_spec=pltpu.PrefetchScalarGridSpec(
            num_scalar_prefetch=1,
            grid=(t // tc,),
            in_specs=[pl.BlockSpec((tc * ROW_TILE, LANES), lambda i, *_: (i, 0)),
                      pl.BlockSpec((tc, ROUTER_LANES), lambda i, *_: (i, 0)),
                      pl.BlockSpec(memory_space=pl.ANY),
                      pl.BlockSpec((1, d), lambda i, *_: (0, 0)), pl.BlockSpec((1, d), lambda i, *_: (0, 0))],
            out_specs=pl.BlockSpec((tc, d), lambda i, *_: (i, 0)),
            scratch_shapes=[pltpu.VMEM((tc * ROW_TILE, LANES), F32), pltpu.VMEM((tc * ROW_TILE, LANES), F32),
                            pltpu.SemaphoreType.DMA(())]),
        out_shape=jax.ShapeDtypeStruct((t, d), F32),
        compiler_params=_cparams("arbitrary"),
        name="combine",
    )(pos, h1t, rt, outs, wp["ln2_g"], wp["ln2_b"])


def _route_plan(rt, cnt):
    t = rt.shape[0]
    n_tiles = (2 * t) // EXPERT_ROWS + N_EXPERTS
    counts = cnt[0, :N_EXPERTS].astype(jnp.int32)
    padded = (counts + (EXPERT_ROWS - 1)) // EXPERT_ROWS * EXPERT_ROWS
    ends = jnp.cumsum(padded)
    starts = ends - padded
    eid = rt[:, RT_E:RT_E + 2].astype(jnp.int32)
    rank = rt[:, RT_RANK:RT_RANK + 2].astype(jnp.int32)
    pos = (jnp.take(starts, eid) + rank).reshape(-1)
    n_used = ends[-1] // EXPERT_ROWS
    tile_blk = jnp.minimum(jnp.arange(n_tiles, dtype=jnp.int32), n_used - 1)
    tile_e = jnp.sum((tile_blk[:, None] * EXPERT_ROWS >= ends[None, :]).astype(jnp.int32), axis=1)
    zrow = jnp.where(padded > 0, ends - EXPERT_ROWS, -1)
    return pos, zrow.astype(jnp.int32), tile_e.astype(jnp.int32), tile_blk, n_used.reshape(1).astype(jnp.int32), n_tiles


def _rope_tables(pos):
    inv = ROPE_THETA ** (-jnp.arange(0, MLA_ROPE, 2, dtype=F32) / MLA_ROPE)
    ang = pos.astype(F32)[:, None] * inv[None, :]
    cs = jnp.concatenate([jnp.cos(ang), jnp.sin(ang)], axis=-1)
    return cs[None], cs.T[None]


def _prep_weights(ln_in_g, ln_in_b, w_in, w_gk2, b_gk, gla_norm_g, q_norm_g, kv_norm_g, w_uq, w_uk, w_uv,
                  w_br_gla, w_br_mla, w_mg, b_mg, w_out, ln1_g, ln1_b, w_rg, b_rg, w_re, b_re,
                  w_gate, w_up, w_down, ln2_g, ln2_b):
    d = w_in.shape[1]
    w = w_in[0]
    c0 = 0
    wk = w[:, c0:c0 + GLA_QK]; c0 += GLA_QK
    wv = w[:, c0:c0 + GLA_V]; c0 += GLA_V
    wgr = w[:, c0:c0 + GLA_RANK]; c0 += GLA_RANK
    wckv = w[:, c0:c0 + MLA_KV_RANK]; c0 += MLA_KV_RANK
    wkr = w[:, c0:c0 + MLA_ROPE]; c0 += MLA_ROPE
    wq = w[:, c0:c0 + GLA_QK]; c0 += GLA_QK
    wg = w[:, c0:c0 + GLA_V]; c0 += GLA_V
    wcq = w[:, c0:c0 + MLA_Q_RANK]
    w_tok = jnp.concatenate([wk, wv, wq, wckv, wkr, wgr, jnp.zeros((d, TOK_PAD - TOK_USED), F32)], axis=1)
    w_tr = jnp.concatenate([wckv, wcq], axis=1).T
    uq = w_uq[0].reshape(MLA_Q_RANK, MLA_HEADS, MLA_QK_DIM)
    uq_perm = jnp.concatenate([
        uq[:, :, :MLA_NOPE].reshape(MLA_Q_RANK, -1),
        uq[:, :, MLA_NOPE:MLA_NOPE + MLA_HALF].reshape(MLA_Q_RANK, -1),
        uq[:, :, MLA_NOPE + MLA_HALF:].reshape(MLA_Q_RANK, -1)], axis=1)
    uv = w_uv[0].transpose(1, 0, 2).reshape(MLA_HEADS // 2, 2, MLA_KV_RANK, MLA_DV)
    eye = jnp.eye(2, dtype=F32)
    w_uv_bd = (uv[:, :, :, None, :] * eye[None, :, None, :, None]).reshape(
        MLA_HEADS // 2, 2 * MLA_KV_RANK, 2 * MLA_DV)
    w_r = jnp.concatenate([w_rg[0], w_re[0].transpose(1, 0, 2).reshape(d, N_EXPERTS),
                           jnp.zeros((d, ROUTER_LANES - N_GROUPS - N_EXPERTS), F32)], axis=1)
    w_r_hi = w_r.astype(BF16)
    b_r = jnp.concatenate([b_rg[0], b_re[0].reshape(-1), jnp.zeros((ROUTER_LANES - N_GROUPS - N_EXPERTS,), F32)])
    row = lambda a: a.reshape(1, -1)
    return {
        "ln_g": row(ln_in_g), "ln_b": row(ln_in_b),
        "w_tok": w_tok.astype(BF16), "w_tr": w_tr.astype(BF16),
        "w_gk2": w_gk2[0].astype(BF16), "b_gk": row(b_gk[0]),
        "kv_g": row(kv_norm_g[0]), "kv_gt": kv_norm_g[0].reshape(-1, 1), "q_gt": q_norm_g[0].reshape(-1, 1),
        "w_uqt": uq_perm.T.astype(BF16), "w_uk": w_uk[0].transpose(1, 0, 2).astype(BF16),
        "w_g": wg.astype(BF16), "gla_g": row(gla_norm_g[0]),
        "w_br_gla": w_br_gla[0].astype(BF16), "w_uv_bd": w_uv_bd.astype(BF16), "w_br_mla": w_br_mla[0].astype(BF16),
        "w_mg": w_mg[0].astype(BF16), "b_mg": row(b_mg[0]), "w_out": w_out[0].astype(BF16),
        "ln1_g": row(ln1_g[0]), "ln1_b": row(ln1_b[0]),
        "w_r_hi": w_r_hi, "w_r_lo": (w_r - w_r_hi.astype(F32)).astype(BF16), "b_r": row(b_r),
        "w_gate": w_gate[0].astype(BF16), "w_up": w_up[0].astype(BF16), "w_down": w_down[0].astype(BF16),
        "ln2_g": row(ln2_g[0]), "ln2_b": row(ln2_b[0]),
    }


def _value_rows(vt):
    lead, length = vt.shape[:-2], vt.shape[-1]
    return jnp.concatenate([vt, jnp.ones(lead + (1, length), vt.dtype),
                            jnp.zeros(lead + (V_ROWS - MLA_KV_RANK - 1, length), vt.dtype)], axis=-2)


def _key_tiles(kcat, vt):
    bsz, length, _ = kcat.shape
    n = -(-length // KEY_TILE)
    pad = n * KEY_TILE - length
    kcat = jnp.pad(kcat, ((0, 0), (0, pad), (0, 0)))
    vt = jnp.pad(_value_rows(vt), ((0, 0), (0, 0), (0, pad)))
    return (kcat.reshape(bsz, n, KEY_TILE, MLA_CAT),
            vt.reshape(bsz, V_ROWS, n, KEY_TILE).transpose(0, 2, 1, 3))


def _ffn(x, og, ol, wp):
    bsz, s, d = x.shape
    t = bsz * s
    h1t, rt, cnt = _merge(x.reshape(t, d), og.reshape(t, -1), ol.reshape(t, -1), wp, tt=min(TOKEN_TILE, t))
    pos, zrow, tile_e, tile_blk, n_used, n_tiles = _route_plan(rt, cnt)
    tile = min(MOE_TOKEN_TILE, t)
    xs = _dispatch(h1t, pos, zrow, n_used, n_rows=n_tiles * EXPERT_ROWS, td=tile)
    outs = _experts(xs, tile_e, tile_blk, n_used, wp)
    return _combine(h1t, rt, outs, pos, wp, tc=tile).reshape(bsz, s, d)


def kernel(x_prompt, x_sample, cache_mla_latent, cache_mla_krope, state_gla, meta_tokens, ln_in_g, ln_in_b, w_in, w_gk2, b_gk, gla_norm_g, q_norm_g, kv_norm_g, w_uq, w_uk, w_uv, w_br_gla, w_br_mla, w_mg, b_mg, w_out, ln1_g, ln1_b, w_rg, b_rg, w_re, b_re, w_gate, w_up, w_down, ln2_g, ln2_b):
    bp, sp, d = x_prompt.shape
    bs, ss, _ = x_sample.shape
    past = cache_mla_latent.shape[2]
    wp = _prep_weights(ln_in_g, ln_in_b, w_in, w_gk2, b_gk, gla_norm_g, q_norm_g, kv_norm_g, w_uq, w_uk, w_uv,
                       w_br_gla, w_br_mla, w_mg, b_mg, w_out, ln1_g, ln1_b, w_rg, b_rg, w_re, b_re,
                       w_gate, w_up, w_down, ln2_g, ln2_b)

    cs_m, cst_m = _rope_tables(jnp.arange(-N_META, 0, dtype=jnp.int32))
    m = _proj(meta_tokens[None], cs_m, cst_m, wp, cl=N_META, tt=N_META)
    _, _, m_kt, m_v, m_ebl, m_ckv, m_kr, m_kcat, m_vt, _ = m
    zero_state = jnp.zeros((1, GLA_HEADS, GLA_DK, GLA_DV), F32)
    _, m_state = _gla(m[0], m[1], m_kt, m_v, m_ebl, zero_state, cl=N_META, ts=N_META)

    cs_p, cst_p = _rope_tables(jnp.arange(sp, dtype=jnp.int32))
    p_qd, p_ki, p_kt, p_v, p_ebl, p_ckv, p_kr, p_kcat, p_vt, p_qt = _proj(
        x_prompt, cs_p, cst_p, wp, cl=CHUNK, tt=TOKEN_TILE)
    p_o, p_state = _gla(p_qd, p_ki, p_kt, p_v, p_ebl, m_state, cl=CHUNK, ts=TOKEN_TILE)
    rep = lambda a, n: jnp.broadcast_to(a, (n,) + a.shape[1:])
    lat_p = jnp.concatenate([rep(m_ckv, bp), p_ckv], axis=1)
    kr_p = jnp.concatenate([rep(m_kr, bp), p_kr], axis=1)
    kcat_p, vt_p = _key_tiles(p_kcat, p_vt)
    km, vm = m_kcat[0], _value_rows(m_vt[0])
    p_ol = _attn(p_qt, kcat_p, vt_p, km, vm, bq=TOKEN_TILE, causal=True, n_valid_last=KEY_TILE)
    y_prompt = _ffn(x_prompt, p_o, p_ol, wp)

    ts_all = bs * ss
    cs_s, cst_s = _rope_tables(past + (jnp.arange(ts_all, dtype=jnp.int32) % ss))
    s_qd, s_ki, s_kt, s_v, s_ebl, s_ckv, s_kr, s_kcat, s_vt, s_qt = _proj(
        x_sample.reshape(1, ts_all, d), cs_s, cst_s, wp, cl=ss, tt=min(TOKEN_TILE, ts_all))
    per_stream = lambda a: a.reshape(bs, ss, a.shape[-1])
    s_o, s_state = _gla(per_stream(s_qd), per_stream(s_ki), per_stream(s_kt), per_stream(s_v),
                        s_ebl.reshape(bs, 1, 1, GLA_QK), state_gla[0].astype(F32), cl=ss, ts=ss)
    s_ckv, s_kr = per_stream(s_ckv), per_stream(s_kr)
    cache_kcat = jnp.concatenate([cache_mla_latent[0], cache_mla_krope[0]], axis=-1).astype(BF16)
    cache_vt = cache_mla_latent[0].astype(BF16).transpose(0, 2, 1)
    new_vt = s_vt.reshape(MLA_KV_RANK, bs, ss).transpose(1, 0, 2)
    kcat_s, vt_s = _key_tiles(jnp.concatenate([cache_kcat, per_stream(s_kcat)], axis=1),
                              jnp.concatenate([cache_vt, new_vt], axis=2))
    qt = s_qt.transpose(0, 2, 3, 1, 4).reshape(MLA_HEADS, MLA_CAT, bs, ss)
    qt = qt.transpose(2, 1, 0, 3).reshape(bs, MLA_CAT, MLA_HEADS * ss // KEY_TILE, KEY_TILE)
    qt = qt.transpose(0, 2, 1, 3)[:, None]
    s_ol = _attn(qt, kcat_s, vt_s, km, vm, bq=ss, causal=False, n_valid_last=(past + ss - 1) % KEY_TILE + 1)
    y_sample = _ffn(x_sample, s_o, s_ol, wp)

    return (y_prompt, y_sample, lat_p[None], kr_p[None], p_state[None].astype(state_gla.dtype),
            s_ckv[None], s_kr[None], s_state[None].astype(state_gla.dtype))
```

```python
import functools

import jax
import jax.numpy as jnp
from jax import lax
from jax.experimental import pallas as pl
from jax.experimental.pallas import tpu as pltpu

F32 = jnp.float32
BF16 = jnp.bfloat16

CHUNK = 64
N_META = 16
GLA_HEADS = 4
GLA_DK = 128
GLA_DV = 256
GLA_RANK = 16
GLA_TAU = 16.0
GLA_QK = GLA_HEADS * GLA_DK
GLA_V = GLA_HEADS * GLA_DV
GLA_SCALE = GLA_DK ** -0.5
MLA_HEADS = 16
MLA_Q_RANK = 384
MLA_KV_RANK = 128
MLA_NOPE = 64
MLA_ROPE = 32
MLA_HALF = MLA_ROPE // 2
MLA_DV = 64
MLA_QK_DIM = MLA_NOPE + MLA_ROPE
MLA_CAT = MLA_KV_RANK + MLA_ROPE
MLA_SCALE = MLA_QK_DIM ** -0.5
LOG2_E = 1.4426950408889634
Q_SCALE = MLA_SCALE * LOG2_E
V_ROWS = MLA_KV_RANK + 16
ROPE_THETA = 10000.0
N_GROUPS = 4
EXPERTS_PER_GROUP = 8
N_EXPERTS = N_GROUPS * EXPERTS_PER_GROUP
D_EXPERT = 256
LN_EPS = 1e-5
RMS_EPS = 1e-6
DEEPNORM_ALPHA = 2.0 ** 0.25

LANES = 128
MXU_DIM = 256
VMEM_LIMIT_BYTES = 56 * 1024 * 1024

TOKEN_TILE = 256
MERGE_TILE = 2 * TOKEN_TILE
KEY_TILE = MXU_DIM
MOE_TOKEN_TILE = 256
EXPERT_ROWS = 256
ROW_TILE = 8
ROUTER_LANES = LANES
RT_E, RT_W, RT_RANK = 0, 2, 4
NEG_BIG = -1e30

NT_DIMS = (((1,), (1,)), ((), ()))
TN_DIMS = (((0,), (0,)), ((), ()))


def _cparams(*sem):
    return pltpu.CompilerParams(dimension_semantics=sem, vmem_limit_bytes=VMEM_LIMIT_BYTES)


def _const_spec(shape):
    nd = len(shape)
    return pl.BlockSpec(shape, lambda *_: (0,) * nd, pipeline_mode=pl.Buffered(1))


def _layer_norm(x, g, b):
    mu = jnp.mean(x, axis=-1, keepdims=True)
    xc = x - mu
    var = jnp.mean(xc * xc, axis=-1, keepdims=True)
    return xc * lax.rsqrt(var + LN_EPS) * g + b


def _dot(a, b):
    return jnp.dot(a, b, preferred_element_type=F32)


TOK_COLS = (GLA_QK, GLA_V, GLA_QK, MLA_KV_RANK, MLA_ROPE, GLA_RANK)
TOK_USED = sum(TOK_COLS)
TOK_PAD = -(-TOK_USED // MXU_DIM) * MXU_DIM
TR_ROWS = MLA_KV_RANK + MLA_Q_RANK


def _proj_kernel(x_ref, cs_ref, cst_ref, lng_ref, lnb_ref, wtok_ref, wtr_ref, wgk2_ref, bgk_ref,
                 kvg_ref, kvgt_ref, qngt_ref, wuqt_ref, wuk_ref,
                 qd_ref, ki_ref, kt_ref, v_ref, ebl_ref, ckv_ref, kr_ref, kcat_ref, vt_ref, qt_ref, *, cl):
    tt = x_ref.shape[1]
    n_chunks = tt // cl
    h = _layer_norm(x_ref[0], lng_ref[...], lnb_ref[...])
    hb = h.astype(BF16)

    z = _dot(hb, wtok_ref[...])
    zt = lax.dot_general(wtr_ref[...], hb, NT_DIMS, preferred_element_type=F32)
    o0 = 0
    k = z[:, o0:o0 + GLA_QK]; o0 += GLA_QK
    v = z[:, o0:o0 + GLA_V]; o0 += GLA_V
    q = z[:, o0:o0 + GLA_QK]; o0 += GLA_QK
    ckv_raw = z[:, o0:o0 + MLA_KV_RANK]; o0 += MLA_KV_RANK
    kr = z[:, o0:o0 + MLA_ROPE]; o0 += MLA_ROPE
    gklr = z[:, o0:o0 + GLA_RANK]

    gz = _dot(gklr.astype(BF16), wgk2_ref[...]) + bgk_ref[...]
    gk = (jnp.minimum(gz, 0.0) - jnp.log(1.0 + jnp.exp(-jnp.abs(gz)))) * (1.0 / GLA_TAU)
    row_in_chunk = lax.broadcasted_iota(jnp.int32, gk.shape, 0) & (cl - 1)
    b = gk
    shift = 1
    while shift < cl:
        b = b + jnp.where(row_in_chunk >= shift, pltpu.roll(b, shift, 0), 0.0)
        shift *= 2
    b3 = b.reshape(n_chunks, cl, GLA_QK)
    bl = b3[:, cl - 1:cl, :]
    qd_ref[0] = (q * GLA_SCALE * jnp.exp(b)).astype(BF16)
    ki_ref[0] = (k * jnp.exp(-b)).astype(BF16)
    kt_ref[0] = (k.reshape(n_chunks, cl, GLA_QK) * jnp.exp(bl - b3)).reshape(tt, GLA_QK).astype(BF16)
    v_ref[0] = v.astype(BF16)
    ebl_ref[0] = jnp.exp(bl)

    ckv = ckv_raw * lax.rsqrt(jnp.mean(ckv_raw * ckv_raw, axis=-1, keepdims=True) + RMS_EPS) * kvg_ref[...]
    cs = cs_ref[0]
    cos, sin = cs[:, :MLA_HALF], cs[:, MLA_HALF:]
    x1, x2 = kr[:, :MLA_HALF], kr[:, MLA_HALF:]
    kr_rot = jnp.concatenate([x1 * cos - x2 * sin, x2 * cos + x1 * sin], axis=-1)
    ckv_ref[0] = ckv
    kr_ref[0] = kr_rot
    kcat_ref[0, :, :MLA_KV_RANK] = ckv.astype(BF16)
    kcat_ref[0, :, MLA_KV_RANK:] = kr_rot.astype(BF16)

    ckvt = zt[:MLA_KV_RANK]
    ckvt = ckvt * lax.rsqrt(jnp.mean(ckvt * ckvt, axis=0, keepdims=True) + RMS_EPS) * kvgt_ref[...]
    vt_ref[0] = ckvt.astype(BF16)
    cqt = zt[MLA_KV_RANK:]
    cqt = cqt * lax.rsqrt(jnp.mean(cqt * cqt, axis=0, keepdims=True) + RMS_EPS) * qngt_ref[...]
    qmt = _dot(wuqt_ref[...], cqt.astype(BF16))
    n_nope = MLA_HEADS * MLA_NOPE
    n_half = MLA_HEADS * MLA_HALF
    cst = cst_ref[0]
    cos_t = jnp.concatenate([cst[:MLA_HALF]] * MLA_HEADS, axis=0)
    sin_t = jnp.concatenate([cst[MLA_HALF:]] * MLA_HEADS, axis=0)
    r1 = qmt[n_nope:n_nope + n_half]
    r2 = qmt[n_nope + n_half:]
    rot1 = ((r1 * cos_t - r2 * sin_t) * Q_SCALE).astype(BF16)
    rot2 = ((r2 * cos_t + r1 * sin_t) * Q_SCALE).astype(BF16)
    for hd in range(MLA_HEADS):
        nope = qmt[hd * MLA_NOPE:(hd + 1) * MLA_NOPE].astype(BF16)
        qlat = _dot(wuk_ref[hd], nope) * Q_SCALE
        qt_ref[0, 0, hd, :MLA_KV_RANK, :] = qlat.astype(BF16)
        qt_ref[0, 0, hd, MLA_KV_RANK:MLA_KV_RANK + MLA_HALF, :] = rot1[hd * MLA_HALF:(hd + 1) * MLA_HALF]
        qt_ref[0, 0, hd, MLA_KV_RANK + MLA_HALF:, :] = rot2[hd * MLA_HALF:(hd + 1) * MLA_HALF]


def _proj(x, cs, cst, wp, *, cl, tt):
    bsz, s, d = x.shape
    nt = s // tt
    n_chunks = tt // cl
    tok = lambda w: pl.BlockSpec((1, tt, w), lambda b, t: (b, t, 0))
    in_specs = [
        tok(d),
        pl.BlockSpec((1, tt, MLA_ROPE), lambda b, t: (0, t, 0)),
        pl.BlockSpec((1, MLA_ROPE, tt), lambda b, t: (0, 0, t)),
    ] + [_const_spec(wp[n].shape) for n in _PROJ_WEIGHTS]
    out_shape = [
        jax.ShapeDtypeStruct((bsz, s, GLA_QK), BF16),
        jax.ShapeDtypeStruct((bsz, s, GLA_QK), BF16),
        jax.ShapeDtypeStruct((bsz, s, GLA_QK), BF16),
        jax.ShapeDtypeStruct((bsz, s, GLA_V), BF16),
        jax.ShapeDtypeStruct((bsz, s // cl, 1, GLA_QK), F32),
        jax.ShapeDtypeStruct((bsz, s, MLA_KV_RANK), F32),
        jax.ShapeDtypeStruct((bsz, s, MLA_ROPE), F32),
        jax.ShapeDtypeStruct((bsz, s, MLA_CAT), BF16),
        jax.ShapeDtypeStruct((bsz, MLA_KV_RANK, s), BF16),
        jax.ShapeDtypeStruct((bsz, nt, MLA_HEADS, MLA_CAT, tt), BF16),
    ]
    out_specs = [
        tok(GLA_QK), tok(GLA_QK), tok(GLA_QK), tok(GLA_V),
        pl.BlockSpec((1, n_chunks, 1, GLA_QK), lambda b, t: (b, t, 0, 0)),
        tok(MLA_KV_RANK), tok(MLA_ROPE), tok(MLA_CAT),
        pl.BlockSpec((1, MLA_KV_RANK, tt), lambda b, t: (b, 0, t)),
        pl.BlockSpec((1, 1, MLA_HEADS, MLA_CAT, tt), lambda b, t: (b, t, 0, 0, 0)),
    ]
    return pl.pallas_call(
        functools.partial(_proj_kernel, cl=cl),
        grid=(bsz, nt),
        in_specs=in_specs,
        out_specs=out_specs,
        out_shape=out_shape,
        compiler_params=_cparams("parallel", "parallel"),
        name="proj",
    )(x, cs, cst, *[wp[n] for n in _PROJ_WEIGHTS])


_PROJ_WEIGHTS = ("ln_g", "ln_b", "w_tok", "w_tr", "w_gk2", "b_gk", "kv_g", "kv_gt", "q_gt", "w_uqt", "w_uk")


def _gla_kernel(qd_ref, ki_ref, kt_ref, v_ref, ebl_ref, s0_ref, o_ref, sfin_ref, st_ref, *, cl):
    t = pl.program_id(1)
    ts = qd_ref.shape[1]

    @pl.when(t == 0)
    def _():
        for hd in range(GLA_HEADS):
            st_ref[hd] = s0_ref[0, hd].T

    row = lax.broadcasted_iota(jnp.int32, (ts, ts), 0)
    col = lax.broadcasted_iota(jnp.int32, (ts, ts), 1)
    keep = (row >= col) & ((row & -cl) == (col & -cl))
    qk = lambda ref, hd: ref[0, :, hd * GLA_DK:(hd + 1) * GLA_DK]
    val = lambda hd: v_ref[0, :, hd * GLA_DV:(hd + 1) * GLA_DV]
    for hd in range(GLA_HEADS):
        a = lax.dot_general(qk(qd_ref, hd), qk(ki_ref, hd), NT_DIMS, preferred_element_type=F32)
        a = jnp.where(keep, a, 0.0).astype(BF16)
        o_ref[0, :, hd * GLA_DV:(hd + 1) * GLA_DV] = _dot(a, val(hd))
    for c in range(ts // cl):
        rows = slice(c * cl, (c + 1) * cl)
        for hd in range(GLA_HEADS):
            st = st_ref[hd]
            o_ref[0, rows, hd * GLA_DV:(hd + 1) * GLA_DV] += lax.dot_general(
                qk(qd_ref, hd)[rows], st.astype(BF16), NT_DIMS, preferred_element_type=F32)
            st_ref[hd] = (st * ebl_ref[0, c, :, hd * GLA_DK:(hd + 1) * GLA_DK]
                          + lax.dot_general(val(hd)[rows], qk(kt_ref, hd)[rows], TN_DIMS, preferred_element_type=F32))

    @pl.when(t == pl.num_programs(1) - 1)
    def _():
        for hd in range(GLA_HEADS):
            sfin_ref[0, hd] = st_ref[hd].T


def _gla(qd, ki, kt, v, ebl, s0, *, cl, ts):
    bsz, s, _ = qd.shape
    n_chunks = ts // cl
    s0_b = s0.shape[0]
    qk_spec = pl.BlockSpec((1, ts, GLA_QK), lambda b, t: (b, t, 0))
    v_spec = pl.BlockSpec((1, ts, GLA_V), lambda b, t: (b, t, 0))
    st_spec = pl.BlockSpec((1, GLA_HEADS, GLA_DK, GLA_DV), lambda b, t: (b, 0, 0, 0))
    s0_spec = st_spec if s0_b == bsz else pl.BlockSpec((1, GLA_HEADS, GLA_DK, GLA_DV), lambda b, t: (0, 0, 0, 0))
    return pl.pallas_call(
        functools.partial(_gla_kernel, cl=cl),
        grid=(bsz, s // ts),
        in_specs=[qk_spec, qk_spec, qk_spec, v_spec,
                  pl.BlockSpec((1, n_chunks, 1, GLA_QK), lambda b, t: (b, t, 0, 0)),
                  s0_spec],
        out_specs=[v_spec, st_spec],
        out_shape=[jax.ShapeDtypeStruct((bsz, s, GLA_V), F32),
                   jax.ShapeDtypeStruct((bsz, GLA_HEADS, GLA_DK, GLA_DV), F32)],
        scratch_shapes=[pltpu.VMEM((GLA_HEADS, GLA_DV, GLA_DK), F32)],
        compiler_params=_cparams("parallel", "arbitrary"),
        name="gla",
    )(qd, ki, kt, v, ebl, s0)


def _attn_kernel(qt_ref, kcat_ref, vt_ref, km_ref, vm_ref, o_ref, s_ref, p_ref, m_ref, a_ref, acc_ref,
                 *, bq, causal, n_valid_last):
    i = pl.program_id(1)
    n_cc = qt_ref.shape[2]
    n_kt = kcat_ref.shape[1]
    n_int = i if causal else n_kt - 1

    def scores(j):
        kt = kcat_ref[0, j]
        for c in range(n_cc):
            s_ref[c] = _dot(kt, qt_ref[0, 0, c])

    def softmax(c, mask):
        s = s_ref[c]
        if mask is not None:
            s = jnp.where(mask, s, NEG_BIG)
        m_prev = m_ref[c]
        m_new = jnp.maximum(m_prev, jnp.max(s, axis=0, keepdims=True))
        a_ref[c] = jnp.exp2(m_prev - m_new)
        m_ref[c] = m_new
        p_ref[c] = jnp.exp2(s - m_new).astype(BF16)

    def values(c, vt):
        acc_ref[c] = a_ref[c] * acc_ref[c] + _dot(vt, p_ref[c])

    km, vm = km_ref[...], vm_ref[...]
    sm = [_dot(km, qt_ref[0, 0, c]) for c in range(n_cc)]
    pm = []
    for c in range(n_cc):
        m0 = jnp.max(sm[c], axis=0, keepdims=True)
        m_ref[c] = m0
        pm.append(jnp.exp2(sm[c] - m0).astype(BF16))
    for c in range(n_cc):
        acc_ref[c] = _dot(vm, pm[c])
    a_ref[...] = jnp.ones(a_ref.shape, F32)
    p_ref[...] = jnp.zeros(p_ref.shape, BF16)
    scores(0)

    def key_step(j):
        kt_next = kcat_ref[0, j + 1]
        vt_prev = vt_ref[0, jnp.maximum(j - 1, 0)]
        for c in range(n_cc):
            values(c, vt_prev)
            softmax(c, None)
            s_ref[c] = _dot(kt_next, qt_ref[0, 0, c])

    def key_step_pair(jj, carry):
        key_step(2 * jj)
        key_step(2 * jj + 1)
        return carry

    lax.fori_loop(0, n_int // 2, key_step_pair, 0)
    if causal or n_int % 2:
        @pl.when(n_int % 2 == 1)
        def _():
            key_step(n_int - 1)

    row = lax.broadcasted_iota(jnp.int32, (KEY_TILE, KEY_TILE), 0)
    if causal:
        col = lax.broadcasted_iota(jnp.int32, (KEY_TILE, KEY_TILE), 1)
        mask = (row >> 6) <= (col >> 6)
    else:
        mask = row < n_valid_last
    vt_prev = vt_ref[0, jnp.maximum(n_int - 1, 0)]
    for c in range(n_cc):
        values(c, vt_prev)
    for c in range(n_cc):
        softmax(c, mask)
    vt_last = vt_ref[0, n_int]
    for c in range(n_cc):
        values(c, vt_last)

    groups_per_cc = KEY_TILE // LANES
    for c in range(n_cc):
        acc = acc_ref[c]
        o_t = acc[:MLA_KV_RANK] * (1.0 / acc[MLA_KV_RANK:MLA_KV_RANK + 1])
        for g in range(groups_per_cc):
            blk = o_t[:, g * LANES:(g + 1) * LANES].T.astype(BF16)
            col0 = c * KEY_TILE + g * LANES
            if bq >= LANES:
                hd, q0 = col0 // bq, col0 % bq
                o_ref[0, q0:q0 + LANES, hd * MLA_KV_RANK:(hd + 1) * MLA_KV_RANK] = blk
            else:
                for hl in range(LANES // bq):
                    hd = col0 // bq + hl
                    o_ref[0, :, hd * MLA_KV_RANK:(hd + 1) * MLA_KV_RANK] = blk[hl * bq:(hl + 1) * bq]


def _attn(qt, kcat, vt, km, vm, *, bq, causal, n_valid_last):
    bsz, nq, n_cc = qt.shape[:3]
    n_kt = kcat.shape[1]
    assert not causal or bq == KEY_TILE
    return pl.pallas_call(
        functools.partial(_attn_kernel, bq=bq, causal=causal, n_valid_last=n_valid_last),
        grid=(bsz, nq),
        in_specs=[pl.BlockSpec((1, 1, n_cc, MLA_CAT, KEY_TILE), lambda b, i: (b, i, 0, 0, 0)),
                  pl.BlockSpec((1, n_kt, KEY_TILE, MLA_CAT), lambda b, i: (b, 0, 0, 0)),
                  pl.BlockSpec((1, n_kt, V_ROWS, KEY_TILE), lambda b, i: (b, 0, 0, 0)),
                  _const_spec(km.shape), _const_spec(vm.shape)],
        out_specs=pl.BlockSpec((1, bq, MLA_HEADS * MLA_KV_RANK), lambda b, i: (b, i, 0)),
        out_shape=jax.ShapeDtypeStruct((bsz, nq * bq, MLA_HEADS * MLA_KV_RANK), BF16),
        scratch_shapes=[pltpu.VMEM((n_cc, KEY_TILE, KEY_TILE), F32),
                        pltpu.VMEM((n_cc, KEY_TILE, KEY_TILE), BF16),
                        pltpu.VMEM((n_cc, 1, KEY_TILE), F32),
                        pltpu.VMEM((n_cc, 1, KEY_TILE), F32),
                        pltpu.VMEM((n_cc, V_ROWS, KEY_TILE), F32)],
        compiler_params=_cparams("parallel", "arbitrary"),
        name="attn",
    )(qt, kcat, vt, km, vm)


_MERGE_WEIGHTS = ("ln_g", "ln_b", "w_g", "gla_g", "w_br_gla", "w_uv_bd", "w_br_mla", "w_mg", "b_mg", "w_out",
                  "ln1_g", "ln1_b", "w_r_hi", "w_r_lo", "b_r")


def _merge_kernel(x_ref, og_ref, ol_ref, lng_ref, lnb_ref, wg_ref, glag_ref, wbg_ref, wuv_ref, wbm_ref,
                  wmg_ref, bmg_ref, wout_ref, l1g_ref, l1b_ref, wrh_ref, wrl_ref, br_ref,
                  h1t_ref, rt_ref, cnt_ref, carry_ref, *, sub):
    tt, d = x_ref.shape

    def branches(rows):
        h = _layer_norm(x_ref[rows, :], lng_ref[...], lnb_ref[...])
        hb = h.astype(BF16)
        g_out = _dot(hb, wg_ref[...])
        gate_pre = _dot(hb, wmg_ref[...])
        pair_in = 2 * MLA_KV_RANK
        y_heads = jnp.concatenate([_dot(ol_ref[rows, p * pair_in:(p + 1) * pair_in], wuv_ref[p])
                                   for p in range(MLA_HEADS // 2)], axis=-1)
        return h, g_out, gate_pre, y_heads

    def mix(rows, h, g_out, gate_pre, y_heads):
        og = og_ref[rows, :]
        parts = []
        for hd in range(GLA_HEADS):
            cols = slice(hd * GLA_DV, (hd + 1) * GLA_DV)
            o_h = og[:, cols]
            g_h = g_out[:, cols]
            o_n = o_h * lax.rsqrt(jnp.mean(o_h * o_h, axis=-1, keepdims=True) + RMS_EPS) * glag_ref[...]
            parts.append(o_n * (g_h * jax.nn.sigmoid(g_h)))
        y_a = _dot(jnp.concatenate(parts, axis=-1).astype(BF16), wbg_ref[...])
        y_b = _dot(y_heads.astype(BF16), wbm_ref[...])
        gates = jax.nn.sigmoid(gate_pre + bmg_ref[...])
        mix_in = gates[:, :d] * y_a + gates[:, d:] * y_b
        return DEEPNORM_ALPHA * h + _dot(mix_in.astype(BF16), wout_ref[...])

    def route(r0, pre):
        h1 = _layer_norm(pre, l1g_ref[...], l1b_ref[...])
        for s in range(ROW_TILE):
            h1t_ref[pl.ds(r0 * ROW_TILE + s, sub, stride=ROW_TILE), :] = h1[:, s * LANES:(s + 1) * LANES]
        h1_hi = h1.astype(BF16)
        h1_lo = (h1 - h1_hi.astype(F32)).astype(BF16)
        logits = (_dot(h1_hi, wrh_ref[...]) + (_dot(h1_hi, wrl_ref[...]) + _dot(h1_lo, wrh_ref[...]))) + br_ref[...]
        lane = lax.broadcasted_iota(jnp.int32, logits.shape, 1)
        is_grp = lane < N_GROUPS
        gl = jnp.where(is_grp, logits, NEG_BIG)
        g_max = jnp.max(gl, axis=-1, keepdims=True)
        g_sel = jnp.min(jnp.where(gl == g_max, lane, ROUTER_LANES), axis=-1, keepdims=True)
        p_grp = 1.0 / jnp.sum(jnp.where(is_grp, jnp.exp(gl - g_max), 0.0), axis=-1, keepdims=True)
        e_lo = N_GROUPS + g_sel * EXPERTS_PER_GROUP
        in_grp = (lane >= e_lo) & (lane < e_lo + EXPERTS_PER_GROUP)
        el = jnp.where(in_grp, logits, NEG_BIG)
        v1 = jnp.max(el, axis=-1, keepdims=True)
        i1 = jnp.min(jnp.where(el == v1, lane, ROUTER_LANES), axis=-1, keepdims=True)
        el2 = jnp.where(lane == i1, NEG_BIG, el)
        v2 = jnp.max(el2, axis=-1, keepdims=True)
        i2 = jnp.min(jnp.where(el2 == v2, lane, ROUTER_LANES), axis=-1, keepdims=True)
        e2 = jnp.exp(v2 - v1)
        w1 = p_grp / (1.0 + e2)
        w2 = p_grp * e2 / (1.0 + e2)
        e1, e2i = i1 - N_GROUPS, i2 - N_GROUPS
        hot = ((lane == e1) | (lane == e2i)).astype(F32)
        earlier = (lax.broadcasted_iota(jnp.int32, (sub, sub), 0) > lax.broadcasted_iota(jnp.int32, (sub, sub), 1))
        before = carry_ref[...] + _dot(earlier.astype(BF16), hot.astype(BF16))
        r1 = jnp.sum(jnp.where(lane == e1, before, 0.0), axis=-1, keepdims=True)
        r2 = jnp.sum(jnp.where(lane == e2i, before, 0.0), axis=-1, keepdims=True)
        carry_ref[...] += jnp.sum(hot, axis=0, keepdims=True)
        fields = (e1.astype(F32), e2i.astype(F32), w1, w2, r1, r2)
        rt = jnp.zeros(logits.shape, F32)
        for k, val in enumerate(fields):
            rt = jnp.where(lane == k, val, rt)
        rt_ref[pl.ds(r0, sub), :] = rt

    @pl.when(pl.program_id(0) == 0)
    def _():
        carry_ref[...] = jnp.zeros(carry_ref.shape, F32)

    pending = None
    for r0 in range(0, tt, sub):
        rows = pl.ds(r0, sub)
        independent = branches(rows)
        if pending is not None:
            route(*pending)
        pending = (r0, mix(rows, *independent))
    route(*pending)
    cnt_ref[...] = carry_ref[...]


def _merge(x2, og2, ol2, wp, *, tt):
    t, d = x2.shape
    assert d == ROW_TILE * LANES
    row = lambda w: pl.BlockSpec((tt, w), lambda i: (i, 0))
    return pl.pallas_call(
        functools.partial(_merge_kernel, sub=min(TOKEN_TILE, tt)),
        grid=(t // tt,),
        in_specs=[row(d), row(GLA_V), row(MLA_HEADS * MLA_KV_RANK)] + [_const_spec(wp[n].shape) for n in _MERGE_WEIGHTS],
        out_specs=[pl.BlockSpec((tt * ROW_TILE, LANES), lambda i: (i, 0)), row(ROUTER_LANES),
                   pl.BlockSpec((1, ROUTER_LANES), lambda i: (0, 0))],
        out_shape=[jax.ShapeDtypeStruct((t * ROW_TILE, LANES), F32), jax.ShapeDtypeStruct((t, ROUTER_LANES), F32),
                   jax.ShapeDtypeStruct((1, ROUTER_LANES), F32)],
        scratch_shapes=[pltpu.VMEM((1, ROUTER_LANES), F32)],
        compiler_params=_cparams("arbitrary"),
        name="merge",
    )(x2, og2, ol2, *[wp[n] for n in _MERGE_WEIGHTS])


def _row_copy(src_ref, src_row, dst_ref, dst_row, sem):
    return pltpu.make_async_copy(src_ref.at[pl.ds(src_row * ROW_TILE, ROW_TILE)],
                                 dst_ref.at[pl.ds(dst_row * ROW_TILE, ROW_TILE)], sem)


def _dispatch_kernel(pos_ref, zrow_ref, nu_ref, h1t_ref, xs_ref, zero_ref, sem, zsem, *, td):
    i = pl.program_id(0)

    @pl.when(i == 0)
    def _():
        zero_ref[...] = jnp.zeros(zero_ref.shape, F32)
        tile_rows = EXPERT_ROWS * ROW_TILE
        fill = lambda row: pltpu.make_async_copy(
            zero_ref, xs_ref.at[pl.ds(pl.multiple_of(row * ROW_TILE, ROW_TILE), tile_rows)], zsem)
        for e in range(N_EXPERTS):
            @pl.when(zrow_ref[e] >= 0)
            def _():
                fill(zrow_ref[e]).start()
        for e in range(N_EXPERTS):
            @pl.when(zrow_ref[e] >= 0)
            def _():
                fill(zrow_ref[e]).wait()

        def fill_tail(r, carry):
            fill(r * EXPERT_ROWS).start()
            fill(r * EXPERT_ROWS).wait()
            return carry

        lax.fori_loop(nu_ref[0], xs_ref.shape[0] // tile_rows, fill_tail, 0)

    base = i * (2 * td)
    for t in range(td):
        for k in range(2):
            dst = pl.multiple_of(pos_ref[base + 2 * t + k] * ROW_TILE, ROW_TILE)
            pltpu.make_async_copy(h1t_ref.at[pl.ds(t * ROW_TILE, ROW_TILE)],
                                  xs_ref.at[pl.ds(dst, ROW_TILE)], sem).start(priority=k)
    for _ in range(2 * td):
        _row_copy(h1t_ref, 0, xs_ref, 0, sem).wait()


def _dispatch(h1t, pos, zrow, n_used, *, n_rows, td):
    t = h1t.shape[0] // ROW_TILE
    return pl.pallas_call(
        functools.partial(_dispatch_kernel, td=td),
        grid_spec=pltpu.PrefetchScalarGridSpec(
            num_scalar_prefetch=3,
            grid=(t // td,),
            in_specs=[pl.BlockSpec((td * ROW_TILE, LANES), lambda i, *_: (i, 0))],
            out_specs=pl.BlockSpec(memory_space=pl.ANY),
            scratch_shapes=[pltpu.VMEM((EXPERT_ROWS * ROW_TILE, LANES), F32),
                            pltpu.SemaphoreType.DMA(()), pltpu.SemaphoreType.DMA(())]),
        out_shape=jax.ShapeDtypeStruct((n_rows * ROW_TILE, LANES), F32),
        compiler_params=_cparams("arbitrary"),
        name="dispatch",
    )(pos, zrow, n_used, h1t)


def _experts_kernel(te_ref, tb_ref, nu_ref, xs_ref, wg_ref, wu_ref, wd_ref, out_ref):
    del te_ref, tb_ref
    rows = xs_ref.shape[0] // ROW_TILE

    @pl.when(pl.program_id(0) < nu_ref[0])
    def _():
        x = jnp.concatenate([xs_ref[pl.ds(s, rows, stride=ROW_TILE), :] for s in range(ROW_TILE)], axis=1).astype(BF16)
        gate = _dot(x, wg_ref[0].astype(BF16))
        up = _dot(x, wu_ref[0].astype(BF16))
        hid = (gate * jax.nn.sigmoid(gate)) * up
        out = _dot(hid.astype(BF16), wd_ref[0].astype(BF16))
        for s in range(ROW_TILE):
            out_ref[pl.ds(s, rows, stride=ROW_TILE), :] = out[:, s * LANES:(s + 1) * LANES]

    @pl.when(pl.program_id(0) >= nu_ref[0])
    def _():
        out_ref[...] = jnp.zeros(out_ref.shape, F32)


def _experts(xs, tile_e, tile_blk, n_used, wp):
    n_rows = xs.shape[0] // ROW_TILE
    d = ROW_TILE * LANES
    blk = pl.BlockSpec((EXPERT_ROWS * ROW_TILE, LANES), lambda r, te, tb, nu: (tb[r], 0))
    return pl.pallas_call(
        _experts_kernel,
        grid_spec=pltpu.PrefetchScalarGridSpec(
            num_scalar_prefetch=3,
            grid=(n_rows // EXPERT_ROWS,),
            in_specs=[blk,
                      pl.BlockSpec((1, d, D_EXPERT), lambda r, te, tb, nu: (te[r], 0, 0)),
                      pl.BlockSpec((1, d, D_EXPERT), lambda r, te, tb, nu: (te[r], 0, 0)),
                      pl.BlockSpec((1, D_EXPERT, d), lambda r, te, tb, nu: (te[r], 0, 0))],
            out_specs=pl.BlockSpec((EXPERT_ROWS * ROW_TILE, LANES), lambda r, te, tb, nu: (r, 0))),
        out_shape=jax.ShapeDtypeStruct(xs.shape, F32),
        compiler_params=_cparams("arbitrary"),
        name="experts",
    )(tile_e, tile_blk, n_used, xs, wp["w_gate"], wp["w_up"], wp["w_down"])


def _combine_kernel(pos_ref, h1t_ref, rt_ref, outs_ref, l2g_ref, l2b_ref, y_ref, g0_ref, g1_ref, sem, *, tc):
    i = pl.program_id(0)
    base = i * (2 * tc)
    for t in range(tc):
        for k, g_ref in enumerate((g0_ref, g1_ref)):
            src = pl.multiple_of(pos_ref[base + 2 * t + k] * ROW_TILE, ROW_TILE)
            pltpu.make_async_copy(outs_ref.at[pl.ds(src, ROW_TILE)],
                                  g_ref.at[pl.ds(t * ROW_TILE, ROW_TILE)], sem).start(priority=k)
    for _ in range(2 * tc):
        _row_copy(outs_ref, 0, g0_ref, 0, sem).wait()
    rt = rt_ref[...]
    w0, w1 = rt[:, RT_W:RT_W + 1], rt[:, RT_W + 1:RT_W + 2]
    cols = []
    for s in range(ROW_TILE):
        rows = pl.ds(s, tc, stride=ROW_TILE)
        cols.append(DEEPNORM_ALPHA * h1t_ref[rows, :] + (w0 * g0_ref[rows, :] + w1 * g1_ref[rows, :]))
    y_ref[...] = _layer_norm(jnp.concatenate(cols, axis=1), l2g_ref[...], l2b_ref[...])


def _combine(h1t, rt, outs, pos, wp, *, tc):
    t = rt.shape[0]
    d = ROW_TILE * LANES
    return pl.pallas_call(
        functools.partial(_combine_kernel, tc=tc),
        grid_spec=pltpu.PrefetchScalarGridSpec(
            num_scalar_prefetch=1,
            grid=(t // tc,),
            in_specs=[pl.BlockSpec((tc * ROW_TILE, LANES), lambda i, *_: (i, 0)),
                      pl.BlockSpec((tc, ROUTER_LANES), lambda i, *_: (i, 0)),
                      pl.BlockSpec(memory_space=pl.ANY),
                      pl.BlockSpec((1, d), lambda i, *_: (0, 0)), pl.BlockSpec((1, d), lambda i, *_: (0, 0))],
            out_specs=pl.BlockSpec((tc, d), lambda i, *_: (i, 0)),
            scratch_shapes=[pltpu.VMEM((tc * ROW_TILE, LANES), F32), pltpu.VMEM((tc * ROW_TILE, LANES), F32),
                            pltpu.SemaphoreType.DMA(())]),
        out_shape=jax.ShapeDtypeStruct((t, d), F32),
        compiler_params=_cparams("arbitrary"),
        name="combine",
    )(pos, h1t, rt, outs, wp["ln2_g"], wp["ln2_b"])


def _route_plan(rt, cnt):
    t = rt.shape[0]
    n_tiles = (2 * t) // EXPERT_ROWS + N_EXPERTS
    counts = cnt[0, :N_EXPERTS].astype(jnp.int32)
    padded = (counts + (EXPERT_ROWS - 1)) // EXPERT_ROWS * EXPERT_ROWS
    ends = jnp.cumsum(padded)
    starts = ends - padded
    eid = rt[:, RT_E:RT_E + 2].astype(jnp.int32)
    rank = rt[:, RT_RANK:RT_RANK + 2].astype(jnp.int32)
    pos = (jnp.take(starts, eid) + rank).reshape(-1)
    n_used = ends[-1] // EXPERT_ROWS
    tile_blk = jnp.minimum(jnp.arange(n_tiles, dtype=jnp.int32), n_used - 1)
    tile_e = jnp.sum((tile_blk[:, None] * EXPERT_ROWS >= ends[None, :]).astype(jnp.int32), axis=1)
    zrow = jnp.where(padded > 0, ends - EXPERT_ROWS, -1)
    return pos, zrow.astype(jnp.int32), tile_e.astype(jnp.int32), tile_blk, n_used.reshape(1).astype(jnp.int32), n_tiles


def _rope_tables(pos):
    inv = ROPE_THETA ** (-jnp.arange(0, MLA_ROPE, 2, dtype=F32) / MLA_ROPE)
    ang = pos.astype(F32)[:, None] * inv[None, :]
    cs = jnp.concatenate([jnp.cos(ang), jnp.sin(ang)], axis=-1)
    return cs[None], cs.T[None]


def _prep_weights(ln_in_g, ln_in_b, w_in, w_gk2, b_gk, gla_norm_g, q_norm_g, kv_norm_g, w_uq, w_uk, w_uv,
                  w_br_gla, w_br_mla, w_mg, b_mg, w_out, ln1_g, ln1_b, w_rg, b_rg, w_re, b_re,
                  w_gate, w_up, w_down, ln2_g, ln2_b):
    d = w_in.shape[1]
    w = w_in[0]
    c0 = 0
    wk = w[:, c0:c0 + GLA_QK]; c0 += GLA_QK
    wv = w[:, c0:c0 + GLA_V]; c0 += GLA_V
    wgr = w[:, c0:c0 + GLA_RANK]; c0 += GLA_RANK
    wckv = w[:, c0:c0 + MLA_KV_RANK]; c0 += MLA_KV_RANK
    wkr = w[:, c0:c0 + MLA_ROPE]; c0 += MLA_ROPE
    wq = w[:, c0:c0 + GLA_QK]; c0 += GLA_QK
    wg = w[:, c0:c0 + GLA_V]; c0 += GLA_V
    wcq = w[:, c0:c0 + MLA_Q_RANK]
    w_tok = jnp.concatenate([wk, wv, wq, wckv, wkr, wgr, jnp.zeros((d, TOK_PAD - TOK_USED), F32)], axis=1)
    w_tr = jnp.concatenate([wckv, wcq], axis=1).T
    uq = w_uq[0].reshape(MLA_Q_RANK, MLA_HEADS, MLA_QK_DIM)
    uq_perm = jnp.concatenate([
        uq[:, :, :MLA_NOPE].reshape(MLA_Q_RANK, -1),
        uq[:, :, MLA_NOPE:MLA_NOPE + MLA_HALF].reshape(MLA_Q_RANK, -1),
        uq[:, :, MLA_NOPE + MLA_HALF:].reshape(MLA_Q_RANK, -1)], axis=1)
    uv = w_uv[0].transpose(1, 0, 2).reshape(MLA_HEADS // 2, 2, MLA_KV_RANK, MLA_DV)
    eye = jnp.eye(2, dtype=F32)
    w_uv_bd = (uv[:, :, :, None, :] * eye[None, :, None, :, None]).reshape(
        MLA_HEADS // 2, 2 * MLA_KV_RANK, 2 * MLA_DV)
    w_r = jnp.concatenate([w_rg[0], w_re[0].transpose(1, 0, 2).reshape(d, N_EXPERTS),
                           jnp.zeros((d, ROUTER_LANES - N_GROUPS - N_EXPERTS), F32)], axis=1)
    w_r_hi = w_r.astype(BF16)
    b_r = jnp.concatenate([b_rg[0], b_re[0].reshape(-1), jnp.zeros((ROUTER_LANES - N_GROUPS - N_EXPERTS,), F32)])
    row = lambda a: a.reshape(1, -1)
    return {
        "ln_g": row(ln_in_g), "ln_b": row(ln_in_b),
        "w_tok": w_tok.astype(BF16), "w_tr": w_tr.astype(BF16),
        "w_gk2": w_gk2[0].astype(BF16), "b_gk": row(b_gk[0]),
        "kv_g": row(kv_norm_g[0]), "kv_gt": kv_norm_g[0].reshape(-1, 1), "q_gt": q_norm_g[0].reshape(-1, 1),
        "w_uqt": uq_perm.T.astype(BF16), "w_uk": w_uk[0].transpose(1, 0, 2).astype(BF16),
        "w_g": wg.astype(BF16), "gla_g": row(gla_norm_g[0]),
        "w_br_gla": w_br_gla[0].astype(BF16), "w_uv_bd": w_uv_bd.astype(BF16), "w_br_mla": w_br_mla[0].astype(BF16),
        "w_mg": w_mg[0].astype(BF16), "b_mg": row(b_mg[0]), "w_out": w_out[0].astype(BF16),
        "ln1_g": row(ln1_g[0]), "ln1_b": row(ln1_b[0]),
        "w_r_hi": w_r_hi, "w_r_lo": (w_r - w_r_hi.astype(F32)).astype(BF16), "b_r": row(b_r),
        "w_gate": w_gate[0], "w_up": w_up[0], "w_down": w_down[0],
        "ln2_g": row(ln2_g[0]), "ln2_b": row(ln2_b[0]),
    }


def _value_rows(vt):
    lead, length = vt.shape[:-2], vt.shape[-1]
    return jnp.concatenate([vt, jnp.ones(lead + (1, length), vt.dtype),
                            jnp.zeros(lead + (V_ROWS - MLA_KV_RANK - 1, length), vt.dtype)], axis=-2)


def _key_tiles(kcat, vt):
    bsz, length, _ = kcat.shape
    n = -(-length // KEY_TILE)
    pad = n * KEY_TILE - length
    kcat = jnp.pad(kcat, ((0, 0), (0, pad), (0, 0)))
    vt = jnp.pad(_value_rows(vt), ((0, 0), (0, 0), (0, pad)))
    return (kcat.reshape(bsz, n, KEY_TILE, MLA_CAT),
            vt.reshape(bsz, V_ROWS, n, KEY_TILE).transpose(0, 2, 1, 3))


def _ffn(x, og, ol, wp):
    bsz, s, d = x.shape
    t = bsz * s
    h1t, rt, cnt = _merge(x.reshape(t, d), og.reshape(t, -1), ol.reshape(t, -1), wp, tt=min(MERGE_TILE, t))
    pos, zrow, tile_e, tile_blk, n_used, n_tiles = _route_plan(rt, cnt)
    tile = min(MOE_TOKEN_TILE, t)
    xs = _dispatch(h1t, pos, zrow, n_used, n_rows=n_tiles * EXPERT_ROWS, td=tile)
    outs = _experts(xs, tile_e, tile_blk, n_used, wp)
    return _combine(h1t, rt, outs, pos, wp, tc=tile).reshape(bsz, s, d)


def kernel(x_prompt, x_sample, cache_mla_latent, cache_mla_krope, state_gla, meta_tokens, ln_in_g, ln_in_b, w_in, w_gk2, b_gk, gla_norm_g, q_norm_g, kv_norm_g, w_uq, w_uk, w_uv, w_br_gla, w_br_mla, w_mg, b_mg, w_out, ln1_g, ln1_b, w_rg, b_rg, w_re, b_re, w_gate, w_up, w_down, ln2_g, ln2_b):
    bp, sp, d = x_prompt.shape
    bs, ss, _ = x_sample.shape
    past = cache_mla_latent.shape[2]
    wp = _prep_weights(ln_in_g, ln_in_b, w_in, w_gk2, b_gk, gla_norm_g, q_norm_g, kv_norm_g, w_uq, w_uk, w_uv,
                       w_br_gla, w_br_mla, w_mg, b_mg, w_out, ln1_g, ln1_b, w_rg, b_rg, w_re, b_re,
                       w_gate, w_up, w_down, ln2_g, ln2_b)

    cs_m, cst_m = _rope_tables(jnp.arange(-N_META, 0, dtype=jnp.int32))
    m = _proj(meta_tokens[None], cs_m, cst_m, wp, cl=N_META, tt=N_META)
    _, _, m_kt, m_v, m_ebl, m_ckv, m_kr, m_kcat, m_vt, _ = m
    zero_state = jnp.zeros((1, GLA_HEADS, GLA_DK, GLA_DV), F32)
    _, m_state = _gla(m[0], m[1], m_kt, m_v, m_ebl, zero_state, cl=N_META, ts=N_META)

    cs_p, cst_p = _rope_tables(jnp.arange(sp, dtype=jnp.int32))
    p_qd, p_ki, p_kt, p_v, p_ebl, p_ckv, p_kr, p_kcat, p_vt, p_qt = _proj(
        x_prompt, cs_p, cst_p, wp, cl=CHUNK, tt=TOKEN_TILE)
    p_o, p_state = _gla(p_qd, p_ki, p_kt, p_v, p_ebl, m_state, cl=CHUNK, ts=TOKEN_TILE)
    rep = lambda a, n: jnp.broadcast_to(a, (n,) + a.shape[1:])
    lat_p = jnp.concatenate([rep(m_ckv, bp), p_ckv], axis=1)
    kr_p = jnp.concatenate([rep(m_kr, bp), p_kr], axis=1)
    kcat_p, vt_p = _key_tiles(p_kcat, p_vt)
    km, vm = m_kcat[0], _value_rows(m_vt[0])
    p_ol = _attn(p_qt, kcat_p, vt_p, km, vm, bq=TOKEN_TILE, causal=True, n_valid_last=KEY_TILE)
    y_prompt = _ffn(x_prompt, p_o, p_ol, wp)

    ts_all = bs * ss
    cs_s, cst_s = _rope_tables(past + (jnp.arange(ts_all, dtype=jnp.int32) % ss))
    s_qd, s_ki, s_kt, s_v, s_ebl, s_ckv, s_kr, s_kcat, s_vt, s_qt = _proj(
        x_sample.reshape(1, ts_all, d), cs_s, cst_s, wp, cl=ss, tt=min(TOKEN_TILE, ts_all))
    per_stream = lambda a: a.reshape(bs, ss, a.shape[-1])
    s_o, s_state = _gla(per_stream(s_qd), per_stream(s_ki), per_stream(s_kt), per_stream(s_v),
                        s_ebl.reshape(bs, 1, 1, GLA_QK), state_gla[0].astype(F32), cl=ss, ts=ss)
    s_ckv, s_kr = per_stream(s_ckv), per_stream(s_kr)
    cache_kcat = jnp.concatenate([cache_mla_latent[0], cache_mla_krope[0]], axis=-1).astype(BF16)
    cache_vt = cache_mla_latent[0].astype(BF16).transpose(0, 2, 1)
    new_vt = s_vt.reshape(MLA_KV_RANK, bs, ss).transpose(1, 0, 2)
    kcat_s, vt_s = _key_tiles(jnp.concatenate([cache_kcat, per_stream(s_kcat)], axis=1),
                              jnp.concatenate([cache_vt, new_vt], axis=2))
    qt = s_qt.transpose(0, 2, 3, 1, 4).reshape(MLA_HEADS, MLA_CAT, bs, ss)
    qt = qt.transpose(2, 1, 0, 3).reshape(bs, MLA_CAT, MLA_HEADS * ss // KEY_TILE, KEY_TILE)
    qt = qt.transpose(0, 2, 1, 3)[:, None]
    s_ol = _attn(qt, kcat_s, vt_s, km, vm, bq=ss, causal=False, n_valid_last=(past + ss - 1) % KEY_TILE + 1)
    y_sample = _ffn(x_sample, s_o, s_ol, wp)

    return (y_prompt, y_sample, lat_p[None], kr_p[None], p_state[None].astype(state_gla.dtype),
            s_ckv[None], s_kr[None], s_state[None].astype(state_gla.dtype))
```

```python
import functools

import jax
import jax.numpy as jnp
from jax import lax
from jax.experimental import pallas as pl
from jax.experimental.pallas import tpu as pltpu

F32 = jnp.float32
BF16 = jnp.bfloat16

CHUNK = 64
N_META = 16
GLA_HEADS = 4
GLA_DK = 128
GLA_DV = 256
GLA_RANK = 16
GLA_TAU = 16.0
GLA_QK = GLA_HEADS * GLA_DK
GLA_V = GLA_HEADS * GLA_DV
GLA_SCALE = GLA_DK ** -0.5
MLA_HEADS = 16
MLA_Q_RANK = 384
MLA_KV_RANK = 128
MLA_NOPE = 64
MLA_ROPE = 32
MLA_HALF = MLA_ROPE // 2
MLA_DV = 64
MLA_QK_DIM = MLA_NOPE + MLA_ROPE
MLA_CAT = MLA_KV_RANK + MLA_ROPE
MLA_SCALE = MLA_QK_DIM ** -0.5
LOG2_E = 1.4426950408889634
Q_SCALE = MLA_SCALE * LOG2_E
V_ROWS = MLA_KV_RANK + 16
ROPE_THETA = 10000.0
N_GROUPS = 4
EXPERTS_PER_GROUP = 8
N_EXPERTS = N_GROUPS * EXPERTS_PER_GROUP
D_EXPERT = 256
LN_EPS = 1e-5
RMS_EPS = 1e-6
DEEPNORM_ALPHA = 2.0 ** 0.25

LANES = 128
MXU_DIM = 256
VMEM_LIMIT_BYTES = 56 * 1024 * 1024

TOKEN_TILE = 256
MERGE_TILE = 2 * TOKEN_TILE
KEY_TILE = MXU_DIM
MOE_TOKEN_TILE = 256
EXPERT_ROWS = 256
ROW_TILE = 8
ROUTER_LANES = LANES
RT_E, RT_W, RT_RANK = 0, 2, 4
NEG_BIG = -1e30

NT_DIMS = (((1,), (1,)), ((), ()))
TN_DIMS = (((0,), (0,)), ((), ()))


def _cparams(*sem):
    return pltpu.CompilerParams(dimension_semantics=sem, vmem_limit_bytes=VMEM_LIMIT_BYTES)


def _const_spec(shape):
    nd = len(shape)
    return pl.BlockSpec(shape, lambda *_: (0,) * nd, pipeline_mode=pl.Buffered(1))


def _layer_norm(x, g, b):
    mu = jnp.mean(x, axis=-1, keepdims=True)
    xc = x - mu
    var = jnp.mean(xc * xc, axis=-1, keepdims=True)
    return xc * lax.rsqrt(var + LN_EPS) * g + b


def _dot(a, b):
    return jnp.dot(a, b, preferred_element_type=F32)


TOK_COLS = (GLA_QK, GLA_V, GLA_QK, MLA_KV_RANK, MLA_ROPE, GLA_RANK)
TOK_USED = sum(TOK_COLS)
TOK_PAD = -(-TOK_USED // MXU_DIM) * MXU_DIM
TR_ROWS = MLA_KV_RANK + MLA_Q_RANK


def _proj_kernel(x_ref, cs_ref, cst_ref, lng_ref, lnb_ref, wtok_ref, wtr_ref, wgk2_ref, bgk_ref,
                 kvg_ref, kvgt_ref, qngt_ref, wuqt_ref, wuk_ref,
                 qd_ref, ki_ref, kt_ref, v_ref, ebl_ref, ckv_ref, kr_ref, kcat_ref, vt_ref, qt_ref, *, cl):
    tt = x_ref.shape[1]
    n_chunks = tt // cl
    h = _layer_norm(x_ref[0], lng_ref[...], lnb_ref[...])
    hb = h.astype(BF16)

    z = _dot(hb, wtok_ref[...])
    zt = lax.dot_general(wtr_ref[...], hb, NT_DIMS, preferred_element_type=F32)
    o0 = 0
    k = z[:, o0:o0 + GLA_QK]; o0 += GLA_QK
    v = z[:, o0:o0 + GLA_V]; o0 += GLA_V
    q = z[:, o0:o0 + GLA_QK]; o0 += GLA_QK
    ckv_raw = z[:, o0:o0 + MLA_KV_RANK]; o0 += MLA_KV_RANK
    kr = z[:, o0:o0 + MLA_ROPE]; o0 += MLA_ROPE
    gklr = z[:, o0:o0 + GLA_RANK]

    gz = _dot(gklr.astype(BF16), wgk2_ref[...]) + bgk_ref[...]
    gk = (jnp.minimum(gz, 0.0) - jnp.log(1.0 + jnp.exp(-jnp.abs(gz)))) * (1.0 / GLA_TAU)
    row_in_chunk = lax.broadcasted_iota(jnp.int32, gk.shape, 0) & (cl - 1)
    b = gk
    shift = 1
    while shift < cl:
        b = b + jnp.where(row_in_chunk >= shift, pltpu.roll(b, shift, 0), 0.0)
        shift *= 2
    b3 = b.reshape(n_chunks, cl, GLA_QK)
    bl = b3[:, cl - 1:cl, :]
    qd_ref[0] = (q * GLA_SCALE * jnp.exp(b)).astype(BF16)
    ki_ref[0] = (k * jnp.exp(-b)).astype(BF16)
    kt_ref[0] = (k.reshape(n_chunks, cl, GLA_QK) * jnp.exp(bl - b3)).reshape(tt, GLA_QK).astype(BF16)
    v_ref[0] = v.astype(BF16)
    ebl_ref[0] = jnp.exp(bl)

    ckv = ckv_raw * lax.rsqrt(jnp.mean(ckv_raw * ckv_raw, axis=-1, keepdims=True) + RMS_EPS) * kvg_ref[...]
    cs = cs_ref[0]
    cos, sin = cs[:, :MLA_HALF], cs[:, MLA_HALF:]
    x1, x2 = kr[:, :MLA_HALF], kr[:, MLA_HALF:]
    kr_rot = jnp.concatenate([x1 * cos - x2 * sin, x2 * cos + x1 * sin], axis=-1)
    ckv_ref[0] = ckv
    kr_ref[0] = kr_rot
    kcat_ref[0, :, :MLA_KV_RANK] = ckv.astype(BF16)
    kcat_ref[0, :, MLA_KV_RANK:] = kr_rot.astype(BF16)

    ckvt = zt[:MLA_KV_RANK]
    ckvt = ckvt * lax.rsqrt(jnp.mean(ckvt * ckvt, axis=0, keepdims=True) + RMS_EPS) * kvgt_ref[...]
    vt_ref[0] = ckvt.astype(BF16)
    cqt = zt[MLA_KV_RANK:]
    cqt = cqt * lax.rsqrt(jnp.mean(cqt * cqt, axis=0, keepdims=True) + RMS_EPS) * qngt_ref[...]
    qmt = _dot(wuqt_ref[...], cqt.astype(BF16))
    n_nope = MLA_HEADS * MLA_NOPE
    n_half = MLA_HEADS * MLA_HALF
    cst = cst_ref[0]
    cos_t = jnp.concatenate([cst[:MLA_HALF]] * MLA_HEADS, axis=0)
    sin_t = jnp.concatenate([cst[MLA_HALF:]] * MLA_HEADS, axis=0)
    r1 = qmt[n_nope:n_nope + n_half]
    r2 = qmt[n_nope + n_half:]
    rot1 = ((r1 * cos_t - r2 * sin_t) * Q_SCALE).astype(BF16)
    rot2 = ((r2 * cos_t + r1 * sin_t) * Q_SCALE).astype(BF16)
    for hd in range(MLA_HEADS):
        nope = qmt[hd * MLA_NOPE:(hd + 1) * MLA_NOPE].astype(BF16)
        qlat = _dot(wuk_ref[hd], nope) * Q_SCALE
        qt_ref[0, 0, hd, :MLA_KV_RANK, :] = qlat.astype(BF16)
        qt_ref[0, 0, hd, MLA_KV_RANK:MLA_KV_RANK + MLA_HALF, :] = rot1[hd * MLA_HALF:(hd + 1) * MLA_HALF]
        qt_ref[0, 0, hd, MLA_KV_RANK + MLA_HALF:, :] = rot2[hd * MLA_HALF:(hd + 1) * MLA_HALF]


def _proj(x, cs, cst, wp, *, cl, tt):
    bsz, s, d = x.shape
    nt = s // tt
    n_chunks = tt // cl
    tok = lambda w: pl.BlockSpec((1, tt, w), lambda b, t: (b, t, 0))
    in_specs = [
        tok(d),
        pl.BlockSpec((1, tt, MLA_ROPE), lambda b, t: (0, t, 0)),
        pl.BlockSpec((1, MLA_ROPE, tt), lambda b, t: (0, 0, t)),
    ] + [_const_spec(wp[n].shape) for n in _PROJ_WEIGHTS]
    out_shape = [
        jax.ShapeDtypeStruct((bsz, s, GLA_QK), BF16),
        jax.ShapeDtypeStruct((bsz, s, GLA_QK), BF16),
        jax.ShapeDtypeStruct((bsz, s, GLA_QK), BF16),
        jax.ShapeDtypeStruct((bsz, s, GLA_V), BF16),
        jax.ShapeDtypeStruct((bsz, s // cl, 1, GLA_QK), F32),
        jax.ShapeDtypeStruct((bsz, s, MLA_KV_RANK), F32),
        jax.ShapeDtypeStruct((bsz, s, MLA_ROPE), F32),
        jax.ShapeDtypeStruct((bsz, s, MLA_CAT), BF16),
        jax.ShapeDtypeStruct((bsz, MLA_KV_RANK, s), BF16),
        jax.ShapeDtypeStruct((bsz, nt, MLA_HEADS, MLA_CAT, tt), BF16),
    ]
    out_specs = [
        tok(GLA_QK), tok(GLA_QK), tok(GLA_QK), tok(GLA_V),
        pl.BlockSpec((1, n_chunks, 1, GLA_QK), lambda b, t: (b, t, 0, 0)),
        tok(MLA_KV_RANK), tok(MLA_ROPE), tok(MLA_CAT),
        pl.BlockSpec((1, MLA_KV_RANK, tt), lambda b, t: (b, 0, t)),
        pl.BlockSpec((1, 1, MLA_HEADS, MLA_CAT, tt), lambda b, t: (b, t, 0, 0, 0)),
    ]
    return pl.pallas_call(
        functools.partial(_proj_kernel, cl=cl),
        grid=(bsz, nt),
        in_specs=in_specs,
        out_specs=out_specs,
        out_shape=out_shape,
        compiler_params=_cparams("parallel", "parallel"),
        name="proj",
    )(x, cs, cst, *[wp[n] for n in _PROJ_WEIGHTS])


_PROJ_WEIGHTS = ("ln_g", "ln_b", "w_tok", "w_tr", "w_gk2", "b_gk", "kv_g", "kv_gt", "q_gt", "w_uqt", "w_uk")


def _gla_kernel(qd_ref, ki_ref, kt_ref, v_ref, ebl_ref, s0_ref, o_ref, sfin_ref, st_ref, *, cl):
    t = pl.program_id(1)
    ts = qd_ref.shape[1]

    @pl.when(t == 0)
    def _():
        for hd in range(GLA_HEADS):
            st_ref[hd] = s0_ref[0, hd].T

    row = lax.broadcasted_iota(jnp.int32, (ts, ts), 0)
    col = lax.broadcasted_iota(jnp.int32, (ts, ts), 1)
    keep = (row >= col) & ((row & -cl) == (col & -cl))
    qk = lambda ref, hd: ref[0, :, hd * GLA_DK:(hd + 1) * GLA_DK]
    val = lambda hd: v_ref[0, :, hd * GLA_DV:(hd + 1) * GLA_DV]
    for hd in range(GLA_HEADS):
        a = lax.dot_general(qk(qd_ref, hd), qk(ki_ref, hd), NT_DIMS, preferred_element_type=F32)
        a = jnp.where(keep, a, 0.0).astype(BF16)
        o_ref[0, :, hd * GLA_DV:(hd + 1) * GLA_DV] = _dot(a, val(hd))
    for c in range(ts // cl):
        rows = slice(c * cl, (c + 1) * cl)
        for hd in range(GLA_HEADS):
            st = st_ref[hd]
            o_ref[0, rows, hd * GLA_DV:(hd + 1) * GLA_DV] += lax.dot_general(
                qk(qd_ref, hd)[rows], st.astype(BF16), NT_DIMS, preferred_element_type=F32)
            st_ref[hd] = (st * ebl_ref[0, c, :, hd * GLA_DK:(hd + 1) * GLA_DK]
                          + lax.dot_general(val(hd)[rows], qk(kt_ref, hd)[rows], TN_DIMS, preferred_element_type=F32))

    @pl.when(t == pl.num_programs(1) - 1)
    def _():
        for hd in range(GLA_HEADS):
            sfin_ref[0, hd] = st_ref[hd].T


def _gla(qd, ki, kt, v, ebl, s0, *, cl, ts):
    bsz, s, _ = qd.shape
    n_chunks = ts // cl
    s0_b = s0.shape[0]
    qk_spec = pl.BlockSpec((1, ts, GLA_QK), lambda b, t: (b, t, 0))
    v_spec = pl.BlockSpec((1, ts, GLA_V), lambda b, t: (b, t, 0))
    st_spec = pl.BlockSpec((1, GLA_HEADS, GLA_DK, GLA_DV), lambda b, t: (b, 0, 0, 0))
    s0_spec = st_spec if s0_b == bsz else pl.BlockSpec((1, GLA_HEADS, GLA_DK, GLA_DV), lambda b, t: (0, 0, 0, 0))
    return pl.pallas_call(
        functools.partial(_gla_kernel, cl=cl),
        grid=(bsz, s // ts),
        in_specs=[qk_spec, qk_spec, qk_spec, v_spec,
                  pl.BlockSpec((1, n_chunks, 1, GLA_QK), lambda b, t: (b, t, 0, 0)),
                  s0_spec],
        out_specs=[v_spec, st_spec],
        out_shape=[jax.ShapeDtypeStruct((bsz, s, GLA_V), F32),
                   jax.ShapeDtypeStruct((bsz, GLA_HEADS, GLA_DK, GLA_DV), F32)],
        scratch_shapes=[pltpu.VMEM((GLA_HEADS, GLA_DV, GLA_DK), F32)],
        compiler_params=_cparams("parallel", "arbitrary"),
        name="gla",
    )(qd, ki, kt, v, ebl, s0)


def _attn_kernel(qt_ref, kcat_ref, vt_ref, km_ref, vm_ref, o_ref, s_ref, p_ref, m_ref, a_ref, acc_ref,
                 *, bq, causal, n_valid_last):
    i = pl.program_id(1)
    n_cc = qt_ref.shape[2]
    n_kt = kcat_ref.shape[1]
    n_int = i if causal else n_kt - 1

    def scores(j):
        kt = kcat_ref[0, j]
        for c in range(n_cc):
            s_ref[c] = _dot(kt, qt_ref[0, 0, c])

    def softmax(c, mask):
        s = s_ref[c]
        if mask is not None:
            s = jnp.where(mask, s, NEG_BIG)
        m_prev = m_ref[c]
        m_new = jnp.maximum(m_prev, jnp.max(s, axis=0, keepdims=True))
        a_ref[c] = jnp.exp2(m_prev - m_new)
        m_ref[c] = m_new
        p_ref[c] = jnp.exp2(s - m_new).astype(BF16)

    def values(c, vt):
        acc_ref[c] = a_ref[c] * acc_ref[c] + _dot(vt, p_ref[c])

    km, vm = km_ref[...], vm_ref[...]
    sm = [_dot(km, qt_ref[0, 0, c]) for c in range(n_cc)]
    pm = []
    for c in range(n_cc):
        m0 = jnp.max(sm[c], axis=0, keepdims=True)
        m_ref[c] = m0
        pm.append(jnp.exp2(sm[c] - m0).astype(BF16))
    for c in range(n_cc):
        acc_ref[c] = _dot(vm, pm[c])
    a_ref[...] = jnp.ones(a_ref.shape, F32)
    p_ref[...] = jnp.zeros(p_ref.shape, BF16)
    scores(0)

    def key_step(j):
        kt_next = kcat_ref[0, j + 1]
        vt_prev = vt_ref[0, jnp.maximum(j - 1, 0)]
        for c in range(n_cc):
            values(c, vt_prev)
            softmax(c, None)
            s_ref[c] = _dot(kt_next, qt_ref[0, 0, c])

    def key_step_pair(jj, carry):
        key_step(2 * jj)
        key_step(2 * jj + 1)
        return carry

    lax.fori_loop(0, n_int // 2, key_step_pair, 0)
    if causal or n_int % 2:
        @pl.when(n_int % 2 == 1)
        def _():
            key_step(n_int - 1)

    row = lax.broadcasted_iota(jnp.int32, (KEY_TILE, KEY_TILE), 0)
    if causal:
        col = lax.broadcasted_iota(jnp.int32, (KEY_TILE, KEY_TILE), 1)
        mask = (row >> 6) <= (col >> 6)
    else:
        mask = row < n_valid_last
    vt_prev = vt_ref[0, jnp.maximum(n_int - 1, 0)]
    for c in range(n_cc):
        values(c, vt_prev)
    for c in range(n_cc):
        softmax(c, mask)
    vt_last = vt_ref[0, n_int]
    for c in range(n_cc):
        values(c, vt_last)

    groups_per_cc = KEY_TILE // LANES
    for c in range(n_cc):
        acc = acc_ref[c]
        o_t = acc[:MLA_KV_RANK] * (1.0 / acc[MLA_KV_RANK:MLA_KV_RANK + 1])
        for g in range(groups_per_cc):
            blk = o_t[:, g * LANES:(g + 1) * LANES].T.astype(BF16)
            col0 = c * KEY_TILE + g * LANES
            if bq >= LANES:
                hd, q0 = col0 // bq, col0 % bq
                o_ref[0, q0:q0 + LANES, hd * MLA_KV_RANK:(hd + 1) * MLA_KV_RANK] = blk
            else:
                for hl in range(LANES // bq):
                    hd = col0 // bq + hl
                    o_ref[0, :, hd * MLA_KV_RANK:(hd + 1) * MLA_KV_RANK] = blk[hl * bq:(hl + 1) * bq]


def _attn(qt, kcat, vt, km, vm, *, bq, causal, n_valid_last):
    bsz, nq, n_cc = qt.shape[:3]
    n_kt = kcat.shape[1]
    assert not causal or bq == KEY_TILE
    return pl.pallas_call(
        functools.partial(_attn_kernel, bq=bq, causal=causal, n_valid_last=n_valid_last),
        grid=(bsz, nq),
        in_specs=[pl.BlockSpec((1, 1, n_cc, MLA_CAT, KEY_TILE), lambda b, i: (b, i, 0, 0, 0)),
                  pl.BlockSpec((1, n_kt, KEY_TILE, MLA_CAT), lambda b, i: (b, 0, 0, 0)),
                  pl.BlockSpec((1, n_kt, V_ROWS, KEY_TILE), lambda b, i: (b, 0, 0, 0)),
                  _const_spec(km.shape), _const_spec(vm.shape)],
        out_specs=pl.BlockSpec((1, bq, MLA_HEADS * MLA_KV_RANK), lambda b, i: (b, i, 0)),
        out_shape=jax.ShapeDtypeStruct((bsz, nq * bq, MLA_HEADS * MLA_KV_RANK), BF16),
        scratch_shapes=[pltpu.VMEM((n_cc, KEY_TILE, KEY_TILE), F32),
                        pltpu.VMEM((n_cc, KEY_TILE, KEY_TILE), BF16),
                        pltpu.VMEM((n_cc, 1, KEY_TILE), F32),
                        pltpu.VMEM((n_cc, 1, KEY_TILE), F32),
                        pltpu.VMEM((n_cc, V_ROWS, KEY_TILE), F32)],
        compiler_params=_cparams("parallel", "arbitrary"),
        name="attn",
    )(qt, kcat, vt, km, vm)


_MERGE_WEIGHTS = ("ln_g", "ln_b", "w_g", "gla_g", "w_br_gla", "w_uv_bd", "w_br_mla", "w_mg", "b_mg", "w_out",
                  "ln1_g", "ln1_b", "w_r_hi", "w_r_lo", "b_r")


def _merge_kernel(x_ref, og_ref, ol_ref, lng_ref, lnb_ref, wg_ref, glag_ref, wbg_ref, wuv_ref, wbm_ref,
                  wmg_ref, bmg_ref, wout_ref, l1g_ref, l1b_ref, wrh_ref, wrl_ref, br_ref, cin_ref,
                  h1t_ref, rt_ref, cnt_ref, carry_ref, *, sub):
    tt, d = x_ref.shape

    def branches(rows):
        h = _layer_norm(x_ref[rows, :], lng_ref[...], lnb_ref[...])
        hb = h.astype(BF16)
        g_out = _dot(hb, wg_ref[...])
        gate_pre = _dot(hb, wmg_ref[...])
        pair_in = 2 * MLA_KV_RANK
        y_heads = jnp.concatenate([_dot(ol_ref[rows, p * pair_in:(p + 1) * pair_in], wuv_ref[p])
                                   for p in range(MLA_HEADS // 2)], axis=-1)
        return h, g_out, gate_pre, y_heads

    def mix(rows, h, g_out, gate_pre, y_heads):
        og = og_ref[rows, :]
        parts = []
        for hd in range(GLA_HEADS):
            cols = slice(hd * GLA_DV, (hd + 1) * GLA_DV)
            o_h = og[:, cols]
            g_h = g_out[:, cols]
            o_n = o_h * lax.rsqrt(jnp.mean(o_h * o_h, axis=-1, keepdims=True) + RMS_EPS) * glag_ref[...]
            parts.append(o_n * (g_h * jax.nn.sigmoid(g_h)))
        y_a = _dot(jnp.concatenate(parts, axis=-1).astype(BF16), wbg_ref[...])
        y_b = _dot(y_heads.astype(BF16), wbm_ref[...])
        gates = jax.nn.sigmoid(gate_pre + bmg_ref[...])
        mix_in = gates[:, :d] * y_a + gates[:, d:] * y_b
        return DEEPNORM_ALPHA * h + _dot(mix_in.astype(BF16), wout_ref[...])

    def route(r0, pre):
        h1 = _layer_norm(pre, l1g_ref[...], l1b_ref[...])
        for s in range(ROW_TILE):
            h1t_ref[pl.ds(r0 * ROW_TILE + s, sub, stride=ROW_TILE), :] = h1[:, s * LANES:(s + 1) * LANES]
        h1_hi = h1.astype(BF16)
        h1_lo = (h1 - h1_hi.astype(F32)).astype(BF16)
        logits = (_dot(h1_hi, wrh_ref[...]) + (_dot(h1_hi, wrl_ref[...]) + _dot(h1_lo, wrh_ref[...]))) + br_ref[...]
        lane = lax.broadcasted_iota(jnp.int32, logits.shape, 1)
        is_grp = lane < N_GROUPS
        gl = jnp.where(is_grp, logits, NEG_BIG)
        g_max = jnp.max(gl, axis=-1, keepdims=True)
        g_sel = jnp.min(jnp.where(gl == g_max, lane, ROUTER_LANES), axis=-1, keepdims=True)
        p_grp = 1.0 / jnp.sum(jnp.where(is_grp, jnp.exp(gl - g_max), 0.0), axis=-1, keepdims=True)
        e_lo = N_GROUPS + g_sel * EXPERTS_PER_GROUP
        in_grp = (lane >= e_lo) & (lane < e_lo + EXPERTS_PER_GROUP)
        el = jnp.where(in_grp, logits, NEG_BIG)
        v1 = jnp.max(el, axis=-1, keepdims=True)
        i1 = jnp.min(jnp.where(el == v1, lane, ROUTER_LANES), axis=-1, keepdims=True)
        el2 = jnp.where(lane == i1, NEG_BIG, el)
        v2 = jnp.max(el2, axis=-1, keepdims=True)
        i2 = jnp.min(jnp.where(el2 == v2, lane, ROUTER_LANES), axis=-1, keepdims=True)
        e2 = jnp.exp(v2 - v1)
        w1 = p_grp / (1.0 + e2)
        w2 = p_grp * e2 / (1.0 + e2)
        e1, e2i = i1 - N_GROUPS, i2 - N_GROUPS
        hot = ((lane == e1) | (lane == e2i)).astype(F32)
        earlier = (lax.broadcasted_iota(jnp.int32, (sub, sub), 0) > lax.broadcasted_iota(jnp.int32, (sub, sub), 1))
        before = carry_ref[...] + _dot(earlier.astype(BF16), hot.astype(BF16))
        r1 = jnp.sum(jnp.where(lane == e1, before, 0.0), axis=-1, keepdims=True)
        r2 = jnp.sum(jnp.where(lane == e2i, before, 0.0), axis=-1, keepdims=True)
        carry_ref[...] += jnp.sum(hot, axis=0, keepdims=True)
        fields = (e1.astype(F32), e2i.astype(F32), w1, w2, r1, r2)
        rt = jnp.zeros(logits.shape, F32)
        for k, val in enumerate(fields):
            rt = jnp.where(lane == k, val, rt)
        rt_ref[pl.ds(r0, sub), :] = rt

    @pl.when(pl.program_id(0) == 0)
    def _():
        carry_ref[...] = cin_ref[...]

    pending = None
    for r0 in range(0, tt, sub):
        rows = pl.ds(r0, sub)
        independent = branches(rows)
        if pending is not None:
            route(*pending)
        pending = (r0, mix(rows, *independent))
    route(*pending)
    cnt_ref[...] = carry_ref[...]


def _merge(x2, og2, ol2, wp, cnt_in, *, tt):
    t, d = x2.shape
    assert d == ROW_TILE * LANES
    row = lambda w: pl.BlockSpec((tt, w), lambda i: (i, 0))
    return pl.pallas_call(
        functools.partial(_merge_kernel, sub=min(TOKEN_TILE, tt)),
        grid=(t // tt,),
        in_specs=([row(d), row(GLA_V), row(MLA_HEADS * MLA_KV_RANK)] + [_const_spec(wp[n].shape) for n in _MERGE_WEIGHTS]
                  + [_const_spec(cnt_in.shape)]),
        out_specs=[pl.BlockSpec((tt * ROW_TILE, LANES), lambda i: (i, 0)), row(ROUTER_LANES),
                   pl.BlockSpec((1, ROUTER_LANES), lambda i: (0, 0))],
        out_shape=[jax.ShapeDtypeStruct((t * ROW_TILE, LANES), F32), jax.ShapeDtypeStruct((t, ROUTER_LANES), F32),
                   jax.ShapeDtypeStruct((1, ROUTER_LANES), F32)],
        scratch_shapes=[pltpu.VMEM((1, ROUTER_LANES), F32)],
        compiler_params=_cparams("arbitrary"),
        name="merge",
    )(x2, og2, ol2, *[wp[n] for n in _MERGE_WEIGHTS], cnt_in)


def _row_copy(src_ref, src_row, dst_ref, dst_row, sem):
    return pltpu.make_async_copy(src_ref.at[pl.ds(src_row * ROW_TILE, ROW_TILE)],
                                 dst_ref.at[pl.ds(dst_row * ROW_TILE, ROW_TILE)], sem)


def _group_spans(n_tokens, tile):
    spans, first = [], 0
    for n in n_tokens:
        spans.append((first, n // tile))
        first += n // tile
    return spans


def _group_spec(block, span):
    first, steps = span
    return pl.BlockSpec(block, lambda i, *_: (jnp.clip(i - first, 0, steps - 1), 0))


def _dispatch_kernel(pos_ref, zrow_ref, nu_ref, *refs, td, spans):
    h1t_refs, (xs_ref, zero_ref, sem, zsem) = refs[:len(spans)], refs[len(spans):]
    i = pl.program_id(0)

    @pl.when(i == 0)
    def _():
        zero_ref[...] = jnp.zeros(zero_ref.shape, F32)
        tile_rows = EXPERT_ROWS * ROW_TILE
        fill = lambda row: pltpu.make_async_copy(
            zero_ref, xs_ref.at[pl.ds(pl.multiple_of(row * ROW_TILE, ROW_TILE), tile_rows)], zsem)
        for e in range(N_EXPERTS):
            @pl.when(zrow_ref[e] >= 0)
            def _():
                fill(zrow_ref[e]).start()
        for e in range(N_EXPERTS):
            @pl.when(zrow_ref[e] >= 0)
            def _():
                fill(zrow_ref[e]).wait()

        def fill_tail(r, carry):
            fill(r * EXPERT_ROWS).start()
            fill(r * EXPERT_ROWS).wait()
            return carry

        lax.fori_loop(nu_ref[0], xs_ref.shape[0] // tile_rows, fill_tail, 0)

    base = i * (2 * td)
    for h1t_ref, (first, steps) in zip(h1t_refs, spans):
        @pl.when((i >= first) & (i < first + steps))
        def _():
            for t in range(td):
                for k in range(2):
                    dst = pl.multiple_of(pos_ref[base + 2 * t + k] * ROW_TILE, ROW_TILE)
                    pltpu.make_async_copy(h1t_ref.at[pl.ds(t * ROW_TILE, ROW_TILE)],
                                          xs_ref.at[pl.ds(dst, ROW_TILE)], sem).start(priority=k)
            for _ in range(2 * td):
                _row_copy(h1t_ref, 0, xs_ref, 0, sem).wait()


def _dispatch(h1ts, pos, zrow, n_used, *, n_rows, td):
    spans = _group_spans([h.shape[0] // ROW_TILE for h in h1ts], td)
    return pl.pallas_call(
        functools.partial(_dispatch_kernel, td=td, spans=spans),
        grid_spec=pltpu.PrefetchScalarGridSpec(
            num_scalar_prefetch=3,
            grid=(sum(steps for _, steps in spans),),
            in_specs=[_group_spec((td * ROW_TILE, LANES), span) for span in spans],
            out_specs=pl.BlockSpec(memory_space=pl.ANY),
            scratch_shapes=[pltpu.VMEM((EXPERT_ROWS * ROW_TILE, LANES), F32),
                            pltpu.SemaphoreType.DMA(()), pltpu.SemaphoreType.DMA(())]),
        out_shape=jax.ShapeDtypeStruct((n_rows * ROW_TILE, LANES), F32),
        compiler_params=_cparams("arbitrary"),
        name="dispatch",
    )(pos, zrow, n_used, *h1ts)


def _experts_kernel(te_ref, tb_ref, nu_ref, xs_ref, wg_ref, wu_ref, wd_ref, out_ref, wgb_ref, wub_ref, wdb_ref):
    del tb_ref
    r = pl.program_id(0)
    rows = xs_ref.shape[0] // ROW_TILE

    @pl.when((r == 0) | (te_ref[r] != te_ref[jnp.maximum(r - 1, 0)]))
    def _():
        wgb_ref[...] = wg_ref[0].astype(BF16)
        wub_ref[...] = wu_ref[0].astype(BF16)
        wdb_ref[...] = wd_ref[0].astype(BF16)

    @pl.when(r < nu_ref[0])
    def _():
        x = jnp.concatenate([xs_ref[pl.ds(s, rows, stride=ROW_TILE), :] for s in range(ROW_TILE)], axis=1).astype(BF16)
        gate = _dot(x, wgb_ref[...])
        up = _dot(x, wub_ref[...])
        hid = (gate * jax.nn.sigmoid(gate)) * up
        out = _dot(hid.astype(BF16), wdb_ref[...])
        for s in range(ROW_TILE):
            out_ref[pl.ds(s, rows, stride=ROW_TILE), :] = out[:, s * LANES:(s + 1) * LANES]

    @pl.when(pl.program_id(0) >= nu_ref[0])
    def _():
        out_ref[...] = jnp.zeros(out_ref.shape, F32)


def _experts(xs, tile_e, tile_blk, n_used, wp):
    n_rows = xs.shape[0] // ROW_TILE
    d = ROW_TILE * LANES
    blk = pl.BlockSpec((EXPERT_ROWS * ROW_TILE, LANES), lambda r, te, tb, nu: (tb[r], 0))
    return pl.pallas_call(
        _experts_kernel,
        grid_spec=pltpu.PrefetchScalarGridSpec(
            num_scalar_prefetch=3,
            grid=(n_rows // EXPERT_ROWS,),
            in_specs=[blk,
                      pl.BlockSpec((1, d, D_EXPERT), lambda r, te, tb, nu: (te[r], 0, 0)),
                      pl.BlockSpec((1, d, D_EXPERT), lambda r, te, tb, nu: (te[r], 0, 0)),
                      pl.BlockSpec((1, D_EXPERT, d), lambda r, te, tb, nu: (te[r], 0, 0))],
            out_specs=pl.BlockSpec((EXPERT_ROWS * ROW_TILE, LANES), lambda r, te, tb, nu: (r, 0)),
            scratch_shapes=[pltpu.VMEM((d, D_EXPERT), BF16), pltpu.VMEM((d, D_EXPERT), BF16),
                            pltpu.VMEM((D_EXPERT, d), BF16)]),
        out_shape=jax.ShapeDtypeStruct(xs.shape, F32),
        compiler_params=_cparams("arbitrary"),
        name="experts",
    )(tile_e, tile_blk, n_used, xs, wp["w_gate"], wp["w_up"], wp["w_down"])


def _combine_kernel(pos_ref, *refs, tc, spans):
    g = len(spans)
    h1t_refs, rt_refs = refs[:g], refs[g:2 * g]
    outs_ref, l2g_ref, l2b_ref = refs[2 * g:2 * g + 3]
    y_refs = refs[2 * g + 3:3 * g + 3]
    g_ref, sem = refs[3 * g + 3:]
    i = pl.program_id(0)
    n_steps = pl.num_programs(0)

    def gather(step):
        slot = step % 2
        base = step * (2 * tc)
        for t in range(tc):
            for k in range(2):
                src = pl.multiple_of(pos_ref[base + 2 * t + k] * ROW_TILE, ROW_TILE)
                pltpu.make_async_copy(outs_ref.at[pl.ds(src, ROW_TILE)],
                                      g_ref.at[slot, k, pl.ds(t * ROW_TILE, ROW_TILE)],
                                      sem.at[slot]).start(priority=k)

    @pl.when(i == 0)
    def _():
        gather(0)

    @pl.when(i + 1 < n_steps)
    def _():
        gather(i + 1)

    slot = i % 2
    for _ in range(2 * tc):
        pltpu.make_async_copy(outs_ref.at[pl.ds(0, ROW_TILE)], g_ref.at[slot, 0, pl.ds(0, ROW_TILE)], sem.at[slot]).wait()
    for h1t_ref, rt_ref, y_ref, (first, steps) in zip(h1t_refs, rt_refs, y_refs, spans):
        @pl.when((i >= first) & (i < first + steps))
        def _():
            rt = rt_ref[...]
            w0, w1 = rt[:, RT_W:RT_W + 1], rt[:, RT_W + 1:RT_W + 2]
            cols = []
            for s in range(ROW_TILE):
                rows = pl.ds(s, tc, stride=ROW_TILE)
                cols.append(DEEPNORM_ALPHA * h1t_ref[rows, :] + (w0 * g_ref[slot, 0, rows, :] + w1 * g_ref[slot, 1, rows, :]))
            y_ref[...] = _layer_norm(jnp.concatenate(cols, axis=1), l2g_ref[...], l2b_ref[...])


def _combine(h1ts, rts, outs, pos, wp, *, tc):
    d = ROW_TILE * LANES
    spans = _group_spans([r.shape[0] for r in rts], tc)
    const = lambda: pl.BlockSpec((1, d), lambda i, *_: (0, 0))
    return pl.pallas_call(
        functools.partial(_combine_kernel, tc=tc, spans=spans),
        grid_spec=pltpu.PrefetchScalarGridSpec(
            num_scalar_prefetch=1,
            grid=(sum(steps for _, steps in spans),),
            in_specs=([_group_spec((tc * ROW_TILE, LANES), span) for span in spans]
                      + [_group_spec((tc, ROUTER_LANES), span) for span in spans]
                      + [pl.BlockSpec(memory_space=pl.ANY), const(), const()]),
            out_specs=[_group_spec((tc, d), span) for span in spans],
            scratch_shapes=[pltpu.VMEM((2, 2, tc * ROW_TILE, LANES), F32), pltpu.SemaphoreType.DMA((2,))]),
        out_shape=[jax.ShapeDtypeStruct((r.shape[0], d), F32) for r in rts],
        compiler_params=_cparams("arbitrary"),
        name="combine",
    )(pos, *h1ts, *rts, outs, wp["ln2_g"], wp["ln2_b"])


def _route_plan(rt, cnt):
    t = rt.shape[0]
    n_tiles = (2 * t) // EXPERT_ROWS + N_EXPERTS
    counts = cnt[0, :N_EXPERTS].astype(jnp.int32)
    padded = (counts + (EXPERT_ROWS - 1)) // EXPERT_ROWS * EXPERT_ROWS
    ends = jnp.cumsum(padded)
    starts = ends - padded
    eid = rt[:, RT_E:RT_E + 2].astype(jnp.int32)
    rank = rt[:, RT_RANK:RT_RANK + 2].astype(jnp.int32)
    pos = (jnp.take(starts, eid) + rank).reshape(-1)
    n_used = ends[-1] // EXPERT_ROWS
    tile_blk = jnp.minimum(jnp.arange(n_tiles, dtype=jnp.int32), n_used - 1)
    tile_e = jnp.sum((tile_blk[:, None] * EXPERT_ROWS >= ends[None, :]).astype(jnp.int32), axis=1)
    zrow = jnp.where(padded > 0, ends - EXPERT_ROWS, -1)
    return pos, zrow.astype(jnp.int32), tile_e.astype(jnp.int32), tile_blk, n_used.reshape(1).astype(jnp.int32), n_tiles


def _rope_tables(pos):
    inv = ROPE_THETA ** (-jnp.arange(0, MLA_ROPE, 2, dtype=F32) / MLA_ROPE)
    ang = pos.astype(F32)[:, None] * inv[None, :]
    cs = jnp.concatenate([jnp.cos(ang), jnp.sin(ang)], axis=-1)
    return cs[None], cs.T[None]


def _prep_weights(ln_in_g, ln_in_b, w_in, w_gk2, b_gk, gla_norm_g, q_norm_g, kv_norm_g, w_uq, w_uk, w_uv,
                  w_br_gla, w_br_mla, w_mg, b_mg, w_out, ln1_g, ln1_b, w_rg, b_rg, w_re, b_re,
                  w_gate, w_up, w_down, ln2_g, ln2_b):
    d = w_in.shape[1]
    w = w_in[0]
    c0 = 0
    wk = w[:, c0:c0 + GLA_QK]; c0 += GLA_QK
    wv = w[:, c0:c0 + GLA_V]; c0 += GLA_V
    wgr = w[:, c0:c0 + GLA_RANK]; c0 += GLA_RANK
    wckv = w[:, c0:c0 + MLA_KV_RANK]; c0 += MLA_KV_RANK
    wkr = w[:, c0:c0 + MLA_ROPE]; c0 += MLA_ROPE
    wq = w[:, c0:c0 + GLA_QK]; c0 += GLA_QK
    wg = w[:, c0:c0 + GLA_V]; c0 += GLA_V
    wcq = w[:, c0:c0 + MLA_Q_RANK]
    w_tok = jnp.concatenate([wk, wv, wq, wckv, wkr, wgr, jnp.zeros((d, TOK_PAD - TOK_USED), F32)], axis=1)
    w_tr = jnp.concatenate([wckv, wcq], axis=1).T
    uq = w_uq[0].reshape(MLA_Q_RANK, MLA_HEADS, MLA_QK_DIM)
    uq_perm = jnp.concatenate([
        uq[:, :, :MLA_NOPE].reshape(MLA_Q_RANK, -1),
        uq[:, :, MLA_NOPE:MLA_NOPE + MLA_HALF].reshape(MLA_Q_RANK, -1),
        uq[:, :, MLA_NOPE + MLA_HALF:].reshape(MLA_Q_RANK, -1)], axis=1)
    uv = w_uv[0].transpose(1, 0, 2).reshape(MLA_HEADS // 2, 2, MLA_KV_RANK, MLA_DV)
    eye = jnp.eye(2, dtype=F32)
    w_uv_bd = (uv[:, :, :, None, :] * eye[None, :, None, :, None]).reshape(
        MLA_HEADS // 2, 2 * MLA_KV_RANK, 2 * MLA_DV)
    w_r = jnp.concatenate([w_rg[0], w_re[0].transpose(1, 0, 2).reshape(d, N_EXPERTS),
                           jnp.zeros((d, ROUTER_LANES - N_GROUPS - N_EXPERTS), F32)], axis=1)
    w_r_hi = w_r.astype(BF16)
    b_r = jnp.concatenate([b_rg[0], b_re[0].reshape(-1), jnp.zeros((ROUTER_LANES - N_GROUPS - N_EXPERTS,), F32)])
    row = lambda a: a.reshape(1, -1)
    return {
        "ln_g": row(ln_in_g), "ln_b": row(ln_in_b),
        "w_tok": w_tok.astype(BF16), "w_tr": w_tr.astype(BF16),
        "w_gk2": w_gk2[0].astype(BF16), "b_gk": row(b_gk[0]),
        "kv_g": row(kv_norm_g[0]), "kv_gt": kv_norm_g[0].reshape(-1, 1), "q_gt": q_norm_g[0].reshape(-1, 1),
        "w_uqt": uq_perm.T.astype(BF16), "w_uk": w_uk[0].transpose(1, 0, 2).astype(BF16),
        "w_g": wg.astype(BF16), "gla_g": row(gla_norm_g[0]),
        "w_br_gla": w_br_gla[0].astype(BF16), "w_uv_bd": w_uv_bd.astype(BF16), "w_br_mla": w_br_mla[0].astype(BF16),
        "w_mg": w_mg[0].astype(BF16), "b_mg": row(b_mg[0]), "w_out": w_out[0].astype(BF16),
        "ln1_g": row(ln1_g[0]), "ln1_b": row(ln1_b[0]),
        "w_r_hi": w_r_hi, "w_r_lo": (w_r - w_r_hi.astype(F32)).astype(BF16), "b_r": row(b_r),
        "w_gate": w_gate[0], "w_up": w_up[0], "w_down": w_down[0],
        "ln2_g": row(ln2_g[0]), "ln2_b": row(ln2_b[0]),
    }


def _value_rows(vt):
    lead, length = vt.shape[:-2], vt.shape[-1]
    return jnp.concatenate([vt, jnp.ones(lead + (1, length), vt.dtype),
                            jnp.zeros(lead + (V_ROWS - MLA_KV_RANK - 1, length), vt.dtype)], axis=-2)


def _key_tiles(kcat, vt):
    bsz, length, _ = kcat.shape
    n = -(-length // KEY_TILE)
    pad = n * KEY_TILE - length
    kcat = jnp.pad(kcat, ((0, 0), (0, pad), (0, 0)))
    vt = jnp.pad(_value_rows(vt), ((0, 0), (0, 0), (0, pad)))
    return (kcat.reshape(bsz, n, KEY_TILE, MLA_CAT),
            vt.reshape(bsz, V_ROWS, n, KEY_TILE).transpose(0, 2, 1, 3))


def _ffn(groups, wp):
    h1ts, rts = [], []
    cnt = jnp.zeros((1, ROUTER_LANES), F32)
    for x, og, ol in groups:
        t, d = x.shape[0] * x.shape[1], x.shape[2]
        h1t, rt, cnt = _merge(x.reshape(t, d), og.reshape(t, -1), ol.reshape(t, -1), wp, cnt, tt=min(MERGE_TILE, t))
        h1ts.append(h1t)
        rts.append(rt)
    n_fields = RT_RANK + 2
    pos, zrow, tile_e, tile_blk, n_used, n_tiles = _route_plan(
        jnp.concatenate([rt[:, :n_fields] for rt in rts], axis=0), cnt)
    xs = _dispatch(h1ts, pos, zrow, n_used, n_rows=n_tiles * EXPERT_ROWS, td=MOE_TOKEN_TILE)
    outs = _experts(xs, tile_e, tile_blk, n_used, wp)
    ys = _combine(h1ts, rts, outs, pos, wp, tc=MOE_TOKEN_TILE)
    return [y.reshape(x.shape) for y, (x, _, _) in zip(ys, groups)]


def kernel(x_prompt, x_sample, cache_mla_latent, cache_mla_krope, state_gla, meta_tokens, ln_in_g, ln_in_b, w_in, w_gk2, b_gk, gla_norm_g, q_norm_g, kv_norm_g, w_uq, w_uk, w_uv, w_br_gla, w_br_mla, w_mg, b_mg, w_out, ln1_g, ln1_b, w_rg, b_rg, w_re, b_re, w_gate, w_up, w_down, ln2_g, ln2_b):
    bp, sp, d = x_prompt.shape
    bs, ss, _ = x_sample.shape
    past = cache_mla_latent.shape[2]
    wp = _prep_weights(ln_in_g, ln_in_b, w_in, w_gk2, b_gk, gla_norm_g, q_norm_g, kv_norm_g, w_uq, w_uk, w_uv,
                       w_br_gla, w_br_mla, w_mg, b_mg, w_out, ln1_g, ln1_b, w_rg, b_rg, w_re, b_re,
                       w_gate, w_up, w_down, ln2_g, ln2_b)

    cs_m, cst_m = _rope_tables(jnp.arange(-N_META, 0, dtype=jnp.int32))
    m = _proj(meta_tokens[None], cs_m, cst_m, wp, cl=N_META, tt=N_META)
    _, _, m_kt, m_v, m_ebl, m_ckv, m_kr, m_kcat, m_vt, _ = m
    zero_state = jnp.zeros((1, GLA_HEADS, GLA_DK, GLA_DV), F32)
    _, m_state = _gla(m[0], m[1], m_kt, m_v, m_ebl, zero_state, cl=N_META, ts=N_META)

    cs_p, cst_p = _rope_tables(jnp.arange(sp, dtype=jnp.int32))
    p_qd, p_ki, p_kt, p_v, p_ebl, p_ckv, p_kr, p_kcat, p_vt, p_qt = _proj(
        x_prompt, cs_p, cst_p, wp, cl=CHUNK, tt=TOKEN_TILE)
    p_o, p_state = _gla(p_qd, p_ki, p_kt, p_v, p_ebl, m_state, cl=CHUNK, ts=TOKEN_TILE)
    rep = lambda a, n: jnp.broadcast_to(a, (n,) + a.shape[1:])
    lat_p = jnp.concatenate([rep(m_ckv, bp), p_ckv], axis=1)
    kr_p = jnp.concatenate([rep(m_kr, bp), p_kr], axis=1)
    kcat_p, vt_p = _key_tiles(p_kcat, p_vt)
    km, vm = m_kcat[0], _value_rows(m_vt[0])
    p_ol = _attn(p_qt, kcat_p, vt_p, km, vm, bq=TOKEN_TILE, causal=True, n_valid_last=KEY_TILE)

    ts_all = bs * ss
    cs_s, cst_s = _rope_tables(past + (jnp.arange(ts_all, dtype=jnp.int32) % ss))
    s_qd, s_ki, s_kt, s_v, s_ebl, s_ckv, s_kr, s_kcat, s_vt, s_qt = _proj(
        x_sample.reshape(1, ts_all, d), cs_s, cst_s, wp, cl=ss, tt=min(TOKEN_TILE, ts_all))
    per_stream = lambda a: a.reshape(bs, ss, a.shape[-1])
    s_o, s_state = _gla(per_stream(s_qd), per_stream(s_ki), per_stream(s_kt), per_stream(s_v),
                        s_ebl.reshape(bs, 1, 1, GLA_QK), state_gla[0].astype(F32), cl=ss, ts=ss)
    s_ckv, s_kr = per_stream(s_ckv), per_stream(s_kr)
    cache_kcat = jnp.concatenate([cache_mla_latent[0], cache_mla_krope[0]], axis=-1).astype(BF16)
    cache_vt = cache_mla_latent[0].astype(BF16).transpose(0, 2, 1)
    new_vt = s_vt.reshape(MLA_KV_RANK, bs, ss).transpose(1, 0, 2)
    kcat_s, vt_s = _key_tiles(jnp.concatenate([cache_kcat, per_stream(s_kcat)], axis=1),
                              jnp.concatenate([cache_vt, new_vt], axis=2))
    qt = s_qt.transpose(0, 2, 3, 1, 4).reshape(MLA_HEADS, MLA_CAT, bs, ss)
    qt = qt.transpose(2, 1, 0, 3).reshape(bs, MLA_CAT, MLA_HEADS * ss // KEY_TILE, KEY_TILE)
    qt = qt.transpose(0, 2, 1, 3)[:, None]
    s_ol = _attn(qt, kcat_s, vt_s, km, vm, bq=ss, causal=False, n_valid_last=(past + ss - 1) % KEY_TILE + 1)
    y_prompt, y_sample = _ffn([(x_prompt, p_o, p_ol), (x_sample, s_o, s_ol)], wp)

    return (y_prompt, y_sample, lat_p[None], kr_p[None], p_state[None].astype(state_gla.dtype),
            s_ckv[None], s_kr[None], s_state[None].astype(state_gla.dtype))
```

```python
import functools

import jax
import jax.numpy as jnp
from jax import lax
from jax.experimental import pallas as pl
from jax.experimental.pallas import tpu as pltpu

F32 = jnp.float32
BF16 = jnp.bfloat16

CHUNK = 64
N_META = 16
GLA_HEADS = 4
GLA_DK = 128
GLA_DV = 256
GLA_RANK = 16
GLA_TAU = 16.0
GLA_QK = GLA_HEADS * GLA_DK
GLA_V = GLA_HEADS * GLA_DV
GLA_SCALE = GLA_DK ** -0.5
MLA_HEADS = 16
MLA_Q_RANK = 384
MLA_KV_RANK = 128
MLA_NOPE = 64
MLA_ROPE = 32
MLA_HALF = MLA_ROPE // 2
MLA_DV = 64
MLA_QK_DIM = MLA_NOPE + MLA_ROPE
MLA_CAT = MLA_KV_RANK + MLA_ROPE
MLA_SCALE = MLA_QK_DIM ** -0.5
LOG2_E = 1.4426950408889634
Q_SCALE = MLA_SCALE * LOG2_E
V_ROWS = MLA_KV_RANK + 16
ROPE_THETA = 10000.0
N_GROUPS = 4
EXPERTS_PER_GROUP = 8
N_EXPERTS = N_GROUPS * EXPERTS_PER_GROUP
D_EXPERT = 256
LN_EPS = 1e-5
RMS_EPS = 1e-6
DEEPNORM_ALPHA = 2.0 ** 0.25

LANES = 128
MXU_DIM = 256
VMEM_LIMIT_BYTES = 56 * 1024 * 1024

TOKEN_TILE = 256
MERGE_TILE = 2 * TOKEN_TILE
PROJ_TILE = 2 * TOKEN_TILE
KEY_TILE = MXU_DIM
MOE_TOKEN_TILE = 256
EXPERT_ROWS = 256
ROW_TILE = 8
ROUTER_LANES = LANES
RT_E, RT_W, RT_RANK = 0, 2, 4
NEG_BIG = -1e30

NT_DIMS = (((1,), (1,)), ((), ()))
TN_DIMS = (((0,), (0,)), ((), ()))


def _cparams(*sem):
    return pltpu.CompilerParams(dimension_semantics=sem, vmem_limit_bytes=VMEM_LIMIT_BYTES)


def _const_spec(shape):
    nd = len(shape)
    return pl.BlockSpec(shape, lambda *_: (0,) * nd, pipeline_mode=pl.Buffered(1))


def _layer_norm(x, g, b):
    mu = jnp.mean(x, axis=-1, keepdims=True)
    xc = x - mu
    var = jnp.mean(xc * xc, axis=-1, keepdims=True)
    return xc * lax.rsqrt(var + LN_EPS) * g + b


def _dot(a, b):
    return jnp.dot(a, b, preferred_element_type=F32)


TOK_COLS = (GLA_QK, GLA_V, GLA_QK, MLA_KV_RANK, MLA_ROPE, GLA_RANK)
TOK_USED = sum(TOK_COLS)
TOK_PAD = -(-TOK_USED // MXU_DIM) * MXU_DIM
TR_ROWS = MLA_KV_RANK + MLA_Q_RANK


def _proj_kernel(x_ref, cs_ref, cst_ref, lng_ref, lnb_ref, wtok_ref, wtr_ref, wgk2_ref, bgk_ref,
                 kvg_ref, kvgt_ref, qngt_ref, wuqt_ref, wuk_ref,
                 qd_ref, ki_ref, kt_ref, v_ref, ebl_ref, ckv_ref, kr_ref, kcat_ref, vt_ref, qt_ref, *, cl, sub):
    tt = x_ref.shape[1]
    n_chunks = sub // cl

    def project(r0):
        h = _layer_norm(x_ref[0, r0:r0 + sub, :], lng_ref[...], lnb_ref[...])
        hb = h.astype(BF16)
        z = _dot(hb, wtok_ref[...])
        zt = lax.dot_general(wtr_ref[...], hb, NT_DIMS, preferred_element_type=F32)
        return z, zt

    def finish(r0, z, zt):
        rows = slice(r0, r0 + sub)
        o0 = 0
        k = z[:, o0:o0 + GLA_QK]; o0 += GLA_QK
        v = z[:, o0:o0 + GLA_V]; o0 += GLA_V
        q = z[:, o0:o0 + GLA_QK]; o0 += GLA_QK
        ckv_raw = z[:, o0:o0 + MLA_KV_RANK]; o0 += MLA_KV_RANK
        kr = z[:, o0:o0 + MLA_ROPE]; o0 += MLA_ROPE
        gklr = z[:, o0:o0 + GLA_RANK]

        gz = _dot(gklr.astype(BF16), wgk2_ref[...]) + bgk_ref[...]
        gk = (jnp.minimum(gz, 0.0) - jnp.log(1.0 + jnp.exp(-jnp.abs(gz)))) * (1.0 / GLA_TAU)
        row_in_chunk = lax.broadcasted_iota(jnp.int32, gk.shape, 0) & (cl - 1)
        b = gk
        shift = 1
        while shift < cl:
            b = b + jnp.where(row_in_chunk >= shift, pltpu.roll(b, shift, 0), 0.0)
            shift *= 2
        b3 = b.reshape(n_chunks, cl, GLA_QK)
        bl = b3[:, cl - 1:cl, :]
        qd_ref[0, rows, :] = (q * GLA_SCALE * jnp.exp(b)).astype(BF16)
        ki_ref[0, rows, :] = (k * jnp.exp(-b)).astype(BF16)
        kt_ref[0, rows, :] = (k.reshape(n_chunks, cl, GLA_QK) * jnp.exp(bl - b3)).reshape(sub, GLA_QK).astype(BF16)
        v_ref[0, rows, :] = v.astype(BF16)
        ebl_ref[0, r0 // cl:r0 // cl + n_chunks] = jnp.exp(bl)

        ckv = ckv_raw * lax.rsqrt(jnp.mean(ckv_raw * ckv_raw, axis=-1, keepdims=True) + RMS_EPS) * kvg_ref[...]
        cs = cs_ref[0, rows, :]
        cos, sin = cs[:, :MLA_HALF], cs[:, MLA_HALF:]
        x1, x2 = kr[:, :MLA_HALF], kr[:, MLA_HALF:]
        kr_rot = jnp.concatenate([x1 * cos - x2 * sin, x2 * cos + x1 * sin], axis=-1)
        ckv_ref[0, rows, :] = ckv
        kr_ref[0, rows, :] = kr_rot
        kcat_ref[0, rows, :MLA_KV_RANK] = ckv.astype(BF16)
        kcat_ref[0, rows, MLA_KV_RANK:] = kr_rot.astype(BF16)

        ckvt = zt[:MLA_KV_RANK]
        ckvt = ckvt * lax.rsqrt(jnp.mean(ckvt * ckvt, axis=0, keepdims=True) + RMS_EPS) * kvgt_ref[...]
        vt_ref[0, :, rows] = ckvt.astype(BF16)
        cqt = zt[MLA_KV_RANK:]
        cqt = cqt * lax.rsqrt(jnp.mean(cqt * cqt, axis=0, keepdims=True) + RMS_EPS) * qngt_ref[...]
        qmt = _dot(wuqt_ref[...], cqt.astype(BF16))
        n_nope = MLA_HEADS * MLA_NOPE
        n_half = MLA_HEADS * MLA_HALF
        cst = cst_ref[0, :, rows]
        cos_t = jnp.concatenate([cst[:MLA_HALF]] * MLA_HEADS, axis=0)
        sin_t = jnp.concatenate([cst[MLA_HALF:]] * MLA_HEADS, axis=0)
        r1 = qmt[n_nope:n_nope + n_half]
        r2 = qmt[n_nope + n_half:]
        rot1 = ((r1 * cos_t - r2 * sin_t) * Q_SCALE).astype(BF16)
        rot2 = ((r2 * cos_t + r1 * sin_t) * Q_SCALE).astype(BF16)
        blk = r0 // sub
        for hd in range(MLA_HEADS):
            nope = qmt[hd * MLA_NOPE:(hd + 1) * MLA_NOPE].astype(BF16)
            qlat = _dot(wuk_ref[hd], nope) * Q_SCALE
            qt_ref[0, blk, hd, :MLA_KV_RANK, :] = qlat.astype(BF16)
            qt_ref[0, blk, hd, MLA_KV_RANK:MLA_KV_RANK + MLA_HALF, :] = rot1[hd * MLA_HALF:(hd + 1) * MLA_HALF]
            qt_ref[0, blk, hd, MLA_KV_RANK + MLA_HALF:, :] = rot2[hd * MLA_HALF:(hd + 1) * MLA_HALF]

    starts = list(range(0, tt, sub))
    projected = [project(r0) for r0 in starts]
    for r0, (z, zt) in zip(starts, projected):
        finish(r0, z, zt)


def _proj(x, cs, cst, wp, *, cl, tt):
    bsz, s, d = x.shape
    nt = s // tt
    n_chunks = tt // cl
    sub = min(TOKEN_TILE, tt)
    tok = lambda w: pl.BlockSpec((1, tt, w), lambda b, t: (b, t, 0))
    in_specs = [
        tok(d),
        pl.BlockSpec((1, tt, MLA_ROPE), lambda b, t: (0, t, 0)),
        pl.BlockSpec((1, MLA_ROPE, tt), lambda b, t: (0, 0, t)),
    ] + [_const_spec(wp[n].shape) for n in _PROJ_WEIGHTS]
    out_shape = [
        jax.ShapeDtypeStruct((bsz, s, GLA_QK), BF16),
        jax.ShapeDtypeStruct((bsz, s, GLA_QK), BF16),
        jax.ShapeDtypeStruct((bsz, s, GLA_QK), BF16),
        jax.ShapeDtypeStruct((bsz, s, GLA_V), BF16),
        jax.ShapeDtypeStruct((bsz, s // cl, 1, GLA_QK), F32),
        jax.ShapeDtypeStruct((bsz, s, MLA_KV_RANK), F32),
        jax.ShapeDtypeStruct((bsz, s, MLA_ROPE), F32),
        jax.ShapeDtypeStruct((bsz, s, MLA_CAT), BF16),
        jax.ShapeDtypeStruct((bsz, MLA_KV_RANK, s), BF16),
        jax.ShapeDtypeStruct((bsz, s // sub, MLA_HEADS, MLA_CAT, sub), BF16),
    ]
    out_specs = [
        tok(GLA_QK), tok(GLA_QK), tok(GLA_QK), tok(GLA_V),
        pl.BlockSpec((1, n_chunks, 1, GLA_QK), lambda b, t: (b, t, 0, 0)),
        tok(MLA_KV_RANK), tok(MLA_ROPE), tok(MLA_CAT),
        pl.BlockSpec((1, MLA_KV_RANK, tt), lambda b, t: (b, 0, t)),
        pl.BlockSpec((1, tt // sub, MLA_HEADS, MLA_CAT, sub), lambda b, t: (b, t, 0, 0, 0)),
    ]
    return pl.pallas_call(
        functools.partial(_proj_kernel, cl=cl, sub=sub),
        grid=(bsz, nt),
        in_specs=in_specs,
        out_specs=out_specs,
        out_shape=out_shape,
        compiler_params=_cparams("parallel", "parallel"),
        name="proj",
    )(x, cs, cst, *[wp[n] for n in _PROJ_WEIGHTS])


_PROJ_WEIGHTS = ("ln_g", "ln_b", "w_tok", "w_tr", "w_gk2", "b_gk", "kv_g", "kv_gt", "q_gt", "w_uqt", "w_uk")


def _gla_kernel(qd_ref, ki_ref, kt_ref, v_ref, ebl_ref, s0_ref, o_ref, sfin_ref, st_ref, *, cl):
    t = pl.program_id(1)
    ts = qd_ref.shape[1]

    @pl.when(t == 0)
    def _():
        for hd in range(GLA_HEADS):
            st_ref[hd] = s0_ref[0, hd].T

    row = lax.broadcasted_iota(jnp.int32, (ts, ts), 0)
    col = lax.broadcasted_iota(jnp.int32, (ts, ts), 1)
    keep = (row >= col) & ((row & -cl) == (col & -cl))
    qk = lambda ref, hd: ref[0, :, hd * GLA_DK:(hd + 1) * GLA_DK]
    val = lambda hd: v_ref[0, :, hd * GLA_DV:(hd + 1) * GLA_DV]
    for hd in range(GLA_HEADS):
        a = lax.dot_general(qk(qd_ref, hd), qk(ki_ref, hd), NT_DIMS, preferred_element_type=F32)
        a = jnp.where(keep, a, 0.0).astype(BF16)
        o_ref[0, :, hd * GLA_DV:(hd + 1) * GLA_DV] = _dot(a, val(hd))
    for c in range(ts // cl):
        rows = slice(c * cl, (c + 1) * cl)
        for hd in range(GLA_HEADS):
            st = st_ref[hd]
            o_ref[0, rows, hd * GLA_DV:(hd + 1) * GLA_DV] += lax.dot_general(
                qk(qd_ref, hd)[rows], st.astype(BF16), NT_DIMS, preferred_element_type=F32)
            st_ref[hd] = (st * ebl_ref[0, c, :, hd * GLA_DK:(hd + 1) * GLA_DK]
                          + lax.dot_general(val(hd)[rows], qk(kt_ref, hd)[rows], TN_DIMS, preferred_element_type=F32))

    @pl.when(t == pl.num_programs(1) - 1)
    def _():
        for hd in range(GLA_HEADS):
            sfin_ref[0, hd] = st_ref[hd].T


def _gla(qd, ki, kt, v, ebl, s0, *, cl, ts):
    bsz, s, _ = qd.shape
    n_chunks = ts // cl
    s0_b = s0.shape[0]
    qk_spec = pl.BlockSpec((1, ts, GLA_QK), lambda b, t: (b, t, 0))
    v_spec = pl.BlockSpec((1, ts, GLA_V), lambda b, t: (b, t, 0))
    st_spec = pl.BlockSpec((1, GLA_HEADS, GLA_DK, GLA_DV), lambda b, t: (b, 0, 0, 0))
    s0_spec = st_spec if s0_b == bsz else pl.BlockSpec((1, GLA_HEADS, GLA_DK, GLA_DV), lambda b, t: (0, 0, 0, 0))
    return pl.pallas_call(
        functools.partial(_gla_kernel, cl=cl),
        grid=(bsz, s // ts),
        in_specs=[qk_spec, qk_spec, qk_spec, v_spec,
                  pl.BlockSpec((1, n_chunks, 1, GLA_QK), lambda b, t: (b, t, 0, 0)),
                  s0_spec],
        out_specs=[v_spec, st_spec],
        out_shape=[jax.ShapeDtypeStruct((bsz, s, GLA_V), F32),
                   jax.ShapeDtypeStruct((bsz, GLA_HEADS, GLA_DK, GLA_DV), F32)],
        scratch_shapes=[pltpu.VMEM((GLA_HEADS, GLA_DV, GLA_DK), F32)],
        compiler_params=_cparams("parallel", "arbitrary"),
        name="gla",
    )(qd, ki, kt, v, ebl, s0)


def _attn_kernel(qt_ref, kcat_ref, vt_ref, km_ref, vm_ref, o_ref, s_ref, p_ref, m_ref, a_ref, acc_ref,
                 *, bq, causal, n_valid_last):
    i = pl.program_id(1)
    n_cc = qt_ref.shape[2]
    n_kt = kcat_ref.shape[1]
    n_int = i if causal else n_kt - 1

    def scores(j):
        kt = kcat_ref[0, j]
        for c in range(n_cc):
            s_ref[c] = _dot(kt, qt_ref[0, 0, c])

    def softmax(c, mask):
        s = s_ref[c]
        if mask is not None:
            s = jnp.where(mask, s, NEG_BIG)
        m_prev = m_ref[c]
        m_new = jnp.maximum(m_prev, jnp.max(s, axis=0, keepdims=True))
        a_ref[c] = jnp.exp2(m_prev - m_new)
        m_ref[c] = m_new
        p_ref[c] = jnp.exp2(s - m_new).astype(BF16)

    def values(c, vt):
        acc_ref[c] = a_ref[c] * acc_ref[c] + _dot(vt, p_ref[c])

    km, vm = km_ref[...], vm_ref[...]
    sm = [_dot(km, qt_ref[0, 0, c]) for c in range(n_cc)]
    pm = []
    for c in range(n_cc):
        m0 = jnp.max(sm[c], axis=0, keepdims=True)
        m_ref[c] = m0
        pm.append(jnp.exp2(sm[c] - m0).astype(BF16))
    for c in range(n_cc):
        acc_ref[c] = _dot(vm, pm[c])
    scores(0)

    def when(cond, fn):
        if isinstance(cond, bool):
            if cond:
                fn()
        else:
            pl.when(cond)(fn)

    def first_step():
        kt_next = kcat_ref[0, 1]
        for c in range(n_cc):
            softmax(c, None)
            s_ref[c] = _dot(kt_next, qt_ref[0, 0, c])

    def no_step():
        a_ref[...] = jnp.ones(a_ref.shape, F32)
        p_ref[...] = jnp.zeros(p_ref.shape, BF16)

    when(n_int >= 1, first_step)
    when(n_int == 0, no_step)

    def key_step(j):
        kt_next = kcat_ref[0, j + 1]
        vt_prev = vt_ref[0, j - 1]
        for c in range(n_cc):
            values(c, vt_prev)
            softmax(c, None)
            s_ref[c] = _dot(kt_next, qt_ref[0, 0, c])

    def key_step_pair(jj, carry):
        key_step(2 * jj + 1)
        key_step(2 * jj + 2)
        return carry

    n_rest = jnp.maximum(n_int - 1, 0) if causal else max(n_int - 1, 0)
    lax.fori_loop(0, n_rest // 2, key_step_pair, 0)
    when(n_rest % 2 == 1, lambda: key_step(n_int - 1))

    row = lax.broadcasted_iota(jnp.int32, (KEY_TILE, KEY_TILE), 0)
    if causal:
        col = lax.broadcasted_iota(jnp.int32, (KEY_TILE, KEY_TILE), 1)
        mask = (row >> 6) <= (col >> 6)
    else:
        mask = row < n_valid_last
    vt_prev = vt_ref[0, jnp.maximum(n_int - 1, 0)]
    for c in range(n_cc):
        values(c, vt_prev)
    vt_last = vt_ref[0, n_int]
    value_lag = 2
    for c in range(n_cc + value_lag):
        if c < n_cc:
            softmax(c, mask)
        if c >= value_lag:
            values(c - value_lag, vt_last)

    groups_per_cc = KEY_TILE // LANES
    for c in range(n_cc):
        acc = acc_ref[c]
        o_t = acc[:MLA_KV_RANK] * (1.0 / acc[MLA_KV_RANK:MLA_KV_RANK + 1])
        for g in range(groups_per_cc):
            blk = o_t[:, g * LANES:(g + 1) * LANES].T.astype(BF16)
            col0 = c * KEY_TILE + g * LANES
            if bq >= LANES:
                hd, q0 = col0 // bq, col0 % bq
                o_ref[0, q0:q0 + LANES, hd * MLA_KV_RANK:(hd + 1) * MLA_KV_RANK] = blk
            else:
                for hl in range(LANES // bq):
                    hd = col0 // bq + hl
                    o_ref[0, :, hd * MLA_KV_RANK:(hd + 1) * MLA_KV_RANK] = blk[hl * bq:(hl + 1) * bq]


def _attn(qt, kcat, vt, km, vm, *, bq, causal, n_valid_last):
    bsz, nq, n_cc = qt.shape[:3]
    n_kt = kcat.shape[1]
    assert not causal or bq == KEY_TILE
    return pl.pallas_call(
        functools.partial(_attn_kernel, bq=bq, causal=causal, n_valid_last=n_valid_last),
        grid=(bsz, nq),
        in_specs=[pl.BlockSpec((1, 1, n_cc, MLA_CAT, KEY_TILE), lambda b, i: (b, i, 0, 0, 0)),
                  pl.BlockSpec((1, n_kt, KEY_TILE, MLA_CAT), lambda b, i: (b, 0, 0, 0)),
                  pl.BlockSpec((1, n_kt, V_ROWS, KEY_TILE), lambda b, i: (b, 0, 0, 0)),
                  _const_spec(km.shape), _const_spec(vm.shape)],
        out_specs=pl.BlockSpec((1, bq, MLA_HEADS * MLA_KV_RANK), lambda b, i: (b, i, 0)),
        out_shape=jax.ShapeDtypeStruct((bsz, nq * bq, MLA_HEADS * MLA_KV_RANK), BF16),
        scratch_shapes=[pltpu.VMEM((n_cc, KEY_TILE, KEY_TILE), F32),
                        pltpu.VMEM((n_cc, KEY_TILE, KEY_TILE), BF16),
                        pltpu.VMEM((n_cc, 1, KEY_TILE), F32),
                        pltpu.VMEM((n_cc, 1, KEY_TILE), F32),
                        pltpu.VMEM((n_cc, V_ROWS, KEY_TILE), F32)],
        compiler_params=_cparams("parallel", "arbitrary"),
        name="attn",
    )(qt, kcat, vt, km, vm)


_MERGE_WEIGHTS = ("ln_g", "ln_b", "w_g", "gla_g", "w_br_gla", "w_uv_bd", "w_br_mla", "w_mg", "b_mg", "w_out",
                  "ln1_g", "ln1_b", "w_r_hi", "w_r_lo", "b_r")


def _merge_kernel(x_ref, og_ref, ol_ref, lng_ref, lnb_ref, wg_ref, glag_ref, wbg_ref, wuv_ref, wbm_ref,
                  wmg_ref, bmg_ref, wout_ref, l1g_ref, l1b_ref, wrh_ref, wrl_ref, br_ref, cin_ref,
                  h1t_ref, rt_ref, cnt_ref, carry_ref, *, sub):
    tt, d = x_ref.shape

    def branches(rows):
        h = _layer_norm(x_ref[rows, :], lng_ref[...], lnb_ref[...])
        hb = h.astype(BF16)
        g_out = _dot(hb, wg_ref[...])
        gate_pre = _dot(hb, wmg_ref[...])
        pair_in = 2 * MLA_KV_RANK
        y_heads = jnp.concatenate([_dot(ol_ref[rows, p * pair_in:(p + 1) * pair_in], wuv_ref[p])
                                   for p in range(MLA_HEADS // 2)], axis=-1)
        return h, g_out, gate_pre, y_heads

    def mix(rows, h, g_out, gate_pre, y_heads):
        og = og_ref[rows, :]
        parts = []
        for hd in range(GLA_HEADS):
            cols = slice(hd * GLA_DV, (hd + 1) * GLA_DV)
            o_h = og[:, cols]
            g_h = g_out[:, cols]
            o_n = o_h * lax.rsqrt(jnp.mean(o_h * o_h, axis=-1, keepdims=True) + RMS_EPS) * glag_ref[...]
            parts.append(o_n * (g_h * jax.nn.sigmoid(g_h)))
        y_a = _dot(jnp.concatenate(parts, axis=-1).astype(BF16), wbg_ref[...])
        y_b = _dot(y_heads.astype(BF16), wbm_ref[...])
        gates = jax.nn.sigmoid(gate_pre + bmg_ref[...])
        mix_in = gates[:, :d] * y_a + gates[:, d:] * y_b
        return DEEPNORM_ALPHA * h + _dot(mix_in.astype(BF16), wout_ref[...])

    def route(r0, pre):
        h1 = _layer_norm(pre, l1g_ref[...], l1b_ref[...])
        for s in range(ROW_TILE):
            h1t_ref[pl.ds(r0 * ROW_TILE + s, sub, stride=ROW_TILE), :] = h1[:, s * LANES:(s + 1) * LANES]
        h1_hi = h1.astype(BF16)
        h1_lo = (h1 - h1_hi.astype(F32)).astype(BF16)
        logits = (_dot(h1_hi, wrh_ref[...]) + (_dot(h1_hi, wrl_ref[...]) + _dot(h1_lo, wrh_ref[...]))) + br_ref[...]
        lane = lax.broadcasted_iota(jnp.int32, logits.shape, 1)
        is_grp = lane < N_GROUPS
        gl = jnp.where(is_grp, logits, NEG_BIG)
        g_max = jnp.max(gl, axis=-1, keepdims=True)
        g_sel = jnp.min(jnp.where(gl == g_max, lane, ROUTER_LANES), axis=-1, keepdims=True)
        p_grp = 1.0 / jnp.sum(jnp.where(is_grp, jnp.exp(gl - g_max), 0.0), axis=-1, keepdims=True)
        e_lo = N_GROUPS + g_sel * EXPERTS_PER_GROUP
        in_grp = (lane >= e_lo) & (lane < e_lo + EXPERTS_PER_GROUP)
        el = jnp.where(in_grp, logits, NEG_BIG)
        v1 = jnp.max(el, axis=-1, keepdims=True)
        i1 = jnp.min(jnp.where(el == v1, lane, ROUTER_LANES), axis=-1, keepdims=True)
        el2 = jnp.where(lane == i1, NEG_BIG, el)
        v2 = jnp.max(el2, axis=-1, keepdims=True)
        i2 = jnp.min(jnp.where(el2 == v2, lane, ROUTER_LANES), axis=-1, keepdims=True)
        e2 = jnp.exp(v2 - v1)
        w1 = p_grp / (1.0 + e2)
        w2 = p_grp * e2 / (1.0 + e2)
        e1, e2i = i1 - N_GROUPS, i2 - N_GROUPS
        hot = ((lane == e1) | (lane == e2i)).astype(F32)
        earlier = (lax.broadcasted_iota(jnp.int32, (sub, sub), 0) > lax.broadcasted_iota(jnp.int32, (sub, sub), 1))
        before = carry_ref[...] + _dot(earlier.astype(BF16), hot.astype(BF16))
        r1 = jnp.sum(jnp.where(lane == e1, before, 0.0), axis=-1, keepdims=True)
        r2 = jnp.sum(jnp.where(lane == e2i, before, 0.0), axis=-1, keepdims=True)
        carry_ref[...] += jnp.sum(hot, axis=0, keepdims=True)
        fields = (e1.astype(F32), e2i.astype(F32), w1, w2, r1, r2)
        rt = jnp.zeros(logits.shape, F32)
        for k, val in enumerate(fields):
            rt = jnp.where(lane == k, val, rt)
        rt_ref[pl.ds(r0, sub), :] = rt

    @pl.when(pl.program_id(0) == 0)
    def _():
        carry_ref[...] = cin_ref[...]

    pending = None
    for r0 in range(0, tt, sub):
        rows = pl.ds(r0, sub)
        independent = branches(rows)
        if pending is not None:
            route(*pending)
        pending = (r0, mix(rows, *independent))
    route(*pending)
    cnt_ref[...] = carry_ref[...]


def _merge(x2, og2, ol2, wp, cnt_in, *, tt):
    t, d = x2.shape
    assert d == ROW_TILE * LANES
    row = lambda w: pl.BlockSpec((tt, w), lambda i: (i, 0))
    return pl.pallas_call(
        functools.partial(_merge_kernel, sub=min(TOKEN_TILE, tt)),
        grid=(t // tt,),
        in_specs=([row(d), row(GLA_V), row(MLA_HEADS * MLA_KV_RANK)] + [_const_spec(wp[n].shape) for n in _MERGE_WEIGHTS]
                  + [_const_spec(cnt_in.shape)]),
        out_specs=[pl.BlockSpec((tt * ROW_TILE, LANES), lambda i: (i, 0)), row(ROUTER_LANES),
                   pl.BlockSpec((1, ROUTER_LANES), lambda i: (0, 0))],
        out_shape=[jax.ShapeDtypeStruct((t * ROW_TILE, LANES), F32), jax.ShapeDtypeStruct((t, ROUTER_LANES), F32),
                   jax.ShapeDtypeStruct((1, ROUTER_LANES), F32)],
        scratch_shapes=[pltpu.VMEM((1, ROUTER_LANES), F32)],
        compiler_params=_cparams("arbitrary"),
        name="merge",
    )(x2, og2, ol2, *[wp[n] for n in _MERGE_WEIGHTS], cnt_in)


def _row_copy(src_ref, src_row, dst_ref, dst_row, sem):
    return pltpu.make_async_copy(src_ref.at[pl.ds(src_row * ROW_TILE, ROW_TILE)],
                                 dst_ref.at[pl.ds(dst_row * ROW_TILE, ROW_TILE)], sem)


def _group_spans(n_tokens, tile):
    spans, first = [], 0
    for n in n_tokens:
        spans.append((first, n // tile))
        first += n // tile
    return spans


def _group_spec(block, span):
    first, steps = span
    return pl.BlockSpec(block, lambda i, *_: (jnp.clip(i - first, 0, steps - 1), 0))


def _dispatch_kernel(pos_ref, zrow_ref, nu_ref, *refs, td, spans):
    h1t_refs, (xs_ref, zero_ref, sem, zsem) = refs[:len(spans)], refs[len(spans):]
    i = pl.program_id(0)

    @pl.when(i == 0)
    def _():
        zero_ref[...] = jnp.zeros(zero_ref.shape, F32)
        tile_rows = EXPERT_ROWS * ROW_TILE
        fill = lambda row: pltpu.make_async_copy(
            zero_ref, xs_ref.at[pl.ds(pl.multiple_of(row * ROW_TILE, ROW_TILE), tile_rows)], zsem)
        for e in range(N_EXPERTS):
            @pl.when(zrow_ref[e] >= 0)
            def _():
                fill(zrow_ref[e]).start()
        for e in range(N_EXPERTS):
            @pl.when(zrow_ref[e] >= 0)
            def _():
                fill(zrow_ref[e]).wait()

        def fill_tail(r, carry):
            fill(r * EXPERT_ROWS).start()
            fill(r * EXPERT_ROWS).wait()
            return carry

        lax.fori_loop(nu_ref[0], xs_ref.shape[0] // tile_rows, fill_tail, 0)

    base = i * (2 * td)
    for h1t_ref, (first, steps) in zip(h1t_refs, spans):
        @pl.when((i >= first) & (i < first + steps))
        def _():
            for t in range(td):
                for k in range(2):
                    dst = pl.multiple_of(pos_ref[base + 2 * t + k] * ROW_TILE, ROW_TILE)
                    pltpu.make_async_copy(h1t_ref.at[pl.ds(t * ROW_TILE, ROW_TILE)],
                                          xs_ref.at[pl.ds(dst, ROW_TILE)], sem).start(priority=k)
            for _ in range(2 * td):
                _row_copy(h1t_ref, 0, xs_ref, 0, sem).wait()


def _dispatch(h1ts, pos, zrow, n_used, *, n_rows, td):
    spans = _group_spans([h.shape[0] // ROW_TILE for h in h1ts], td)
    return pl.pallas_call(
        functools.partial(_dispatch_kernel, td=td, spans=spans),
        grid_spec=pltpu.PrefetchScalarGridSpec(
            num_scalar_prefetch=3,
            grid=(sum(steps for _, steps in spans),),
            in_specs=[_group_spec((td * ROW_TILE, LANES), span) for span in spans],
            out_specs=pl.BlockSpec(memory_space=pl.ANY),
            scratch_shapes=[pltpu.VMEM((EXPERT_ROWS * ROW_TILE, LANES), F32),
                            pltpu.SemaphoreType.DMA(()), pltpu.SemaphoreType.DMA(())]),
        out_shape=jax.ShapeDtypeStruct((n_rows * ROW_TILE, LANES), F32),
        compiler_params=_cparams("arbitrary"),
        name="dispatch",
    )(pos, zrow, n_used, *h1ts)


def _experts_kernel(te_ref, tb_ref, nu_ref, xs_ref, wg_ref, wu_ref, wd_ref, out_ref, wgb_ref, wub_ref, wdb_ref):
    del tb_ref
    r = pl.program_id(0)
    rows = xs_ref.shape[0] // ROW_TILE

    @pl.when((r == 0) | (te_ref[r] != te_ref[jnp.maximum(r - 1, 0)]))
    def _():
        wgb_ref[...] = wg_ref[0].astype(BF16)
        wub_ref[...] = wu_ref[0].astype(BF16)
        wdb_ref[...] = wd_ref[0].astype(BF16)

    @pl.when(r < nu_ref[0])
    def _():
        x = jnp.concatenate([xs_ref[pl.ds(s, rows, stride=ROW_TILE), :] for s in range(ROW_TILE)], axis=1).astype(BF16)
        gate = _dot(x, wgb_ref[...])
        up = _dot(x, wub_ref[...])
        hid = (gate * jax.nn.sigmoid(gate)) * up
        out = _dot(hid.astype(BF16), wdb_ref[...])
        for s in range(ROW_TILE):
            out_ref[pl.ds(s, rows, stride=ROW_TILE), :] = out[:, s * LANES:(s + 1) * LANES]

    @pl.when(pl.program_id(0) >= nu_ref[0])
    def _():
        out_ref[...] = jnp.zeros(out_ref.shape, F32)


def _experts(xs, tile_e, tile_blk, n_used, wp):
    n_rows = xs.shape[0] // ROW_TILE
    d = ROW_TILE * LANES
    blk = pl.BlockSpec((EXPERT_ROWS * ROW_TILE, LANES), lambda r, te, tb, nu: (tb[r], 0))
    return pl.pallas_call(
        _experts_kernel,
        grid_spec=pltpu.PrefetchScalarGridSpec(
            num_scalar_prefetch=3,
            grid=(n_rows // EXPERT_ROWS,),
            in_specs=[blk,
                      pl.BlockSpec((1, d, D_EXPERT), lambda r, te, tb, nu: (te[r], 0, 0)),
                      pl.BlockSpec((1, d, D_EXPERT), lambda r, te, tb, nu: (te[r], 0, 0)),
                      pl.BlockSpec((1, D_EXPERT, d), lambda r, te, tb, nu: (te[r], 0, 0))],
            out_specs=pl.BlockSpec((EXPERT_ROWS * ROW_TILE, LANES), lambda r, te, tb, nu: (r, 0)),
            scratch_shapes=[pltpu.VMEM((d, D_EXPERT), BF16), pltpu.VMEM((d, D_EXPERT), BF16),
                            pltpu.VMEM((D_EXPERT, d), BF16)]),
        out_shape=jax.ShapeDtypeStruct(xs.shape, F32),
        compiler_params=_cparams("arbitrary"),
        name="experts",
    )(tile_e, tile_blk, n_used, xs, wp["w_gate"], wp["w_up"], wp["w_down"])


def _combine_kernel(pos_ref, *refs, tc, spans):
    g = len(spans)
    h1t_refs, rt_refs = refs[:g], refs[g:2 * g]
    outs_ref, l2g_ref, l2b_ref = refs[2 * g:2 * g + 3]
    y_refs = refs[2 * g + 3:3 * g + 3]
    g_ref, sem = refs[3 * g + 3:]
    i = pl.program_id(0)
    n_steps = pl.num_programs(0)

    def gather(step):
        slot = step % 2
        base = step * (2 * tc)
        for t in range(tc):
            for k in range(2):
                src = pl.multiple_of(pos_ref[base + 2 * t + k] * ROW_TILE, ROW_TILE)
                pltpu.make_async_copy(outs_ref.at[pl.ds(src, ROW_TILE)],
                                      g_ref.at[slot, k, pl.ds(t * ROW_TILE, ROW_TILE)],
                                      sem.at[slot]).start(priority=k)

    @pl.when(i == 0)
    def _():
        gather(0)

    @pl.when(i + 1 < n_steps)
    def _():
        gather(i + 1)

    slot = i % 2
    for _ in range(2 * tc):
        pltpu.make_async_copy(outs_ref.at[pl.ds(0, ROW_TILE)], g_ref.at[slot, 0, pl.ds(0, ROW_TILE)], sem.at[slot]).wait()
    for h1t_ref, rt_ref, y_ref, (first, steps) in zip(h1t_refs, rt_refs, y_refs, spans):
        @pl.when((i >= first) & (i < first + steps))
        def _():
            rt = rt_ref[...]
            w0, w1 = rt[:, RT_W:RT_W + 1], rt[:, RT_W + 1:RT_W + 2]
            cols = []
            for s in range(ROW_TILE):
                rows = pl.ds(s, tc, stride=ROW_TILE)
                cols.append(DEEPNORM_ALPHA * h1t_ref[rows, :] + (w0 * g_ref[slot, 0, rows, :] + w1 * g_ref[slot, 1, rows, :]))
            y_ref[...] = _layer_norm(jnp.concatenate(cols, axis=1), l2g_ref[...], l2b_ref[...])


def _combine(h1ts, rts, outs, pos, wp, *, tc):
    d = ROW_TILE * LANES
    spans = _group_spans([r.shape[0] for r in rts], tc)
    const = lambda: pl.BlockSpec((1, d), lambda i, *_: (0, 0))
    return pl.pallas_call(
        functools.partial(_combine_kernel, tc=tc, spans=spans),
        grid_spec=pltpu.PrefetchScalarGridSpec(
            num_scalar_prefetch=1,
            grid=(sum(steps for _, steps in spans),),
            in_specs=([_group_spec((tc * ROW_TILE, LANES), span) for span in spans]
                      + [_group_spec((tc, ROUTER_LANES), span) for span in spans]
                      + [pl.BlockSpec(memory_space=pl.ANY), const(), const()]),
            out_specs=[_group_spec((tc, d), span) for span in spans],
            scratch_shapes=[pltpu.VMEM((2, 2, tc * ROW_TILE, LANES), F32), pltpu.SemaphoreType.DMA((2,))]),
        out_shape=[jax.ShapeDtypeStruct((r.shape[0], d), F32) for r in rts],
        compiler_params=_cparams("arbitrary"),
        name="combine",
    )(pos, *h1ts, *rts, outs, wp["ln2_g"], wp["ln2_b"])


def _route_plan(rt, cnt):
    t = rt.shape[0]
    n_tiles = (2 * t) // EXPERT_ROWS + N_EXPERTS
    counts = cnt[0, :N_EXPERTS].astype(jnp.int32)
    padded = (counts + (EXPERT_ROWS - 1)) // EXPERT_ROWS * EXPERT_ROWS
    ends = jnp.cumsum(padded)
    starts = ends - padded
    eid = rt[:, RT_E:RT_E + 2].astype(jnp.int32)
    rank = rt[:, RT_RANK:RT_RANK + 2].astype(jnp.int32)
    pos = (jnp.take(starts, eid) + rank).reshape(-1)
    n_used = ends[-1] // EXPERT_ROWS
    tile_blk = jnp.minimum(jnp.arange(n_tiles, dtype=jnp.int32), n_used - 1)
    tile_e = jnp.sum((tile_blk[:, None] * EXPERT_ROWS >= ends[None, :]).astype(jnp.int32), axis=1)
    zrow = jnp.where(padded > 0, ends - EXPERT_ROWS, -1)
    return pos, zrow.astype(jnp.int32), tile_e.astype(jnp.int32), tile_blk, n_used.reshape(1).astype(jnp.int32), n_tiles


def _rope_tables(pos):
    inv = ROPE_THETA ** (-jnp.arange(0, MLA_ROPE, 2, dtype=F32) / MLA_ROPE)
    ang = pos.astype(F32)[:, None] * inv[None, :]
    cs = jnp.concatenate([jnp.cos(ang), jnp.sin(ang)], axis=-1)
    return cs[None], cs.T[None]


def _prep_weights(ln_in_g, ln_in_b, w_in, w_gk2, b_gk, gla_norm_g, q_norm_g, kv_norm_g, w_uq, w_uk, w_uv,
                  w_br_gla, w_br_mla, w_mg, b_mg, w_out, ln1_g, ln1_b, w_rg, b_rg, w_re, b_re,
                  w_gate, w_up, w_down, ln2_g, ln2_b):
    d = w_in.shape[1]
    w = w_in[0]
    c0 = 0
    wk = w[:, c0:c0 + GLA_QK]; c0 += GLA_QK
    wv = w[:, c0:c0 + GLA_V]; c0 += GLA_V
    wgr = w[:, c0:c0 + GLA_RANK]; c0 += GLA_RANK
    wckv = w[:, c0:c0 + MLA_KV_RANK]; c0 += MLA_KV_RANK
    wkr = w[:, c0:c0 + MLA_ROPE]; c0 += MLA_ROPE
    wq = w[:, c0:c0 + GLA_QK]; c0 += GLA_QK
    wg = w[:, c0:c0 + GLA_V]; c0 += GLA_V
    wcq = w[:, c0:c0 + MLA_Q_RANK]
    w_tok = jnp.concatenate([wk, wv, wq, wckv, wkr, wgr, jnp.zeros((d, TOK_PAD - TOK_USED), F32)], axis=1)
    w_tr = jnp.concatenate([wckv, wcq], axis=1).T
    uq = w_uq[0].reshape(MLA_Q_RANK, MLA_HEADS, MLA_QK_DIM)
    uq_perm = jnp.concatenate([
        uq[:, :, :MLA_NOPE].reshape(MLA_Q_RANK, -1),
        uq[:, :, MLA_NOPE:MLA_NOPE + MLA_HALF].reshape(MLA_Q_RANK, -1),
        uq[:, :, MLA_NOPE + MLA_HALF:].reshape(MLA_Q_RANK, -1)], axis=1)
    uv = w_uv[0].transpose(1, 0, 2).reshape(MLA_HEADS // 2, 2, MLA_KV_RANK, MLA_DV)
    eye = jnp.eye(2, dtype=F32)
    w_uv_bd = (uv[:, :, :, None, :] * eye[None, :, None, :, None]).reshape(
        MLA_HEADS // 2, 2 * MLA_KV_RANK, 2 * MLA_DV)
    w_r = jnp.concatenate([w_rg[0], w_re[0].transpose(1, 0, 2).reshape(d, N_EXPERTS),
                           jnp.zeros((d, ROUTER_LANES - N_GROUPS - N_EXPERTS), F32)], axis=1)
    w_r_hi = w_r.astype(BF16)
    b_r = jnp.concatenate([b_rg[0], b_re[0].reshape(-1), jnp.zeros((ROUTER_LANES - N_GROUPS - N_EXPERTS,), F32)])
    row = lambda a: a.reshape(1, -1)
    return {
        "ln_g": row(ln_in_g), "ln_b": row(ln_in_b),
        "w_tok": w_tok.astype(BF16), "w_tr": w_tr.astype(BF16),
        "w_gk2": w_gk2[0].astype(BF16), "b_gk": row(b_gk[0]),
        "kv_g": row(kv_norm_g[0]), "kv_gt": kv_norm_g[0].reshape(-1, 1), "q_gt": q_norm_g[0].reshape(-1, 1),
        "w_uqt": uq_perm.T.astype(BF16), "w_uk": w_uk[0].transpose(1, 0, 2).astype(BF16),
        "w_g": wg.astype(BF16), "gla_g": row(gla_norm_g[0]),
        "w_br_gla": w_br_gla[0].astype(BF16), "w_uv_bd": w_uv_bd.astype(BF16), "w_br_mla": w_br_mla[0].astype(BF16),
        "w_mg": w_mg[0].astype(BF16), "b_mg": row(b_mg[0]), "w_out": w_out[0].astype(BF16),
        "ln1_g": row(ln1_g[0]), "ln1_b": row(ln1_b[0]),
        "w_r_hi": w_r_hi, "w_r_lo": (w_r - w_r_hi.astype(F32)).astype(BF16), "b_r": row(b_r),
        "w_gate": w_gate[0], "w_up": w_up[0], "w_down": w_down[0],
        "ln2_g": row(ln2_g[0]), "ln2_b": row(ln2_b[0]),
    }


def _value_rows(vt):
    lead, length = vt.shape[:-2], vt.shape[-1]
    return jnp.concatenate([vt, jnp.ones(lead + (1, length), vt.dtype),
                            jnp.zeros(lead + (V_ROWS - MLA_KV_RANK - 1, length), vt.dtype)], axis=-2)


def _key_tiles(kcat, vt):
    bsz, length, _ = kcat.shape
    n = -(-length // KEY_TILE)
    pad = n * KEY_TILE - length
    kcat = jnp.pad(kcat, ((0, 0), (0, pad), (0, 0)))
    vt = jnp.pad(_value_rows(vt), ((0, 0), (0, 0), (0, pad)))
    return (kcat.reshape(bsz, n, KEY_TILE, MLA_CAT),
            vt.reshape(bsz, V_ROWS, n, KEY_TILE).transpose(0, 2, 1, 3))


def _ffn(groups, wp):
    h1ts, rts = [], []
    cnt = jnp.zeros((1, ROUTER_LANES), F32)
    for x, og, ol in groups:
        t, d = x.shape[0] * x.shape[1], x.shape[2]
        h1t, rt, cnt = _merge(x.reshape(t, d), og.reshape(t, -1), ol.reshape(t, -1), wp, cnt, tt=min(MERGE_TILE, t))
        h1ts.append(h1t)
        rts.append(rt)
    n_fields = RT_RANK + 2
    pos, zrow, tile_e, tile_blk, n_used, n_tiles = _route_plan(
        jnp.concatenate([rt[:, :n_fields] for rt in rts], axis=0), cnt)
    xs = _dispatch(h1ts, pos, zrow, n_used, n_rows=n_tiles * EXPERT_ROWS, td=MOE_TOKEN_TILE)
    outs = _experts(xs, tile_e, tile_blk, n_used, wp)
    ys = _combine(h1ts, rts, outs, pos, wp, tc=MOE_TOKEN_TILE)
    return [y.reshape(x.shape) for y, (x, _, _) in zip(ys, groups)]


def kernel(x_prompt, x_sample, cache_mla_latent, cache_mla_krope, state_gla, meta_tokens, ln_in_g, ln_in_b, w_in, w_gk2, b_gk, gla_norm_g, q_norm_g, kv_norm_g, w_uq, w_uk, w_uv, w_br_gla, w_br_mla, w_mg, b_mg, w_out, ln1_g, ln1_b, w_rg, b_rg, w_re, b_re, w_gate, w_up, w_down, ln2_g, ln2_b):
    bp, sp, d = x_prompt.shape
    bs, ss, _ = x_sample.shape
    past = cache_mla_latent.shape[2]
    wp = _prep_weights(ln_in_g, ln_in_b, w_in, w_gk2, b_gk, gla_norm_g, q_norm_g, kv_norm_g, w_uq, w_uk, w_uv,
                       w_br_gla, w_br_mla, w_mg, b_mg, w_out, ln1_g, ln1_b, w_rg, b_rg, w_re, b_re,
                       w_gate, w_up, w_down, ln2_g, ln2_b)

    cs_m, cst_m = _rope_tables(jnp.arange(-N_META, 0, dtype=jnp.int32))
    m = _proj(meta_tokens[None], cs_m, cst_m, wp, cl=N_META, tt=N_META)
    _, _, m_kt, m_v, m_ebl, m_ckv, m_kr, m_kcat, m_vt, _ = m
    zero_state = jnp.zeros((1, GLA_HEADS, GLA_DK, GLA_DV), F32)
    _, m_state = _gla(m[0], m[1], m_kt, m_v, m_ebl, zero_state, cl=N_META, ts=N_META)

    cs_p, cst_p = _rope_tables(jnp.arange(sp, dtype=jnp.int32))
    p_qd, p_ki, p_kt, p_v, p_ebl, p_ckv, p_kr, p_kcat, p_vt, p_qt = _proj(
        x_prompt, cs_p, cst_p, wp, cl=CHUNK, tt=PROJ_TILE)
    p_o, p_state = _gla(p_qd, p_ki, p_kt, p_v, p_ebl, m_state, cl=CHUNK, ts=TOKEN_TILE)
    rep = lambda a, n: jnp.broadcast_to(a, (n,) + a.shape[1:])
    lat_p = jnp.concatenate([rep(m_ckv, bp), p_ckv], axis=1)
    kr_p = jnp.concatenate([rep(m_kr, bp), p_kr], axis=1)
    kcat_p, vt_p = _key_tiles(p_kcat, p_vt)
    km, vm = m_kcat[0], _value_rows(m_vt[0])
    p_ol = _attn(p_qt, kcat_p, vt_p, km, vm, bq=TOKEN_TILE, causal=True, n_valid_last=KEY_TILE)

    ts_all = bs * ss
    cs_s, cst_s = _rope_tables(past + (jnp.arange(ts_all, dtype=jnp.int32) % ss))
    s_qd, s_ki, s_kt, s_v, s_ebl, s_ckv, s_kr, s_kcat, s_vt, s_qt = _proj(
        x_sample.reshape(1, ts_all, d), cs_s, cst_s, wp, cl=ss, tt=min(TOKEN_TILE, ts_all))
    per_stream = lambda a: a.reshape(bs, ss, a.shape[-1])
    s_o, s_state = _gla(per_stream(s_qd), per_stream(s_ki), per_stream(s_kt), per_stream(s_v),
                        s_ebl.reshape(bs, 1, 1, GLA_QK), state_gla[0].astype(F32), cl=ss, ts=ss)
    s_ckv, s_kr = per_stream(s_ckv), per_stream(s_kr)
    cache_kcat = jnp.concatenate([cache_mla_latent[0], cache_mla_krope[0]], axis=-1).astype(BF16)
    cache_vt = cache_mla_latent[0].astype(BF16).transpose(0, 2, 1)
    new_vt = s_vt.reshape(MLA_KV_RANK, bs, ss).transpose(1, 0, 2)
    kcat_s, vt_s = _key_tiles(jnp.concatenate([cache_kcat, per_stream(s_kcat)], axis=1),
                              jnp.concatenate([cache_vt, new_vt], axis=2))
    qt = s_qt.transpose(0, 2, 3, 1, 4).reshape(MLA_HEADS, MLA_CAT, bs, ss)
    qt = qt.transpose(2, 1, 0, 3).reshape(bs, MLA_CAT, MLA_HEADS * ss // KEY_TILE, KEY_TILE)
    qt = qt.transpose(0, 2, 1, 3)[:, None]
    s_ol = _attn(qt, kcat_s, vt_s, km, vm, bq=ss, causal=False, n_valid_last=(past + ss - 1) % KEY_TILE + 1)
    y_prompt, y_sample = _ffn([(x_prompt, p_o, p_ol), (x_sample, s_o, s_ol)], wp)

    return (y_prompt, y_sample, lat_p[None], kr_p[None], p_state[None].astype(state_gla.dtype),
            s_ckv[None], s_kr[None], s_state[None].astype(state_gla.dtype))
```

```python
import functools

import jax
import jax.numpy as jnp
from jax import lax
from jax.experimental import pallas as pl
from jax.experimental.pallas import tpu as pltpu

F32 = jnp.float32
BF16 = jnp.bfloat16

CHUNK = 64
N_META = 16
GLA_HEADS = 4
GLA_DK = 128
GLA_DV = 256
GLA_RANK = 16
GLA_TAU = 16.0
GLA_QK = GLA_HEADS * GLA_DK
GLA_V = GLA_HEADS * GLA_DV
GLA_SCALE = GLA_DK ** -0.5
MLA_HEADS = 16
MLA_Q_RANK = 384
MLA_KV_RANK = 128
MLA_NOPE = 64
MLA_ROPE = 32
MLA_HALF = MLA_ROPE // 2
MLA_DV = 64
MLA_QK_DIM = MLA_NOPE + MLA_ROPE
MLA_CAT = MLA_KV_RANK + MLA_ROPE
MLA_SCALE = MLA_QK_DIM ** -0.5
LOG2_E = 1.4426950408889634
Q_SCALE = MLA_SCALE * LOG2_E
V_ROWS = MLA_KV_RANK + 16
ROPE_THETA = 10000.0
N_GROUPS = 4
EXPERTS_PER_GROUP = 8
N_EXPERTS = N_GROUPS * EXPERTS_PER_GROUP
D_EXPERT = 256
LN_EPS = 1e-5
RMS_EPS = 1e-6
DEEPNORM_ALPHA = 2.0 ** 0.25

LANES = 128
MXU_DIM = 256
VMEM_LIMIT_BYTES = 56 * 1024 * 1024

TOKEN_TILE = 256
MERGE_TILE = 2 * TOKEN_TILE
PROJ_TILE = 2 * TOKEN_TILE
KEY_TILE = MXU_DIM
MOE_TOKEN_TILE = 256
EXPERT_ROWS = 256
ROW_TILE = 8
ROUTER_LANES = LANES
RT_E, RT_W, RT_RANK = 0, 2, 4
NEG_BIG = -1e30

NT_DIMS = (((1,), (1,)), ((), ()))
TN_DIMS = (((0,), (0,)), ((), ()))


def _cparams(*sem):
    return pltpu.CompilerParams(dimension_semantics=sem, vmem_limit_bytes=VMEM_LIMIT_BYTES)


def _const_spec(shape):
    nd = len(shape)
    return pl.BlockSpec(shape, lambda *_: (0,) * nd, pipeline_mode=pl.Buffered(1))


def _layer_norm(x, g, b):
    mu = jnp.mean(x, axis=-1, keepdims=True)
    xc = x - mu
    var = jnp.mean(xc * xc, axis=-1, keepdims=True)
    return xc * lax.rsqrt(var + LN_EPS) * g + b


def _dot(a, b):
    return jnp.dot(a, b, preferred_element_type=F32)


TOK_COLS = (GLA_QK, GLA_V, GLA_QK, MLA_KV_RANK, MLA_ROPE, GLA_RANK)
TOK_USED = sum(TOK_COLS)
TOK_PAD = -(-TOK_USED // MXU_DIM) * MXU_DIM
TR_ROWS = MLA_KV_RANK + MLA_Q_RANK


def _proj_kernel(x_ref, cs_ref, cst_ref, lng_ref, lnb_ref, wtok_ref, wtr_ref, wgk2_ref, bgk_ref,
                 kvg_ref, kvgt_ref, qngt_ref, wuqt_ref, wuk_ref,
                 qd_ref, ki_ref, kt_ref, v_ref, ebl_ref, ckv_ref, kr_ref, kcat_ref, vt_ref, qt_ref, *, cl, sub):
    tt = x_ref.shape[1]
    n_chunks = sub // cl

    def project(r0):
        h = _layer_norm(x_ref[0, r0:r0 + sub, :], lng_ref[...], lnb_ref[...])
        hb = h.astype(BF16)
        z = _dot(hb, wtok_ref[...])
        zt = lax.dot_general(wtr_ref[...], hb, NT_DIMS, preferred_element_type=F32)
        return z, zt

    def finish(r0, z, zt):
        rows = slice(r0, r0 + sub)
        o0 = 0
        k = z[:, o0:o0 + GLA_QK]; o0 += GLA_QK
        v = z[:, o0:o0 + GLA_V]; o0 += GLA_V
        q = z[:, o0:o0 + GLA_QK]; o0 += GLA_QK
        ckv_raw = z[:, o0:o0 + MLA_KV_RANK]; o0 += MLA_KV_RANK
        kr = z[:, o0:o0 + MLA_ROPE]; o0 += MLA_ROPE
        gklr = z[:, o0:o0 + GLA_RANK]

        gz = _dot(gklr.astype(BF16), wgk2_ref[...]) + bgk_ref[...]
        gk = (jnp.minimum(gz, 0.0) - jnp.log(1.0 + jnp.exp(-jnp.abs(gz)))) * (1.0 / GLA_TAU)
        row_in_chunk = lax.broadcasted_iota(jnp.int32, gk.shape, 0) & (cl - 1)
        b = gk
        shift = 1
        while shift < cl:
            b = b + jnp.where(row_in_chunk >= shift, pltpu.roll(b, shift, 0), 0.0)
            shift *= 2
        b3 = b.reshape(n_chunks, cl, GLA_QK)
        bl = b3[:, cl - 1:cl, :]
        qd_ref[0, rows, :] = (q * GLA_SCALE * jnp.exp(b)).astype(BF16)
        ki_ref[0, rows, :] = (k * jnp.exp(-b)).astype(BF16)
        kt_ref[0, rows, :] = (k.reshape(n_chunks, cl, GLA_QK) * jnp.exp(bl - b3)).reshape(sub, GLA_QK).astype(BF16)
        v_ref[0, rows, :] = v.astype(BF16)
        ebl_ref[0, r0 // cl:r0 // cl + n_chunks] = jnp.exp(bl)

        ckv = ckv_raw * lax.rsqrt(jnp.mean(ckv_raw * ckv_raw, axis=-1, keepdims=True) + RMS_EPS) * kvg_ref[...]
        cs = cs_ref[0, rows, :]
        cos, sin = cs[:, :MLA_HALF], cs[:, MLA_HALF:]
        x1, x2 = kr[:, :MLA_HALF], kr[:, MLA_HALF:]
        kr_rot = jnp.concatenate([x1 * cos - x2 * sin, x2 * cos + x1 * sin], axis=-1)
        ckv_ref[0, rows, :] = ckv
        kr_ref[0, rows, :] = kr_rot
        kcat_ref[0, rows, :MLA_KV_RANK] = ckv.astype(BF16)
        kcat_ref[0, rows, MLA_KV_RANK:] = kr_rot.astype(BF16)

        ckvt = zt[:MLA_KV_RANK]
        ckvt = ckvt * lax.rsqrt(jnp.mean(ckvt * ckvt, axis=0, keepdims=True) + RMS_EPS) * kvgt_ref[...]
        vt_ref[0, :, rows] = ckvt.astype(BF16)
        cqt = zt[MLA_KV_RANK:]
        cqt = cqt * lax.rsqrt(jnp.mean(cqt * cqt, axis=0, keepdims=True) + RMS_EPS) * qngt_ref[...]
        qmt = _dot(wuqt_ref[...], cqt.astype(BF16))
        n_nope = MLA_HEADS * MLA_NOPE
        n_half = MLA_HEADS * MLA_HALF
        cst = cst_ref[0, :, rows]
        cos_t = jnp.concatenate([cst[:MLA_HALF]] * MLA_HEADS, axis=0)
        sin_t = jnp.concatenate([cst[MLA_HALF:]] * MLA_HEADS, axis=0)
        r1 = qmt[n_nope:n_nope + n_half]
        r2 = qmt[n_nope + n_half:]
        rot1 = ((r1 * cos_t - r2 * sin_t) * Q_SCALE).astype(BF16)
        rot2 = ((r2 * cos_t + r1 * sin_t) * Q_SCALE).astype(BF16)
        blk = r0 // sub
        for hd in range(MLA_HEADS):
            nope = qmt[hd * MLA_NOPE:(hd + 1) * MLA_NOPE].astype(BF16)
            qlat = _dot(wuk_ref[hd], nope) * Q_SCALE
            qt_ref[0, blk, hd, :MLA_KV_RANK, :] = qlat.astype(BF16)
            qt_ref[0, blk, hd, MLA_KV_RANK:MLA_KV_RANK + MLA_HALF, :] = rot1[hd * MLA_HALF:(hd + 1) * MLA_HALF]
            qt_ref[0, blk, hd, MLA_KV_RANK + MLA_HALF:, :] = rot2[hd * MLA_HALF:(hd + 1) * MLA_HALF]

    starts = list(range(0, tt, sub))
    projected = [project(r0) for r0 in starts]
    for r0, (z, zt) in zip(starts, projected):
        finish(r0, z, zt)


def _proj(x, cs, cst, wp, *, cl, tt):
    bsz, s, d = x.shape
    nt = s // tt
    n_chunks = tt // cl
    sub = min(TOKEN_TILE, tt)
    tok = lambda w: pl.BlockSpec((1, tt, w), lambda b, t: (b, t, 0))
    in_specs = [
        tok(d),
        pl.BlockSpec((1, tt, MLA_ROPE), lambda b, t: (0, t, 0)),
        pl.BlockSpec((1, MLA_ROPE, tt), lambda b, t: (0, 0, t)),
    ] + [_const_spec(wp[n].shape) for n in _PROJ_WEIGHTS]
    out_shape = [
        jax.ShapeDtypeStruct((bsz, s, GLA_QK), BF16),
        jax.ShapeDtypeStruct((bsz, s, GLA_QK), BF16),
        jax.ShapeDtypeStruct((bsz, s, GLA_QK), BF16),
        jax.ShapeDtypeStruct((bsz, s, GLA_V), BF16),
        jax.ShapeDtypeStruct((bsz, s // cl, 1, GLA_QK), F32),
        jax.ShapeDtypeStruct((bsz, s, MLA_KV_RANK), F32),
        jax.ShapeDtypeStruct((bsz, s, MLA_ROPE), F32),
        jax.ShapeDtypeStruct((bsz, s, MLA_CAT), BF16),
        jax.ShapeDtypeStruct((bsz, MLA_KV_RANK, s), BF16),
        jax.ShapeDtypeStruct((bsz, s // sub, MLA_HEADS, MLA_CAT, sub), BF16),
    ]
    out_specs = [
        tok(GLA_QK), tok(GLA_QK), tok(GLA_QK), tok(GLA_V),
        pl.BlockSpec((1, n_chunks, 1, GLA_QK), lambda b, t: (b, t, 0, 0)),
        tok(MLA_KV_RANK), tok(MLA_ROPE), tok(MLA_CAT),
        pl.BlockSpec((1, MLA_KV_RANK, tt), lambda b, t: (b, 0, t)),
        pl.BlockSpec((1, tt // sub, MLA_HEADS, MLA_CAT, sub), lambda b, t: (b, t, 0, 0, 0)),
    ]
    return pl.pallas_call(
        functools.partial(_proj_kernel, cl=cl, sub=sub),
        grid=(bsz, nt),
        in_specs=in_specs,
        out_specs=out_specs,
        out_shape=out_shape,
        compiler_params=_cparams("parallel", "parallel"),
        name="proj",
    )(x, cs, cst, *[wp[n] for n in _PROJ_WEIGHTS])


_PROJ_WEIGHTS = ("ln_g", "ln_b", "w_tok", "w_tr", "w_gk2", "b_gk", "kv_g", "kv_gt", "q_gt", "w_uqt", "w_uk")


def _gla_kernel(qd_ref, ki_ref, kt_ref, v_ref, ebl_ref, s0_ref, o_ref, sfin_ref, st_ref, *, cl):
    t = pl.program_id(1)
    ts = qd_ref.shape[1]

    @pl.when(t == 0)
    def _():
        for hd in range(GLA_HEADS):
            st_ref[hd] = s0_ref[0, hd].T

    row = lax.broadcasted_iota(jnp.int32, (ts, ts), 0)
    col = lax.broadcasted_iota(jnp.int32, (ts, ts), 1)
    keep = (row >= col) & ((row & -cl) == (col & -cl))
    qk = lambda ref, hd: ref[0, :, hd * GLA_DK:(hd + 1) * GLA_DK]
    val = lambda hd: v_ref[0, :, hd * GLA_DV:(hd + 1) * GLA_DV]
    for hd in range(GLA_HEADS):
        a = lax.dot_general(qk(qd_ref, hd), qk(ki_ref, hd), NT_DIMS, preferred_element_type=F32)
        a = jnp.where(keep, a, 0.0).astype(BF16)
        o_ref[0, :, hd * GLA_DV:(hd + 1) * GLA_DV] = _dot(a, val(hd))
    for c in range(ts // cl):
        rows = slice(c * cl, (c + 1) * cl)
        for hd in range(GLA_HEADS):
            st = st_ref[hd]
            o_ref[0, rows, hd * GLA_DV:(hd + 1) * GLA_DV] += lax.dot_general(
                qk(qd_ref, hd)[rows], st.astype(BF16), NT_DIMS, preferred_element_type=F32)
            st_ref[hd] = (st * ebl_ref[0, c, :, hd * GLA_DK:(hd + 1) * GLA_DK]
                          + lax.dot_general(val(hd)[rows], qk(kt_ref, hd)[rows], TN_DIMS, preferred_element_type=F32))

    @pl.when(t == pl.num_programs(1) - 1)
    def _():
        for hd in range(GLA_HEADS):
            sfin_ref[0, hd] = st_ref[hd].T


def _gla(qd, ki, kt, v, ebl, s0, *, cl, ts):
    bsz, s, _ = qd.shape
    n_chunks = ts // cl
    s0_b = s0.shape[0]
    qk_spec = pl.BlockSpec((1, ts, GLA_QK), lambda b, t: (b, t, 0))
    v_spec = pl.BlockSpec((1, ts, GLA_V), lambda b, t: (b, t, 0))
    st_spec = pl.BlockSpec((1, GLA_HEADS, GLA_DK, GLA_DV), lambda b, t: (b, 0, 0, 0))
    s0_spec = st_spec if s0_b == bsz else pl.BlockSpec((1, GLA_HEADS, GLA_DK, GLA_DV), lambda b, t: (0, 0, 0, 0))
    return pl.pallas_call(
        functools.partial(_gla_kernel, cl=cl),
        grid=(bsz, s // ts),
        in_specs=[qk_spec, qk_spec, qk_spec, v_spec,
                  pl.BlockSpec((1, n_chunks, 1, GLA_QK), lambda b, t: (b, t, 0, 0)),
                  s0_spec],
        out_specs=[v_spec, st_spec],
        out_shape=[jax.ShapeDtypeStruct((bsz, s, GLA_V), F32),
                   jax.ShapeDtypeStruct((bsz, GLA_HEADS, GLA_DK, GLA_DV), F32)],
        scratch_shapes=[pltpu.VMEM((GLA_HEADS, GLA_DV, GLA_DK), F32)],
        compiler_params=_cparams("parallel", "arbitrary"),
        name="gla",
    )(qd, ki, kt, v, ebl, s0)


SCRATCH_PER_SET = 6


def _attn_kernel(qt_ref, kcat_ref, vt_ref, km_ref, vm_ref, o_ref, *scratch, bq, causal, n_valid_last):
    n_sets = qt_ref.shape[1]
    n_cc = qt_ref.shape[3]
    n_kt = kcat_ref.shape[1]
    groups_per_cc = KEY_TILE // LANES

    def when(cond, fn):
        if isinstance(cond, bool):
            if cond:
                fn()
        else:
            pl.when(cond)(fn)

    def block(h):
        s_ref, p_ref, m_ref, a_ref, acc_ref, mx_ref = scratch[SCRATCH_PER_SET * h:SCRATCH_PER_SET * (h + 1)]
        blk = pl.program_id(1) + h * pl.num_programs(1)
        n_int = blk if causal else n_kt - 1
        q = lambda c: qt_ref[0, h, 0, c]

        def scores(c, kt):
            s = _dot(kt, q(c))
            s_ref[c] = s
            mx_ref[c] = jnp.max(s, axis=0, keepdims=True)

        def softmax(c, mask):
            s = s_ref[c]
            if mask is None:
                s_max = mx_ref[c]
            else:
                s = jnp.where(mask, s, NEG_BIG)
                s_max = jnp.max(s, axis=0, keepdims=True)
            m_prev = m_ref[c]
            m_new = jnp.maximum(m_prev, s_max)
            a_ref[c] = jnp.exp2(m_prev - m_new)
            m_ref[c] = m_new
            p_ref[c] = jnp.exp2(s - m_new).astype(BF16)

        def values(c, vt):
            acc_ref[c] = a_ref[c] * acc_ref[c] + _dot(vt, p_ref[c])

        def start():
            km, vm = km_ref[...], vm_ref[...]
            sm = [_dot(km, q(c)) for c in range(n_cc)]
            pm = []
            for c in range(n_cc):
                m0 = jnp.max(sm[c], axis=0, keepdims=True)
                m_ref[c] = m0
                pm.append(jnp.exp2(sm[c] - m0).astype(BF16))
            for c in range(n_cc):
                acc_ref[c] = _dot(vm, pm[c])
            kt = kcat_ref[0, 0]
            for c in range(n_cc):
                scores(c, kt)

        def first_step():
            kt_next = kcat_ref[0, 1]
            for c in range(n_cc):
                softmax(c, None)
                scores(c, kt_next)

        def no_step():
            a_ref[...] = jnp.ones(a_ref.shape, F32)
            p_ref[...] = jnp.zeros(p_ref.shape, BF16)

        def key_step(j):
            kt_next = kcat_ref[0, j + 1]
            vt_prev = vt_ref[0, j - 1]
            for c in range(n_cc):
                values(c, vt_prev)
                softmax(c, None)
                scores(c, kt_next)

        def key_step_pair(jj, carry):
            key_step(2 * jj + 1)
            key_step(2 * jj + 2)
            return carry

        def unmasked_tiles():
            when(n_int >= 1, first_step)
            when(n_int == 0, no_step)
            n_rest = jnp.maximum(n_int - 1, 0) if causal else max(n_int - 1, 0)
            lax.fori_loop(0, n_rest // 2, key_step_pair, 0)
            when(n_rest % 2 == 1, lambda: key_step(n_int - 1))

        def last_values_of_unmasked():
            vt_prev = vt_ref[0, jnp.maximum(n_int - 1, 0)]
            for c in range(n_cc):
                values(c, vt_prev)

        def masked_tile_and_emit():
            row = lax.broadcasted_iota(jnp.int32, (KEY_TILE, KEY_TILE), 0)
            if causal:
                col = lax.broadcasted_iota(jnp.int32, (KEY_TILE, KEY_TILE), 1)
                mask = (row >> 6) <= (col >> 6)
            else:
                mask = row < n_valid_last
            vt_last = vt_ref[0, n_int]
            value_lag = 2
            for c in range(n_cc + value_lag):
                if c < n_cc:
                    softmax(c, mask)
                if c >= value_lag:
                    values(c - value_lag, vt_last)
            for c in range(n_cc):
                acc = acc_ref[c]
                o_t = acc[:MLA_KV_RANK] * (1.0 / acc[MLA_KV_RANK:MLA_KV_RANK + 1])
                for g in range(groups_per_cc):
                    tile = o_t[:, g * LANES:(g + 1) * LANES].T.astype(BF16)
                    col0 = c * KEY_TILE + g * LANES
                    if bq >= LANES:
                        hd, q0 = col0 // bq, col0 % bq
                        o_ref[0, h, q0:q0 + LANES, hd * MLA_KV_RANK:(hd + 1) * MLA_KV_RANK] = tile
                    else:
                        for hl in range(LANES // bq):
                            hd = col0 // bq + hl
                            o_ref[0, h, :, hd * MLA_KV_RANK:(hd + 1) * MLA_KV_RANK] = tile[hl * bq:(hl + 1) * bq]

        return start, unmasked_tiles, last_values_of_unmasked, masked_tile_and_emit

    blocks = [block(h) for h in range(n_sets)]
    blocks[0][0]()
    for h in range(n_sets):
        start, unmasked_tiles, last_values_of_unmasked, masked_tile_and_emit = blocks[h]
        unmasked_tiles()
        last_values_of_unmasked()
        if h + 1 < n_sets:
            blocks[h + 1][0]()
        masked_tile_and_emit()


def _attn(qt, kcat, vt, km, vm, *, bq, causal, n_valid_last, n_sets):
    bsz, nq, n_cc = qt.shape[:3]
    n_kt = kcat.shape[1]
    assert not causal or bq == KEY_TILE
    per_set = nq // n_sets
    width = MLA_HEADS * MLA_KV_RANK
    scratch_set = [pltpu.VMEM((n_cc, KEY_TILE, KEY_TILE), F32),
                   pltpu.VMEM((n_cc, KEY_TILE, KEY_TILE), BF16),
                   pltpu.VMEM((n_cc, 1, KEY_TILE), F32),
                   pltpu.VMEM((n_cc, 1, KEY_TILE), F32),
                   pltpu.VMEM((n_cc, V_ROWS, KEY_TILE), F32),
                   pltpu.VMEM((n_cc, 1, KEY_TILE), F32)]
    assert len(scratch_set) == SCRATCH_PER_SET
    out = pl.pallas_call(
        functools.partial(_attn_kernel, bq=bq, causal=causal, n_valid_last=n_valid_last),
        grid=(bsz, per_set),
        in_specs=[pl.BlockSpec((1, n_sets, 1, n_cc, MLA_CAT, KEY_TILE), lambda b, i: (b, 0, i, 0, 0, 0)),
                  pl.BlockSpec((1, n_kt, KEY_TILE, MLA_CAT), lambda b, i: (b, 0, 0, 0)),
                  pl.BlockSpec((1, n_kt, V_ROWS, KEY_TILE), lambda b, i: (b, 0, 0, 0)),
                  _const_spec(km.shape), _const_spec(vm.shape)],
        out_specs=pl.BlockSpec((1, n_sets, bq, width), lambda b, i: (b, 0, i, 0)),
        out_shape=jax.ShapeDtypeStruct((bsz, n_sets, per_set * bq, width), BF16),
        scratch_shapes=scratch_set * n_sets,
        compiler_params=_cparams("parallel", "arbitrary"),
        name="attn",
    )(qt.reshape(bsz, n_sets, per_set, *qt.shape[2:]), kcat, vt, km, vm)
    return out.reshape(bsz, nq * bq, width)


_MERGE_WEIGHTS = ("ln_g", "ln_b", "w_g", "gla_g", "w_br_gla", "w_uv_bd", "w_br_mla", "w_mg", "b_mg", "w_out",
                  "ln1_g", "ln1_b", "w_r_hi", "w_r_lo", "b_r")


def _merge_kernel(x_ref, og_ref, ol_ref, lng_ref, lnb_ref, wg_ref, glag_ref, wbg_ref, wuv_ref, wbm_ref,
                  wmg_ref, bmg_ref, wout_ref, l1g_ref, l1b_ref, wrh_ref, wrl_ref, br_ref, cin_ref,
                  h1t_ref, rt_ref, cnt_ref, carry_ref, *, sub):
    tt, d = x_ref.shape

    def branches(rows):
        h = _layer_norm(x_ref[rows, :], lng_ref[...], lnb_ref[...])
        hb = h.astype(BF16)
        g_out = _dot(hb, wg_ref[...])
        gate_pre = _dot(hb, wmg_ref[...])
        pair_in = 2 * MLA_KV_RANK
        y_heads = jnp.concatenate([_dot(ol_ref[rows, p * pair_in:(p + 1) * pair_in], wuv_ref[p])
                                   for p in range(MLA_HEADS // 2)], axis=-1)
        return h, g_out, gate_pre, y_heads

    def mix(rows, h, g_out, gate_pre, y_heads):
        og = og_ref[rows, :]
        parts = []
        for hd in range(GLA_HEADS):
            cols = slice(hd * GLA_DV, (hd + 1) * GLA_DV)
            o_h = og[:, cols]
            g_h = g_out[:, cols]
            o_n = o_h * lax.rsqrt(jnp.mean(o_h * o_h, axis=-1, keepdims=True) + RMS_EPS) * glag_ref[...]
            parts.append(o_n * (g_h * jax.nn.sigmoid(g_h)))
        y_a = _dot(jnp.concatenate(parts, axis=-1).astype(BF16), wbg_ref[...])
        y_b = _dot(y_heads.astype(BF16), wbm_ref[...])
        gates = jax.nn.sigmoid(gate_pre + bmg_ref[...])
        mix_in = gates[:, :d] * y_a + gates[:, d:] * y_b
        return DEEPNORM_ALPHA * h + _dot(mix_in.astype(BF16), wout_ref[...])

    def route(r0, pre):
        h1 = _layer_norm(pre, l1g_ref[...], l1b_ref[...])
        for s in range(ROW_TILE):
            h1t_ref[pl.ds(r0 * ROW_TILE + s, sub, stride=ROW_TILE), :] = h1[:, s * LANES:(s + 1) * LANES]
        h1_hi = h1.astype(BF16)
        h1_lo = (h1 - h1_hi.astype(F32)).astype(BF16)
        logits = (_dot(h1_hi, wrh_ref[...]) + (_dot(h1_hi, wrl_ref[...]) + _dot(h1_lo, wrh_ref[...]))) + br_ref[...]
        lane = lax.broadcasted_iota(jnp.int32, logits.shape, 1)
        is_grp = lane < N_GROUPS
        gl = jnp.where(is_grp, logits, NEG_BIG)
        g_max = jnp.max(gl, axis=-1, keepdims=True)
        g_sel = jnp.min(jnp.where(gl == g_max, lane, ROUTER_LANES), axis=-1, keepdims=True)
        p_grp = 1.0 / jnp.sum(jnp.where(is_grp, jnp.exp(gl - g_max), 0.0), axis=-1, keepdims=True)
        e_lo = N_GROUPS + g_sel * EXPERTS_PER_GROUP
        in_grp = (lane >= e_lo) & (lane < e_lo + EXPERTS_PER_GROUP)
        el = jnp.where(in_grp, logits, NEG_BIG)
        v1 = jnp.max(el, axis=-1, keepdims=True)
        i1 = jnp.min(jnp.where(el == v1, lane, ROUTER_LANES), axis=-1, keepdims=True)
        el2 = jnp.where(lane == i1, NEG_BIG, el)
        v2 = jnp.max(el2, axis=-1, keepdims=True)
        i2 = jnp.min(jnp.where(el2 == v2, lane, ROUTER_LANES), axis=-1, keepdims=True)
        e2 = jnp.exp(v2 - v1)
        w1 = p_grp / (1.0 + e2)
        w2 = p_grp * e2 / (1.0 + e2)
        e1, e2i = i1 - N_GROUPS, i2 - N_GROUPS
        hot = ((lane == e1) | (lane == e2i)).astype(F32)
        earlier = (lax.broadcasted_iota(jnp.int32, (sub, sub), 0) > lax.broadcasted_iota(jnp.int32, (sub, sub), 1))
        before = carry_ref[...] + _dot(earlier.astype(BF16), hot.astype(BF16))
        r1 = jnp.sum(jnp.where(lane == e1, before, 0.0), axis=-1, keepdims=True)
        r2 = jnp.sum(jnp.where(lane == e2i, before, 0.0), axis=-1, keepdims=True)
        carry_ref[...] += jnp.sum(hot, axis=0, keepdims=True)
        fields = (e1.astype(F32), e2i.astype(F32), w1, w2, r1, r2)
        rt = jnp.zeros(logits.shape, F32)
        for k, val in enumerate(fields):
            rt = jnp.where(lane == k, val, rt)
        rt_ref[pl.ds(r0, sub), :] = rt

    @pl.when(pl.program_id(0) == 0)
    def _():
        carry_ref[...] = cin_ref[...]

    pending = None
    for r0 in range(0, tt, sub):
        rows = pl.ds(r0, sub)
        independent = branches(rows)
        if pending is not None:
            route(*pending)
        pending = (r0, mix(rows, *independent))
    route(*pending)
    cnt_ref[...] = carry_ref[...]


def _merge(x2, og2, ol2, wp, cnt_in, *, tt):
    t, d = x2.shape
    assert d == ROW_TILE * LANES
    row = lambda w: pl.BlockSpec((tt, w), lambda i: (i, 0))
    return pl.pallas_call(
        functools.partial(_merge_kernel, sub=min(TOKEN_TILE, tt)),
        grid=(t // tt,),
        in_specs=([row(d), row(GLA_V), row(MLA_HEADS * MLA_KV_RANK)] + [_const_spec(wp[n].shape) for n in _MERGE_WEIGHTS]
                  + [_const_spec(cnt_in.shape)]),
        out_specs=[pl.BlockSpec((tt * ROW_TILE, LANES), lambda i: (i, 0)), row(ROUTER_LANES),
                   pl.BlockSpec((1, ROUTER_LANES), lambda i: (0, 0))],
        out_shape=[jax.ShapeDtypeStruct((t * ROW_TILE, LANES), F32), jax.ShapeDtypeStruct((t, ROUTER_LANES), F32),
                   jax.ShapeDtypeStruct((1, ROUTER_LANES), F32)],
        scratch_shapes=[pltpu.VMEM((1, ROUTER_LANES), F32)],
        compiler_params=_cparams("arbitrary"),
        name="merge",
    )(x2, og2, ol2, *[wp[n] for n in _MERGE_WEIGHTS], cnt_in)


def _row_copy(src_ref, src_row, dst_ref, dst_row, sem):
    return pltpu.make_async_copy(src_ref.at[pl.ds(src_row * ROW_TILE, ROW_TILE)],
                                 dst_ref.at[pl.ds(dst_row * ROW_TILE, ROW_TILE)], sem)


def _group_spans(n_tokens, tile):
    spans, first = [], 0
    for n in n_tokens:
        spans.append((first, n // tile))
        first += n // tile
    return spans


def _group_spec(block, span):
    first, steps = span
    return pl.BlockSpec(block, lambda i, *_: (jnp.clip(i - first, 0, steps - 1), 0))


def _dispatch_kernel(pos_ref, zrow_ref, nu_ref, *refs, td, spans):
    h1t_refs, (xs_ref, zero_ref, sem, zsem) = refs[:len(spans)], refs[len(spans):]
    i = pl.program_id(0)

    @pl.when(i == 0)
    def _():
        zero_ref[...] = jnp.zeros(zero_ref.shape, F32)
        tile_rows = EXPERT_ROWS * ROW_TILE
        fill = lambda row: pltpu.make_async_copy(
            zero_ref, xs_ref.at[pl.ds(pl.multiple_of(row * ROW_TILE, ROW_TILE), tile_rows)], zsem)
        for e in range(N_EXPERTS):
            @pl.when(zrow_ref[e] >= 0)
            def _():
                fill(zrow_ref[e]).start()
        for e in range(N_EXPERTS):
            @pl.when(zrow_ref[e] >= 0)
            def _():
                fill(zrow_ref[e]).wait()

        def fill_tail(r, carry):
            fill(r * EXPERT_ROWS).start()
            fill(r * EXPERT_ROWS).wait()
            return carry

        lax.fori_loop(nu_ref[0], xs_ref.shape[0] // tile_rows, fill_tail, 0)

    base = i * (2 * td)
    for h1t_ref, (first, steps) in zip(h1t_refs, spans):
        @pl.when((i >= first) & (i < first + steps))
        def _():
            for t in range(td):
                for k in range(2):
                    dst = pl.multiple_of(pos_ref[base + 2 * t + k] * ROW_TILE, ROW_TILE)
                    pltpu.make_async_copy(h1t_ref.at[pl.ds(t * ROW_TILE, ROW_TILE)],
                                          xs_ref.at[pl.ds(dst, ROW_TILE)], sem).start(priority=k)
            for _ in range(2 * td):
                _row_copy(h1t_ref, 0, xs_ref, 0, sem).wait()


def _dispatch(h1ts, pos, zrow, n_used, *, n_rows, td):
    spans = _group_spans([h.shape[0] // ROW_TILE for h in h1ts], td)
    return pl.pallas_call(
        functools.partial(_dispatch_kernel, td=td, spans=spans),
        grid_spec=pltpu.PrefetchScalarGridSpec(
            num_scalar_prefetch=3,
            grid=(sum(steps for _, steps in spans),),
            in_specs=[_group_spec((td * ROW_TILE, LANES), span) for span in spans],
            out_specs=pl.BlockSpec(memory_space=pl.ANY),
            scratch_shapes=[pltpu.VMEM((EXPERT_ROWS * ROW_TILE, LANES), F32),
                            pltpu.SemaphoreType.DMA(()), pltpu.SemaphoreType.DMA(())]),
        out_shape=jax.ShapeDtypeStruct((n_rows * ROW_TILE, LANES), F32),
        compiler_params=_cparams("arbitrary"),
        name="dispatch",
    )(pos, zrow, n_used, *h1ts)


def _experts_kernel(te_ref, tb_ref, nu_ref, xs_ref, wg_ref, wu_ref, wd_ref, out_ref, wgb_ref, wub_ref, wdb_ref):
    del tb_ref
    r = pl.program_id(0)
    rows = xs_ref.shape[0] // ROW_TILE

    @pl.when((r == 0) | (te_ref[r] != te_ref[jnp.maximum(r - 1, 0)]))
    def _():
        wgb_ref[...] = wg_ref[0].astype(BF16)
        wub_ref[...] = wu_ref[0].astype(BF16)
        wdb_ref[...] = wd_ref[0].astype(BF16)

    @pl.when(r < nu_ref[0])
    def _():
        x = jnp.concatenate([xs_ref[pl.ds(s, rows, stride=ROW_TILE), :] for s in range(ROW_TILE)], axis=1).astype(BF16)
        gate = _dot(x, wgb_ref[...])
        up = _dot(x, wub_ref[...])
        hid = (gate * jax.nn.sigmoid(gate)) * up
        out = _dot(hid.astype(BF16), wdb_ref[...])
        for s in range(ROW_TILE):
            out_ref[pl.ds(s, rows, stride=ROW_TILE), :] = out[:, s * LANES:(s + 1) * LANES]

    @pl.when(pl.program_id(0) >= nu_ref[0])
    def _():
        out_ref[...] = jnp.zeros(out_ref.shape, F32)


def _experts(xs, tile_e, tile_blk, n_used, wp):
    n_rows = xs.shape[0] // ROW_TILE
    d = ROW_TILE * LANES
    blk = pl.BlockSpec((EXPERT_ROWS * ROW_TILE, LANES), lambda r, te, tb, nu: (tb[r], 0))
    return pl.pallas_call(
        _experts_kernel,
        grid_spec=pltpu.PrefetchScalarGridSpec(
            num_scalar_prefetch=3,
            grid=(n_rows // EXPERT_ROWS,),
            in_specs=[blk,
                      pl.BlockSpec((1, d, D_EXPERT), lambda r, te, tb, nu: (te[r], 0, 0)),
                      pl.BlockSpec((1, d, D_EXPERT), lambda r, te, tb, nu: (te[r], 0, 0)),
                      pl.BlockSpec((1, D_EXPERT, d), lambda r, te, tb, nu: (te[r], 0, 0))],
            out_specs=pl.BlockSpec((EXPERT_ROWS * ROW_TILE, LANES), lambda r, te, tb, nu: (r, 0)),
            scratch_shapes=[pltpu.VMEM((d, D_EXPERT), BF16), pltpu.VMEM((d, D_EXPERT), BF16),
                            pltpu.VMEM((D_EXPERT, d), BF16)]),
        out_shape=jax.ShapeDtypeStruct(xs.shape, F32),
        compiler_params=_cparams("arbitrary"),
        name="experts",
    )(tile_e, tile_blk, n_used, xs, wp["w_gate"], wp["w_up"], wp["w_down"])


def _combine_kernel(pos_ref, *refs, tc, spans):
    g = len(spans)
    h1t_refs, rt_refs = refs[:g], refs[g:2 * g]
    outs_ref, l2g_ref, l2b_ref = refs[2 * g:2 * g + 3]
    y_refs = refs[2 * g + 3:3 * g + 3]
    g_ref, sem = refs[3 * g + 3:]
    i = pl.program_id(0)
    n_steps = pl.num_programs(0)

    def gather(step):
        slot = step % 2
        base = step * (2 * tc)
        for t in range(tc):
            for k in range(2):
                src = pl.multiple_of(pos_ref[base + 2 * t + k] * ROW_TILE, ROW_TILE)
                pltpu.make_async_copy(outs_ref.at[pl.ds(src, ROW_TILE)],
                                      g_ref.at[slot, k, pl.ds(t * ROW_TILE, ROW_TILE)],
                                      sem.at[slot]).start(priority=k)

    @pl.when(i == 0)
    def _():
        gather(0)

    @pl.when(i + 1 < n_steps)
    def _():
        gather(i + 1)

    slot = i % 2
    for _ in range(2 * tc):
        pltpu.make_async_copy(outs_ref.at[pl.ds(0, ROW_TILE)], g_ref.at[slot, 0, pl.ds(0, ROW_TILE)], sem.at[slot]).wait()
    for h1t_ref, rt_ref, y_ref, (first, steps) in zip(h1t_refs, rt_refs, y_refs, spans):
        @pl.when((i >= first) & (i < first + steps))
        def _():
            rt = rt_ref[...]
            w0, w1 = rt[:, RT_W:RT_W + 1], rt[:, RT_W + 1:RT_W + 2]
            cols = []
            for s in range(ROW_TILE):
                rows = pl.ds(s, tc, stride=ROW_TILE)
                cols.append(DEEPNORM_ALPHA * h1t_ref[rows, :] + (w0 * g_ref[slot, 0, rows, :] + w1 * g_ref[slot, 1, rows, :]))
            y_ref[...] = _layer_norm(jnp.concatenate(cols, axis=1), l2g_ref[...], l2b_ref[...])


def _combine(h1ts, rts, outs, pos, wp, *, tc):
    d = ROW_TILE * LANES
    spans = _group_spans([r.shape[0] for r in rts], tc)
    const = lambda: pl.BlockSpec((1, d), lambda i, *_: (0, 0))
    return pl.pallas_call(
        functools.partial(_combine_kernel, tc=tc, spans=spans),
        grid_spec=pltpu.PrefetchScalarGridSpec(
            num_scalar_prefetch=1,
            grid=(sum(steps for _, steps in spans),),
            in_specs=([_group_spec((tc * ROW_TILE, LANES), span) for span in spans]
                      + [_group_spec((tc, ROUTER_LANES), span) for span in spans]
                      + [pl.BlockSpec(memory_space=pl.ANY), const(), const()]),
            out_specs=[_group_spec((tc, d), span) for span in spans],
            scratch_shapes=[pltpu.VMEM((2, 2, tc * ROW_TILE, LANES), F32), pltpu.SemaphoreType.DMA((2,))]),
        out_shape=[jax.ShapeDtypeStruct((r.shape[0], d), F32) for r in rts],
        compiler_params=_cparams("arbitrary"),
        name="combine",
    )(pos, *h1ts, *rts, outs, wp["ln2_g"], wp["ln2_b"])


def _route_plan(rt, cnt):
    t = rt.shape[0]
    n_tiles = (2 * t) // EXPERT_ROWS + N_EXPERTS
    counts = cnt[0, :N_EXPERTS].astype(jnp.int32)
    padded = (counts + (EXPERT_ROWS - 1)) // EXPERT_ROWS * EXPERT_ROWS
    ends = jnp.cumsum(padded)
    starts = ends - padded
    eid = rt[:, RT_E:RT_E + 2].astype(jnp.int32)
    rank = rt[:, RT_RANK:RT_RANK + 2].astype(jnp.int32)
    pos = (jnp.take(starts, eid) + rank).reshape(-1)
    n_used = ends[-1] // EXPERT_ROWS
    tile_blk = jnp.minimum(jnp.arange(n_tiles, dtype=jnp.int32), n_used - 1)
    tile_e = jnp.sum((tile_blk[:, None] * EXPERT_ROWS >= ends[None, :]).astype(jnp.int32), axis=1)
    zrow = jnp.where(padded > 0, ends - EXPERT_ROWS, -1)
    return pos, zrow.astype(jnp.int32), tile_e.astype(jnp.int32), tile_blk, n_used.reshape(1).astype(jnp.int32), n_tiles


def _rope_tables(pos):
    inv = ROPE_THETA ** (-jnp.arange(0, MLA_ROPE, 2, dtype=F32) / MLA_ROPE)
    ang = pos.astype(F32)[:, None] * inv[None, :]
    cs = jnp.concatenate([jnp.cos(ang), jnp.sin(ang)], axis=-1)
    return cs[None], cs.T[None]


def _prep_weights(ln_in_g, ln_in_b, w_in, w_gk2, b_gk, gla_norm_g, q_norm_g, kv_norm_g, w_uq, w_uk, w_uv,
                  w_br_gla, w_br_mla, w_mg, b_mg, w_out, ln1_g, ln1_b, w_rg, b_rg, w_re, b_re,
                  w_gate, w_up, w_down, ln2_g, ln2_b):
    d = w_in.shape[1]
    w = w_in[0]
    c0 = 0
    wk = w[:, c0:c0 + GLA_QK]; c0 += GLA_QK
    wv = w[:, c0:c0 + GLA_V]; c0 += GLA_V
    wgr = w[:, c0:c0 + GLA_RANK]; c0 += GLA_RANK
    wckv = w[:, c0:c0 + MLA_KV_RANK]; c0 += MLA_KV_RANK
    wkr = w[:, c0:c0 + MLA_ROPE]; c0 += MLA_ROPE
    wq = w[:, c0:c0 + GLA_QK]; c0 += GLA_QK
    wg = w[:, c0:c0 + GLA_V]; c0 += GLA_V
    wcq = w[:, c0:c0 + MLA_Q_RANK]
    w_tok = jnp.concatenate([wk, wv, wq, wckv, wkr, wgr, jnp.zeros((d, TOK_PAD - TOK_USED), F32)], axis=1)
    w_tr = jnp.concatenate([wckv, wcq], axis=1).T
    uq = w_uq[0].reshape(MLA_Q_RANK, MLA_HEADS, MLA_QK_DIM)
    uq_perm = jnp.concatenate([
        uq[:, :, :MLA_NOPE].reshape(MLA_Q_RANK, -1),
        uq[:, :, MLA_NOPE:MLA_NOPE + MLA_HALF].reshape(MLA_Q_RANK, -1),
        uq[:, :, MLA_NOPE + MLA_HALF:].reshape(MLA_Q_RANK, -1)], axis=1)
    uv = w_uv[0].transpose(1, 0, 2).reshape(MLA_HEADS // 2, 2, MLA_KV_RANK, MLA_DV)
    eye = jnp.eye(2, dtype=F32)
    w_uv_bd = (uv[:, :, :, None, :] * eye[None, :, None, :, None]).reshape(
        MLA_HEADS // 2, 2 * MLA_KV_RANK, 2 * MLA_DV)
    w_r = jnp.concatenate([w_rg[0], w_re[0].transpose(1, 0, 2).reshape(d, N_EXPERTS),
                           jnp.zeros((d, ROUTER_LANES - N_GROUPS - N_EXPERTS), F32)], axis=1)
    w_r_hi = w_r.astype(BF16)
    b_r = jnp.concatenate([b_rg[0], b_re[0].reshape(-1), jnp.zeros((ROUTER_LANES - N_GROUPS - N_EXPERTS,), F32)])
    row = lambda a: a.reshape(1, -1)
    return {
        "ln_g": row(ln_in_g), "ln_b": row(ln_in_b),
        "w_tok": w_tok.astype(BF16), "w_tr": w_tr.astype(BF16),
        "w_gk2": w_gk2[0].astype(BF16), "b_gk": row(b_gk[0]),
        "kv_g": row(kv_norm_g[0]), "kv_gt": kv_norm_g[0].reshape(-1, 1), "q_gt": q_norm_g[0].reshape(-1, 1),
        "w_uqt": uq_perm.T.astype(BF16), "w_uk": w_uk[0].transpose(1, 0, 2).astype(BF16),
        "w_g": wg.astype(BF16), "gla_g": row(gla_norm_g[0]),
        "w_br_gla": w_br_gla[0].astype(BF16), "w_uv_bd": w_uv_bd.astype(BF16), "w_br_mla": w_br_mla[0].astype(BF16),
        "w_mg": w_mg[0].astype(BF16), "b_mg": row(b_mg[0]), "w_out": w_out[0].astype(BF16),
        "ln1_g": row(ln1_g[0]), "ln1_b": row(ln1_b[0]),
        "w_r_hi": w_r_hi, "w_r_lo": (w_r - w_r_hi.astype(F32)).astype(BF16), "b_r": row(b_r),
        "w_gate": w_gate[0], "w_up": w_up[0], "w_down": w_down[0],
        "ln2_g": row(ln2_g[0]), "ln2_b": row(ln2_b[0]),
    }


def _value_rows(vt):
    lead, length = vt.shape[:-2], vt.shape[-1]
    return jnp.concatenate([vt, jnp.ones(lead + (1, length), vt.dtype),
                            jnp.zeros(lead + (V_ROWS - MLA_KV_RANK - 1, length), vt.dtype)], axis=-2)


def _key_tiles(kcat, vt):
    bsz, length, _ = kcat.shape
    n = -(-length // KEY_TILE)
    pad = n * KEY_TILE - length
    kcat = jnp.pad(kcat, ((0, 0), (0, pad), (0, 0)))
    vt = jnp.pad(_value_rows(vt), ((0, 0), (0, 0), (0, pad)))
    return (kcat.reshape(bsz, n, KEY_TILE, MLA_CAT),
            vt.reshape(bsz, V_ROWS, n, KEY_TILE).transpose(0, 2, 1, 3))


def _ffn(groups, wp):
    h1ts, rts = [], []
    cnt = jnp.zeros((1, ROUTER_LANES), F32)
    for x, og, ol in groups:
        t, d = x.shape[0] * x.shape[1], x.shape[2]
        h1t, rt, cnt = _merge(x.reshape(t, d), og.reshape(t, -1), ol.reshape(t, -1), wp, cnt, tt=min(MERGE_TILE, t))
        h1ts.append(h1t)
        rts.append(rt)
    n_fields = RT_RANK + 2
    pos, zrow, tile_e, tile_blk, n_used, n_tiles = _route_plan(
        jnp.concatenate([rt[:, :n_fields] for rt in rts], axis=0), cnt)
    xs = _dispatch(h1ts, pos, zrow, n_used, n_rows=n_tiles * EXPERT_ROWS, td=MOE_TOKEN_TILE)
    outs = _experts(xs, tile_e, tile_blk, n_used, wp)
    ys = _combine(h1ts, rts, outs, pos, wp, tc=MOE_TOKEN_TILE)
    return [y.reshape(x.shape) for y, (x, _, _) in zip(ys, groups)]


def kernel(x_prompt, x_sample, cache_mla_latent, cache_mla_krope, state_gla, meta_tokens, ln_in_g, ln_in_b, w_in, w_gk2, b_gk, gla_norm_g, q_norm_g, kv_norm_g, w_uq, w_uk, w_uv, w_br_gla, w_br_mla, w_mg, b_mg, w_out, ln1_g, ln1_b, w_rg, b_rg, w_re, b_re, w_gate, w_up, w_down, ln2_g, ln2_b):
    bp, sp, d = x_prompt.shape
    bs, ss, _ = x_sample.shape
    past = cache_mla_latent.shape[2]
    wp = _prep_weights(ln_in_g, ln_in_b, w_in, w_gk2, b_gk, gla_norm_g, q_norm_g, kv_norm_g, w_uq, w_uk, w_uv,
                       w_br_gla, w_br_mla, w_mg, b_mg, w_out, ln1_g, ln1_b, w_rg, b_rg, w_re, b_re,
                       w_gate, w_up, w_down, ln2_g, ln2_b)

    cs_m, cst_m = _rope_tables(jnp.arange(-N_META, 0, dtype=jnp.int32))
    m = _proj(meta_tokens[None], cs_m, cst_m, wp, cl=N_META, tt=N_META)
    _, _, m_kt, m_v, m_ebl, m_ckv, m_kr, m_kcat, m_vt, _ = m
    zero_state = jnp.zeros((1, GLA_HEADS, GLA_DK, GLA_DV), F32)
    _, m_state = _gla(m[0], m[1], m_kt, m_v, m_ebl, zero_state, cl=N_META, ts=N_META)

    cs_p, cst_p = _rope_tables(jnp.arange(sp, dtype=jnp.int32))
    p_qd, p_ki, p_kt, p_v, p_ebl, p_ckv, p_kr, p_kcat, p_vt, p_qt = _proj(
        x_prompt, cs_p, cst_p, wp, cl=CHUNK, tt=PROJ_TILE)
    p_o, p_state = _gla(p_qd, p_ki, p_kt, p_v, p_ebl, m_state, cl=CHUNK, ts=TOKEN_TILE)
    rep = lambda a, n: jnp.broadcast_to(a, (n,) + a.shape[1:])
    lat_p = jnp.concatenate([rep(m_ckv, bp), p_ckv], axis=1)
    kr_p = jnp.concatenate([rep(m_kr, bp), p_kr], axis=1)
    kcat_p, vt_p = _key_tiles(p_kcat, p_vt)
    km, vm = m_kcat[0], _value_rows(m_vt[0])
    p_ol = _attn(p_qt, kcat_p, vt_p, km, vm, bq=TOKEN_TILE, causal=True, n_valid_last=KEY_TILE, n_sets=2)

    ts_all = bs * ss
    cs_s, cst_s = _rope_tables(past + (jnp.arange(ts_all, dtype=jnp.int32) % ss))
    s_qd, s_ki, s_kt, s_v, s_ebl, s_ckv, s_kr, s_kcat, s_vt, s_qt = _proj(
        x_sample.reshape(1, ts_all, d), cs_s, cst_s, wp, cl=ss, tt=min(TOKEN_TILE, ts_all))
    per_stream = lambda a: a.reshape(bs, ss, a.shape[-1])
    s_o, s_state = _gla(per_stream(s_qd), per_stream(s_ki), per_stream(s_kt), per_stream(s_v),
                        s_ebl.reshape(bs, 1, 1, GLA_QK), state_gla[0].astype(F32), cl=ss, ts=ss)
    s_ckv, s_kr = per_stream(s_ckv), per_stream(s_kr)
    cache_kcat = jnp.concatenate([cache_mla_latent[0], cache_mla_krope[0]], axis=-1).astype(BF16)
    cache_vt = cache_mla_latent[0].astype(BF16).transpose(0, 2, 1)
    new_vt = s_vt.reshape(MLA_KV_RANK, bs, ss).transpose(1, 0, 2)
    kcat_s, vt_s = _key_tiles(jnp.concatenate([cache_kcat, per_stream(s_kcat)], axis=1),
                              jnp.concatenate([cache_vt, new_vt], axis=2))
    qt = s_qt.transpose(0, 2, 3, 1, 4).reshape(MLA_HEADS, MLA_CAT, bs, ss)
    qt = qt.transpose(2, 1, 0, 3).reshape(bs, MLA_CAT, MLA_HEADS * ss // KEY_TILE, KEY_TILE)
    qt = qt.transpose(0, 2, 1, 3)[:, None]
    s_ol = _attn(qt, kcat_s, vt_s, km, vm, bq=ss, causal=False, n_valid_last=(past + ss - 1) % KEY_TILE + 1, n_sets=1)
    y_prompt, y_sample = _ffn([(x_prompt, p_o, p_ol), (x_sample, s_o, s_ol)], wp)

    return (y_prompt, y_sample, lat_p[None], kr_p[None], p_state[None].astype(state_gla.dtype),
            s_ckv[None], s_kr[None], s_state[None].astype(state_gla.dtype))
```

```python
import functools

import jax
import jax.numpy as jnp
from jax import lax
from jax.experimental import pallas as pl
from jax.experimental.pallas import tpu as pltpu

F32 = jnp.float32
BF16 = jnp.bfloat16

CHUNK = 64
N_META = 16
GLA_HEADS = 4
GLA_DK = 128
GLA_DV = 256
GLA_RANK = 16
GLA_TAU = 16.0
GLA_QK = GLA_HEADS * GLA_DK
GLA_V = GLA_HEADS * GLA_DV
GLA_SCALE = GLA_DK ** -0.5
MLA_HEADS = 16
MLA_Q_RANK = 384
MLA_KV_RANK = 128
MLA_NOPE = 64
MLA_ROPE = 32
MLA_HALF = MLA_ROPE // 2
MLA_DV = 64
MLA_QK_DIM = MLA_NOPE + MLA_ROPE
MLA_CAT = MLA_KV_RANK + MLA_ROPE
MLA_SCALE = MLA_QK_DIM ** -0.5
LOG2_E = 1.4426950408889634
Q_SCALE = MLA_SCALE * LOG2_E
ROPE_THETA = 10000.0
N_GROUPS = 4
EXPERTS_PER_GROUP = 8
N_EXPERTS = N_GROUPS * EXPERTS_PER_GROUP
D_EXPERT = 256
LN_EPS = 1e-5
RMS_EPS = 1e-6
DEEPNORM_ALPHA = 2.0 ** 0.25

LANES = 128
MXU_DIM = 256
VMEM_LIMIT_BYTES = 56 * 1024 * 1024

TOKEN_TILE = 256
MERGE_TILE = 2 * TOKEN_TILE
PROJ_TILE = 2 * TOKEN_TILE
KEY_TILE = MXU_DIM
MOE_TOKEN_TILE = 256
EXPERT_ROWS = 256
ROW_TILE = 8
ROUTER_LANES = LANES
RT_E, RT_W, RT_RANK = 0, 2, 4
NEG_BIG = -1e30

NT_DIMS = (((1,), (1,)), ((), ()))
TN_DIMS = (((0,), (0,)), ((), ()))


def _cparams(*sem):
    return pltpu.CompilerParams(dimension_semantics=sem, vmem_limit_bytes=VMEM_LIMIT_BYTES)


def _const_spec(shape):
    nd = len(shape)
    return pl.BlockSpec(shape, lambda *_: (0,) * nd, pipeline_mode=pl.Buffered(1))


def _layer_norm(x, g, b):
    mu = jnp.mean(x, axis=-1, keepdims=True)
    xc = x - mu
    var = jnp.mean(xc * xc, axis=-1, keepdims=True)
    return xc * lax.rsqrt(var + LN_EPS) * g + b


def _dot(a, b):
    return jnp.dot(a, b, preferred_element_type=F32)


TOK_COLS = (GLA_QK, GLA_V, GLA_QK, MLA_KV_RANK, MLA_ROPE, GLA_RANK)
TOK_USED = sum(TOK_COLS)
TOK_PAD = -(-TOK_USED // MXU_DIM) * MXU_DIM
TR_ROWS = MLA_KV_RANK + MLA_Q_RANK


def _proj_kernel(x_ref, cs_ref, cst_ref, lng_ref, lnb_ref, wtok_ref, wtr_ref, wgk2_ref, bgk_ref,
                 kvg_ref, kvgt_ref, qngt_ref, wuqt_ref, wuk_ref,
                 qd_ref, ki_ref, kt_ref, v_ref, ebl_ref, ckv_ref, kr_ref, kcat_ref, vt_ref, qt_ref, *, cl, sub):
    tt = x_ref.shape[1]
    n_chunks = sub // cl

    def project(r0):
        h = _layer_norm(x_ref[0, r0:r0 + sub, :], lng_ref[...], lnb_ref[...])
        hb = h.astype(BF16)
        z = _dot(hb, wtok_ref[...])
        zt = lax.dot_general(wtr_ref[...], hb, NT_DIMS, preferred_element_type=F32)
        return z, zt

    def finish(r0, z, zt):
        rows = slice(r0, r0 + sub)
        o0 = 0
        k = z[:, o0:o0 + GLA_QK]; o0 += GLA_QK
        v = z[:, o0:o0 + GLA_V]; o0 += GLA_V
        q = z[:, o0:o0 + GLA_QK]; o0 += GLA_QK
        ckv_raw = z[:, o0:o0 + MLA_KV_RANK]; o0 += MLA_KV_RANK
        kr = z[:, o0:o0 + MLA_ROPE]; o0 += MLA_ROPE
        gklr = z[:, o0:o0 + GLA_RANK]

        gz = _dot(gklr.astype(BF16), wgk2_ref[...]) + bgk_ref[...]
        gk = (jnp.minimum(gz, 0.0) - jnp.log(1.0 + jnp.exp(-jnp.abs(gz)))) * (1.0 / GLA_TAU)
        row_in_chunk = lax.broadcasted_iota(jnp.int32, gk.shape, 0) & (cl - 1)
        b = gk
        shift = 1
        while shift < cl:
            b = b + jnp.where(row_in_chunk >= shift, pltpu.roll(b, shift, 0), 0.0)
            shift *= 2
        b3 = b.reshape(n_chunks, cl, GLA_QK)
        bl = b3[:, cl - 1:cl, :]
        qd_ref[0, rows, :] = (q * GLA_SCALE * jnp.exp(b)).astype(BF16)
        ki_ref[0, rows, :] = (k * jnp.exp(-b)).astype(BF16)
        kt_ref[0, rows, :] = (k.reshape(n_chunks, cl, GLA_QK) * jnp.exp(bl - b3)).reshape(sub, GLA_QK).astype(BF16)
        v_ref[0, rows, :] = v.astype(BF16)
        ebl_ref[0, r0 // cl:r0 // cl + n_chunks] = jnp.exp(bl)

        ckv = ckv_raw * lax.rsqrt(jnp.mean(ckv_raw * ckv_raw, axis=-1, keepdims=True) + RMS_EPS) * kvg_ref[...]
        cs = cs_ref[0, rows, :]
        cos, sin = cs[:, :MLA_HALF], cs[:, MLA_HALF:]
        x1, x2 = kr[:, :MLA_HALF], kr[:, MLA_HALF:]
        kr_rot = jnp.concatenate([x1 * cos - x2 * sin, x2 * cos + x1 * sin], axis=-1)
        ckv_ref[0, rows, :] = ckv
        kr_ref[0, rows, :] = kr_rot
        kcat_ref[0, rows, :MLA_KV_RANK] = ckv.astype(BF16)
        kcat_ref[0, rows, MLA_KV_RANK:] = kr_rot.astype(BF16)

        ckvt = zt[:MLA_KV_RANK]
        ckvt = ckvt * lax.rsqrt(jnp.mean(ckvt * ckvt, axis=0, keepdims=True) + RMS_EPS) * kvgt_ref[...]
        vt_ref[0, :, rows] = ckvt.astype(BF16)
        cqt = zt[MLA_KV_RANK:]
        cqt = cqt * lax.rsqrt(jnp.mean(cqt * cqt, axis=0, keepdims=True) + RMS_EPS) * qngt_ref[...]
        qmt = _dot(wuqt_ref[...], cqt.astype(BF16))
        n_nope = MLA_HEADS * MLA_NOPE
        n_half = MLA_HEADS * MLA_HALF
        cst = cst_ref[0, :, rows]
        cos_t = jnp.concatenate([cst[:MLA_HALF]] * MLA_HEADS, axis=0)
        sin_t = jnp.concatenate([cst[MLA_HALF:]] * MLA_HEADS, axis=0)
        r1 = qmt[n_nope:n_nope + n_half]
        r2 = qmt[n_nope + n_half:]
        rot1 = ((r1 * cos_t - r2 * sin_t) * Q_SCALE).astype(BF16)
        rot2 = ((r2 * cos_t + r1 * sin_t) * Q_SCALE).astype(BF16)
        blk = r0 // sub
        for hd in range(MLA_HEADS):
            nope = qmt[hd * MLA_NOPE:(hd + 1) * MLA_NOPE].astype(BF16)
            qlat = _dot(wuk_ref[hd], nope) * Q_SCALE
            qt_ref[0, blk, hd, :MLA_KV_RANK, :] = qlat.astype(BF16)
            qt_ref[0, blk, hd, MLA_KV_RANK:MLA_KV_RANK + MLA_HALF, :] = rot1[hd * MLA_HALF:(hd + 1) * MLA_HALF]
            qt_ref[0, blk, hd, MLA_KV_RANK + MLA_HALF:, :] = rot2[hd * MLA_HALF:(hd + 1) * MLA_HALF]

    starts = list(range(0, tt, sub))
    projected = [project(r0) for r0 in starts]
    for r0, (z, zt) in zip(starts, projected):
        finish(r0, z, zt)


def _proj(x, cs, cst, wp, *, cl, tt):
    bsz, s, d = x.shape
    nt = s // tt
    n_chunks = tt // cl
    sub = min(TOKEN_TILE, tt)
    tok = lambda w: pl.BlockSpec((1, tt, w), lambda b, t: (b, t, 0))
    in_specs = [
        tok(d),
        pl.BlockSpec((1, tt, MLA_ROPE), lambda b, t: (0, t, 0)),
        pl.BlockSpec((1, MLA_ROPE, tt), lambda b, t: (0, 0, t)),
    ] + [_const_spec(wp[n].shape) for n in _PROJ_WEIGHTS]
    out_shape = [
        jax.ShapeDtypeStruct((bsz, s, GLA_QK), BF16),
        jax.ShapeDtypeStruct((bsz, s, GLA_QK), BF16),
        jax.ShapeDtypeStruct((bsz, s, GLA_QK), BF16),
        jax.ShapeDtypeStruct((bsz, s, GLA_V), BF16),
        jax.ShapeDtypeStruct((bsz, s // cl, 1, GLA_QK), F32),
        jax.ShapeDtypeStruct((bsz, s, MLA_KV_RANK), F32),
        jax.ShapeDtypeStruct((bsz, s, MLA_ROPE), F32),
        jax.ShapeDtypeStruct((bsz, s, MLA_CAT), BF16),
        jax.ShapeDtypeStruct((bsz, MLA_KV_RANK, s), BF16),
        jax.ShapeDtypeStruct((bsz, s // sub, MLA_HEADS, MLA_CAT, sub), BF16),
    ]
    out_specs = [
        tok(GLA_QK), tok(GLA_QK), tok(GLA_QK), tok(GLA_V),
        pl.BlockSpec((1, n_chunks, 1, GLA_QK), lambda b, t: (b, t, 0, 0)),
        tok(MLA_KV_RANK), tok(MLA_ROPE), tok(MLA_CAT),
        pl.BlockSpec((1, MLA_KV_RANK, tt), lambda b, t: (b, 0, t)),
        pl.BlockSpec((1, tt // sub, MLA_HEADS, MLA_CAT, sub), lambda b, t: (b, t, 0, 0, 0)),
    ]
    return pl.pallas_call(
        functools.partial(_proj_kernel, cl=cl, sub=sub),
        grid=(bsz, nt),
        in_specs=in_specs,
        out_specs=out_specs,
        out_shape=out_shape,
        compiler_params=_cparams("parallel", "parallel"),
        name="proj",
    )(x, cs, cst, *[wp[n] for n in _PROJ_WEIGHTS])


_PROJ_WEIGHTS = ("ln_g", "ln_b", "w_tok", "w_tr", "w_gk2", "b_gk", "kv_g", "kv_gt", "q_gt", "w_uqt", "w_uk")


def _gla_kernel(qd_ref, ki_ref, kt_ref, v_ref, ebl_ref, s0_ref, o_ref, sfin_ref, st_ref, *, cl):
    t = pl.program_id(1)
    ts = qd_ref.shape[1]

    @pl.when(t == 0)
    def _():
        for hd in range(GLA_HEADS):
            st_ref[hd] = s0_ref[0, hd].T

    row = lax.broadcasted_iota(jnp.int32, (ts, ts), 0)
    col = lax.broadcasted_iota(jnp.int32, (ts, ts), 1)
    keep = (row >= col) & ((row & -cl) == (col & -cl))
    qk = lambda ref, hd: ref[0, :, hd * GLA_DK:(hd + 1) * GLA_DK]
    val = lambda hd: v_ref[0, :, hd * GLA_DV:(hd + 1) * GLA_DV]
    for hd in range(GLA_HEADS):
        a = lax.dot_general(qk(qd_ref, hd), qk(ki_ref, hd), NT_DIMS, preferred_element_type=F32)
        a = jnp.where(keep, a, 0.0).astype(BF16)
        o_ref[0, :, hd * GLA_DV:(hd + 1) * GLA_DV] = _dot(a, val(hd))
    for c in range(ts // cl):
        rows = slice(c * cl, (c + 1) * cl)
        for hd in range(GLA_HEADS):
            st = st_ref[hd]
            o_ref[0, rows, hd * GLA_DV:(hd + 1) * GLA_DV] += lax.dot_general(
                qk(qd_ref, hd)[rows], st.astype(BF16), NT_DIMS, preferred_element_type=F32)
            st_ref[hd] = (st * ebl_ref[0, c, :, hd * GLA_DK:(hd + 1) * GLA_DK]
                          + lax.dot_general(val(hd)[rows], qk(kt_ref, hd)[rows], TN_DIMS, preferred_element_type=F32))

    @pl.when(t == pl.num_programs(1) - 1)
    def _():
        for hd in range(GLA_HEADS):
            sfin_ref[0, hd] = st_ref[hd].T


def _gla(qd, ki, kt, v, ebl, s0, *, cl, ts):
    bsz, s, _ = qd.shape
    n_chunks = ts // cl
    s0_b = s0.shape[0]
    qk_spec = pl.BlockSpec((1, ts, GLA_QK), lambda b, t: (b, t, 0))
    v_spec = pl.BlockSpec((1, ts, GLA_V), lambda b, t: (b, t, 0))
    st_spec = pl.BlockSpec((1, GLA_HEADS, GLA_DK, GLA_DV), lambda b, t: (b, 0, 0, 0))
    s0_spec = st_spec if s0_b == bsz else pl.BlockSpec((1, GLA_HEADS, GLA_DK, GLA_DV), lambda b, t: (0, 0, 0, 0))
    return pl.pallas_call(
        functools.partial(_gla_kernel, cl=cl),
        grid=(bsz, s // ts),
        in_specs=[qk_spec, qk_spec, qk_spec, v_spec,
                  pl.BlockSpec((1, n_chunks, 1, GLA_QK), lambda b, t: (b, t, 0, 0)),
                  s0_spec],
        out_specs=[v_spec, st_spec],
        out_shape=[jax.ShapeDtypeStruct((bsz, s, GLA_V), F32),
                   jax.ShapeDtypeStruct((bsz, GLA_HEADS, GLA_DK, GLA_DV), F32)],
        scratch_shapes=[pltpu.VMEM((GLA_HEADS, GLA_DV, GLA_DK), F32)],
        compiler_params=_cparams("parallel", "arbitrary"),
        name="gla",
    )(qd, ki, kt, v, ebl, s0)


SCRATCH_PER_SET = 7


def _attn_kernel(qt_ref, kcat_ref, vt_ref, km_ref, vm_ref, o_ref, *scratch, bq, causal, n_valid_last):
    n_sets = qt_ref.shape[1]
    n_cc = qt_ref.shape[3]
    n_kt = kcat_ref.shape[1]
    groups_per_cc = KEY_TILE // LANES

    def when(cond, fn):
        if isinstance(cond, bool):
            if cond:
                fn()
        else:
            pl.when(cond)(fn)

    def block(h):
        s_ref, p_ref, m_ref, a_ref, l_ref, acc_ref, mx_ref = scratch[SCRATCH_PER_SET * h:SCRATCH_PER_SET * (h + 1)]
        blk = pl.program_id(1) + h * pl.num_programs(1)
        n_int = blk if causal else n_kt - 1
        q = lambda c: qt_ref[0, h, 0, c]
        col_max = lambda x: jnp.max(x, axis=0, keepdims=True)
        col_sum = lambda x: jnp.sum(x, axis=0, keepdims=True)

        def scores(c, kt):
            s = _dot(kt, q(c))
            s_ref[c] = s
            mx_ref[c] = col_max(s)

        def softmax(c, mask):
            s = s_ref[c]
            if mask is None:
                s_max = mx_ref[c]
            else:
                s = jnp.where(mask, s, NEG_BIG)
                s_max = col_max(s)
            m_prev = m_ref[c]
            m_new = jnp.maximum(m_prev, s_max)
            alpha = jnp.exp2(m_prev - m_new)
            p = jnp.exp2(s - m_new)
            a_ref[c] = alpha
            m_ref[c] = m_new
            l_ref[c] = alpha * l_ref[c] + col_sum(p)
            p_ref[c] = p.astype(BF16)

        def softmax_diagonal(c):
            half = KEY_TILE // 2
            chunks_per_half = half // CHUNK
            second = lax.broadcasted_iota(jnp.int32, (CHUNK, half), 1) >= CHUNK
            m_prev, l_prev = m_ref[c], l_ref[c]
            m_out, a_out, l_out = [], [], []
            for lh in range(KEY_TILE // half):
                lanes = slice(lh * half, (lh + 1) * half)
                n_full = chunks_per_half * lh + 1
                rows_full = slice(0, n_full * CHUNK)
                rows_part = slice(n_full * CHUNK, (n_full + 1) * CHUNK)
                full = s_ref[c, rows_full, lanes]
                part = jnp.where(second, s_ref[c, rows_part, lanes], NEG_BIG)
                m_new = jnp.maximum(m_prev[:, lanes], jnp.maximum(col_max(full), col_max(part)))
                alpha = jnp.exp2(m_prev[:, lanes] - m_new)
                p_full = jnp.exp2(full - m_new)
                p_part = jnp.exp2(part - m_new)
                p_ref[c, rows_full, lanes] = p_full.astype(BF16)
                p_ref[c, rows_part, lanes] = p_part.astype(BF16)
                if (n_full + 1) * CHUNK < KEY_TILE:
                    hidden = KEY_TILE - (n_full + 1) * CHUNK
                    p_ref[c, (n_full + 1) * CHUNK:, lanes] = jnp.zeros((hidden, half), BF16)
                m_out.append(m_new)
                a_out.append(alpha)
                l_out.append(alpha * l_prev[:, lanes] + (col_sum(p_full) + col_sum(p_part)))
            m_ref[c] = jnp.concatenate(m_out, axis=1)
            a_ref[c] = jnp.concatenate(a_out, axis=1)
            l_ref[c] = jnp.concatenate(l_out, axis=1)

        def values(c, vt):
            acc_ref[c] = a_ref[c] * acc_ref[c] + _dot(vt, p_ref[c])

        def start():
            km, vm = km_ref[...], vm_ref[...]
            sm = [_dot(km, q(c)) for c in range(n_cc)]
            pm = []
            for c in range(n_cc):
                m0 = col_max(sm[c])
                m_ref[c] = m0
                p0 = jnp.exp2(sm[c] - m0)
                l_ref[c] = col_sum(p0)
                pm.append(p0.astype(BF16))
            for c in range(n_cc):
                acc_ref[c] = _dot(vm, pm[c])
            kt = kcat_ref[0, 0]
            for c in range(n_cc):
                scores(c, kt)

        def first_step():
            kt_next = kcat_ref[0, 1]
            for c in range(n_cc):
                softmax(c, None)
                scores(c, kt_next)

        def no_step():
            a_ref[...] = jnp.ones(a_ref.shape, F32)
            p_ref[...] = jnp.zeros(p_ref.shape, BF16)

        def key_step(j):
            kt_next = kcat_ref[0, j + 1]
            vt_prev = vt_ref[0, j - 1]
            for c in range(n_cc):
                values(c, vt_prev)
                softmax(c, None)
                scores(c, kt_next)

        def key_step_pair(jj, carry):
            key_step(2 * jj + 1)
            key_step(2 * jj + 2)
            return carry

        def unmasked_tiles():
            when(n_int >= 1, first_step)
            when(n_int == 0, no_step)
            n_rest = jnp.maximum(n_int - 1, 0) if causal else max(n_int - 1, 0)
            lax.fori_loop(0, n_rest // 2, key_step_pair, 0)
            when(n_rest % 2 == 1, lambda: key_step(n_int - 1))

        def last_values_of_unmasked():
            vt_prev = vt_ref[0, jnp.maximum(n_int - 1, 0)]
            for c in range(n_cc):
                values(c, vt_prev)

        def masked_tile_and_emit():
            if causal:
                masked_softmax = softmax_diagonal
            else:
                mask = lax.broadcasted_iota(jnp.int32, (KEY_TILE, KEY_TILE), 0) < n_valid_last
                masked_softmax = lambda c: softmax(c, mask)
            vt_last = vt_ref[0, n_int]
            value_lag = 2
            for c in range(n_cc + value_lag):
                if c < n_cc:
                    masked_softmax(c)
                if c >= value_lag:
                    values(c - value_lag, vt_last)
            for c in range(n_cc):
                o_t = acc_ref[c] * (1.0 / l_ref[c])
                for g in range(groups_per_cc):
                    tile = o_t[:, g * LANES:(g + 1) * LANES].T.astype(BF16)
                    col0 = c * KEY_TILE + g * LANES
                    if bq >= LANES:
                        hd, q0 = col0 // bq, col0 % bq
                        o_ref[0, h, q0:q0 + LANES, hd * MLA_KV_RANK:(hd + 1) * MLA_KV_RANK] = tile
                    else:
                        for hl in range(LANES // bq):
                            hd = col0 // bq + hl
                            o_ref[0, h, :, hd * MLA_KV_RANK:(hd + 1) * MLA_KV_RANK] = tile[hl * bq:(hl + 1) * bq]

        return start, unmasked_tiles, last_values_of_unmasked, masked_tile_and_emit

    blocks = [block(h) for h in range(n_sets)]
    blocks[0][0]()
    for h in range(n_sets):
        start, unmasked_tiles, last_values_of_unmasked, masked_tile_and_emit = blocks[h]
        unmasked_tiles()
        last_values_of_unmasked()
        if h + 1 < n_sets:
            blocks[h + 1][0]()
        masked_tile_and_emit()


def _attn(qt, kcat, vt, km, vm, *, bq, causal, n_valid_last, n_sets):
    bsz, nq, n_cc = qt.shape[:3]
    n_kt = kcat.shape[1]
    assert not causal or bq == KEY_TILE
    per_set = nq // n_sets
    width = MLA_HEADS * MLA_KV_RANK
    scratch_set = [pltpu.VMEM((n_cc, KEY_TILE, KEY_TILE), F32),
                   pltpu.VMEM((n_cc, KEY_TILE, KEY_TILE), BF16),
                   pltpu.VMEM((n_cc, 1, KEY_TILE), F32),
                   pltpu.VMEM((n_cc, 1, KEY_TILE), F32),
                   pltpu.VMEM((n_cc, 1, KEY_TILE), F32),
                   pltpu.VMEM((n_cc, MLA_KV_RANK, KEY_TILE), F32),
                   pltpu.VMEM((n_cc, 1, KEY_TILE), F32)]
    assert len(scratch_set) == SCRATCH_PER_SET
    out = pl.pallas_call(
        functools.partial(_attn_kernel, bq=bq, causal=causal, n_valid_last=n_valid_last),
        grid=(bsz, per_set),
        in_specs=[pl.BlockSpec((1, n_sets, 1, n_cc, MLA_CAT, KEY_TILE), lambda b, i: (b, 0, i, 0, 0, 0)),
                  pl.BlockSpec((1, n_kt, KEY_TILE, MLA_CAT), lambda b, i: (b, 0, 0, 0)),
                  pl.BlockSpec((1, n_kt, MLA_KV_RANK, KEY_TILE), lambda b, i: (b, 0, 0, 0)),
                  _const_spec(km.shape), _const_spec(vm.shape)],
        out_specs=pl.BlockSpec((1, n_sets, bq, width), lambda b, i: (b, 0, i, 0)),
        out_shape=jax.ShapeDtypeStruct((bsz, n_sets, per_set * bq, width), BF16),
        scratch_shapes=scratch_set * n_sets,
        compiler_params=_cparams("parallel", "arbitrary"),
        name="attn",
    )(qt.reshape(bsz, n_sets, per_set, *qt.shape[2:]), kcat, vt, km, vm)
    return out.reshape(bsz, nq * bq, width)


_MERGE_WEIGHTS = ("ln_g", "ln_b", "w_g", "gla_g", "w_br_gla", "w_uv_bd", "w_br_mla", "w_mg", "b_mg", "w_out",
                  "ln1_g", "ln1_b", "w_r_hi", "w_r_lo", "b_r")


def _merge_kernel(x_ref, og_ref, ol_ref, lng_ref, lnb_ref, wg_ref, glag_ref, wbg_ref, wuv_ref, wbm_ref,
                  wmg_ref, bmg_ref, wout_ref, l1g_ref, l1b_ref, wrh_ref, wrl_ref, br_ref, cin_ref,
                  h1t_ref, rt_ref, cnt_ref, carry_ref, *, sub):
    tt, d = x_ref.shape

    def branches(rows):
        h = _layer_norm(x_ref[rows, :], lng_ref[...], lnb_ref[...])
        hb = h.astype(BF16)
        g_out = _dot(hb, wg_ref[...])
        gate_pre = _dot(hb, wmg_ref[...])
        pair_in = 2 * MLA_KV_RANK
        y_heads = jnp.concatenate([_dot(ol_ref[rows, p * pair_in:(p + 1) * pair_in], wuv_ref[p])
                                   for p in range(MLA_HEADS // 2)], axis=-1)
        return h, g_out, gate_pre, y_heads

    def mix(rows, h, g_out, gate_pre, y_heads):
        og = og_ref[rows, :]
        parts = []
        for hd in range(GLA_HEADS):
            cols = slice(hd * GLA_DV, (hd + 1) * GLA_DV)
            o_h = og[:, cols]
            g_h = g_out[:, cols]
            o_n = o_h * lax.rsqrt(jnp.mean(o_h * o_h, axis=-1, keepdims=True) + RMS_EPS) * glag_ref[...]
            parts.append(o_n * (g_h * jax.nn.sigmoid(g_h)))
        y_a = _dot(jnp.concatenate(parts, axis=-1).astype(BF16), wbg_ref[...])
        y_b = _dot(y_heads.astype(BF16), wbm_ref[...])
        gates = jax.nn.sigmoid(gate_pre + bmg_ref[...])
        mix_in = gates[:, :d] * y_a + gates[:, d:] * y_b
        return DEEPNORM_ALPHA * h + _dot(mix_in.astype(BF16), wout_ref[...])

    def route(r0, pre):
        h1 = _layer_norm(pre, l1g_ref[...], l1b_ref[...])
        for s in range(ROW_TILE):
            h1t_ref[pl.ds(r0 * ROW_TILE + s, sub, stride=ROW_TILE), :] = h1[:, s * LANES:(s + 1) * LANES]
        h1_hi = h1.astype(BF16)
        h1_lo = (h1 - h1_hi.astype(F32)).astype(BF16)
        logits = (_dot(h1_hi, wrh_ref[...]) + (_dot(h1_hi, wrl_ref[...]) + _dot(h1_lo, wrh_ref[...]))) + br_ref[...]
        lane = lax.broadcasted_iota(jnp.int32, logits.shape, 1)
        is_grp = lane < N_GROUPS
        gl = jnp.where(is_grp, logits, NEG_BIG)
        g_max = jnp.max(gl, axis=-1, keepdims=True)
        g_sel = jnp.min(jnp.where(gl == g_max, lane, ROUTER_LANES), axis=-1, keepdims=True)
        p_grp = 1.0 / jnp.sum(jnp.where(is_grp, jnp.exp(gl - g_max), 0.0), axis=-1, keepdims=True)
        e_lo = N_GROUPS + g_sel * EXPERTS_PER_GROUP
        in_grp = (lane >= e_lo) & (lane < e_lo + EXPERTS_PER_GROUP)
        el = jnp.where(in_grp, logits, NEG_BIG)
        v1 = jnp.max(el, axis=-1, keepdims=True)
        i1 = jnp.min(jnp.where(el == v1, lane, ROUTER_LANES), axis=-1, keepdims=True)
        el2 = jnp.where(lane == i1, NEG_BIG, el)
        v2 = jnp.max(el2, axis=-1, keepdims=True)
        i2 = jnp.min(jnp.where(el2 == v2, lane, ROUTER_LANES), axis=-1, keepdims=True)
        e2 = jnp.exp(v2 - v1)
        w1 = p_grp / (1.0 + e2)
        w2 = p_grp * e2 / (1.0 + e2)
        e1, e2i = i1 - N_GROUPS, i2 - N_GROUPS
        hot = ((lane == e1) | (lane == e2i)).astype(F32)
        earlier = (lax.broadcasted_iota(jnp.int32, (sub, sub), 0) > lax.broadcasted_iota(jnp.int32, (sub, sub), 1))
        before = carry_ref[...] + _dot(earlier.astype(BF16), hot.astype(BF16))
        r1 = jnp.sum(jnp.where(lane == e1, before, 0.0), axis=-1, keepdims=True)
        r2 = jnp.sum(jnp.where(lane == e2i, before, 0.0), axis=-1, keepdims=True)
        carry_ref[...] += jnp.sum(hot, axis=0, keepdims=True)
        fields = (e1.astype(F32), e2i.astype(F32), w1, w2, r1, r2)
        rt = jnp.zeros(logits.shape, F32)
        for k, val in enumerate(fields):
            rt = jnp.where(lane == k, val, rt)
        rt_ref[pl.ds(r0, sub), :] = rt

    @pl.when(pl.program_id(0) == 0)
    def _():
        carry_ref[...] = cin_ref[...]

    starts = list(range(0, tt, sub))
    independent = [branches(pl.ds(r0, sub)) for r0 in starts]
    pending = None
    for r0, ind in zip(starts, independent):
        pre = mix(pl.ds(r0, sub), *ind)
        if pending is not None:
            route(*pending)
        pending = (r0, pre)
    route(*pending)
    cnt_ref[...] = carry_ref[...]


def _merge(x2, og2, ol2, wp, cnt_in, *, tt):
    t, d = x2.shape
    assert d == ROW_TILE * LANES
    row = lambda w: pl.BlockSpec((tt, w), lambda i: (i, 0))
    return pl.pallas_call(
        functools.partial(_merge_kernel, sub=min(TOKEN_TILE, tt)),
        grid=(t // tt,),
        in_specs=([row(d), row(GLA_V), row(MLA_HEADS * MLA_KV_RANK)] + [_const_spec(wp[n].shape) for n in _MERGE_WEIGHTS]
                  + [_const_spec(cnt_in.shape)]),
        out_specs=[pl.BlockSpec((tt * ROW_TILE, LANES), lambda i: (i, 0)), row(ROUTER_LANES),
                   pl.BlockSpec((1, ROUTER_LANES), lambda i: (0, 0))],
        out_shape=[jax.ShapeDtypeStruct((t * ROW_TILE, LANES), F32), jax.ShapeDtypeStruct((t, ROUTER_LANES), F32),
                   jax.ShapeDtypeStruct((1, ROUTER_LANES), F32)],
        scratch_shapes=[pltpu.VMEM((1, ROUTER_LANES), F32)],
        compiler_params=_cparams("arbitrary"),
        name="merge",
    )(x2, og2, ol2, *[wp[n] for n in _MERGE_WEIGHTS], cnt_in)


def _row_copy(src_ref, src_row, dst_ref, dst_row, sem):
    return pltpu.make_async_copy(src_ref.at[pl.ds(src_row * ROW_TILE, ROW_TILE)],
                                 dst_ref.at[pl.ds(dst_row * ROW_TILE, ROW_TILE)], sem)


def _group_spans(n_tokens, tile):
    spans, first = [], 0
    for n in n_tokens:
        spans.append((first, n // tile))
        first += n // tile
    return spans


def _group_spec(block, span):
    first, steps = span
    return pl.BlockSpec(block, lambda i, *_: (jnp.clip(i - first, 0, steps - 1), 0))


def _dispatch_kernel(pos_ref, zrow_ref, nu_ref, *refs, td, spans):
    h1t_refs, (xs_ref, zero_ref, sem, zsem) = refs[:len(spans)], refs[len(spans):]
    i = pl.program_id(0)

    @pl.when(i == 0)
    def _():
        zero_ref[...] = jnp.zeros(zero_ref.shape, F32)
        tile_rows = EXPERT_ROWS * ROW_TILE
        fill = lambda row: pltpu.make_async_copy(
            zero_ref, xs_ref.at[pl.ds(pl.multiple_of(row * ROW_TILE, ROW_TILE), tile_rows)], zsem)
        for e in range(N_EXPERTS):
            @pl.when(zrow_ref[e] >= 0)
            def _():
                fill(zrow_ref[e]).start()
        for e in range(N_EXPERTS):
            @pl.when(zrow_ref[e] >= 0)
            def _():
                fill(zrow_ref[e]).wait()

        def fill_tail(r, carry):
            fill(r * EXPERT_ROWS).start()
            fill(r * EXPERT_ROWS).wait()
            return carry

        lax.fori_loop(nu_ref[0], xs_ref.shape[0] // tile_rows, fill_tail, 0)

    base = i * (2 * td)
    for h1t_ref, (first, steps) in zip(h1t_refs, spans):
        @pl.when((i >= first) & (i < first + steps))
        def _():
            for t in range(td):
                for k in range(2):
                    dst = pl.multiple_of(pos_ref[base + 2 * t + k] * ROW_TILE, ROW_TILE)
                    pltpu.make_async_copy(h1t_ref.at[pl.ds(t * ROW_TILE, ROW_TILE)],
                                          xs_ref.at[pl.ds(dst, ROW_TILE)], sem).start(priority=k)
            for _ in range(2 * td):
                _row_copy(h1t_ref, 0, xs_ref, 0, sem).wait()


def _dispatch(h1ts, pos, zrow, n_used, *, n_rows, td):
    spans = _group_spans([h.shape[0] // ROW_TILE for h in h1ts], td)
    return pl.pallas_call(
        functools.partial(_dispatch_kernel, td=td, spans=spans),
        grid_spec=pltpu.PrefetchScalarGridSpec(
            num_scalar_prefetch=3,
            grid=(sum(steps for _, steps in spans),),
            in_specs=[_group_spec((td * ROW_TILE, LANES), span) for span in spans],
            out_specs=pl.BlockSpec(memory_space=pl.ANY),
            scratch_shapes=[pltpu.VMEM((EXPERT_ROWS * ROW_TILE, LANES), F32),
                            pltpu.SemaphoreType.DMA(()), pltpu.SemaphoreType.DMA(())]),
        out_shape=jax.ShapeDtypeStruct((n_rows * ROW_TILE, LANES), F32),
        compiler_params=_cparams("arbitrary"),
        name="dispatch",
    )(pos, zrow, n_used, *h1ts)


def _experts_kernel(te_ref, tb_ref, nu_ref, xs_ref, wg_ref, wu_ref, wd_ref, out_ref, wgb_ref, wub_ref, wdb_ref):
    del tb_ref
    r = pl.program_id(0)
    rows = xs_ref.shape[0] // ROW_TILE

    @pl.when((r == 0) | (te_ref[r] != te_ref[jnp.maximum(r - 1, 0)]))
    def _():
        wgb_ref[...] = wg_ref[0].astype(BF16)
        wub_ref[...] = wu_ref[0].astype(BF16)
        wdb_ref[...] = wd_ref[0].astype(BF16)

    @pl.when(r < nu_ref[0])
    def _():
        x = jnp.concatenate([xs_ref[pl.ds(s, rows, stride=ROW_TILE), :] for s in range(ROW_TILE)], axis=1).astype(BF16)
        gate = _dot(x, wgb_ref[...])
        up = _dot(x, wub_ref[...])
        hid = (gate * jax.nn.sigmoid(gate)) * up
        out = _dot(hid.astype(BF16), wdb_ref[...])
        for s in range(ROW_TILE):
            out_ref[pl.ds(s, rows, stride=ROW_TILE), :] = out[:, s * LANES:(s + 1) * LANES]

    @pl.when(pl.program_id(0) >= nu_ref[0])
    def _():
        out_ref[...] = jnp.zeros(out_ref.shape, F32)


def _experts(xs, tile_e, tile_blk, n_used, wp):
    n_rows = xs.shape[0] // ROW_TILE
    d = ROW_TILE * LANES
    blk = pl.BlockSpec((EXPERT_ROWS * ROW_TILE, LANES), lambda r, te, tb, nu: (tb[r], 0))
    return pl.pallas_call(
        _experts_kernel,
        grid_spec=pltpu.PrefetchScalarGridSpec(
            num_scalar_prefetch=3,
            grid=(n_rows // EXPERT_ROWS,),
            in_specs=[blk,
                      pl.BlockSpec((1, d, D_EXPERT), lambda r, te, tb, nu: (te[r], 0, 0)),
                      pl.BlockSpec((1, d, D_EXPERT), lambda r, te, tb, nu: (te[r], 0, 0)),
                      pl.BlockSpec((1, D_EXPERT, d), lambda r, te, tb, nu: (te[r], 0, 0))],
            out_specs=pl.BlockSpec((EXPERT_ROWS * ROW_TILE, LANES), lambda r, te, tb, nu: (r, 0)),
            scratch_shapes=[pltpu.VMEM((d, D_EXPERT), BF16), pltpu.VMEM((d, D_EXPERT), BF16),
                            pltpu.VMEM((D_EXPERT, d), BF16)]),
        out_shape=jax.ShapeDtypeStruct(xs.shape, F32),
        compiler_params=_cparams("arbitrary"),
        name="experts",
    )(tile_e, tile_blk, n_used, xs, wp["w_gate"], wp["w_up"], wp["w_down"])


def _combine_kernel(pos_ref, *refs, tc, spans):
    g = len(spans)
    h1t_refs, rt_refs = refs[:g], refs[g:2 * g]
    outs_ref, l2g_ref, l2b_ref = refs[2 * g:2 * g + 3]
    y_refs = refs[2 * g + 3:3 * g + 3]
    g_ref, sem = refs[3 * g + 3:]
    i = pl.program_id(0)
    n_steps = pl.num_programs(0)

    def gather(step):
        slot = step % 2
        base = step * (2 * tc)
        for t in range(tc):
            for k in range(2):
                src = pl.multiple_of(pos_ref[base + 2 * t + k] * ROW_TILE, ROW_TILE)
                pltpu.make_async_copy(outs_ref.at[pl.ds(src, ROW_TILE)],
                                      g_ref.at[slot, k, pl.ds(t * ROW_TILE, ROW_TILE)],
                                      sem.at[slot]).start(priority=k)

    @pl.when(i == 0)
    def _():
        gather(0)

    @pl.when(i + 1 < n_steps)
    def _():
        gather(i + 1)

    slot = i % 2
    for _ in range(2 * tc):
        pltpu.make_async_copy(outs_ref.at[pl.ds(0, ROW_TILE)], g_ref.at[slot, 0, pl.ds(0, ROW_TILE)], sem.at[slot]).wait()
    for h1t_ref, rt_ref, y_ref, (first, steps) in zip(h1t_refs, rt_refs, y_refs, spans):
        @pl.when((i >= first) & (i < first + steps))
        def _():
            rt = rt_ref[...]
            w0, w1 = rt[:, RT_W:RT_W + 1], rt[:, RT_W + 1:RT_W + 2]
            cols = []
            for s in range(ROW_TILE):
                rows = pl.ds(s, tc, stride=ROW_TILE)
                cols.append(DEEPNORM_ALPHA * h1t_ref[rows, :] + (w0 * g_ref[slot, 0, rows, :] + w1 * g_ref[slot, 1, rows, :]))
            y_ref[...] = _layer_norm(jnp.concatenate(cols, axis=1), l2g_ref[...], l2b_ref[...])


def _combine(h1ts, rts, outs, pos, wp, *, tc):
    d = ROW_TILE * LANES
    spans = _group_spans([r.shape[0] for r in rts], tc)
    const = lambda: pl.BlockSpec((1, d), lambda i, *_: (0, 0))
    return pl.pallas_call(
        functools.partial(_combine_kernel, tc=tc, spans=spans),
        grid_spec=pltpu.PrefetchScalarGridSpec(
            num_scalar_prefetch=1,
            grid=(sum(steps for _, steps in spans),),
            in_specs=([_group_spec((tc * ROW_TILE, LANES), span) for span in spans]
                      + [_group_spec((tc, ROUTER_LANES), span) for span in spans]
                      + [pl.BlockSpec(memory_space=pl.ANY), const(), const()]),
            out_specs=[_group_spec((tc, d), span) for span in spans],
            scratch_shapes=[pltpu.VMEM((2, 2, tc * ROW_TILE, LANES), F32), pltpu.SemaphoreType.DMA((2,))]),
        out_shape=[jax.ShapeDtypeStruct((r.shape[0], d), F32) for r in rts],
        compiler_params=_cparams("arbitrary"),
        name="combine",
    )(pos, *h1ts, *rts, outs, wp["ln2_g"], wp["ln2_b"])


def _route_plan(rt, cnt):
    t = rt.shape[0]
    n_tiles = (2 * t) // EXPERT_ROWS + N_EXPERTS
    counts = cnt[0, :N_EXPERTS].astype(jnp.int32)
    padded = (counts + (EXPERT_ROWS - 1)) // EXPERT_ROWS * EXPERT_ROWS
    ends = jnp.cumsum(padded)
    starts = ends - padded
    eid = rt[:, RT_E:RT_E + 2].astype(jnp.int32)
    rank = rt[:, RT_RANK:RT_RANK + 2].astype(jnp.int32)
    pos = (jnp.take(starts, eid) + rank).reshape(-1)
    n_used = ends[-1] // EXPERT_ROWS
    tile_blk = jnp.minimum(jnp.arange(n_tiles, dtype=jnp.int32), n_used - 1)
    tile_e = jnp.sum((tile_blk[:, None] * EXPERT_ROWS >= ends[None, :]).astype(jnp.int32), axis=1)
    zrow = jnp.where(padded > 0, ends - EXPERT_ROWS, -1)
    return pos, zrow.astype(jnp.int32), tile_e.astype(jnp.int32), tile_blk, n_used.reshape(1).astype(jnp.int32), n_tiles


def _rope_tables(pos):
    inv = ROPE_THETA ** (-jnp.arange(0, MLA_ROPE, 2, dtype=F32) / MLA_ROPE)
    ang = pos.astype(F32)[:, None] * inv[None, :]
    cs = jnp.concatenate([jnp.cos(ang), jnp.sin(ang)], axis=-1)
    return cs[None], cs.T[None]


def _prep_weights(ln_in_g, ln_in_b, w_in, w_gk2, b_gk, gla_norm_g, q_norm_g, kv_norm_g, w_uq, w_uk, w_uv,
                  w_br_gla, w_br_mla, w_mg, b_mg, w_out, ln1_g, ln1_b, w_rg, b_rg, w_re, b_re,
                  w_gate, w_up, w_down, ln2_g, ln2_b):
    d = w_in.shape[1]
    w = w_in[0]
    c0 = 0
    wk = w[:, c0:c0 + GLA_QK]; c0 += GLA_QK
    wv = w[:, c0:c0 + GLA_V]; c0 += GLA_V
    wgr = w[:, c0:c0 + GLA_RANK]; c0 += GLA_RANK
    wckv = w[:, c0:c0 + MLA_KV_RANK]; c0 += MLA_KV_RANK
    wkr = w[:, c0:c0 + MLA_ROPE]; c0 += MLA_ROPE
    wq = w[:, c0:c0 + GLA_QK]; c0 += GLA_QK
    wg = w[:, c0:c0 + GLA_V]; c0 += GLA_V
    wcq = w[:, c0:c0 + MLA_Q_RANK]
    w_tok = jnp.concatenate([wk, wv, wq, wckv, wkr, wgr, jnp.zeros((d, TOK_PAD - TOK_USED), F32)], axis=1)
    w_tr = jnp.concatenate([wckv, wcq], axis=1).T
    uq = w_uq[0].reshape(MLA_Q_RANK, MLA_HEADS, MLA_QK_DIM)
    uq_perm = jnp.concatenate([
        uq[:, :, :MLA_NOPE].reshape(MLA_Q_RANK, -1),
        uq[:, :, MLA_NOPE:MLA_NOPE + MLA_HALF].reshape(MLA_Q_RANK, -1),
        uq[:, :, MLA_NOPE + MLA_HALF:].reshape(MLA_Q_RANK, -1)], axis=1)
    uv = w_uv[0].transpose(1, 0, 2).reshape(MLA_HEADS // 2, 2, MLA_KV_RANK, MLA_DV)
    eye = jnp.eye(2, dtype=F32)
    w_uv_bd = (uv[:, :, :, None, :] * eye[None, :, None, :, None]).reshape(
        MLA_HEADS // 2, 2 * MLA_KV_RANK, 2 * MLA_DV)
    w_r = jnp.concatenate([w_rg[0], w_re[0].transpose(1, 0, 2).reshape(d, N_EXPERTS),
                           jnp.zeros((d, ROUTER_LANES - N_GROUPS - N_EXPERTS), F32)], axis=1)
    w_r_hi = w_r.astype(BF16)
    b_r = jnp.concatenate([b_rg[0], b_re[0].reshape(-1), jnp.zeros((ROUTER_LANES - N_GROUPS - N_EXPERTS,), F32)])
    row = lambda a: a.reshape(1, -1)
    return {
        "ln_g": row(ln_in_g), "ln_b": row(ln_in_b),
        "w_tok": w_tok.astype(BF16), "w_tr": w_tr.astype(BF16),
        "w_gk2": w_gk2[0].astype(BF16), "b_gk": row(b_gk[0]),
        "kv_g": row(kv_norm_g[0]), "kv_gt": kv_norm_g[0].reshape(-1, 1), "q_gt": q_norm_g[0].reshape(-1, 1),
        "w_uqt": uq_perm.T.astype(BF16), "w_uk": w_uk[0].transpose(1, 0, 2).astype(BF16),
        "w_g": wg.astype(BF16), "gla_g": row(gla_norm_g[0]),
        "w_br_gla": w_br_gla[0].astype(BF16), "w_uv_bd": w_uv_bd.astype(BF16), "w_br_mla": w_br_mla[0].astype(BF16),
        "w_mg": w_mg[0].astype(BF16), "b_mg": row(b_mg[0]), "w_out": w_out[0].astype(BF16),
        "ln1_g": row(ln1_g[0]), "ln1_b": row(ln1_b[0]),
        "w_r_hi": w_r_hi, "w_r_lo": (w_r - w_r_hi.astype(F32)).astype(BF16), "b_r": row(b_r),
        "w_gate": w_gate[0], "w_up": w_up[0], "w_down": w_down[0],
        "ln2_g": row(ln2_g[0]), "ln2_b": row(ln2_b[0]),
    }


def _key_tiles(kcat, vt):
    bsz, length, _ = kcat.shape
    n = -(-length // KEY_TILE)
    pad = n * KEY_TILE - length
    kcat = jnp.pad(kcat, ((0, 0), (0, pad), (0, 0)))
    vt = jnp.pad(vt, ((0, 0), (0, 0), (0, pad)))
    return (kcat.reshape(bsz, n, KEY_TILE, MLA_CAT),
            vt.reshape(bsz, MLA_KV_RANK, n, KEY_TILE).transpose(0, 2, 1, 3))


def _ffn(groups, wp):
    h1ts, rts = [], []
    cnt = jnp.zeros((1, ROUTER_LANES), F32)
    for x, og, ol in groups:
        t, d = x.shape[0] * x.shape[1], x.shape[2]
        h1t, rt, cnt = _merge(x.reshape(t, d), og.reshape(t, -1), ol.reshape(t, -1), wp, cnt, tt=min(MERGE_TILE, t))
        h1ts.append(h1t)
        rts.append(rt)
    n_fields = RT_RANK + 2
    pos, zrow, tile_e, tile_blk, n_used, n_tiles = _route_plan(
        jnp.concatenate([rt[:, :n_fields] for rt in rts], axis=0), cnt)
    xs = _dispatch(h1ts, pos, zrow, n_used, n_rows=n_tiles * EXPERT_ROWS, td=MOE_TOKEN_TILE)
    outs = _experts(xs, tile_e, tile_blk, n_used, wp)
    ys = _combine(h1ts, rts, outs, pos, wp, tc=MOE_TOKEN_TILE)
    return [y.reshape(x.shape) for y, (x, _, _) in zip(ys, groups)]


def kernel(x_prompt, x_sample, cache_mla_latent, cache_mla_krope, state_gla, meta_tokens, ln_in_g, ln_in_b, w_in, w_gk2, b_gk, gla_norm_g, q_norm_g, kv_norm_g, w_uq, w_uk, w_uv, w_br_gla, w_br_mla, w_mg, b_mg, w_out, ln1_g, ln1_b, w_rg, b_rg, w_re, b_re, w_gate, w_up, w_down, ln2_g, ln2_b):
    bp, sp, d = x_prompt.shape
    bs, ss, _ = x_sample.shape
    past = cache_mla_latent.shape[2]
    wp = _prep_weights(ln_in_g, ln_in_b, w_in, w_gk2, b_gk, gla_norm_g, q_norm_g, kv_norm_g, w_uq, w_uk, w_uv,
                       w_br_gla, w_br_mla, w_mg, b_mg, w_out, ln1_g, ln1_b, w_rg, b_rg, w_re, b_re,
                       w_gate, w_up, w_down, ln2_g, ln2_b)

    cs_m, cst_m = _rope_tables(jnp.arange(-N_META, 0, dtype=jnp.int32))
    m = _proj(meta_tokens[None], cs_m, cst_m, wp, cl=N_META, tt=N_META)
    _, _, m_kt, m_v, m_ebl, m_ckv, m_kr, m_kcat, m_vt, _ = m
    zero_state = jnp.zeros((1, GLA_HEADS, GLA_DK, GLA_DV), F32)
    _, m_state = _gla(m[0], m[1], m_kt, m_v, m_ebl, zero_state, cl=N_META, ts=N_META)

    cs_p, cst_p = _rope_tables(jnp.arange(sp, dtype=jnp.int32))
    p_qd, p_ki, p_kt, p_v, p_ebl, p_ckv, p_kr, p_kcat, p_vt, p_qt = _proj(
        x_prompt, cs_p, cst_p, wp, cl=CHUNK, tt=PROJ_TILE)
    p_o, p_state = _gla(p_qd, p_ki, p_kt, p_v, p_ebl, m_state, cl=CHUNK, ts=TOKEN_TILE)
    rep = lambda a, n: jnp.broadcast_to(a, (n,) + a.shape[1:])
    lat_p = jnp.concatenate([rep(m_ckv, bp), p_ckv], axis=1)
    kr_p = jnp.concatenate([rep(m_kr, bp), p_kr], axis=1)
    kcat_p, vt_p = _key_tiles(p_kcat, p_vt)
    km, vm = m_kcat[0], m_vt[0]
    p_ol = _attn(p_qt, kcat_p, vt_p, km, vm, bq=TOKEN_TILE, causal=True, n_valid_last=KEY_TILE, n_sets=2)

    ts_all = bs * ss
    cs_s, cst_s = _rope_tables(past + (jnp.arange(ts_all, dtype=jnp.int32) % ss))
    s_qd, s_ki, s_kt, s_v, s_ebl, s_ckv, s_kr, s_kcat, s_vt, s_qt = _proj(
        x_sample.reshape(1, ts_all, d), cs_s, cst_s, wp, cl=ss, tt=min(TOKEN_TILE, ts_all))
    per_stream = lambda a: a.reshape(bs, ss, a.shape[-1])
    s_o, s_state = _gla(per_stream(s_qd), per_stream(s_ki), per_stream(s_kt), per_stream(s_v),
                        s_ebl.reshape(bs, 1, 1, GLA_QK), state_gla[0].astype(F32), cl=ss, ts=ss)
    s_ckv, s_kr = per_stream(s_ckv), per_stream(s_kr)
    cache_kcat = jnp.concatenate([cache_mla_latent[0], cache_mla_krope[0]], axis=-1).astype(BF16)
    cache_vt = cache_mla_latent[0].astype(BF16).transpose(0, 2, 1)
    new_vt = s_vt.reshape(MLA_KV_RANK, bs, ss).transpose(1, 0, 2)
    kcat_s, vt_s = _key_tiles(jnp.concatenate([cache_kcat, per_stream(s_kcat)], axis=1),
                              jnp.concatenate([cache_vt, new_vt], axis=2))
    qt = s_qt.transpose(0, 2, 3, 1, 4).reshape(MLA_HEADS, MLA_CAT, bs, ss)
    qt = qt.transpose(2, 1, 0, 3).reshape(bs, MLA_CAT, MLA_HEADS * ss // KEY_TILE, KEY_TILE)
    qt = qt.transpose(0, 2, 1, 3)[:, None]
    s_ol = _attn(qt, kcat_s, vt_s, km, vm, bq=ss, causal=False, n_valid_last=(past + ss - 1) % KEY_TILE + 1, n_sets=1)
    y_prompt, y_sample = _ffn([(x_prompt, p_o, p_ol), (x_sample, s_o, s_ol)], wp)

    return (y_prompt, y_sample, lat_p[None], kr_p[None], p_state[None].astype(state_gla.dtype),
            s_ckv[None], s_kr[None], s_state[None].astype(state_gla.dtype))
```

```python
import functools

import jax
import jax.numpy as jnp
from jax import lax
from jax.experimental import pallas as pl
from jax.experimental.pallas import tpu as pltpu

F32 = jnp.float32
BF16 = jnp.bfloat16

CHUNK = 64
N_META = 16
GLA_HEADS = 4
GLA_DK = 128
GLA_DV = 256
GLA_RANK = 16
GLA_TAU = 16.0
GLA_QK = GLA_HEADS * GLA_DK
GLA_V = GLA_HEADS * GLA_DV
GLA_SCALE = GLA_DK ** -0.5
MLA_HEADS = 16
MLA_Q_RANK = 384
MLA_KV_RANK = 128
MLA_NOPE = 64
MLA_ROPE = 32
MLA_HALF = MLA_ROPE // 2
MLA_DV = 64
MLA_QK_DIM = MLA_NOPE + MLA_ROPE
MLA_CAT = MLA_KV_RANK + MLA_ROPE
MLA_SCALE = MLA_QK_DIM ** -0.5
LOG2_E = 1.4426950408889634
Q_SCALE = MLA_SCALE * LOG2_E
V_ROWS = MLA_KV_RANK + 16
ROPE_THETA = 10000.0
N_GROUPS = 4
EXPERTS_PER_GROUP = 8
N_EXPERTS = N_GROUPS * EXPERTS_PER_GROUP
D_EXPERT = 256
LN_EPS = 1e-5
RMS_EPS = 1e-6
DEEPNORM_ALPHA = 2.0 ** 0.25

LANES = 128
MXU_DIM = 256
VMEM_LIMIT_BYTES = 56 * 1024 * 1024

TOKEN_TILE = 256
MERGE_TILE = 2 * TOKEN_TILE
PROJ_TILE = 2 * TOKEN_TILE
KEY_TILE = MXU_DIM
MOE_TOKEN_TILE = 256
EXPERT_ROWS = 256
ROW_TILE = 8
ROUTER_LANES = LANES
RT_E, RT_W, RT_RANK = 0, 2, 4
NEG_BIG = -1e30

NT_DIMS = (((1,), (1,)), ((), ()))
TN_DIMS = (((0,), (0,)), ((), ()))


def _cparams(*sem):
    return pltpu.CompilerParams(dimension_semantics=sem, vmem_limit_bytes=VMEM_LIMIT_BYTES)


def _const_spec(shape):
    nd = len(shape)
    return pl.BlockSpec(shape, lambda *_: (0,) * nd, pipeline_mode=pl.Buffered(1))


def _layer_norm(x, g, b):
    mu = jnp.mean(x, axis=-1, keepdims=True)
    xc = x - mu
    var = jnp.mean(xc * xc, axis=-1, keepdims=True)
    return xc * lax.rsqrt(var + LN_EPS) * g + b


def _dot(a, b):
    return jnp.dot(a, b, preferred_element_type=F32)


TOK_COLS = (GLA_QK, GLA_V, GLA_QK, MLA_KV_RANK, MLA_ROPE, GLA_RANK)
TOK_USED = sum(TOK_COLS)
TOK_PAD = -(-TOK_USED // MXU_DIM) * MXU_DIM
TR_ROWS = MLA_KV_RANK + MLA_Q_RANK


def _proj_kernel(x_ref, cs_ref, cst_ref, lng_ref, lnb_ref, wtok_ref, wtr_ref, wgk2_ref, bgk_ref,
                 kvg_ref, kvgt_ref, qngt_ref, wuqt_ref, wuk_ref,
                 qd_ref, ki_ref, kt_ref, v_ref, ebl_ref, ckv_ref, kr_ref, kcat_ref, vt_ref, qt_ref, *, cl, sub):
    tt = x_ref.shape[1]
    n_chunks = sub // cl

    def project(r0):
        h = _layer_norm(x_ref[0, r0:r0 + sub, :], lng_ref[...], lnb_ref[...])
        hb = h.astype(BF16)
        z = _dot(hb, wtok_ref[...])
        zt = lax.dot_general(wtr_ref[...], hb, NT_DIMS, preferred_element_type=F32)
        return z, zt

    def finish(r0, z, zt):
        rows = slice(r0, r0 + sub)
        o0 = 0
        k = z[:, o0:o0 + GLA_QK]; o0 += GLA_QK
        v = z[:, o0:o0 + GLA_V]; o0 += GLA_V
        q = z[:, o0:o0 + GLA_QK]; o0 += GLA_QK
        ckv_raw = z[:, o0:o0 + MLA_KV_RANK]; o0 += MLA_KV_RANK
        kr = z[:, o0:o0 + MLA_ROPE]; o0 += MLA_ROPE
        gklr = z[:, o0:o0 + GLA_RANK]

        gz = _dot(gklr.astype(BF16), wgk2_ref[...]) + bgk_ref[...]
        gk = (jnp.minimum(gz, 0.0) - jnp.log(1.0 + jnp.exp(-jnp.abs(gz)))) * (1.0 / GLA_TAU)
        row_in_chunk = lax.broadcasted_iota(jnp.int32, gk.shape, 0) & (cl - 1)
        b = gk
        shift = 1
        while shift < cl:
            b = b + jnp.where(row_in_chunk >= shift, pltpu.roll(b, shift, 0), 0.0)
            shift *= 2
        b3 = b.reshape(n_chunks, cl, GLA_QK)
        bl = b3[:, cl - 1:cl, :]
        qd_ref[0, rows, :] = (q * GLA_SCALE * jnp.exp(b)).astype(BF16)
        ki_ref[0, rows, :] = (k * jnp.exp(-b)).astype(BF16)
        kt_ref[0, rows, :] = (k.reshape(n_chunks, cl, GLA_QK) * jnp.exp(bl - b3)).reshape(sub, GLA_QK).astype(BF16)
        v_ref[0, rows, :] = v.astype(BF16)
        ebl_ref[0, r0 // cl:r0 // cl + n_chunks] = jnp.exp(bl)

        ckv = ckv_raw * lax.rsqrt(jnp.mean(ckv_raw * ckv_raw, axis=-1, keepdims=True) + RMS_EPS) * kvg_ref[...]
        cs = cs_ref[0, rows, :]
        cos, sin = cs[:, :MLA_HALF], cs[:, MLA_HALF:]
        x1, x2 = kr[:, :MLA_HALF], kr[:, MLA_HALF:]
        kr_rot = jnp.concatenate([x1 * cos - x2 * sin, x2 * cos + x1 * sin], axis=-1)
        ckv_ref[0, rows, :] = ckv
        kr_ref[0, rows, :] = kr_rot
        kcat_ref[0, rows, :MLA_KV_RANK] = ckv.astype(BF16)
        kcat_ref[0, rows, MLA_KV_RANK:] = kr_rot.astype(BF16)

        ckvt = zt[:MLA_KV_RANK]
        ckvt = ckvt * lax.rsqrt(jnp.mean(ckvt * ckvt, axis=0, keepdims=True) + RMS_EPS) * kvgt_ref[...]
        vt_ref[0, :, rows] = ckvt.astype(BF16)
        cqt = zt[MLA_KV_RANK:]
        cqt = cqt * lax.rsqrt(jnp.mean(cqt * cqt, axis=0, keepdims=True) + RMS_EPS) * qngt_ref[...]
        qmt = _dot(wuqt_ref[...], cqt.astype(BF16))
        n_nope = MLA_HEADS * MLA_NOPE
        n_half = MLA_HEADS * MLA_HALF
        cst = cst_ref[0, :, rows]
        cos_t = jnp.concatenate([cst[:MLA_HALF]] * MLA_HEADS, axis=0)
        sin_t = jnp.concatenate([cst[MLA_HALF:]] * MLA_HEADS, axis=0)
        r1 = qmt[n_nope:n_nope + n_half]
        r2 = qmt[n_nope + n_half:]
        rot1 = ((r1 * cos_t - r2 * sin_t) * Q_SCALE).astype(BF16)
        rot2 = ((r2 * cos_t + r1 * sin_t) * Q_SCALE).astype(BF16)
        blk = r0 // sub
        for hd in range(MLA_HEADS):
            nope = qmt[hd * MLA_NOPE:(hd + 1) * MLA_NOPE].astype(BF16)
            qlat = _dot(wuk_ref[hd], nope) * Q_SCALE
            qt_ref[0, blk, hd, :MLA_KV_RANK, :] = qlat.astype(BF16)
            qt_ref[0, blk, hd, MLA_KV_RANK:MLA_KV_RANK + MLA_HALF, :] = rot1[hd * MLA_HALF:(hd + 1) * MLA_HALF]
            qt_ref[0, blk, hd, MLA_KV_RANK + MLA_HALF:, :] = rot2[hd * MLA_HALF:(hd + 1) * MLA_HALF]

    starts = list(range(0, tt, sub))
    projected = [project(r0) for r0 in starts]
    for r0, (z, zt) in zip(starts, projected):
        finish(r0, z, zt)


def _proj(x, cs, cst, wp, *, cl, tt):
    bsz, s, d = x.shape
    nt = s // tt
    n_chunks = tt // cl
    sub = min(TOKEN_TILE, tt)
    tok = lambda w: pl.BlockSpec((1, tt, w), lambda b, t: (b, t, 0))
    in_specs = [
        tok(d),
        pl.BlockSpec((1, tt, MLA_ROPE), lambda b, t: (0, t, 0)),
        pl.BlockSpec((1, MLA_ROPE, tt), lambda b, t: (0, 0, t)),
    ] + [_const_spec(wp[n].shape) for n in _PROJ_WEIGHTS]
    out_shape = [
        jax.ShapeDtypeStruct((bsz, s, GLA_QK), BF16),
        jax.ShapeDtypeStruct((bsz, s, GLA_QK), BF16),
        jax.ShapeDtypeStruct((bsz, s, GLA_QK), BF16),
        jax.ShapeDtypeStruct((bsz, s, GLA_V), BF16),
        jax.ShapeDtypeStruct((bsz, s // cl, 1, GLA_QK), F32),
        jax.ShapeDtypeStruct((bsz, s, MLA_KV_RANK), F32),
        jax.ShapeDtypeStruct((bsz, s, MLA_ROPE), F32),
        jax.ShapeDtypeStruct((bsz, s, MLA_CAT), BF16),
        jax.ShapeDtypeStruct((bsz, MLA_KV_RANK, s), BF16),
        jax.ShapeDtypeStruct((bsz, s // sub, MLA_HEADS, MLA_CAT, sub), BF16),
    ]
    out_specs = [
        tok(GLA_QK), tok(GLA_QK), tok(GLA_QK), tok(GLA_V),
        pl.BlockSpec((1, n_chunks, 1, GLA_QK), lambda b, t: (b, t, 0, 0)),
        tok(MLA_KV_RANK), tok(MLA_ROPE), tok(MLA_CAT),
        pl.BlockSpec((1, MLA_KV_RANK, tt), lambda b, t: (b, 0, t)),
        pl.BlockSpec((1, tt // sub, MLA_HEADS, MLA_CAT, sub), lambda b, t: (b, t, 0, 0, 0)),
    ]
    return pl.pallas_call(
        functools.partial(_proj_kernel, cl=cl, sub=sub),
        grid=(bsz, nt),
        in_specs=in_specs,
        out_specs=out_specs,
        out_shape=out_shape,
        compiler_params=_cparams("parallel", "parallel"),
        name="proj",
    )(x, cs, cst, *[wp[n] for n in _PROJ_WEIGHTS])


_PROJ_WEIGHTS = ("ln_g", "ln_b", "w_tok", "w_tr", "w_gk2", "b_gk", "kv_g", "kv_gt", "q_gt", "w_uqt", "w_uk")


def _gla_kernel(qd_ref, ki_ref, kt_ref, v_ref, ebl_ref, s0_ref, o_ref, sfin_ref, st_ref, *, cl):
    t = pl.program_id(1)
    ts = qd_ref.shape[1]

    @pl.when(t == 0)
    def _():
        for hd in range(GLA_HEADS):
            st_ref[hd] = s0_ref[0, hd].T

    row = lax.broadcasted_iota(jnp.int32, (ts, ts), 0)
    col = lax.broadcasted_iota(jnp.int32, (ts, ts), 1)
    keep = (row >= col) & ((row & -cl) == (col & -cl))
    qk = lambda ref, hd: ref[0, :, hd * GLA_DK:(hd + 1) * GLA_DK]
    val = lambda hd: v_ref[0, :, hd * GLA_DV:(hd + 1) * GLA_DV]
    for hd in range(GLA_HEADS):
        a = lax.dot_general(qk(qd_ref, hd), qk(ki_ref, hd), NT_DIMS, preferred_element_type=F32)
        a = jnp.where(keep, a, 0.0).astype(BF16)
        o_ref[0, :, hd * GLA_DV:(hd + 1) * GLA_DV] = _dot(a, val(hd))
    for c in range(ts // cl):
        rows = slice(c * cl, (c + 1) * cl)
        for hd in range(GLA_HEADS):
            st = st_ref[hd]
            o_ref[0, rows, hd * GLA_DV:(hd + 1) * GLA_DV] += lax.dot_general(
                qk(qd_ref, hd)[rows], st.astype(BF16), NT_DIMS, preferred_element_type=F32)
            st_ref[hd] = (st * ebl_ref[0, c, :, hd * GLA_DK:(hd + 1) * GLA_DK]
                          + lax.dot_general(val(hd)[rows], qk(kt_ref, hd)[rows], TN_DIMS, preferred_element_type=F32))

    @pl.when(t == pl.num_programs(1) - 1)
    def _():
        for hd in range(GLA_HEADS):
            sfin_ref[0, hd] = st_ref[hd].T


def _gla(qd, ki, kt, v, ebl, s0, *, cl, ts):
    bsz, s, _ = qd.shape
    n_chunks = ts // cl
    s0_b = s0.shape[0]
    qk_spec = pl.BlockSpec((1, ts, GLA_QK), lambda b, t: (b, t, 0))
    v_spec = pl.BlockSpec((1, ts, GLA_V), lambda b, t: (b, t, 0))
    st_spec = pl.BlockSpec((1, GLA_HEADS, GLA_DK, GLA_DV), lambda b, t: (b, 0, 0, 0))
    s0_spec = st_spec if s0_b == bsz else pl.BlockSpec((1, GLA_HEADS, GLA_DK, GLA_DV), lambda b, t: (0, 0, 0, 0))
    return pl.pallas_call(
        functools.partial(_gla_kernel, cl=cl),
        grid=(bsz, s // ts),
        in_specs=[qk_spec, qk_spec, qk_spec, v_spec,
                  pl.BlockSpec((1, n_chunks, 1, GLA_QK), lambda b, t: (b, t, 0, 0)),
                  s0_spec],
        out_specs=[v_spec, st_spec],
        out_shape=[jax.ShapeDtypeStruct((bsz, s, GLA_V), F32),
                   jax.ShapeDtypeStruct((bsz, GLA_HEADS, GLA_DK, GLA_DV), F32)],
        scratch_shapes=[pltpu.VMEM((GLA_HEADS, GLA_DV, GLA_DK), F32)],
        compiler_params=_cparams("parallel", "arbitrary"),
        name="gla",
    )(qd, ki, kt, v, ebl, s0)


SCRATCH_PER_SET = 6
TILES_PER_TRIP = 4


def _attn_kernel(qt_ref, kcat_ref, vt_ref, km_ref, vm_ref, o_ref, *scratch, bq, causal, n_valid_last):
    n_sets = qt_ref.shape[1]
    n_cc = qt_ref.shape[3]
    n_kt = kcat_ref.shape[1]
    groups_per_cc = KEY_TILE // LANES

    def when(cond, fn):
        if isinstance(cond, bool):
            if cond:
                fn()
        else:
            pl.when(cond)(fn)

    def block(h):
        s_ref, p_ref, m_ref, a_ref, acc_ref, mx_ref = scratch[SCRATCH_PER_SET * h:SCRATCH_PER_SET * (h + 1)]
        blk = pl.program_id(1) + h * pl.num_programs(1)
        n_int = blk if causal else n_kt - 1
        q = lambda c: qt_ref[0, h, 0, c]
        col_max = lambda x: jnp.max(x, axis=0, keepdims=True)
        col_sum = lambda x: jnp.sum(x, axis=0, keepdims=True)

        def scores(c, kt):
            s = _dot(kt, q(c))
            s_ref[c] = s
            mx_ref[c] = col_max(s)

        def softmax(c, mask):
            s = s_ref[c]
            if mask is None:
                s_max = mx_ref[c]
            else:
                s = jnp.where(mask, s, NEG_BIG)
                s_max = col_max(s)
            m_prev = m_ref[c]
            m_new = jnp.maximum(m_prev, s_max)
            a_ref[c] = jnp.exp2(m_prev - m_new)
            m_ref[c] = m_new
            p_ref[c] = jnp.exp2(s - m_new).astype(BF16)

        def softmax_diagonal(c):
            half = KEY_TILE // 2
            chunks_per_half = half // CHUNK
            second = lax.broadcasted_iota(jnp.int32, (CHUNK, half), 1) >= CHUNK
            m_prev = m_ref[c]
            m_out, a_out = [], []
            for lh in range(KEY_TILE // half):
                lanes = slice(lh * half, (lh + 1) * half)
                n_full = chunks_per_half * lh + 1
                rows_full = slice(0, n_full * CHUNK)
                rows_part = slice(n_full * CHUNK, (n_full + 1) * CHUNK)
                full = s_ref[c, rows_full, lanes]
                part = jnp.where(second, s_ref[c, rows_part, lanes], NEG_BIG)
                m_new = jnp.maximum(m_prev[:, lanes], jnp.maximum(col_max(full), col_max(part)))
                p_ref[c, rows_full, lanes] = jnp.exp2(full - m_new).astype(BF16)
                p_ref[c, rows_part, lanes] = jnp.exp2(part - m_new).astype(BF16)
                if (n_full + 1) * CHUNK < KEY_TILE:
                    hidden = KEY_TILE - (n_full + 1) * CHUNK
                    p_ref[c, (n_full + 1) * CHUNK:, lanes] = jnp.zeros((hidden, half), BF16)
                m_out.append(m_new)
                a_out.append(jnp.exp2(m_prev[:, lanes] - m_new))
            m_ref[c] = jnp.concatenate(m_out, axis=1)
            a_ref[c] = jnp.concatenate(a_out, axis=1)

        def values(c, vt):
            acc_ref[c] = a_ref[c] * acc_ref[c] + _dot(vt, p_ref[c])

        def start():
            km, vm = km_ref[...], vm_ref[...]
            sm = [_dot(km, q(c)) for c in range(n_cc)]
            pm = []
            for c in range(n_cc):
                m0 = col_max(sm[c])
                m_ref[c] = m0
                pm.append(jnp.exp2(sm[c] - m0).astype(BF16))
            for c in range(n_cc):
                acc_ref[c] = _dot(vm, pm[c])
            kt = kcat_ref[0, 0]
            for c in range(n_cc):
                scores(c, kt)

        def first_step():
            kt_next = kcat_ref[0, 1]
            for c in range(n_cc):
                softmax(c, None)
                scores(c, kt_next)

        def no_step():
            a_ref[...] = jnp.ones(a_ref.shape, F32)
            p_ref[...] = jnp.zeros(p_ref.shape, BF16)

        def key_step(j):
            kt_next = kcat_ref[0, j + 1]
            vt_prev = vt_ref[0, j - 1]
            for c in range(n_cc):
                values(c, vt_prev)
                softmax(c, None)
                scores(c, kt_next)

        def key_step_group(jj, carry):
            for u in range(TILES_PER_TRIP):
                key_step(TILES_PER_TRIP * jj + 1 + u)
            return carry

        def unmasked_tiles():
            when(n_int >= 1, first_step)
            when(n_int == 0, no_step)
            n_rest = jnp.maximum(n_int - 1, 0) if causal else max(n_int - 1, 0)
            trips = n_rest // TILES_PER_TRIP
            lax.fori_loop(0, trips, key_step_group, 0)
            for u in range(TILES_PER_TRIP - 1):
                when(n_rest % TILES_PER_TRIP > u, lambda u=u: key_step(trips * TILES_PER_TRIP + 1 + u))

        def last_values_of_unmasked():
            vt_prev = vt_ref[0, jnp.maximum(n_int - 1, 0)]
            for c in range(n_cc):
                values(c, vt_prev)

        def masked_tile_and_emit():
            if causal:
                masked_softmax = softmax_diagonal
            else:
                mask = lax.broadcasted_iota(jnp.int32, (KEY_TILE, KEY_TILE), 0) < n_valid_last
                masked_softmax = lambda c: softmax(c, mask)
            vt_last = vt_ref[0, n_int]
            value_lag = 2
            for c in range(n_cc + value_lag):
                if c < n_cc:
                    masked_softmax(c)
                if c >= value_lag:
                    values(c - value_lag, vt_last)
            for c in range(n_cc):
                acc = acc_ref[c]
                o_t = acc[:MLA_KV_RANK] * (1.0 / acc[MLA_KV_RANK:MLA_KV_RANK + 1])
                for g in range(groups_per_cc):
                    tile = o_t[:, g * LANES:(g + 1) * LANES].T.astype(BF16)
                    col0 = c * KEY_TILE + g * LANES
                    if bq >= LANES:
                        hd, q0 = col0 // bq, col0 % bq
                        o_ref[0, h, q0:q0 + LANES, hd * MLA_KV_RANK:(hd + 1) * MLA_KV_RANK] = tile
                    else:
                        for hl in range(LANES // bq):
                            hd = col0 // bq + hl
                            o_ref[0, h, :, hd * MLA_KV_RANK:(hd + 1) * MLA_KV_RANK] = tile[hl * bq:(hl + 1) * bq]

        return start, unmasked_tiles, last_values_of_unmasked, masked_tile_and_emit

    blocks = [block(h) for h in range(n_sets)]
    blocks[0][0]()
    for h in range(n_sets):
        start, unmasked_tiles, last_values_of_unmasked, masked_tile_and_emit = blocks[h]
        unmasked_tiles()
        last_values_of_unmasked()
        if h + 1 < n_sets:
            blocks[h + 1][0]()
        masked_tile_and_emit()


def _attn(qt, kcat, vt, km, vm, *, bq, causal, n_valid_last, n_sets):
    bsz, nq, n_cc = qt.shape[:3]
    n_kt = kcat.shape[1]
    assert not causal or bq == KEY_TILE
    per_set = nq // n_sets
    width = MLA_HEADS * MLA_KV_RANK
    scratch_set = [pltpu.VMEM((n_cc, KEY_TILE, KEY_TILE), F32),
                   pltpu.VMEM((n_cc, KEY_TILE, KEY_TILE), BF16),
                   pltpu.VMEM((n_cc, 1, KEY_TILE), F32),
                   pltpu.VMEM((n_cc, 1, KEY_TILE), F32),
                   pltpu.VMEM((n_cc, V_ROWS, KEY_TILE), F32),
                   pltpu.VMEM((n_cc, 1, KEY_TILE), F32)]
    assert len(scratch_set) == SCRATCH_PER_SET
    out = pl.pallas_call(
        functools.partial(_attn_kernel, bq=bq, causal=causal, n_valid_last=n_valid_last),
        grid=(bsz, per_set),
        in_specs=[pl.BlockSpec((1, n_sets, 1, n_cc, MLA_CAT, KEY_TILE), lambda b, i: (b, 0, i, 0, 0, 0)),
                  pl.BlockSpec((1, n_kt, KEY_TILE, MLA_CAT), lambda b, i: (b, 0, 0, 0)),
                  pl.BlockSpec((1, n_kt, V_ROWS, KEY_TILE), lambda b, i: (b, 0, 0, 0)),
                  _const_spec(km.shape), _const_spec(vm.shape)],
        out_specs=pl.BlockSpec((1, n_sets, bq, width), lambda b, i: (b, 0, i, 0)),
        out_shape=jax.ShapeDtypeStruct((bsz, n_sets, per_set * bq, width), BF16),
        scratch_shapes=scratch_set * n_sets,
        compiler_params=_cparams("parallel", "arbitrary"),
        name="attn",
    )(qt.reshape(bsz, n_sets, per_set, *qt.shape[2:]), kcat, vt, km, vm)
    return out.reshape(bsz, nq * bq, width)


_MERGE_WEIGHTS = ("ln_g", "ln_b", "w_g", "gla_g", "w_br_gla", "w_uv_bd", "w_br_mla", "w_mg", "b_mg", "w_out",
                  "ln1_g", "ln1_b", "w_r_hi", "w_r_lo", "b_r")


def _merge_kernel(x_ref, og_ref, ol_ref, lng_ref, lnb_ref, wg_ref, glag_ref, wbg_ref, wuv_ref, wbm_ref,
                  wmg_ref, bmg_ref, wout_ref, l1g_ref, l1b_ref, wrh_ref, wrl_ref, br_ref, cin_ref,
                  h1t_ref, rt_ref, cnt_ref, carry_ref, *, sub):
    tt, d = x_ref.shape

    def branches(rows):
        h = _layer_norm(x_ref[rows, :], lng_ref[...], lnb_ref[...])
        hb = h.astype(BF16)
        g_out = _dot(hb, wg_ref[...])
        gate_pre = _dot(hb, wmg_ref[...])
        pair_in = 2 * MLA_KV_RANK
        y_heads = jnp.concatenate([_dot(ol_ref[rows, p * pair_in:(p + 1) * pair_in], wuv_ref[p])
                                   for p in range(MLA_HEADS // 2)], axis=-1)
        return h, g_out, gate_pre, y_heads

    def mix(rows, h, g_out, gate_pre, y_heads):
        og = og_ref[rows, :]
        parts = []
        for hd in range(GLA_HEADS):
            cols = slice(hd * GLA_DV, (hd + 1) * GLA_DV)
            o_h = og[:, cols]
            g_h = g_out[:, cols]
            o_n = o_h * lax.rsqrt(jnp.mean(o_h * o_h, axis=-1, keepdims=True) + RMS_EPS) * glag_ref[...]
            parts.append(o_n * (g_h * jax.nn.sigmoid(g_h)))
        y_a = _dot(jnp.concatenate(parts, axis=-1).astype(BF16), wbg_ref[...])
        y_b = _dot(y_heads.astype(BF16), wbm_ref[...])
        gates = jax.nn.sigmoid(gate_pre + bmg_ref[...])
        mix_in = gates[:, :d] * y_a + gates[:, d:] * y_b
        return DEEPNORM_ALPHA * h + _dot(mix_in.astype(BF16), wout_ref[...])

    def route(r0, pre):
        h1 = _layer_norm(pre, l1g_ref[...], l1b_ref[...])
        for s in range(ROW_TILE):
            h1t_ref[pl.ds(r0 * ROW_TILE + s, sub, stride=ROW_TILE), :] = h1[:, s * LANES:(s + 1) * LANES]
        h1_hi = h1.astype(BF16)
        h1_lo = (h1 - h1_hi.astype(F32)).astype(BF16)
        logits = (_dot(h1_hi, wrh_ref[...]) + (_dot(h1_hi, wrl_ref[...]) + _dot(h1_lo, wrh_ref[...]))) + br_ref[...]
        lane = lax.broadcasted_iota(jnp.int32, logits.shape, 1)
        is_grp = lane < N_GROUPS
        gl = jnp.where(is_grp, logits, NEG_BIG)
        g_max = jnp.max(gl, axis=-1, keepdims=True)
        g_sel = jnp.min(jnp.where(gl == g_max, lane, ROUTER_LANES), axis=-1, keepdims=True)
        p_grp = 1.0 / jnp.sum(jnp.where(is_grp, jnp.exp(gl - g_max), 0.0), axis=-1, keepdims=True)
        e_lo = N_GROUPS + g_sel * EXPERTS_PER_GROUP
        in_grp = (lane >= e_lo) & (lane < e_lo + EXPERTS_PER_GROUP)
        el = jnp.where(in_grp, logits, NEG_BIG)
        v1 = jnp.max(el, axis=-1, keepdims=True)
        i1 = jnp.min(jnp.where(el == v1, lane, ROUTER_LANES), axis=-1, keepdims=True)
        el2 = jnp.where(lane == i1, NEG_BIG, el)
        v2 = jnp.max(el2, axis=-1, keepdims=True)
        i2 = jnp.min(jnp.where(el2 == v2, lane, ROUTER_LANES), axis=-1, keepdims=True)
        e2 = jnp.exp(v2 - v1)
        w1 = p_grp / (1.0 + e2)
        w2 = p_grp * e2 / (1.0 + e2)
        e1, e2i = i1 - N_GROUPS, i2 - N_GROUPS
        hot = ((lane == e1) | (lane == e2i)).astype(F32)
        earlier = (lax.broadcasted_iota(jnp.int32, (sub, sub), 0) > lax.broadcasted_iota(jnp.int32, (sub, sub), 1))
        before = carry_ref[...] + _dot(earlier.astype(BF16), hot.astype(BF16))
        r1 = jnp.sum(jnp.where(lane == e1, before, 0.0), axis=-1, keepdims=True)
        r2 = jnp.sum(jnp.where(lane == e2i, before, 0.0), axis=-1, keepdims=True)
        carry_ref[...] += jnp.sum(hot, axis=0, keepdims=True)
        fields = (e1.astype(F32), e2i.astype(F32), w1, w2, r1, r2)
        rt = jnp.zeros(logits.shape, F32)
        for k, val in enumerate(fields):
            rt = jnp.where(lane == k, val, rt)
        rt_ref[pl.ds(r0, sub), :] = rt

    @pl.when(pl.program_id(0) == 0)
    def _():
        carry_ref[...] = cin_ref[...]

    starts = list(range(0, tt, sub))
    independent = [branches(pl.ds(r0, sub)) for r0 in starts]
    pending = None
    for r0, ind in zip(starts, independent):
        pre = mix(pl.ds(r0, sub), *ind)
        if pending is not None:
            route(*pending)
        pending = (r0, pre)
    route(*pending)
    cnt_ref[...] = carry_ref[...]


def _merge(x2, og2, ol2, wp, cnt_in, *, tt):
    t, d = x2.shape
    assert d == ROW_TILE * LANES
    row = lambda w: pl.BlockSpec((tt, w), lambda i: (i, 0))
    return pl.pallas_call(
        functools.partial(_merge_kernel, sub=min(TOKEN_TILE, tt)),
        grid=(t // tt,),
        in_specs=([row(d), row(GLA_V), row(MLA_HEADS * MLA_KV_RANK)] + [_const_spec(wp[n].shape) for n in _MERGE_WEIGHTS]
                  + [_const_spec(cnt_in.shape)]),
        out_specs=[pl.BlockSpec((tt * ROW_TILE, LANES), lambda i: (i, 0)), row(ROUTER_LANES),
                   pl.BlockSpec((1, ROUTER_LANES), lambda i: (0, 0))],
        out_shape=[jax.ShapeDtypeStruct((t * ROW_TILE, LANES), F32), jax.ShapeDtypeStruct((t, ROUTER_LANES), F32),
                   jax.ShapeDtypeStruct((1, ROUTER_LANES), F32)],
        scratch_shapes=[pltpu.VMEM((1, ROUTER_LANES), F32)],
        compiler_params=_cparams("arbitrary"),
        name="merge",
    )(x2, og2, ol2, *[wp[n] for n in _MERGE_WEIGHTS], cnt_in)


def _row_copy(src_ref, src_row, dst_ref, dst_row, sem):
    return pltpu.make_async_copy(src_ref.at[pl.ds(src_row * ROW_TILE, ROW_TILE)],
                                 dst_ref.at[pl.ds(dst_row * ROW_TILE, ROW_TILE)], sem)


def _group_spans(n_tokens, tile):
    spans, first = [], 0
    for n in n_tokens:
        spans.append((first, n // tile))
        first += n // tile
    return spans


def _group_spec(block, span):
    first, steps = span
    return pl.BlockSpec(block, lambda i, *_: (jnp.clip(i - first, 0, steps - 1), 0))


def _dispatch_kernel(pos_ref, zrow_ref, nu_ref, *refs, td, spans):
    h1t_refs, (xs_ref, zero_ref, sem, zsem) = refs[:len(spans)], refs[len(spans):]
    i = pl.program_id(0)

    @pl.when(i == 0)
    def _():
        zero_ref[...] = jnp.zeros(zero_ref.shape, F32)
        tile_rows = EXPERT_ROWS * ROW_TILE
        fill = lambda row: pltpu.make_async_copy(
            zero_ref, xs_ref.at[pl.ds(pl.multiple_of(row * ROW_TILE, ROW_TILE), tile_rows)], zsem)
        for e in range(N_EXPERTS):
            @pl.when(zrow_ref[e] >= 0)
            def _():
                fill(zrow_ref[e]).start()
        for e in range(N_EXPERTS):
            @pl.when(zrow_ref[e] >= 0)
            def _():
                fill(zrow_ref[e]).wait()

        def fill_tail(r, carry):
            fill(r * EXPERT_ROWS).start()
            fill(r * EXPERT_ROWS).wait()
            return carry

        lax.fori_loop(nu_ref[0], xs_ref.shape[0] // tile_rows, fill_tail, 0)

    base = i * (2 * td)
    for h1t_ref, (first, steps) in zip(h1t_refs, spans):
        @pl.when((i >= first) & (i < first + steps))
        def _():
            for t in range(td):
                for k in range(2):
                    dst = pl.multiple_of(pos_ref[base + 2 * t + k] * ROW_TILE, ROW_TILE)
                    pltpu.make_async_copy(h1t_ref.at[pl.ds(t * ROW_TILE, ROW_TILE)],
                                          xs_ref.at[pl.ds(dst, ROW_TILE)], sem).start(priority=k)
            for _ in range(2 * td):
                _row_copy(h1t_ref, 0, xs_ref, 0, sem).wait()


def _dispatch(h1ts, pos, zrow, n_used, *, n_rows, td):
    spans = _group_spans([h.shape[0] // ROW_TILE for h in h1ts], td)
    return pl.pallas_call(
        functools.partial(_dispatch_kernel, td=td, spans=spans),
        grid_spec=pltpu.PrefetchScalarGridSpec(
            num_scalar_prefetch=3,
            grid=(sum(steps for _, steps in spans),),
            in_specs=[_group_spec((td * ROW_TILE, LANES), span) for span in spans],
            out_specs=pl.BlockSpec(memory_space=pl.ANY),
            scratch_shapes=[pltpu.VMEM((EXPERT_ROWS * ROW_TILE, LANES), F32),
                            pltpu.SemaphoreType.DMA(()), pltpu.SemaphoreType.DMA(())]),
        out_shape=jax.ShapeDtypeStruct((n_rows * ROW_TILE, LANES), F32),
        compiler_params=_cparams("arbitrary"),
        name="dispatch",
    )(pos, zrow, n_used, *h1ts)


def _experts_kernel(te_ref, tb_ref, nu_ref, xs_ref, wg_ref, wu_ref, wd_ref, out_ref, wgb_ref, wub_ref, wdb_ref):
    del tb_ref
    r = pl.program_id(0)
    rows = xs_ref.shape[0] // ROW_TILE

    @pl.when((r == 0) | (te_ref[r] != te_ref[jnp.maximum(r - 1, 0)]))
    def _():
        wgb_ref[...] = wg_ref[0].astype(BF16)
        wub_ref[...] = wu_ref[0].astype(BF16)
        wdb_ref[...] = wd_ref[0].astype(BF16)

    @pl.when(r < nu_ref[0])
    def _():
        x = jnp.concatenate([xs_ref[pl.ds(s, rows, stride=ROW_TILE), :] for s in range(ROW_TILE)], axis=1).astype(BF16)
        gate = _dot(x, wgb_ref[...])
        up = _dot(x, wub_ref[...])
        hid = (gate * jax.nn.sigmoid(gate)) * up
        out = _dot(hid.astype(BF16), wdb_ref[...])
        for s in range(ROW_TILE):
            out_ref[pl.ds(s, rows, stride=ROW_TILE), :] = out[:, s * LANES:(s + 1) * LANES]

    @pl.when(pl.program_id(0) >= nu_ref[0])
    def _():
        out_ref[...] = jnp.zeros(out_ref.shape, F32)


def _experts(xs, tile_e, tile_blk, n_used, wp):
    n_rows = xs.shape[0] // ROW_TILE
    d = ROW_TILE * LANES
    blk = pl.BlockSpec((EXPERT_ROWS * ROW_TILE, LANES), lambda r, te, tb, nu: (tb[r], 0))
    return pl.pallas_call(
        _experts_kernel,
        grid_spec=pltpu.PrefetchScalarGridSpec(
            num_scalar_prefetch=3,
            grid=(n_rows // EXPERT_ROWS,),
            in_specs=[blk,
                      pl.BlockSpec((1, d, D_EXPERT), lambda r, te, tb, nu: (te[r], 0, 0)),
                      pl.BlockSpec((1, d, D_EXPERT), lambda r, te, tb, nu: (te[r], 0, 0)),
                      pl.BlockSpec((1, D_EXPERT, d), lambda r, te, tb, nu: (te[r], 0, 0))],
            out_specs=pl.BlockSpec((EXPERT_ROWS * ROW_TILE, LANES), lambda r, te, tb, nu: (r, 0)),
            scratch_shapes=[pltpu.VMEM((d, D_EXPERT), BF16), pltpu.VMEM((d, D_EXPERT), BF16),
                            pltpu.VMEM((D_EXPERT, d), BF16)]),
        out_shape=jax.ShapeDtypeStruct(xs.shape, F32),
        compiler_params=_cparams("arbitrary"),
        name="experts",
    )(tile_e, tile_blk, n_used, xs, wp["w_gate"], wp["w_up"], wp["w_down"])


def _combine_kernel(pos_ref, *refs, tc, spans):
    g = len(spans)
    h1t_refs, rt_refs = refs[:g], refs[g:2 * g]
    outs_ref, l2g_ref, l2b_ref = refs[2 * g:2 * g + 3]
    y_refs = refs[2 * g + 3:3 * g + 3]
    g_ref, sem = refs[3 * g + 3:]
    i = pl.program_id(0)
    n_steps = pl.num_programs(0)

    def gather(step):
        slot = step % 2
        base = step * (2 * tc)
        for t in range(tc):
            for k in range(2):
                src = pl.multiple_of(pos_ref[base + 2 * t + k] * ROW_TILE, ROW_TILE)
                pltpu.make_async_copy(outs_ref.at[pl.ds(src, ROW_TILE)],
                                      g_ref.at[slot, k, pl.ds(t * ROW_TILE, ROW_TILE)],
                                      sem.at[slot]).start(priority=k)

    @pl.when(i == 0)
    def _():
        gather(0)

    @pl.when(i + 1 < n_steps)
    def _():
        gather(i + 1)

    slot = i % 2
    for _ in range(2 * tc):
        pltpu.make_async_copy(outs_ref.at[pl.ds(0, ROW_TILE)], g_ref.at[slot, 0, pl.ds(0, ROW_TILE)], sem.at[slot]).wait()
    for h1t_ref, rt_ref, y_ref, (first, steps) in zip(h1t_refs, rt_refs, y_refs, spans):
        @pl.when((i >= first) & (i < first + steps))
        def _():
            rt = rt_ref[...]
            w0, w1 = rt[:, RT_W:RT_W + 1], rt[:, RT_W + 1:RT_W + 2]
            cols = []
            for s in range(ROW_TILE):
                rows = pl.ds(s, tc, stride=ROW_TILE)
                cols.append(DEEPNORM_ALPHA * h1t_ref[rows, :] + (w0 * g_ref[slot, 0, rows, :] + w1 * g_ref[slot, 1, rows, :]))
            y_ref[...] = _layer_norm(jnp.concatenate(cols, axis=1), l2g_ref[...], l2b_ref[...])


def _combine(h1ts, rts, outs, pos, wp, *, tc):
    d = ROW_TILE * LANES
    spans = _group_spans([r.shape[0] for r in rts], tc)
    const = lambda: pl.BlockSpec((1, d), lambda i, *_: (0, 0))
    return pl.pallas_call(
        functools.partial(_combine_kernel, tc=tc, spans=spans),
        grid_spec=pltpu.PrefetchScalarGridSpec(
            num_scalar_prefetch=1,
            grid=(sum(steps for _, steps in spans),),
            in_specs=([_group_spec((tc * ROW_TILE, LANES), span) for span in spans]
                      + [_group_spec((tc, ROUTER_LANES), span) for span in spans]
                      + [pl.BlockSpec(memory_space=pl.ANY), const(), const()]),
            out_specs=[_group_spec((tc, d), span) for span in spans],
            scratch_shapes=[pltpu.VMEM((2, 2, tc * ROW_TILE, LANES), F32), pltpu.SemaphoreType.DMA((2,))]),
        out_shape=[jax.ShapeDtypeStruct((r.shape[0], d), F32) for r in rts],
        compiler_params=_cparams("arbitrary"),
        name="combine",
    )(pos, *h1ts, *rts, outs, wp["ln2_g"], wp["ln2_b"])


def _route_plan(rt, cnt):
    t = rt.shape[0]
    n_tiles = (2 * t) // EXPERT_ROWS + N_EXPERTS
    counts = cnt[0, :N_EXPERTS].astype(jnp.int32)
    padded = (counts + (EXPERT_ROWS - 1)) // EXPERT_ROWS * EXPERT_ROWS
    ends = jnp.cumsum(padded)
    starts = ends - padded
    eid = rt[:, RT_E:RT_E + 2].astype(jnp.int32)
    rank = rt[:, RT_RANK:RT_RANK + 2].astype(jnp.int32)
    pos = (jnp.take(starts, eid) + rank).reshape(-1)
    n_used = ends[-1] // EXPERT_ROWS
    tile_blk = jnp.minimum(jnp.arange(n_tiles, dtype=jnp.int32), n_used - 1)
    tile_e = jnp.sum((tile_blk[:, None] * EXPERT_ROWS >= ends[None, :]).astype(jnp.int32), axis=1)
    zrow = jnp.where(padded > 0, ends - EXPERT_ROWS, -1)
    return pos, zrow.astype(jnp.int32), tile_e.astype(jnp.int32), tile_blk, n_used.reshape(1).astype(jnp.int32), n_tiles


def _rope_tables(pos):
    inv = ROPE_THETA ** (-jnp.arange(0, MLA_ROPE, 2, dtype=F32) / MLA_ROPE)
    ang = pos.astype(F32)[:, None] * inv[None, :]
    cs = jnp.concatenate([jnp.cos(ang), jnp.sin(ang)], axis=-1)
    return cs[None], cs.T[None]


def _prep_weights(ln_in_g, ln_in_b, w_in, w_gk2, b_gk, gla_norm_g, q_norm_g, kv_norm_g, w_uq, w_uk, w_uv,
                  w_br_gla, w_br_mla, w_mg, b_mg, w_out, ln1_g, ln1_b, w_rg, b_rg, w_re, b_re,
                  w_gate, w_up, w_down, ln2_g, ln2_b):
    d = w_in.shape[1]
    w = w_in[0]
    c0 = 0
    wk = w[:, c0:c0 + GLA_QK]; c0 += GLA_QK
    wv = w[:, c0:c0 + GLA_V]; c0 += GLA_V
    wgr = w[:, c0:c0 + GLA_RANK]; c0 += GLA_RANK
    wckv = w[:, c0:c0 + MLA_KV_RANK]; c0 += MLA_KV_RANK
    wkr = w[:, c0:c0 + MLA_ROPE]; c0 += MLA_ROPE
    wq = w[:, c0:c0 + GLA_QK]; c0 += GLA_QK
    wg = w[:, c0:c0 + GLA_V]; c0 += GLA_V
    wcq = w[:, c0:c0 + MLA_Q_RANK]
    w_tok = jnp.concatenate([wk, wv, wq, wckv, wkr, wgr, jnp.zeros((d, TOK_PAD - TOK_USED), F32)], axis=1)
    w_tr = jnp.concatenate([wckv, wcq], axis=1).T
    uq = w_uq[0].reshape(MLA_Q_RANK, MLA_HEADS, MLA_QK_DIM)
    uq_perm = jnp.concatenate([
        uq[:, :, :MLA_NOPE].reshape(MLA_Q_RANK, -1),
        uq[:, :, MLA_NOPE:MLA_NOPE + MLA_HALF].reshape(MLA_Q_RANK, -1),
        uq[:, :, MLA_NOPE + MLA_HALF:].reshape(MLA_Q_RANK, -1)], axis=1)
    uv = w_uv[0].transpose(1, 0, 2).reshape(MLA_HEADS // 2, 2, MLA_KV_RANK, MLA_DV)
    eye = jnp.eye(2, dtype=F32)
    w_uv_bd = (uv[:, :, :, None, :] * eye[None, :, None, :, None]).reshape(
        MLA_HEADS // 2, 2 * MLA_KV_RANK, 2 * MLA_DV)
    w_r = jnp.concatenate([w_rg[0], w_re[0].transpose(1, 0, 2).reshape(d, N_EXPERTS),
                           jnp.zeros((d, ROUTER_LANES - N_GROUPS - N_EXPERTS), F32)], axis=1)
    w_r_hi = w_r.astype(BF16)
    b_r = jnp.concatenate([b_rg[0], b_re[0].reshape(-1), jnp.zeros((ROUTER_LANES - N_GROUPS - N_EXPERTS,), F32)])
    row = lambda a: a.reshape(1, -1)
    return {
        "ln_g": row(ln_in_g), "ln_b": row(ln_in_b),
        "w_tok": w_tok.astype(BF16), "w_tr": w_tr.astype(BF16),
        "w_gk2": w_gk2[0].astype(BF16), "b_gk": row(b_gk[0]),
        "kv_g": row(kv_norm_g[0]), "kv_gt": kv_norm_g[0].reshape(-1, 1), "q_gt": q_norm_g[0].reshape(-1, 1),
        "w_uqt": uq_perm.T.astype(BF16), "w_uk": w_uk[0].transpose(1, 0, 2).astype(BF16),
        "w_g": wg.astype(BF16), "gla_g": row(gla_norm_g[0]),
        "w_br_gla": w_br_gla[0].astype(BF16), "w_uv_bd": w_uv_bd.astype(BF16), "w_br_mla": w_br_mla[0].astype(BF16),
        "w_mg": w_mg[0].astype(BF16), "b_mg": row(b_mg[0]), "w_out": w_out[0].astype(BF16),
        "ln1_g": row(ln1_g[0]), "ln1_b": row(ln1_b[0]),
        "w_r_hi": w_r_hi, "w_r_lo": (w_r - w_r_hi.astype(F32)).astype(BF16), "b_r": row(b_r),
        "w_gate": w_gate[0], "w_up": w_up[0], "w_down": w_down[0],
        "ln2_g": row(ln2_g[0]), "ln2_b": row(ln2_b[0]),
    }


def _value_rows(vt):
    lead, length = vt.shape[:-2], vt.shape[-1]
    return jnp.concatenate([vt, jnp.ones(lead + (1, length), vt.dtype),
                            jnp.zeros(lead + (V_ROWS - MLA_KV_RANK - 1, length), vt.dtype)], axis=-2)


def _key_tiles(kcat, vt):
    bsz, length, _ = kcat.shape
    n = -(-length // KEY_TILE)
    pad = n * KEY_TILE - length
    kcat = jnp.pad(kcat, ((0, 0), (0, pad), (0, 0)))
    vt = jnp.pad(_value_rows(vt), ((0, 0), (0, 0), (0, pad)))
    return (kcat.reshape(bsz, n, KEY_TILE, MLA_CAT),
            vt.reshape(bsz, V_ROWS, n, KEY_TILE).transpose(0, 2, 1, 3))


def _ffn(groups, wp):
    h1ts, rts = [], []
    cnt = jnp.zeros((1, ROUTER_LANES), F32)
    for x, og, ol in groups:
        t, d = x.shape[0] * x.shape[1], x.shape[2]
        h1t, rt, cnt = _merge(x.reshape(t, d), og.reshape(t, -1), ol.reshape(t, -1), wp, cnt, tt=min(MERGE_TILE, t))
        h1ts.append(h1t)
        rts.append(rt)
    n_fields = RT_RANK + 2
    pos, zrow, tile_e, tile_blk, n_used, n_tiles = _route_plan(
        jnp.concatenate([rt[:, :n_fields] for rt in rts], axis=0), cnt)
    xs = _dispatch(h1ts, pos, zrow, n_used, n_rows=n_tiles * EXPERT_ROWS, td=MOE_TOKEN_TILE)
    outs = _experts(xs, tile_e, tile_blk, n_used, wp)
    ys = _combine(h1ts, rts, outs, pos, wp, tc=MOE_TOKEN_TILE)
    return [y.reshape(x.shape) for y, (x, _, _) in zip(ys, groups)]


def kernel(x_prompt, x_sample, cache_mla_latent, cache_mla_krope, state_gla, meta_tokens, ln_in_g, ln_in_b, w_in, w_gk2, b_gk, gla_norm_g, q_norm_g, kv_norm_g, w_uq, w_uk, w_uv, w_br_gla, w_br_mla, w_mg, b_mg, w_out, ln1_g, ln1_b, w_rg, b_rg, w_re, b_re, w_gate, w_up, w_down, ln2_g, ln2_b):
    bp, sp, d = x_prompt.shape
    bs, ss, _ = x_sample.shape
    past = cache_mla_latent.shape[2]
    wp = _prep_weights(ln_in_g, ln_in_b, w_in, w_gk2, b_gk, gla_norm_g, q_norm_g, kv_norm_g, w_uq, w_uk, w_uv,
                       w_br_gla, w_br_mla, w_mg, b_mg, w_out, ln1_g, ln1_b, w_rg, b_rg, w_re, b_re,
                       w_gate, w_up, w_down, ln2_g, ln2_b)

    cs_m, cst_m = _rope_tables(jnp.arange(-N_META, 0, dtype=jnp.int32))
    m = _proj(meta_tokens[None], cs_m, cst_m, wp, cl=N_META, tt=N_META)
    _, _, m_kt, m_v, m_ebl, m_ckv, m_kr, m_kcat, m_vt, _ = m
    zero_state = jnp.zeros((1, GLA_HEADS, GLA_DK, GLA_DV), F32)
    _, m_state = _gla(m[0], m[1], m_kt, m_v, m_ebl, zero_state, cl=N_META, ts=N_META)

    cs_p, cst_p = _rope_tables(jnp.arange(sp, dtype=jnp.int32))
    p_qd, p_ki, p_kt, p_v, p_ebl, p_ckv, p_kr, p_kcat, p_vt, p_qt = _proj(
        x_prompt, cs_p, cst_p, wp, cl=CHUNK, tt=PROJ_TILE)
    p_o, p_state = _gla(p_qd, p_ki, p_kt, p_v, p_ebl, m_state, cl=CHUNK, ts=TOKEN_TILE)
    rep = lambda a, n: jnp.broadcast_to(a, (n,) + a.shape[1:])
    lat_p = jnp.concatenate([rep(m_ckv, bp), p_ckv], axis=1)
    kr_p = jnp.concatenate([rep(m_kr, bp), p_kr], axis=1)
    kcat_p, vt_p = _key_tiles(p_kcat, p_vt)
    km, vm = m_kcat[0], _value_rows(m_vt[0])
    p_ol = _attn(p_qt, kcat_p, vt_p, km, vm, bq=TOKEN_TILE, causal=True, n_valid_last=KEY_TILE, n_sets=2)

    ts_all = bs * ss
    cs_s, cst_s = _rope_tables(past + (jnp.arange(ts_all, dtype=jnp.int32) % ss))
    s_qd, s_ki, s_kt, s_v, s_ebl, s_ckv, s_kr, s_kcat, s_vt, s_qt = _proj(
        x_sample.reshape(1, ts_all, d), cs_s, cst_s, wp, cl=ss, tt=min(TOKEN_TILE, ts_all))
    per_stream = lambda a: a.reshape(bs, ss, a.shape[-1])
    s_o, s_state = _gla(per_stream(s_qd), per_stream(s_ki), per_stream(s_kt), per_stream(s_v),
                        s_ebl.reshape(bs, 1, 1, GLA_QK), state_gla[0].astype(F32), cl=ss, ts=ss)
    s_ckv, s_kr = per_stream(s_ckv), per_stream(s_kr)
    cache_kcat = jnp.concatenate([cache_mla_latent[0], cache_mla_krope[0]], axis=-1).astype(BF16)
    cache_vt = cache_mla_latent[0].astype(BF16).transpose(0, 2, 1)
    new_vt = s_vt.reshape(MLA_KV_RANK, bs, ss).transpose(1, 0, 2)
    kcat_s, vt_s = _key_tiles(jnp.concatenate([cache_kcat, per_stream(s_kcat)], axis=1),
                              jnp.concatenate([cache_vt, new_vt], axis=2))
    qt = s_qt.transpose(0, 2, 3, 1, 4).reshape(MLA_HEADS, MLA_CAT, bs, ss)
    qt = qt.transpose(2, 1, 0, 3).reshape(bs, MLA_CAT, MLA_HEADS * ss // KEY_TILE, KEY_TILE)
    qt = qt.transpose(0, 2, 1, 3)[:, None]
    s_ol = _attn(qt, kcat_s, vt_s, km, vm, bq=ss, causal=False, n_valid_last=(past + ss - 1) % KEY_TILE + 1, n_sets=1)
    y_prompt, y_sample = _ffn([(x_prompt, p_o, p_ol), (x_sample, s_o, s_ol)], wp)

    return (y_prompt, y_sample, lat_p[None], kr_p[None], p_state[None].astype(state_gla.dtype),
            s_ckv[None], s_kr[None], s_state[None].astype(state_gla.dtype))
```

```python
import functools

import jax
import jax.numpy as jnp
from jax import lax
from jax.experimental import pallas as pl
from jax.experimental.pallas import tpu as pltpu

F32 = jnp.float32
BF16 = jnp.bfloat16

CHUNK = 64
N_META = 16
GLA_HEADS = 4
GLA_DK = 128
GLA_DV = 256
GLA_RANK = 16
GLA_TAU = 16.0
GLA_QK = GLA_HEADS * GLA_DK
GLA_V = GLA_HEADS * GLA_DV
GLA_SCALE = GLA_DK ** -0.5
MLA_HEADS = 16
MLA_Q_RANK = 384
MLA_KV_RANK = 128
MLA_NOPE = 64
MLA_ROPE = 32
MLA_HALF = MLA_ROPE // 2
MLA_DV = 64
MLA_QK_DIM = MLA_NOPE + MLA_ROPE
MLA_CAT = MLA_KV_RANK + MLA_ROPE
MLA_SCALE = MLA_QK_DIM ** -0.5
LOG2_E = 1.4426950408889634
Q_SCALE = MLA_SCALE * LOG2_E
V_ROWS = MLA_KV_RANK + 16
ROPE_THETA = 10000.0
N_GROUPS = 4
EXPERTS_PER_GROUP = 8
N_EXPERTS = N_GROUPS * EXPERTS_PER_GROUP
D_EXPERT = 256
LN_EPS = 1e-5
RMS_EPS = 1e-6
DEEPNORM_ALPHA = 2.0 ** 0.25

LANES = 128
MXU_DIM = 256
VMEM_LIMIT_BYTES = 56 * 1024 * 1024

TOKEN_TILE = 256
MERGE_TILE = 2 * TOKEN_TILE
PROJ_TILE = 2 * TOKEN_TILE
KEY_TILE = MXU_DIM
MOE_TOKEN_TILE = 256
EXPERT_ROWS = 256
ROW_TILE = 8
ROUTER_LANES = LANES
RT_E, RT_W, RT_RANK = 0, 2, 4
NEG_BIG = -1e30

NT_DIMS = (((1,), (1,)), ((), ()))
TN_DIMS = (((0,), (0,)), ((), ()))


def _cparams(*sem):
    return pltpu.CompilerParams(dimension_semantics=sem, vmem_limit_bytes=VMEM_LIMIT_BYTES)


def _const_spec(shape):
    nd = len(shape)
    return pl.BlockSpec(shape, lambda *_: (0,) * nd, pipeline_mode=pl.Buffered(1))


def _layer_norm(x, g, b):
    mu = jnp.mean(x, axis=-1, keepdims=True)
    xc = x - mu
    var = jnp.mean(xc * xc, axis=-1, keepdims=True)
    return xc * lax.rsqrt(var + LN_EPS) * g + b


def _dot(a, b):
    return jnp.dot(a, b, preferred_element_type=F32)


TOK_COLS = (GLA_QK, GLA_V, GLA_QK, MLA_KV_RANK, MLA_ROPE, GLA_RANK)
TOK_USED = sum(TOK_COLS)
TOK_PAD = -(-TOK_USED // MXU_DIM) * MXU_DIM
TR_ROWS = MLA_KV_RANK + MLA_Q_RANK


def _proj_kernel(x_ref, cs_ref, cst_ref, lng_ref, lnb_ref, wtok_ref, wtr_ref, wgk2_ref, bgk_ref,
                 kvg_ref, kvgt_ref, qngt_ref, wuqt_ref, wuk_ref,
                 qd_ref, ki_ref, kt_ref, v_ref, ebl_ref, ckv_ref, kr_ref, kcat_ref, vt_ref, qt_ref, *, cl, sub):
    tt = x_ref.shape[1]
    n_chunks = sub // cl

    def project(r0):
        h = _layer_norm(x_ref[0, r0:r0 + sub, :], lng_ref[...], lnb_ref[...])
        hb = h.astype(BF16)
        z = _dot(hb, wtok_ref[...])
        zt = lax.dot_general(wtr_ref[...], hb, NT_DIMS, preferred_element_type=F32)
        return z, zt

    def finish(r0, z, zt):
        rows = slice(r0, r0 + sub)
        o0 = 0
        k = z[:, o0:o0 + GLA_QK]; o0 += GLA_QK
        v = z[:, o0:o0 + GLA_V]; o0 += GLA_V
        q = z[:, o0:o0 + GLA_QK]; o0 += GLA_QK
        ckv_raw = z[:, o0:o0 + MLA_KV_RANK]; o0 += MLA_KV_RANK
        kr = z[:, o0:o0 + MLA_ROPE]; o0 += MLA_ROPE
        gklr = z[:, o0:o0 + GLA_RANK]

        gz = _dot(gklr.astype(BF16), wgk2_ref[...]) + bgk_ref[...]
        gk = (jnp.minimum(gz, 0.0) - jnp.log(1.0 + jnp.exp(-jnp.abs(gz)))) * (1.0 / GLA_TAU)
        row_in_chunk = lax.broadcasted_iota(jnp.int32, gk.shape, 0) & (cl - 1)
        b = gk
        shift = 1
        while shift < cl:
            b = b + jnp.where(row_in_chunk >= shift, pltpu.roll(b, shift, 0), 0.0)
            shift *= 2
        b3 = b.reshape(n_chunks, cl, GLA_QK)
        bl = b3[:, cl - 1:cl, :]
        qd_ref[0, rows, :] = (q * GLA_SCALE * jnp.exp(b)).astype(BF16)
        ki_ref[0, rows, :] = (k * jnp.exp(-b)).astype(BF16)
        kt_ref[0, rows, :] = (k.reshape(n_chunks, cl, GLA_QK) * jnp.exp(bl - b3)).reshape(sub, GLA_QK).astype(BF16)
        v_ref[0, rows, :] = v.astype(BF16)
        ebl_ref[0, r0 // cl:r0 // cl + n_chunks] = jnp.exp(bl)

        ckv = ckv_raw * lax.rsqrt(jnp.mean(ckv_raw * ckv_raw, axis=-1, keepdims=True) + RMS_EPS) * kvg_ref[...]
        cs = cs_ref[0, rows, :]
        cos, sin = cs[:, :MLA_HALF], cs[:, MLA_HALF:]
        x1, x2 = kr[:, :MLA_HALF], kr[:, MLA_HALF:]
        kr_rot = jnp.concatenate([x1 * cos - x2 * sin, x2 * cos + x1 * sin], axis=-1)
        ckv_ref[0, rows, :] = ckv
        kr_ref[0, rows, :] = kr_rot
        kcat_ref[0, rows, :MLA_KV_RANK] = ckv.astype(BF16)
        kcat_ref[0, rows, MLA_KV_RANK:] = kr_rot.astype(BF16)

        ckvt = zt[:MLA_KV_RANK]
        ckvt = ckvt * lax.rsqrt(jnp.mean(ckvt * ckvt, axis=0, keepdims=True) + RMS_EPS) * kvgt_ref[...]
        blk = r0 // sub
        extra = lax.broadcasted_iota(jnp.int32, (V_ROWS - MLA_KV_RANK, sub), 0) == 0
        vt_ref[0, blk, :MLA_KV_RANK, :] = ckvt.astype(BF16)
        vt_ref[0, blk, MLA_KV_RANK:, :] = jnp.where(extra, 1.0, 0.0).astype(BF16)
        cqt = zt[MLA_KV_RANK:]
        cqt = cqt * lax.rsqrt(jnp.mean(cqt * cqt, axis=0, keepdims=True) + RMS_EPS) * qngt_ref[...]
        qmt = _dot(wuqt_ref[...], cqt.astype(BF16))
        n_nope = MLA_HEADS * MLA_NOPE
        n_half = MLA_HEADS * MLA_HALF
        cst = cst_ref[0, :, rows]
        cos_t = jnp.concatenate([cst[:MLA_HALF]] * MLA_HEADS, axis=0)
        sin_t = jnp.concatenate([cst[MLA_HALF:]] * MLA_HEADS, axis=0)
        r1 = qmt[n_nope:n_nope + n_half]
        r2 = qmt[n_nope + n_half:]
        rot1 = ((r1 * cos_t - r2 * sin_t) * Q_SCALE).astype(BF16)
        rot2 = ((r2 * cos_t + r1 * sin_t) * Q_SCALE).astype(BF16)
        for hd in range(MLA_HEADS):
            nope = qmt[hd * MLA_NOPE:(hd + 1) * MLA_NOPE].astype(BF16)
            qlat = _dot(wuk_ref[hd], nope) * Q_SCALE
            qt_ref[0, blk, hd, :MLA_KV_RANK, :] = qlat.astype(BF16)
            qt_ref[0, blk, hd, MLA_KV_RANK:MLA_KV_RANK + MLA_HALF, :] = rot1[hd * MLA_HALF:(hd + 1) * MLA_HALF]
            qt_ref[0, blk, hd, MLA_KV_RANK + MLA_HALF:, :] = rot2[hd * MLA_HALF:(hd + 1) * MLA_HALF]

    starts = list(range(0, tt, sub))
    projected = [project(r0) for r0 in starts]
    for r0, (z, zt) in zip(starts, projected):
        finish(r0, z, zt)


def _proj(x, cs, cst, wp, *, cl, tt):
    bsz, s, d = x.shape
    nt = s // tt
    n_chunks = tt // cl
    sub = min(TOKEN_TILE, tt)
    tok = lambda w: pl.BlockSpec((1, tt, w), lambda b, t: (b, t, 0))
    in_specs = [
        tok(d),
        pl.BlockSpec((1, tt, MLA_ROPE), lambda b, t: (0, t, 0)),
        pl.BlockSpec((1, MLA_ROPE, tt), lambda b, t: (0, 0, t)),
    ] + [_const_spec(wp[n].shape) for n in _PROJ_WEIGHTS]
    out_shape = [
        jax.ShapeDtypeStruct((bsz, s, GLA_QK), BF16),
        jax.ShapeDtypeStruct((bsz, s, GLA_QK), BF16),
        jax.ShapeDtypeStruct((bsz, s, GLA_QK), BF16),
        jax.ShapeDtypeStruct((bsz, s, GLA_V), BF16),
        jax.ShapeDtypeStruct((bsz, s // cl, 1, GLA_QK), F32),
        jax.ShapeDtypeStruct((bsz, s, MLA_KV_RANK), F32),
        jax.ShapeDtypeStruct((bsz, s, MLA_ROPE), F32),
        jax.ShapeDtypeStruct((bsz, s, MLA_CAT), BF16),
        jax.ShapeDtypeStruct((bsz, s // sub, V_ROWS, sub), BF16),
        jax.ShapeDtypeStruct((bsz, s // sub, MLA_HEADS, MLA_CAT, sub), BF16),
    ]
    out_specs = [
        tok(GLA_QK), tok(GLA_QK), tok(GLA_QK), tok(GLA_V),
        pl.BlockSpec((1, n_chunks, 1, GLA_QK), lambda b, t: (b, t, 0, 0)),
        tok(MLA_KV_RANK), tok(MLA_ROPE), tok(MLA_CAT),
        pl.BlockSpec((1, tt // sub, V_ROWS, sub), lambda b, t: (b, t, 0, 0)),
        pl.BlockSpec((1, tt // sub, MLA_HEADS, MLA_CAT, sub), lambda b, t: (b, t, 0, 0, 0)),
    ]
    return pl.pallas_call(
        functools.partial(_proj_kernel, cl=cl, sub=sub),
        grid=(bsz, nt),
        in_specs=in_specs,
        out_specs=out_specs,
        out_shape=out_shape,
        compiler_params=_cparams("parallel", "parallel"),
        name="proj",
    )(x, cs, cst, *[wp[n] for n in _PROJ_WEIGHTS])


_PROJ_WEIGHTS = ("ln_g", "ln_b", "w_tok", "w_tr", "w_gk2", "b_gk", "kv_g", "kv_gt", "q_gt", "w_uqt", "w_uk")


def _gla_kernel(qd_ref, ki_ref, kt_ref, v_ref, ebl_ref, s0_ref, o_ref, sfin_ref, st_ref, *, cl):
    t = pl.program_id(1)
    ts = qd_ref.shape[1]

    @pl.when(t == 0)
    def _():
        for hd in range(GLA_HEADS):
            st_ref[hd] = s0_ref[0, hd].T

    row = lax.broadcasted_iota(jnp.int32, (ts, ts), 0)
    col = lax.broadcasted_iota(jnp.int32, (ts, ts), 1)
    keep = (row >= col) & ((row & -cl) == (col & -cl))
    qk = lambda ref, hd: ref[0, :, hd * GLA_DK:(hd + 1) * GLA_DK]
    val = lambda hd: v_ref[0, :, hd * GLA_DV:(hd + 1) * GLA_DV]
    for hd in range(GLA_HEADS):
        a = lax.dot_general(qk(qd_ref, hd), qk(ki_ref, hd), NT_DIMS, preferred_element_type=F32)
        a = jnp.where(keep, a, 0.0).astype(BF16)
        o_ref[0, :, hd * GLA_DV:(hd + 1) * GLA_DV] = _dot(a, val(hd))
    for c in range(ts // cl):
        rows = slice(c * cl, (c + 1) * cl)
        for hd in range(GLA_HEADS):
            st = st_ref[hd]
            o_ref[0, rows, hd * GLA_DV:(hd + 1) * GLA_DV] += lax.dot_general(
                qk(qd_ref, hd)[rows], st.astype(BF16), NT_DIMS, preferred_element_type=F32)
            st_ref[hd] = (st * ebl_ref[0, c, :, hd * GLA_DK:(hd + 1) * GLA_DK]
                          + lax.dot_general(val(hd)[rows], qk(kt_ref, hd)[rows], TN_DIMS, preferred_element_type=F32))

    @pl.when(t == pl.num_programs(1) - 1)
    def _():
        for hd in range(GLA_HEADS):
            sfin_ref[0, hd] = st_ref[hd].T


def _gla(qd, ki, kt, v, ebl, s0, *, cl, ts):
    bsz, s, _ = qd.shape
    n_chunks = ts // cl
    s0_b = s0.shape[0]
    qk_spec = pl.BlockSpec((1, ts, GLA_QK), lambda b, t: (b, t, 0))
    v_spec = pl.BlockSpec((1, ts, GLA_V), lambda b, t: (b, t, 0))
    st_spec = pl.BlockSpec((1, GLA_HEADS, GLA_DK, GLA_DV), lambda b, t: (b, 0, 0, 0))
    s0_spec = st_spec if s0_b == bsz else pl.BlockSpec((1, GLA_HEADS, GLA_DK, GLA_DV), lambda b, t: (0, 0, 0, 0))
    return pl.pallas_call(
        functools.partial(_gla_kernel, cl=cl),
        grid=(bsz, s // ts),
        in_specs=[qk_spec, qk_spec, qk_spec, v_spec,
                  pl.BlockSpec((1, n_chunks, 1, GLA_QK), lambda b, t: (b, t, 0, 0)),
                  s0_spec],
        out_specs=[v_spec, st_spec],
        out_shape=[jax.ShapeDtypeStruct((bsz, s, GLA_V), F32),
                   jax.ShapeDtypeStruct((bsz, GLA_HEADS, GLA_DK, GLA_DV), F32)],
        scratch_shapes=[pltpu.VMEM((GLA_HEADS, GLA_DV, GLA_DK), F32)],
        compiler_params=_cparams("parallel", "arbitrary"),
        name="gla",
    )(qd, ki, kt, v, ebl, s0)


SCRATCH_PER_SET = 6
TILES_PER_TRIP = 4


def _attn_kernel(qt_ref, kcat_ref, vt_ref, km_ref, vm_ref, o_ref, *scratch, bq, causal, n_valid_last):
    n_sets = qt_ref.shape[1]
    n_cc = qt_ref.shape[3]
    n_kt = kcat_ref.shape[1]
    groups_per_cc = KEY_TILE // LANES

    def when(cond, fn):
        if isinstance(cond, bool):
            if cond:
                fn()
        else:
            pl.when(cond)(fn)

    def block(h):
        s_ref, p_ref, m_ref, a_ref, acc_ref, mx_ref = scratch[SCRATCH_PER_SET * h:SCRATCH_PER_SET * (h + 1)]
        blk = pl.program_id(1) + h * pl.num_programs(1)
        n_int = blk if causal else n_kt - 1
        q = lambda c: qt_ref[0, h, 0, c]
        col_max = lambda x: jnp.max(x, axis=0, keepdims=True)
        col_sum = lambda x: jnp.sum(x, axis=0, keepdims=True)

        def scores(c, kt):
            s = _dot(kt, q(c))
            s_ref[c] = s
            mx_ref[c] = col_max(s)

        def softmax(c, mask):
            s = s_ref[c]
            if mask is None:
                s_max = mx_ref[c]
            else:
                s = jnp.where(mask, s, NEG_BIG)
                s_max = col_max(s)
            m_prev = m_ref[c]
            m_new = jnp.maximum(m_prev, s_max)
            a_ref[c] = jnp.exp2(m_prev - m_new)
            m_ref[c] = m_new
            p_ref[c] = jnp.exp2(s - m_new).astype(BF16)

        def softmax_diagonal(c):
            half = KEY_TILE // 2
            chunks_per_half = half // CHUNK
            second = lax.broadcasted_iota(jnp.int32, (CHUNK, half), 1) >= CHUNK
            m_prev = m_ref[c]
            m_out, a_out = [], []
            for lh in range(KEY_TILE // half):
                lanes = slice(lh * half, (lh + 1) * half)
                n_full = chunks_per_half * lh + 1
                rows_full = slice(0, n_full * CHUNK)
                rows_part = slice(n_full * CHUNK, (n_full + 1) * CHUNK)
                full = s_ref[c, rows_full, lanes]
                part = jnp.where(second, s_ref[c, rows_part, lanes], NEG_BIG)
                m_new = jnp.maximum(m_prev[:, lanes], jnp.maximum(col_max(full), col_max(part)))
                p_ref[c, rows_full, lanes] = jnp.exp2(full - m_new).astype(BF16)
                p_ref[c, rows_part, lanes] = jnp.exp2(part - m_new).astype(BF16)
                if (n_full + 1) * CHUNK < KEY_TILE:
                    hidden = KEY_TILE - (n_full + 1) * CHUNK
                    p_ref[c, (n_full + 1) * CHUNK:, lanes] = jnp.zeros((hidden, half), BF16)
                m_out.append(m_new)
                a_out.append(jnp.exp2(m_prev[:, lanes] - m_new))
            m_ref[c] = jnp.concatenate(m_out, axis=1)
            a_ref[c] = jnp.concatenate(a_out, axis=1)

        def values(c, vt):
            acc_ref[c] = a_ref[c] * acc_ref[c] + _dot(vt, p_ref[c])

        def start():
            km, vm = km_ref[...], vm_ref[...]
            sm = [_dot(km, q(c)) for c in range(n_cc)]
            pm = []
            for c in range(n_cc):
                m0 = col_max(sm[c])
                m_ref[c] = m0
                pm.append(jnp.exp2(sm[c] - m0).astype(BF16))
            for c in range(n_cc):
                acc_ref[c] = _dot(vm, pm[c])
            kt = kcat_ref[0, 0]
            for c in range(n_cc):
                scores(c, kt)

        def first_step():
            kt_next = kcat_ref[0, 1]
            for c in range(n_cc):
                softmax(c, None)
                scores(c, kt_next)

        def no_step():
            a_ref[...] = jnp.ones(a_ref.shape, F32)
            p_ref[...] = jnp.zeros(p_ref.shape, BF16)

        def key_step(j):
            kt_next = kcat_ref[0, j + 1]
            vt_prev = vt_ref[0, j - 1]
            for c in range(n_cc):
                values(c, vt_prev)
                softmax(c, None)
                scores(c, kt_next)

        def key_step_group(jj, carry):
            for u in range(TILES_PER_TRIP):
                key_step(TILES_PER_TRIP * jj + 1 + u)
            return carry

        def unmasked_tiles():
            when(n_int >= 1, first_step)
            when(n_int == 0, no_step)
            n_rest = jnp.maximum(n_int - 1, 0) if causal else max(n_int - 1, 0)
            trips = n_rest // TILES_PER_TRIP
            lax.fori_loop(0, trips, key_step_group, 0)
            for u in range(TILES_PER_TRIP - 1):
                when(n_rest % TILES_PER_TRIP > u, lambda u=u: key_step(trips * TILES_PER_TRIP + 1 + u))

        def last_values_of_unmasked():
            vt_prev = vt_ref[0, jnp.maximum(n_int - 1, 0)]
            for c in range(n_cc):
                values(c, vt_prev)

        def masked_tile_and_emit():
            if causal:
                masked_softmax = softmax_diagonal
            else:
                mask = lax.broadcasted_iota(jnp.int32, (KEY_TILE, KEY_TILE), 0) < n_valid_last
                masked_softmax = lambda c: softmax(c, mask)
            vt_last = vt_ref[0, n_int]
            value_lag = 2
            for c in range(n_cc + value_lag):
                if c < n_cc:
                    masked_softmax(c)
                if c >= value_lag:
                    values(c - value_lag, vt_last)
            for c in range(n_cc):
                acc = acc_ref[c]
                o_t = acc[:MLA_KV_RANK] * (1.0 / acc[MLA_KV_RANK:MLA_KV_RANK + 1])
                for g in range(groups_per_cc):
                    tile = o_t[:, g * LANES:(g + 1) * LANES].T.astype(BF16)
                    col0 = c * KEY_TILE + g * LANES
                    if bq >= LANES:
                        hd, q0 = col0 // bq, col0 % bq
                        o_ref[0, h, q0:q0 + LANES, hd * MLA_KV_RANK:(hd + 1) * MLA_KV_RANK] = tile
                    else:
                        for hl in range(LANES // bq):
                            hd = col0 // bq + hl
                            o_ref[0, h, :, hd * MLA_KV_RANK:(hd + 1) * MLA_KV_RANK] = tile[hl * bq:(hl + 1) * bq]

        return start, unmasked_tiles, last_values_of_unmasked, masked_tile_and_emit

    blocks = [block(h) for h in range(n_sets)]
    blocks[0][0]()
    for h in range(n_sets):
        start, unmasked_tiles, last_values_of_unmasked, masked_tile_and_emit = blocks[h]
        unmasked_tiles()
        last_values_of_unmasked()
        if h + 1 < n_sets:
            blocks[h + 1][0]()
        masked_tile_and_emit()


def _attn(qt, kcat, vt, km, vm, *, bq, causal, n_valid_last, n_sets):
    bsz, nq, n_cc = qt.shape[:3]
    n_kt = kcat.shape[1]
    assert not causal or bq == KEY_TILE
    per_set = nq // n_sets
    width = MLA_HEADS * MLA_KV_RANK
    scratch_set = [pltpu.VMEM((n_cc, KEY_TILE, KEY_TILE), F32),
                   pltpu.VMEM((n_cc, KEY_TILE, KEY_TILE), BF16),
                   pltpu.VMEM((n_cc, 1, KEY_TILE), F32),
                   pltpu.VMEM((n_cc, 1, KEY_TILE), F32),
                   pltpu.VMEM((n_cc, V_ROWS, KEY_TILE), F32),
                   pltpu.VMEM((n_cc, 1, KEY_TILE), F32)]
    assert len(scratch_set) == SCRATCH_PER_SET
    out = pl.pallas_call(
        functools.partial(_attn_kernel, bq=bq, causal=causal, n_valid_last=n_valid_last),
        grid=(bsz, per_set),
        in_specs=[pl.BlockSpec((1, n_sets, 1, n_cc, MLA_CAT, KEY_TILE), lambda b, i: (b, 0, i, 0, 0, 0)),
                  pl.BlockSpec((1, n_kt, KEY_TILE, MLA_CAT), lambda b, i: (b, 0, 0, 0)),
                  pl.BlockSpec((1, n_kt, V_ROWS, KEY_TILE), lambda b, i: (b, 0, 0, 0)),
                  _const_spec(km.shape), _const_spec(vm.shape)],
        out_specs=pl.BlockSpec((1, n_sets, bq, width), lambda b, i: (b, 0, i, 0)),
        out_shape=jax.ShapeDtypeStruct((bsz, n_sets, per_set * bq, width), BF16),
        scratch_shapes=scratch_set * n_sets,
        compiler_params=_cparams("parallel", "arbitrary"),
        name="attn",
    )(qt.reshape(bsz, n_sets, per_set, *qt.shape[2:]), kcat, vt, km, vm)
    return out.reshape(bsz, nq * bq, width)


_MERGE_WEIGHTS = ("ln_g", "ln_b", "w_g", "gla_g", "w_br_gla", "w_uv_bd", "w_br_mla", "w_mg", "b_mg", "w_out",
                  "ln1_g", "ln1_b", "w_r_hi", "w_r_lo", "b_r")


def _merge_kernel(x_ref, og_ref, ol_ref, lng_ref, lnb_ref, wg_ref, glag_ref, wbg_ref, wuv_ref, wbm_ref,
                  wmg_ref, bmg_ref, wout_ref, l1g_ref, l1b_ref, wrh_ref, wrl_ref, br_ref, cin_ref,
                  h1t_ref, rt_ref, cnt_ref, carry_ref, *, sub):
    tt, d = x_ref.shape

    def branches(rows):
        h = _layer_norm(x_ref[rows, :], lng_ref[...], lnb_ref[...])
        hb = h.astype(BF16)
        g_out = _dot(hb, wg_ref[...])
        gate_pre = _dot(hb, wmg_ref[...])
        pair_in = 2 * MLA_KV_RANK
        y_heads = jnp.concatenate([_dot(ol_ref[rows, p * pair_in:(p + 1) * pair_in], wuv_ref[p])
                                   for p in range(MLA_HEADS // 2)], axis=-1)
        return h, g_out, gate_pre, y_heads

    def mix(rows, h, g_out, gate_pre, y_heads):
        og = og_ref[rows, :]
        parts = []
        for hd in range(GLA_HEADS):
            cols = slice(hd * GLA_DV, (hd + 1) * GLA_DV)
            o_h = og[:, cols]
            g_h = g_out[:, cols]
            o_n = o_h * lax.rsqrt(jnp.mean(o_h * o_h, axis=-1, keepdims=True) + RMS_EPS) * glag_ref[...]
            parts.append(o_n * (g_h * jax.nn.sigmoid(g_h)))
        y_a = _dot(jnp.concatenate(parts, axis=-1).astype(BF16), wbg_ref[...])
        y_b = _dot(y_heads.astype(BF16), wbm_ref[...])
        gates = jax.nn.sigmoid(gate_pre + bmg_ref[...])
        mix_in = gates[:, :d] * y_a + gates[:, d:] * y_b
        return DEEPNORM_ALPHA * h + _dot(mix_in.astype(BF16), wout_ref[...])

    def route(r0, pre):
        h1 = _layer_norm(pre, l1g_ref[...], l1b_ref[...])
        for s in range(ROW_TILE):
            h1t_ref[pl.ds(r0 * ROW_TILE + s, sub, stride=ROW_TILE), :] = h1[:, s * LANES:(s + 1) * LANES]
        h1_hi = h1.astype(BF16)
        h1_lo = (h1 - h1_hi.astype(F32)).astype(BF16)
        logits = (_dot(h1_hi, wrh_ref[...]) + (_dot(h1_hi, wrl_ref[...]) + _dot(h1_lo, wrh_ref[...]))) + br_ref[...]
        lane = lax.broadcasted_iota(jnp.int32, logits.shape, 1)
        is_grp = lane < N_GROUPS
        gl = jnp.where(is_grp, logits, NEG_BIG)
        g_max = jnp.max(gl, axis=-1, keepdims=True)
        g_sel = jnp.min(jnp.where(gl == g_max, lane, ROUTER_LANES), axis=-1, keepdims=True)
        p_grp = 1.0 / jnp.sum(jnp.where(is_grp, jnp.exp(gl - g_max), 0.0), axis=-1, keepdims=True)
        e_lo = N_GROUPS + g_sel * EXPERTS_PER_GROUP
        in_grp = (lane >= e_lo) & (lane < e_lo + EXPERTS_PER_GROUP)
        el = jnp.where(in_grp, logits, NEG_BIG)
        v1 = jnp.max(el, axis=-1, keepdims=True)
        i1 = jnp.min(jnp.where(el == v1, lane, ROUTER_LANES), axis=-1, keepdims=True)
        el2 = jnp.where(lane == i1, NEG_BIG, el)
        v2 = jnp.max(el2, axis=-1, keepdims=True)
        i2 = jnp.min(jnp.where(el2 == v2, lane, ROUTER_LANES), axis=-1, keepdims=True)
        e2 = jnp.exp(v2 - v1)
        w1 = p_grp / (1.0 + e2)
        w2 = p_grp * e2 / (1.0 + e2)
        e1, e2i = i1 - N_GROUPS, i2 - N_GROUPS
        hot = ((lane == e1) | (lane == e2i)).astype(F32)
        earlier = (lax.broadcasted_iota(jnp.int32, (sub, sub), 0) > lax.broadcasted_iota(jnp.int32, (sub, sub), 1))
        before = carry_ref[...] + _dot(earlier.astype(BF16), hot.astype(BF16))
        r1 = jnp.sum(jnp.where(lane == e1, before, 0.0), axis=-1, keepdims=True)
        r2 = jnp.sum(jnp.where(lane == e2i, before, 0.0), axis=-1, keepdims=True)
        carry_ref[...] += jnp.sum(hot, axis=0, keepdims=True)
        fields = (e1.astype(F32), e2i.astype(F32), w1, w2, r1, r2)
        rt = jnp.zeros(logits.shape, F32)
        for k, val in enumerate(fields):
            rt = jnp.where(lane == k, val, rt)
        rt_ref[pl.ds(r0, sub), :] = rt

    @pl.when(pl.program_id(0) == 0)
    def _():
        carry_ref[...] = cin_ref[...]

    starts = list(range(0, tt, sub))
    independent = [branches(pl.ds(r0, sub)) for r0 in starts]
    pending = None
    for r0, ind in zip(starts, independent):
        pre = mix(pl.ds(r0, sub), *ind)
        if pending is not None:
            route(*pending)
        pending = (r0, pre)
    route(*pending)
    cnt_ref[...] = carry_ref[...]


def _merge(x2, og2, ol2, wp, cnt_in, *, tt):
    t, d = x2.shape
    assert d == ROW_TILE * LANES
    row = lambda w: pl.BlockSpec((tt, w), lambda i: (i, 0))
    return pl.pallas_call(
        functools.partial(_merge_kernel, sub=min(TOKEN_TILE, tt)),
        grid=(t // tt,),
        in_specs=([row(d), row(GLA_V), row(MLA_HEADS * MLA_KV_RANK)] + [_const_spec(wp[n].shape) for n in _MERGE_WEIGHTS]
                  + [_const_spec(cnt_in.shape)]),
        out_specs=[pl.BlockSpec((tt * ROW_TILE, LANES), lambda i: (i, 0)), row(ROUTER_LANES),
                   pl.BlockSpec((1, ROUTER_LANES), lambda i: (0, 0))],
        out_shape=[jax.ShapeDtypeStruct((t * ROW_TILE, LANES), F32), jax.ShapeDtypeStruct((t, ROUTER_LANES), F32),
                   jax.ShapeDtypeStruct((1, ROUTER_LANES), F32)],
        scratch_shapes=[pltpu.VMEM((1, ROUTER_LANES), F32)],
        compiler_params=_cparams("arbitrary"),
        name="merge",
    )(x2, og2, ol2, *[wp[n] for n in _MERGE_WEIGHTS], cnt_in)


def _row_copy(src_ref, src_row, dst_ref, dst_row, sem):
    return pltpu.make_async_copy(src_ref.at[pl.ds(src_row * ROW_TILE, ROW_TILE)],
                                 dst_ref.at[pl.ds(dst_row * ROW_TILE, ROW_TILE)], sem)


def _group_spans(n_tokens, tile):
    spans, first = [], 0
    for n in n_tokens:
        spans.append((first, n // tile))
        first += n // tile
    return spans


def _group_spec(block, span):
    first, steps = span
    return pl.BlockSpec(block, lambda i, *_: (jnp.clip(i - first, 0, steps - 1), 0))


def _dispatch_kernel(pos_ref, zrow_ref, nu_ref, *refs, td, spans):
    h1t_refs, (xs_ref, zero_ref, sem, zsem) = refs[:len(spans)], refs[len(spans):]
    i = pl.program_id(0)

    @pl.when(i == 0)
    def _():
        zero_ref[...] = jnp.zeros(zero_ref.shape, F32)
        tile_rows = EXPERT_ROWS * ROW_TILE
        fill = lambda row: pltpu.make_async_copy(
            zero_ref, xs_ref.at[pl.ds(pl.multiple_of(row * ROW_TILE, ROW_TILE), tile_rows)], zsem)
        for e in range(N_EXPERTS):
            @pl.when(zrow_ref[e] >= 0)
            def _():
                fill(zrow_ref[e]).start()
        for e in range(N_EXPERTS):
            @pl.when(zrow_ref[e] >= 0)
            def _():
                fill(zrow_ref[e]).wait()

        def fill_tail(r, carry):
            fill(r * EXPERT_ROWS).start()
            fill(r * EXPERT_ROWS).wait()
            return carry

        lax.fori_loop(nu_ref[0], xs_ref.shape[0] // tile_rows, fill_tail, 0)

    base = i * (2 * td)
    for h1t_ref, (first, steps) in zip(h1t_refs, spans):
        @pl.when((i >= first) & (i < first + steps))
        def _():
            for t in range(td):
                for k in range(2):
                    dst = pl.multiple_of(pos_ref[base + 2 * t + k] * ROW_TILE, ROW_TILE)
                    pltpu.make_async_copy(h1t_ref.at[pl.ds(t * ROW_TILE, ROW_TILE)],
                                          xs_ref.at[pl.ds(dst, ROW_TILE)], sem).start(priority=k)
            for _ in range(2 * td):
                _row_copy(h1t_ref, 0, xs_ref, 0, sem).wait()


def _dispatch(h1ts, pos, zrow, n_used, *, n_rows, td):
    spans = _group_spans([h.shape[0] // ROW_TILE for h in h1ts], td)
    return pl.pallas_call(
        functools.partial(_dispatch_kernel, td=td, spans=spans),
        grid_spec=pltpu.PrefetchScalarGridSpec(
            num_scalar_prefetch=3,
            grid=(sum(steps for _, steps in spans),),
            in_specs=[_group_spec((td * ROW_TILE, LANES), span) for span in spans],
            out_specs=pl.BlockSpec(memory_space=pl.ANY),
            scratch_shapes=[pltpu.VMEM((EXPERT_ROWS * ROW_TILE, LANES), F32),
                            pltpu.SemaphoreType.DMA(()), pltpu.SemaphoreType.DMA(())]),
        out_shape=jax.ShapeDtypeStruct((n_rows * ROW_TILE, LANES), F32),
        compiler_params=_cparams("arbitrary"),
        name="dispatch",
    )(pos, zrow, n_used, *h1ts)


def _experts_kernel(te_ref, tb_ref, nu_ref, xs_ref, wg_ref, wu_ref, wd_ref, out_ref, wgb_ref, wub_ref, wdb_ref):
    del tb_ref
    r = pl.program_id(0)
    rows = xs_ref.shape[0] // ROW_TILE

    @pl.when((r == 0) | (te_ref[r] != te_ref[jnp.maximum(r - 1, 0)]))
    def _():
        wgb_ref[...] = wg_ref[0].astype(BF16)
        wub_ref[...] = wu_ref[0].astype(BF16)
        wdb_ref[...] = wd_ref[0].astype(BF16)

    @pl.when(r < nu_ref[0])
    def _():
        x = jnp.concatenate([xs_ref[pl.ds(s, rows, stride=ROW_TILE), :] for s in range(ROW_TILE)], axis=1).astype(BF16)
        gate = _dot(x, wgb_ref[...])
        up = _dot(x, wub_ref[...])
        hid = (gate * jax.nn.sigmoid(gate)) * up
        out = _dot(hid.astype(BF16), wdb_ref[...])
        for s in range(ROW_TILE):
            out_ref[pl.ds(s, rows, stride=ROW_TILE), :] = out[:, s * LANES:(s + 1) * LANES]

    @pl.when(pl.program_id(0) >= nu_ref[0])
    def _():
        out_ref[...] = jnp.zeros(out_ref.shape, F32)


def _experts(xs, tile_e, tile_blk, n_used, wp):
    n_rows = xs.shape[0] // ROW_TILE
    d = ROW_TILE * LANES
    blk = pl.BlockSpec((EXPERT_ROWS * ROW_TILE, LANES), lambda r, te, tb, nu: (tb[r], 0))
    return pl.pallas_call(
        _experts_kernel,
        grid_spec=pltpu.PrefetchScalarGridSpec(
            num_scalar_prefetch=3,
            grid=(n_rows // EXPERT_ROWS,),
            in_specs=[blk,
                      pl.BlockSpec((1, d, D_EXPERT), lambda r, te, tb, nu: (te[r], 0, 0)),
                      pl.BlockSpec((1, d, D_EXPERT), lambda r, te, tb, nu: (te[r], 0, 0)),
                      pl.BlockSpec((1, D_EXPERT, d), lambda r, te, tb, nu: (te[r], 0, 0))],
            out_specs=pl.BlockSpec((EXPERT_ROWS * ROW_TILE, LANES), lambda r, te, tb, nu: (r, 0)),
            scratch_shapes=[pltpu.VMEM((d, D_EXPERT), BF16), pltpu.VMEM((d, D_EXPERT), BF16),
                            pltpu.VMEM((D_EXPERT, d), BF16)]),
        out_shape=jax.ShapeDtypeStruct(xs.shape, F32),
        compiler_params=_cparams("arbitrary"),
        name="experts",
    )(tile_e, tile_blk, n_used, xs, wp["w_gate"], wp["w_up"], wp["w_down"])


def _combine_kernel(pos_ref, *refs, tc, spans):
    g = len(spans)
    h1t_refs, rt_refs = refs[:g], refs[g:2 * g]
    outs_ref, l2g_ref, l2b_ref = refs[2 * g:2 * g + 3]
    y_refs = refs[2 * g + 3:3 * g + 3]
    g_ref, sem = refs[3 * g + 3:]
    i = pl.program_id(0)
    n_steps = pl.num_programs(0)

    def gather(step):
        slot = step % 2
        base = step * (2 * tc)
        for t in range(tc):
            for k in range(2):
                src = pl.multiple_of(pos_ref[base + 2 * t + k] * ROW_TILE, ROW_TILE)
                pltpu.make_async_copy(outs_ref.at[pl.ds(src, ROW_TILE)],
                                      g_ref.at[slot, k, pl.ds(t * ROW_TILE, ROW_TILE)],
                                      sem.at[slot]).start(priority=k)

    @pl.when(i == 0)
    def _():
        gather(0)

    @pl.when(i + 1 < n_steps)
    def _():
        gather(i + 1)

    slot = i % 2
    for _ in range(2 * tc):
        pltpu.make_async_copy(outs_ref.at[pl.ds(0, ROW_TILE)], g_ref.at[slot, 0, pl.ds(0, ROW_TILE)], sem.at[slot]).wait()
    for h1t_ref, rt_ref, y_ref, (first, steps) in zip(h1t_refs, rt_refs, y_refs, spans):
        @pl.when((i >= first) & (i < first + steps))
        def _():
            rt = rt_ref[...]
            w0, w1 = rt[:, RT_W:RT_W + 1], rt[:, RT_W + 1:RT_W + 2]
            cols = []
            for s in range(ROW_TILE):
                rows = pl.ds(s, tc, stride=ROW_TILE)
                cols.append(DEEPNORM_ALPHA * h1t_ref[rows, :] + (w0 * g_ref[slot, 0, rows, :] + w1 * g_ref[slot, 1, rows, :]))
            y_ref[...] = _layer_norm(jnp.concatenate(cols, axis=1), l2g_ref[...], l2b_ref[...])


def _combine(h1ts, rts, outs, pos, wp, *, tc):
    d = ROW_TILE * LANES
    spans = _group_spans([r.shape[0] for r in rts], tc)
    const = lambda: pl.BlockSpec((1, d), lambda i, *_: (0, 0))
    return pl.pallas_call(
        functools.partial(_combine_kernel, tc=tc, spans=spans),
        grid_spec=pltpu.PrefetchScalarGridSpec(
            num_scalar_prefetch=1,
            grid=(sum(steps for _, steps in spans),),
            in_specs=([_group_spec((tc * ROW_TILE, LANES), span) for span in spans]
                      + [_group_spec((tc, ROUTER_LANES), span) for span in spans]
                      + [pl.BlockSpec(memory_space=pl.ANY), const(), const()]),
            out_specs=[_group_spec((tc, d), span) for span in spans],
            scratch_shapes=[pltpu.VMEM((2, 2, tc * ROW_TILE, LANES), F32), pltpu.SemaphoreType.DMA((2,))]),
        out_shape=[jax.ShapeDtypeStruct((r.shape[0], d), F32) for r in rts],
        compiler_params=_cparams("arbitrary"),
        name="combine",
    )(pos, *h1ts, *rts, outs, wp["ln2_g"], wp["ln2_b"])


def _route_plan(rt, cnt):
    t = rt.shape[0]
    n_tiles = (2 * t) // EXPERT_ROWS + N_EXPERTS
    counts = cnt[0, :N_EXPERTS].astype(jnp.int32)
    padded = (counts + (EXPERT_ROWS - 1)) // EXPERT_ROWS * EXPERT_ROWS
    ends = jnp.cumsum(padded)
    starts = ends - padded
    eid = rt[:, RT_E:RT_E + 2].astype(jnp.int32)
    rank = rt[:, RT_RANK:RT_RANK + 2].astype(jnp.int32)
    first = jnp.sum(jnp.where(eid[..., None] == jnp.arange(N_EXPERTS, dtype=jnp.int32), starts, 0), axis=-1)
    pos = (first + rank).reshape(-1)
    n_used = ends[-1] // EXPERT_ROWS
    tile_blk = jnp.minimum(jnp.arange(n_tiles, dtype=jnp.int32), n_used - 1)
    tile_e = jnp.sum((tile_blk[:, None] * EXPERT_ROWS >= ends[None, :]).astype(jnp.int32), axis=1)
    zrow = jnp.where(padded > 0, ends - EXPERT_ROWS, -1)
    return pos, zrow.astype(jnp.int32), tile_e.astype(jnp.int32), tile_blk, n_used.reshape(1).astype(jnp.int32), n_tiles


def _rope_tables(pos):
    inv = ROPE_THETA ** (-jnp.arange(0, MLA_ROPE, 2, dtype=F32) / MLA_ROPE)
    ang = pos.astype(F32)[:, None] * inv[None, :]
    cs = jnp.concatenate([jnp.cos(ang), jnp.sin(ang)], axis=-1)
    return cs[None], cs.T[None]


def _prep_weights(ln_in_g, ln_in_b, w_in, w_gk2, b_gk, gla_norm_g, q_norm_g, kv_norm_g, w_uq, w_uk, w_uv,
                  w_br_gla, w_br_mla, w_mg, b_mg, w_out, ln1_g, ln1_b, w_rg, b_rg, w_re, b_re,
                  w_gate, w_up, w_down, ln2_g, ln2_b):
    d = w_in.shape[1]
    w = w_in[0]
    c0 = 0
    wk = w[:, c0:c0 + GLA_QK]; c0 += GLA_QK
    wv = w[:, c0:c0 + GLA_V]; c0 += GLA_V
    wgr = w[:, c0:c0 + GLA_RANK]; c0 += GLA_RANK
    wckv = w[:, c0:c0 + MLA_KV_RANK]; c0 += MLA_KV_RANK
    wkr = w[:, c0:c0 + MLA_ROPE]; c0 += MLA_ROPE
    wq = w[:, c0:c0 + GLA_QK]; c0 += GLA_QK
    wg = w[:, c0:c0 + GLA_V]; c0 += GLA_V
    wcq = w[:, c0:c0 + MLA_Q_RANK]
    w_tok = jnp.concatenate([wk, wv, wq, wckv, wkr, wgr, jnp.zeros((d, TOK_PAD - TOK_USED), F32)], axis=1)
    w_tr = jnp.concatenate([wckv, wcq], axis=1).T
    uq = w_uq[0].reshape(MLA_Q_RANK, MLA_HEADS, MLA_QK_DIM)
    uq_perm = jnp.concatenate([
        uq[:, :, :MLA_NOPE].reshape(MLA_Q_RANK, -1),
        uq[:, :, MLA_NOPE:MLA_NOPE + MLA_HALF].reshape(MLA_Q_RANK, -1),
        uq[:, :, MLA_NOPE + MLA_HALF:].reshape(MLA_Q_RANK, -1)], axis=1)
    uv = w_uv[0].transpose(1, 0, 2).reshape(MLA_HEADS // 2, 2, MLA_KV_RANK, MLA_DV)
    eye = jnp.eye(2, dtype=F32)
    w_uv_bd = (uv[:, :, :, None, :] * eye[None, :, None, :, None]).reshape(
        MLA_HEADS // 2, 2 * MLA_KV_RANK, 2 * MLA_DV)
    w_r = jnp.concatenate([w_rg[0], w_re[0].transpose(1, 0, 2).reshape(d, N_EXPERTS),
                           jnp.zeros((d, ROUTER_LANES - N_GROUPS - N_EXPERTS), F32)], axis=1)
    w_r_hi = w_r.astype(BF16)
    b_r = jnp.concatenate([b_rg[0], b_re[0].reshape(-1), jnp.zeros((ROUTER_LANES - N_GROUPS - N_EXPERTS,), F32)])
    row = lambda a: a.reshape(1, -1)
    return {
        "ln_g": row(ln_in_g), "ln_b": row(ln_in_b),
        "w_tok": w_tok.astype(BF16), "w_tr": w_tr.astype(BF16),
        "w_gk2": w_gk2[0].astype(BF16), "b_gk": row(b_gk[0]),
        "kv_g": row(kv_norm_g[0]), "kv_gt": kv_norm_g[0].reshape(-1, 1), "q_gt": q_norm_g[0].reshape(-1, 1),
        "w_uqt": uq_perm.T.astype(BF16), "w_uk": w_uk[0].transpose(1, 0, 2).astype(BF16),
        "w_g": wg.astype(BF16), "gla_g": row(gla_norm_g[0]),
        "w_br_gla": w_br_gla[0].astype(BF16), "w_uv_bd": w_uv_bd.astype(BF16), "w_br_mla": w_br_mla[0].astype(BF16),
        "w_mg": w_mg[0].astype(BF16), "b_mg": row(b_mg[0]), "w_out": w_out[0].astype(BF16),
        "ln1_g": row(ln1_g[0]), "ln1_b": row(ln1_b[0]),
        "w_r_hi": w_r_hi, "w_r_lo": (w_r - w_r_hi.astype(F32)).astype(BF16), "b_r": row(b_r),
        "w_gate": w_gate[0], "w_up": w_up[0], "w_down": w_down[0],
        "ln2_g": row(ln2_g[0]), "ln2_b": row(ln2_b[0]),
    }


def _value_rows(vt):
    lead, length = vt.shape[:-2], vt.shape[-1]
    return jnp.concatenate([vt, jnp.ones(lead + (1, length), vt.dtype),
                            jnp.zeros(lead + (V_ROWS - MLA_KV_RANK - 1, length), vt.dtype)], axis=-2)


def _key_tiles(kcat, vt):
    bsz, length, _ = kcat.shape
    n = -(-length // KEY_TILE)
    pad = n * KEY_TILE - length
    kcat = jnp.pad(kcat, ((0, 0), (0, pad), (0, 0)))
    vt = jnp.pad(vt, ((0, 0), (0, 0), (0, pad)))
    return (kcat.reshape(bsz, n, KEY_TILE, MLA_CAT),
            vt.reshape(bsz, V_ROWS, n, KEY_TILE).transpose(0, 2, 1, 3))


def _ffn(groups, wp):
    h1ts, rts = [], []
    cnt = jnp.zeros((1, ROUTER_LANES), F32)
    for x, og, ol in groups:
        t, d = x.shape[0] * x.shape[1], x.shape[2]
        h1t, rt, cnt = _merge(x.reshape(t, d), og.reshape(t, -1), ol.reshape(t, -1), wp, cnt, tt=min(MERGE_TILE, t))
        h1ts.append(h1t)
        rts.append(rt)
    n_fields = RT_RANK + 2
    pos, zrow, tile_e, tile_blk, n_used, n_tiles = _route_plan(
        jnp.concatenate([rt[:, :n_fields] for rt in rts], axis=0), cnt)
    xs = _dispatch(h1ts, pos, zrow, n_used, n_rows=n_tiles * EXPERT_ROWS, td=MOE_TOKEN_TILE)
    outs = _experts(xs, tile_e, tile_blk, n_used, wp)
    ys = _combine(h1ts, rts, outs, pos, wp, tc=MOE_TOKEN_TILE)
    return [y.reshape(x.shape) for y, (x, _, _) in zip(ys, groups)]


def kernel(x_prompt, x_sample, cache_mla_latent, cache_mla_krope, state_gla, meta_tokens, ln_in_g, ln_in_b, w_in, w_gk2, b_gk, gla_norm_g, q_norm_g, kv_norm_g, w_uq, w_uk, w_uv, w_br_gla, w_br_mla, w_mg, b_mg, w_out, ln1_g, ln1_b, w_rg, b_rg, w_re, b_re, w_gate, w_up, w_down, ln2_g, ln2_b):
    bp, sp, d = x_prompt.shape
    bs, ss, _ = x_sample.shape
    past = cache_mla_latent.shape[2]
    wp = _prep_weights(ln_in_g, ln_in_b, w_in, w_gk2, b_gk, gla_norm_g, q_norm_g, kv_norm_g, w_uq, w_uk, w_uv,
                       w_br_gla, w_br_mla, w_mg, b_mg, w_out, ln1_g, ln1_b, w_rg, b_rg, w_re, b_re,
                       w_gate, w_up, w_down, ln2_g, ln2_b)

    cs_m, cst_m = _rope_tables(jnp.arange(-N_META, 0, dtype=jnp.int32))
    m = _proj(meta_tokens[None], cs_m, cst_m, wp, cl=N_META, tt=N_META)
    _, _, m_kt, m_v, m_ebl, m_ckv, m_kr, m_kcat, m_vt, _ = m
    zero_state = jnp.zeros((1, GLA_HEADS, GLA_DK, GLA_DV), F32)
    _, m_state = _gla(m[0], m[1], m_kt, m_v, m_ebl, zero_state, cl=N_META, ts=N_META)

    cs_p, cst_p = _rope_tables(jnp.arange(sp, dtype=jnp.int32))
    p_qd, p_ki, p_kt, p_v, p_ebl, p_ckv, p_kr, p_kcat, p_vt, p_qt = _proj(
        x_prompt, cs_p, cst_p, wp, cl=CHUNK, tt=PROJ_TILE)
    p_o, p_state = _gla(p_qd, p_ki, p_kt, p_v, p_ebl, m_state, cl=CHUNK, ts=TOKEN_TILE)
    rep = lambda a, n: jnp.broadcast_to(a, (n,) + a.shape[1:])
    lat_p = jnp.concatenate([rep(m_ckv, bp), p_ckv], axis=1)
    kr_p = jnp.concatenate([rep(m_kr, bp), p_kr], axis=1)
    kcat_p = p_kcat.reshape(bp, sp // KEY_TILE, KEY_TILE, MLA_CAT)
    km, vm = m_kcat[0], m_vt[0, 0]
    p_ol = _attn(p_qt, kcat_p, p_vt, km, vm, bq=TOKEN_TILE, causal=True, n_valid_last=KEY_TILE, n_sets=2)

    ts_all = bs * ss
    cs_s, cst_s = _rope_tables(past + (jnp.arange(ts_all, dtype=jnp.int32) % ss))
    s_qd, s_ki, s_kt, s_v, s_ebl, s_ckv, s_kr, s_kcat, s_vt, s_qt = _proj(
        x_sample.reshape(1, ts_all, d), cs_s, cst_s, wp, cl=ss, tt=min(TOKEN_TILE, ts_all))
    per_stream = lambda a: a.reshape(bs, ss, a.shape[-1])
    s_o, s_state = _gla(per_stream(s_qd), per_stream(s_ki), per_stream(s_kt), per_stream(s_v),
                        s_ebl.reshape(bs, 1, 1, GLA_QK), state_gla[0].astype(F32), cl=ss, ts=ss)
    s_ckv, s_kr = per_stream(s_ckv), per_stream(s_kr)
    cache_kcat = jnp.concatenate([cache_mla_latent[0], cache_mla_krope[0]], axis=-1).astype(BF16)
    cache_vt = _value_rows(cache_mla_latent[0].astype(BF16).transpose(0, 2, 1))
    new_vt = s_vt[0].transpose(1, 0, 2).reshape(V_ROWS, bs, ss).transpose(1, 0, 2)
    kcat_s, vt_s = _key_tiles(jnp.concatenate([cache_kcat, per_stream(s_kcat)], axis=1),
                              jnp.concatenate([cache_vt, new_vt], axis=2))
    qt = s_qt.transpose(0, 2, 3, 1, 4).reshape(MLA_HEADS, MLA_CAT, bs, ss)
    qt = qt.transpose(2, 1, 0, 3).reshape(bs, MLA_CAT, MLA_HEADS * ss // KEY_TILE, KEY_TILE)
    qt = qt.transpose(0, 2, 1, 3)[:, None]
    s_ol = _attn(qt, kcat_s, vt_s, km, vm, bq=ss, causal=False, n_valid_last=(past + ss - 1) % KEY_TILE + 1, n_sets=1)
    y_prompt, y_sample = _ffn([(x_prompt, p_o, p_ol), (x_sample, s_o, s_ol)], wp)

    return (y_prompt, y_sample, lat_p[None], kr_p[None], p_state[None].astype(state_gla.dtype),
            s_ckv[None], s_kr[None], s_state[None].astype(state_gla.dtype))
```

```python
import functools

import jax
import jax.numpy as jnp
from jax import lax
from jax.experimental import pallas as pl
from jax.experimental.pallas import tpu as pltpu

F32 = jnp.float32
BF16 = jnp.bfloat16

CHUNK = 64
N_META = 16
GLA_HEADS = 4
GLA_DK = 128
GLA_DV = 256
GLA_RANK = 16
GLA_TAU = 16.0
GLA_QK = GLA_HEADS * GLA_DK
GLA_V = GLA_HEADS * GLA_DV
GLA_SCALE = GLA_DK ** -0.5
MLA_HEADS = 16
MLA_Q_RANK = 384
MLA_KV_RANK = 128
MLA_NOPE = 64
MLA_ROPE = 32
MLA_HALF = MLA_ROPE // 2
MLA_DV = 64
MLA_QK_DIM = MLA_NOPE + MLA_ROPE
MLA_CAT = MLA_KV_RANK + MLA_ROPE
MLA_SCALE = MLA_QK_DIM ** -0.5
LOG2_E = 1.4426950408889634
Q_SCALE = MLA_SCALE * LOG2_E
V_ROWS = MLA_KV_RANK + 16
ROPE_THETA = 10000.0
N_GROUPS = 4
EXPERTS_PER_GROUP = 8
N_EXPERTS = N_GROUPS * EXPERTS_PER_GROUP
D_EXPERT = 256
LN_EPS = 1e-5
RMS_EPS = 1e-6
DEEPNORM_ALPHA = 2.0 ** 0.25

LANES = 128
MXU_DIM = 256
VMEM_LIMIT_BYTES = 56 * 1024 * 1024

TOKEN_TILE = 256
MERGE_TILE = 2 * TOKEN_TILE
PROJ_TILE = 2 * TOKEN_TILE
KEY_TILE = MXU_DIM
MOE_TOKEN_TILE = 512
EXPERT_ROWS = 256
ROW_TILE = 8
ROUTER_LANES = LANES
RT_E, RT_W, RT_RANK = 0, 2, 4
NEG_BIG = -1e30

NT_DIMS = (((1,), (1,)), ((), ()))
TN_DIMS = (((0,), (0,)), ((), ()))


def _cparams(*sem):
    return pltpu.CompilerParams(dimension_semantics=sem, vmem_limit_bytes=VMEM_LIMIT_BYTES)


def _const_spec(shape):
    nd = len(shape)
    return pl.BlockSpec(shape, lambda *_: (0,) * nd, pipeline_mode=pl.Buffered(1))


def _layer_norm(x, g, b):
    mu = jnp.mean(x, axis=-1, keepdims=True)
    xc = x - mu
    var = jnp.mean(xc * xc, axis=-1, keepdims=True)
    return xc * lax.rsqrt(var + LN_EPS) * g + b


def _dot(a, b):
    return jnp.dot(a, b, preferred_element_type=F32)


TOK_COLS = (GLA_QK, GLA_V, GLA_QK, MLA_KV_RANK, MLA_ROPE, GLA_RANK)
TOK_USED = sum(TOK_COLS)
TOK_PAD = -(-TOK_USED // MXU_DIM) * MXU_DIM
TR_ROWS = MLA_KV_RANK + MLA_Q_RANK


def _proj_kernel(x_ref, cs_ref, cst_ref, lng_ref, lnb_ref, wtok_ref, wtr_ref, wgk2_ref, bgk_ref,
                 kvg_ref, kvgt_ref, qngt_ref, wuqt_ref, wuk_ref,
                 qd_ref, ki_ref, kt_ref, v_ref, ebl_ref, ckv_ref, kr_ref, kcat_ref, vt_ref, qt_ref, *, cl, sub):
    tt = x_ref.shape[1]
    n_chunks = sub // cl

    def project(r0):
        h = _layer_norm(x_ref[0, r0:r0 + sub, :], lng_ref[...], lnb_ref[...])
        hb = h.astype(BF16)
        z = _dot(hb, wtok_ref[...])
        zt = lax.dot_general(wtr_ref[...], hb, NT_DIMS, preferred_element_type=F32)
        return z, zt

    def finish(r0, z, zt):
        rows = slice(r0, r0 + sub)
        o0 = 0
        k = z[:, o0:o0 + GLA_QK]; o0 += GLA_QK
        v = z[:, o0:o0 + GLA_V]; o0 += GLA_V
        q = z[:, o0:o0 + GLA_QK]; o0 += GLA_QK
        ckv_raw = z[:, o0:o0 + MLA_KV_RANK]; o0 += MLA_KV_RANK
        kr = z[:, o0:o0 + MLA_ROPE]; o0 += MLA_ROPE
        gklr = z[:, o0:o0 + GLA_RANK]

        gz = _dot(gklr.astype(BF16), wgk2_ref[...]) + bgk_ref[...]
        gk = (jnp.minimum(gz, 0.0) - jnp.log(1.0 + jnp.exp(-jnp.abs(gz)))) * (1.0 / GLA_TAU)
        row_in_chunk = lax.broadcasted_iota(jnp.int32, gk.shape, 0) & (cl - 1)
        b = gk
        shift = 1
        while shift < cl:
            b = b + jnp.where(row_in_chunk >= shift, pltpu.roll(b, shift, 0), 0.0)
            shift *= 2
        b3 = b.reshape(n_chunks, cl, GLA_QK)
        bl = b3[:, cl - 1:cl, :]
        qd_ref[0, rows, :] = (q * GLA_SCALE * jnp.exp(b)).astype(BF16)
        ki_ref[0, rows, :] = (k * jnp.exp(-b)).astype(BF16)
        kt_ref[0, rows, :] = (k.reshape(n_chunks, cl, GLA_QK) * jnp.exp(bl - b3)).reshape(sub, GLA_QK).astype(BF16)
        v_ref[0, rows, :] = v.astype(BF16)
        ebl_ref[0, r0 // cl:r0 // cl + n_chunks] = jnp.exp(bl)

        ckv = ckv_raw * lax.rsqrt(jnp.mean(ckv_raw * ckv_raw, axis=-1, keepdims=True) + RMS_EPS) * kvg_ref[...]
        cs = cs_ref[0, rows, :]
        cos, sin = cs[:, :MLA_HALF], cs[:, MLA_HALF:]
        x1, x2 = kr[:, :MLA_HALF], kr[:, MLA_HALF:]
        kr_rot = jnp.concatenate([x1 * cos - x2 * sin, x2 * cos + x1 * sin], axis=-1)
        ckv_ref[0, rows, :] = ckv
        kr_ref[0, rows, :] = kr_rot
        kcat_ref[0, rows, :MLA_KV_RANK] = ckv.astype(BF16)
        kcat_ref[0, rows, MLA_KV_RANK:] = kr_rot.astype(BF16)

        ckvt = zt[:MLA_KV_RANK]
        ckvt = ckvt * lax.rsqrt(jnp.mean(ckvt * ckvt, axis=0, keepdims=True) + RMS_EPS) * kvgt_ref[...]
        blk = r0 // sub
        extra = lax.broadcasted_iota(jnp.int32, (V_ROWS - MLA_KV_RANK, sub), 0) == 0
        vt_ref[0, blk, :MLA_KV_RANK, :] = ckvt.astype(BF16)
        vt_ref[0, blk, MLA_KV_RANK:, :] = jnp.where(extra, 1.0, 0.0).astype(BF16)
        cqt = zt[MLA_KV_RANK:]
        cqt = cqt * lax.rsqrt(jnp.mean(cqt * cqt, axis=0, keepdims=True) + RMS_EPS) * qngt_ref[...]
        qmt = _dot(wuqt_ref[...], cqt.astype(BF16))
        n_nope = MLA_HEADS * MLA_NOPE
        n_half = MLA_HEADS * MLA_HALF
        cst = cst_ref[0, :, rows]
        cos_t = jnp.concatenate([cst[:MLA_HALF]] * MLA_HEADS, axis=0)
        sin_t = jnp.concatenate([cst[MLA_HALF:]] * MLA_HEADS, axis=0)
        r1 = qmt[n_nope:n_nope + n_half]
        r2 = qmt[n_nope + n_half:]
        rot1 = ((r1 * cos_t - r2 * sin_t) * Q_SCALE).astype(BF16)
        rot2 = ((r2 * cos_t + r1 * sin_t) * Q_SCALE).astype(BF16)
        for hd in range(MLA_HEADS):
            nope = qmt[hd * MLA_NOPE:(hd + 1) * MLA_NOPE].astype(BF16)
            qlat = _dot(wuk_ref[hd], nope) * Q_SCALE
            qt_ref[0, blk, hd, :MLA_KV_RANK, :] = qlat.astype(BF16)
            qt_ref[0, blk, hd, MLA_KV_RANK:MLA_KV_RANK + MLA_HALF, :] = rot1[hd * MLA_HALF:(hd + 1) * MLA_HALF]
            qt_ref[0, blk, hd, MLA_KV_RANK + MLA_HALF:, :] = rot2[hd * MLA_HALF:(hd + 1) * MLA_HALF]

    starts = list(range(0, tt, sub))
    projected = [project(r0) for r0 in starts]
    for r0, (z, zt) in zip(starts, projected):
        finish(r0, z, zt)


def _proj(x, cs, cst, wp, *, cl, tt):
    bsz, s, d = x.shape
    nt = s // tt
    n_chunks = tt // cl
    sub = min(TOKEN_TILE, tt)
    tok = lambda w: pl.BlockSpec((1, tt, w), lambda b, t: (b, t, 0))
    in_specs = [
        tok(d),
        pl.BlockSpec((1, tt, MLA_ROPE), lambda b, t: (0, t, 0)),
        pl.BlockSpec((1, MLA_ROPE, tt), lambda b, t: (0, 0, t)),
    ] + [_const_spec(wp[n].shape) for n in _PROJ_WEIGHTS]
    out_shape = [
        jax.ShapeDtypeStruct((bsz, s, GLA_QK), BF16),
        jax.ShapeDtypeStruct((bsz, s, GLA_QK), BF16),
        jax.ShapeDtypeStruct((bsz, s, GLA_QK), BF16),
        jax.ShapeDtypeStruct((bsz, s, GLA_V), BF16),
        jax.ShapeDtypeStruct((bsz, s // cl, 1, GLA_QK), F32),
        jax.ShapeDtypeStruct((bsz, s, MLA_KV_RANK), F32),
        jax.ShapeDtypeStruct((bsz, s, MLA_ROPE), F32),
        jax.ShapeDtypeStruct((bsz, s, MLA_CAT), BF16),
        jax.ShapeDtypeStruct((bsz, s // sub, V_ROWS, sub), BF16),
        jax.ShapeDtypeStruct((bsz, s // sub, MLA_HEADS, MLA_CAT, sub), BF16),
    ]
    out_specs = [
        tok(GLA_QK), tok(GLA_QK), tok(GLA_QK), tok(GLA_V),
        pl.BlockSpec((1, n_chunks, 1, GLA_QK), lambda b, t: (b, t, 0, 0)),
        tok(MLA_KV_RANK), tok(MLA_ROPE), tok(MLA_CAT),
        pl.BlockSpec((1, tt // sub, V_ROWS, sub), lambda b, t: (b, t, 0, 0)),
        pl.BlockSpec((1, tt // sub, MLA_HEADS, MLA_CAT, sub), lambda b, t: (b, t, 0, 0, 0)),
    ]
    return pl.pallas_call(
        functools.partial(_proj_kernel, cl=cl, sub=sub),
        grid=(bsz, nt),
        in_specs=in_specs,
        out_specs=out_specs,
        out_shape=out_shape,
        compiler_params=_cparams("parallel", "parallel"),
        name="proj",
    )(x, cs, cst, *[wp[n] for n in _PROJ_WEIGHTS])


_PROJ_WEIGHTS = ("ln_g", "ln_b", "w_tok", "w_tr", "w_gk2", "b_gk", "kv_g", "kv_gt", "q_gt", "w_uqt", "w_uk")


def _gla_kernel(qd_ref, ki_ref, kt_ref, v_ref, ebl_ref, s0_ref, o_ref, sfin_ref, st_ref, *, cl):
    t = pl.program_id(1)
    ts = qd_ref.shape[1]

    @pl.when(t == 0)
    def _():
        for hd in range(GLA_HEADS):
            st_ref[hd] = s0_ref[0, hd].T

    row = lax.broadcasted_iota(jnp.int32, (ts, ts), 0)
    col = lax.broadcasted_iota(jnp.int32, (ts, ts), 1)
    keep = (row >= col) & ((row & -cl) == (col & -cl))
    qk = lambda ref, hd: ref[0, :, hd * GLA_DK:(hd + 1) * GLA_DK]
    val = lambda hd: v_ref[0, :, hd * GLA_DV:(hd + 1) * GLA_DV]
    for hd in range(GLA_HEADS):
        a = lax.dot_general(qk(qd_ref, hd), qk(ki_ref, hd), NT_DIMS, preferred_element_type=F32)
        a = jnp.where(keep, a, 0.0).astype(BF16)
        o_ref[0, :, hd * GLA_DV:(hd + 1) * GLA_DV] = _dot(a, val(hd))
    for c in range(ts // cl):
        rows = slice(c * cl, (c + 1) * cl)
        for hd in range(GLA_HEADS):
            st = st_ref[hd]
            o_ref[0, rows, hd * GLA_DV:(hd + 1) * GLA_DV] += lax.dot_general(
                qk(qd_ref, hd)[rows], st.astype(BF16), NT_DIMS, preferred_element_type=F32)
            st_ref[hd] = (st * ebl_ref[0, c, :, hd * GLA_DK:(hd + 1) * GLA_DK]
                          + lax.dot_general(val(hd)[rows], qk(kt_ref, hd)[rows], TN_DIMS, preferred_element_type=F32))

    @pl.when(t == pl.num_programs(1) - 1)
    def _():
        for hd in range(GLA_HEADS):
            sfin_ref[0, hd] = st_ref[hd].T


def _gla(qd, ki, kt, v, ebl, s0, *, cl, ts):
    bsz, s, _ = qd.shape
    n_chunks = ts // cl
    s0_b = s0.shape[0]
    qk_spec = pl.BlockSpec((1, ts, GLA_QK), lambda b, t: (b, t, 0))
    v_spec = pl.BlockSpec((1, ts, GLA_V), lambda b, t: (b, t, 0))
    st_spec = pl.BlockSpec((1, GLA_HEADS, GLA_DK, GLA_DV), lambda b, t: (b, 0, 0, 0))
    s0_spec = st_spec if s0_b == bsz else pl.BlockSpec((1, GLA_HEADS, GLA_DK, GLA_DV), lambda b, t: (0, 0, 0, 0))
    return pl.pallas_call(
        functools.partial(_gla_kernel, cl=cl),
        grid=(bsz, s // ts),
        in_specs=[qk_spec, qk_spec, qk_spec, v_spec,
                  pl.BlockSpec((1, n_chunks, 1, GLA_QK), lambda b, t: (b, t, 0, 0)),
                  s0_spec],
        out_specs=[v_spec, st_spec],
        out_shape=[jax.ShapeDtypeStruct((bsz, s, GLA_V), F32),
                   jax.ShapeDtypeStruct((bsz, GLA_HEADS, GLA_DK, GLA_DV), F32)],
        scratch_shapes=[pltpu.VMEM((GLA_HEADS, GLA_DV, GLA_DK), F32)],
        compiler_params=_cparams("parallel", "arbitrary"),
        name="gla",
    )(qd, ki, kt, v, ebl, s0)


SCRATCH_PER_SET = 6
TILES_PER_TRIP = 4


def _attn_kernel(qt_ref, kcat_ref, vt_ref, km_ref, vm_ref, o_ref, *scratch, bq, causal, n_valid_last):
    n_sets = qt_ref.shape[1]
    n_cc = qt_ref.shape[3]
    n_kt = kcat_ref.shape[1]
    groups_per_cc = KEY_TILE // LANES

    def when(cond, fn):
        if isinstance(cond, bool):
            if cond:
                fn()
        else:
            pl.when(cond)(fn)

    def block(h):
        s_ref, p_ref, m_ref, a_ref, acc_ref, mx_ref = scratch[SCRATCH_PER_SET * h:SCRATCH_PER_SET * (h + 1)]
        blk = pl.program_id(1) + h * pl.num_programs(1)
        n_int = blk if causal else n_kt - 1
        q = lambda c: qt_ref[0, h, 0, c]
        col_max = lambda x: jnp.max(x, axis=0, keepdims=True)
        col_sum = lambda x: jnp.sum(x, axis=0, keepdims=True)

        def scores(c, kt):
            s = _dot(kt, q(c))
            s_ref[c] = s
            mx_ref[c] = col_max(s)

        def softmax(c, mask):
            s = s_ref[c]
            if mask is None:
                s_max = mx_ref[c]
            else:
                s = jnp.where(mask, s, NEG_BIG)
                s_max = col_max(s)
            m_prev = m_ref[c]
            m_new = jnp.maximum(m_prev, s_max)
            a_ref[c] = jnp.exp2(m_prev - m_new)
            m_ref[c] = m_new
            p_ref[c] = jnp.exp2(s - m_new).astype(BF16)

        def softmax_diagonal(c):
            half = KEY_TILE // 2
            chunks_per_half = half // CHUNK
            second = lax.broadcasted_iota(jnp.int32, (CHUNK, half), 1) >= CHUNK
            m_prev = m_ref[c]
            m_out, a_out = [], []
            for lh in range(KEY_TILE // half):
                lanes = slice(lh * half, (lh + 1) * half)
                n_full = chunks_per_half * lh + 1
                rows_full = slice(0, n_full * CHUNK)
                rows_part = slice(n_full * CHUNK, (n_full + 1) * CHUNK)
                full = s_ref[c, rows_full, lanes]
                part = jnp.where(second, s_ref[c, rows_part, lanes], NEG_BIG)
                m_new = jnp.maximum(m_prev[:, lanes], jnp.maximum(col_max(full), col_max(part)))
                p_ref[c, rows_full, lanes] = jnp.exp2(full - m_new).astype(BF16)
                p_ref[c, rows_part, lanes] = jnp.exp2(part - m_new).astype(BF16)
                if (n_full + 1) * CHUNK < KEY_TILE:
                    hidden = KEY_TILE - (n_full + 1) * CHUNK
                    p_ref[c, (n_full + 1) * CHUNK:, lanes] = jnp.zeros((hidden, half), BF16)
                m_out.append(m_new)
                a_out.append(jnp.exp2(m_prev[:, lanes] - m_new))
            m_ref[c] = jnp.concatenate(m_out, axis=1)
            a_ref[c] = jnp.concatenate(a_out, axis=1)

        def values(c, vt):
            acc_ref[c] = a_ref[c] * acc_ref[c] + _dot(vt, p_ref[c])

        def start():
            km, vm = km_ref[...], vm_ref[...]
            sm = [_dot(km, q(c)) for c in range(n_cc)]
            pm = []
            for c in range(n_cc):
                m0 = col_max(sm[c])
                m_ref[c] = m0
                pm.append(jnp.exp2(sm[c] - m0).astype(BF16))
            for c in range(n_cc):
                acc_ref[c] = _dot(vm, pm[c])
            kt = kcat_ref[0, 0]
            for c in range(n_cc):
                scores(c, kt)

        def first_step():
            kt_next = kcat_ref[0, 1]
            for c in range(n_cc):
                softmax(c, None)
                scores(c, kt_next)

        def no_step():
            a_ref[...] = jnp.ones(a_ref.shape, F32)
            p_ref[...] = jnp.zeros(p_ref.shape, BF16)

        def key_step(j):
            kt_next = kcat_ref[0, j + 1]
            vt_prev = vt_ref[0, j - 1]
            for c in range(n_cc):
                values(c, vt_prev)
                softmax(c, None)
                scores(c, kt_next)

        def key_step_group(jj, carry):
            for u in range(TILES_PER_TRIP):
                key_step(TILES_PER_TRIP * jj + 1 + u)
            return carry

        def unmasked_tiles():
            when(n_int >= 1, first_step)
            when(n_int == 0, no_step)
            n_rest = jnp.maximum(n_int - 1, 0) if causal else max(n_int - 1, 0)
            trips = n_rest // TILES_PER_TRIP
            lax.fori_loop(0, trips, key_step_group, 0)
            for u in range(TILES_PER_TRIP - 1):
                when(n_rest % TILES_PER_TRIP > u, lambda u=u: key_step(trips * TILES_PER_TRIP + 1 + u))

        def last_values_of_unmasked():
            vt_prev = vt_ref[0, jnp.maximum(n_int - 1, 0)]
            for c in range(n_cc):
                values(c, vt_prev)

        def masked_tile_and_emit():
            if causal:
                masked_softmax = softmax_diagonal
            else:
                mask = lax.broadcasted_iota(jnp.int32, (KEY_TILE, KEY_TILE), 0) < n_valid_last
                masked_softmax = lambda c: softmax(c, mask)
            vt_last = vt_ref[0, n_int]
            value_lag = 2
            for c in range(n_cc + value_lag):
                if c < n_cc:
                    masked_softmax(c)
                if c >= value_lag:
                    values(c - value_lag, vt_last)
            for c in range(n_cc):
                acc = acc_ref[c]
                o_t = acc[:MLA_KV_RANK] * (1.0 / acc[MLA_KV_RANK:MLA_KV_RANK + 1])
                for g in range(groups_per_cc):
                    tile = o_t[:, g * LANES:(g + 1) * LANES].T.astype(BF16)
                    col0 = c * KEY_TILE + g * LANES
                    if bq >= LANES:
                        hd, q0 = col0 // bq, col0 % bq
                        o_ref[0, h, q0:q0 + LANES, hd * MLA_KV_RANK:(hd + 1) * MLA_KV_RANK] = tile
                    else:
                        for hl in range(LANES // bq):
                            hd = col0 // bq + hl
                            o_ref[0, h, :, hd * MLA_KV_RANK:(hd + 1) * MLA_KV_RANK] = tile[hl * bq:(hl + 1) * bq]

        return start, unmasked_tiles, last_values_of_unmasked, masked_tile_and_emit

    blocks = [block(h) for h in range(n_sets)]
    blocks[0][0]()
    for h in range(n_sets):
        start, unmasked_tiles, last_values_of_unmasked, masked_tile_and_emit = blocks[h]
        unmasked_tiles()
        last_values_of_unmasked()
        if h + 1 < n_sets:
            blocks[h + 1][0]()
        masked_tile_and_emit()


def _attn(qt, kcat, vt, km, vm, *, bq, causal, n_valid_last, n_sets):
    bsz, nq, n_cc = qt.shape[:3]
    n_kt = kcat.shape[1]
    assert not causal or bq == KEY_TILE
    per_set = nq // n_sets
    width = MLA_HEADS * MLA_KV_RANK
    scratch_set = [pltpu.VMEM((n_cc, KEY_TILE, KEY_TILE), F32),
                   pltpu.VMEM((n_cc, KEY_TILE, KEY_TILE), BF16),
                   pltpu.VMEM((n_cc, 1, KEY_TILE), F32),
                   pltpu.VMEM((n_cc, 1, KEY_TILE), F32),
                   pltpu.VMEM((n_cc, V_ROWS, KEY_TILE), F32),
                   pltpu.VMEM((n_cc, 1, KEY_TILE), F32)]
    assert len(scratch_set) == SCRATCH_PER_SET
    out = pl.pallas_call(
        functools.partial(_attn_kernel, bq=bq, causal=causal, n_valid_last=n_valid_last),
        grid=(bsz, per_set),
        in_specs=[pl.BlockSpec((1, n_sets, 1, n_cc, MLA_CAT, KEY_TILE), lambda b, i: (b, 0, i, 0, 0, 0)),
                  pl.BlockSpec((1, n_kt, KEY_TILE, MLA_CAT), lambda b, i: (b, 0, 0, 0)),
                  pl.BlockSpec((1, n_kt, V_ROWS, KEY_TILE), lambda b, i: (b, 0, 0, 0)),
                  _const_spec(km.shape), _const_spec(vm.shape)],
        out_specs=pl.BlockSpec((1, n_sets, bq, width), lambda b, i: (b, 0, i, 0)),
        out_shape=jax.ShapeDtypeStruct((bsz, n_sets, per_set * bq, width), BF16),
        scratch_shapes=scratch_set * n_sets,
        compiler_params=_cparams("parallel", "arbitrary"),
        name="attn",
    )(qt.reshape(bsz, n_sets, per_set, *qt.shape[2:]), kcat, vt, km, vm)
    return out.reshape(bsz, nq * bq, width)


_MERGE_WEIGHTS = ("ln_g", "ln_b", "w_g", "gla_g", "w_br_gla", "w_uv_bd", "w_br_mla", "w_mg", "b_mg", "w_out",
                  "ln1_g", "ln1_b", "w_r_hi", "w_r_lo", "b_r")


def _merge_kernel(x_ref, og_ref, ol_ref, lng_ref, lnb_ref, wg_ref, glag_ref, wbg_ref, wuv_ref, wbm_ref,
                  wmg_ref, bmg_ref, wout_ref, l1g_ref, l1b_ref, wrh_ref, wrl_ref, br_ref, cin_ref,
                  h1t_ref, rt_ref, cnt_ref, carry_ref, *, sub):
    tt, d = x_ref.shape

    def branches(rows):
        h = _layer_norm(x_ref[rows, :], lng_ref[...], lnb_ref[...])
        hb = h.astype(BF16)
        g_out = _dot(hb, wg_ref[...])
        gate_pre = _dot(hb, wmg_ref[...])
        pair_in = 2 * MLA_KV_RANK
        y_heads = jnp.concatenate([_dot(ol_ref[rows, p * pair_in:(p + 1) * pair_in], wuv_ref[p])
                                   for p in range(MLA_HEADS // 2)], axis=-1)
        return h, g_out, gate_pre, y_heads

    def mix(rows, h, g_out, gate_pre, y_heads):
        og = og_ref[rows, :]
        parts = []
        for hd in range(GLA_HEADS):
            cols = slice(hd * GLA_DV, (hd + 1) * GLA_DV)
            o_h = og[:, cols]
            g_h = g_out[:, cols]
            o_n = o_h * lax.rsqrt(jnp.mean(o_h * o_h, axis=-1, keepdims=True) + RMS_EPS) * glag_ref[...]
            parts.append(o_n * (g_h * jax.nn.sigmoid(g_h)))
        y_a = _dot(jnp.concatenate(parts, axis=-1).astype(BF16), wbg_ref[...])
        y_b = _dot(y_heads.astype(BF16), wbm_ref[...])
        gates = jax.nn.sigmoid(gate_pre + bmg_ref[...])
        mix_in = gates[:, :d] * y_a + gates[:, d:] * y_b
        return DEEPNORM_ALPHA * h + _dot(mix_in.astype(BF16), wout_ref[...])

    def route(r0, pre):
        h1 = _layer_norm(pre, l1g_ref[...], l1b_ref[...])
        for s in range(ROW_TILE):
            h1t_ref[pl.ds(r0 * ROW_TILE + s, sub, stride=ROW_TILE), :] = h1[:, s * LANES:(s + 1) * LANES]
        h1_hi = h1.astype(BF16)
        h1_lo = (h1 - h1_hi.astype(F32)).astype(BF16)
        logits = (_dot(h1_hi, wrh_ref[...]) + (_dot(h1_hi, wrl_ref[...]) + _dot(h1_lo, wrh_ref[...]))) + br_ref[...]
        lane = lax.broadcasted_iota(jnp.int32, logits.shape, 1)
        is_grp = lane < N_GROUPS
        gl = jnp.where(is_grp, logits, NEG_BIG)
        g_max = jnp.max(gl, axis=-1, keepdims=True)
        g_sel = jnp.min(jnp.where(gl == g_max, lane, ROUTER_LANES), axis=-1, keepdims=True)
        p_grp = 1.0 / jnp.sum(jnp.where(is_grp, jnp.exp(gl - g_max), 0.0), axis=-1, keepdims=True)
        e_lo = N_GROUPS + g_sel * EXPERTS_PER_GROUP
        in_grp = (lane >= e_lo) & (lane < e_lo + EXPERTS_PER_GROUP)
        el = jnp.where(in_grp, logits, NEG_BIG)
        v1 = jnp.max(el, axis=-1, keepdims=True)
        i1 = jnp.min(jnp.where(el == v1, lane, ROUTER_LANES), axis=-1, keepdims=True)
        el2 = jnp.where(lane == i1, NEG_BIG, el)
        v2 = jnp.max(el2, axis=-1, keepdims=True)
        i2 = jnp.min(jnp.where(el2 == v2, lane, ROUTER_LANES), axis=-1, keepdims=True)
        e2 = jnp.exp(v2 - v1)
        w1 = p_grp / (1.0 + e2)
        w2 = p_grp * e2 / (1.0 + e2)
        e1, e2i = i1 - N_GROUPS, i2 - N_GROUPS
        hot = ((lane == e1) | (lane == e2i)).astype(F32)
        earlier = (lax.broadcasted_iota(jnp.int32, (sub, sub), 0) > lax.broadcasted_iota(jnp.int32, (sub, sub), 1))
        before = carry_ref[...] + _dot(earlier.astype(BF16), hot.astype(BF16))
        r1 = jnp.sum(jnp.where(lane == e1, before, 0.0), axis=-1, keepdims=True)
        r2 = jnp.sum(jnp.where(lane == e2i, before, 0.0), axis=-1, keepdims=True)
        carry_ref[...] += jnp.sum(hot, axis=0, keepdims=True)
        fields = (e1.astype(F32), e2i.astype(F32), w1, w2, r1, r2)
        rt = jnp.zeros(logits.shape, F32)
        for k, val in enumerate(fields):
            rt = jnp.where(lane == k, val, rt)
        rt_ref[pl.ds(r0, sub), :] = rt

    @pl.when(pl.program_id(0) == 0)
    def _():
        carry_ref[...] = cin_ref[...]

    starts = list(range(0, tt, sub))
    independent = [branches(pl.ds(r0, sub)) for r0 in starts]
    pending = None
    for r0, ind in zip(starts, independent):
        pre = mix(pl.ds(r0, sub), *ind)
        if pending is not None:
            route(*pending)
        pending = (r0, pre)
    route(*pending)
    cnt_ref[...] = carry_ref[...]


def _merge(x2, og2, ol2, wp, cnt_in, *, tt):
    t, d = x2.shape
    assert d == ROW_TILE * LANES
    row = lambda w: pl.BlockSpec((tt, w), lambda i: (i, 0))
    return pl.pallas_call(
        functools.partial(_merge_kernel, sub=min(TOKEN_TILE, tt)),
        grid=(t // tt,),
        in_specs=([row(d), row(GLA_V), row(MLA_HEADS * MLA_KV_RANK)] + [_const_spec(wp[n].shape) for n in _MERGE_WEIGHTS]
                  + [_const_spec(cnt_in.shape)]),
        out_specs=[pl.BlockSpec((tt * ROW_TILE, LANES), lambda i: (i, 0)), row(ROUTER_LANES),
                   pl.BlockSpec((1, ROUTER_LANES), lambda i: (0, 0))],
        out_shape=[jax.ShapeDtypeStruct((t * ROW_TILE, LANES), F32), jax.ShapeDtypeStruct((t, ROUTER_LANES), F32),
                   jax.ShapeDtypeStruct((1, ROUTER_LANES), F32)],
        scratch_shapes=[pltpu.VMEM((1, ROUTER_LANES), F32)],
        compiler_params=_cparams("arbitrary"),
        name="merge",
    )(x2, og2, ol2, *[wp[n] for n in _MERGE_WEIGHTS], cnt_in)


def _row_copy(src_ref, src_row, dst_ref, dst_row, sem):
    return pltpu.make_async_copy(src_ref.at[pl.ds(src_row * ROW_TILE, ROW_TILE)],
                                 dst_ref.at[pl.ds(dst_row * ROW_TILE, ROW_TILE)], sem)


def _group_spans(n_tokens, tile):
    spans, first = [], 0
    for n in n_tokens:
        spans.append((first, n // tile))
        first += n // tile
    return spans


def _group_spec(block, span):
    first, steps = span
    return pl.BlockSpec(block, lambda i, *_: (jnp.clip(i - first, 0, steps - 1), 0))


def _dispatch_kernel(pos_ref, zrow_ref, nu_ref, *refs, td, spans):
    h1t_refs, (xs_ref, zero_ref, sem, zsem) = refs[:len(spans)], refs[len(spans):]
    i = pl.program_id(0)

    @pl.when(i == 0)
    def _():
        zero_ref[...] = jnp.zeros(zero_ref.shape, F32)
        tile_rows = EXPERT_ROWS * ROW_TILE
        fill = lambda row: pltpu.make_async_copy(
            zero_ref, xs_ref.at[pl.ds(pl.multiple_of(row * ROW_TILE, ROW_TILE), tile_rows)], zsem)
        for e in range(N_EXPERTS):
            @pl.when(zrow_ref[e] >= 0)
            def _():
                fill(zrow_ref[e]).start()
        for e in range(N_EXPERTS):
            @pl.when(zrow_ref[e] >= 0)
            def _():
                fill(zrow_ref[e]).wait()

        def fill_tail(r, carry):
            fill(r * EXPERT_ROWS).start()
            fill(r * EXPERT_ROWS).wait()
            return carry

        lax.fori_loop(nu_ref[0], xs_ref.shape[0] // tile_rows, fill_tail, 0)

    base = i * (2 * td)
    for h1t_ref, (first, steps) in zip(h1t_refs, spans):
        @pl.when((i >= first) & (i < first + steps))
        def _():
            for t in range(td):
                for k in range(2):
                    dst = pl.multiple_of(pos_ref[base + 2 * t + k] * ROW_TILE, ROW_TILE)
                    pltpu.make_async_copy(h1t_ref.at[pl.ds(t * ROW_TILE, ROW_TILE)],
                                          xs_ref.at[pl.ds(dst, ROW_TILE)], sem).start(priority=k)
            for _ in range(2 * td):
                _row_copy(h1t_ref, 0, xs_ref, 0, sem).wait()


def _dispatch(h1ts, pos, zrow, n_used, *, n_rows, td):
    spans = _group_spans([h.shape[0] // ROW_TILE for h in h1ts], td)
    return pl.pallas_call(
        functools.partial(_dispatch_kernel, td=td, spans=spans),
        grid_spec=pltpu.PrefetchScalarGridSpec(
            num_scalar_prefetch=3,
            grid=(sum(steps for _, steps in spans),),
            in_specs=[_group_spec((td * ROW_TILE, LANES), span) for span in spans],
            out_specs=pl.BlockSpec(memory_space=pl.ANY),
            scratch_shapes=[pltpu.VMEM((EXPERT_ROWS * ROW_TILE, LANES), F32),
                            pltpu.SemaphoreType.DMA(()), pltpu.SemaphoreType.DMA(())]),
        out_shape=jax.ShapeDtypeStruct((n_rows * ROW_TILE, LANES), F32),
        compiler_params=_cparams("arbitrary"),
        name="dispatch",
    )(pos, zrow, n_used, *h1ts)


def _experts_kernel(te_ref, tb_ref, nu_ref, xs_ref, wg_ref, wu_ref, wd_ref, out_ref, wgb_ref, wub_ref, wdb_ref):
    del tb_ref
    r = pl.program_id(0)
    rows = xs_ref.shape[0] // ROW_TILE

    @pl.when((r == 0) | (te_ref[r] != te_ref[jnp.maximum(r - 1, 0)]))
    def _():
        wgb_ref[...] = wg_ref[0].astype(BF16)
        wub_ref[...] = wu_ref[0].astype(BF16)
        wdb_ref[...] = wd_ref[0].astype(BF16)

    @pl.when(r < nu_ref[0])
    def _():
        x = jnp.concatenate([xs_ref[pl.ds(s, rows, stride=ROW_TILE), :] for s in range(ROW_TILE)], axis=1).astype(BF16)
        gate = _dot(x, wgb_ref[...])
        up = _dot(x, wub_ref[...])
        hid = (gate * jax.nn.sigmoid(gate)) * up
        out = _dot(hid.astype(BF16), wdb_ref[...])
        for s in range(ROW_TILE):
            out_ref[pl.ds(s, rows, stride=ROW_TILE), :] = out[:, s * LANES:(s + 1) * LANES]

    @pl.when(pl.program_id(0) >= nu_ref[0])
    def _():
        out_ref[...] = jnp.zeros(out_ref.shape, F32)


def _experts(xs, tile_e, tile_blk, n_used, wp):
    n_rows = xs.shape[0] // ROW_TILE
    d = ROW_TILE * LANES
    blk = pl.BlockSpec((EXPERT_ROWS * ROW_TILE, LANES), lambda r, te, tb, nu: (tb[r], 0))
    return pl.pallas_call(
        _experts_kernel,
        grid_spec=pltpu.PrefetchScalarGridSpec(
            num_scalar_prefetch=3,
            grid=(n_rows // EXPERT_ROWS,),
            in_specs=[blk,
                      pl.BlockSpec((1, d, D_EXPERT), lambda r, te, tb, nu: (te[r], 0, 0)),
                      pl.BlockSpec((1, d, D_EXPERT), lambda r, te, tb, nu: (te[r], 0, 0)),
                      pl.BlockSpec((1, D_EXPERT, d), lambda r, te, tb, nu: (te[r], 0, 0))],
            out_specs=pl.BlockSpec((EXPERT_ROWS * ROW_TILE, LANES), lambda r, te, tb, nu: (r, 0)),
            scratch_shapes=[pltpu.VMEM((d, D_EXPERT), BF16), pltpu.VMEM((d, D_EXPERT), BF16),
                            pltpu.VMEM((D_EXPERT, d), BF16)]),
        out_shape=jax.ShapeDtypeStruct(xs.shape, F32),
        compiler_params=_cparams("arbitrary"),
        name="experts",
    )(tile_e, tile_blk, n_used, xs, wp["w_gate"], wp["w_up"], wp["w_down"])


def _combine_kernel(pos_ref, *refs, tc, spans):
    g = len(spans)
    h1t_refs, rt_refs = refs[:g], refs[g:2 * g]
    outs_ref, l2g_ref, l2b_ref = refs[2 * g:2 * g + 3]
    y_refs = refs[2 * g + 3:3 * g + 3]
    g_ref, sem = refs[3 * g + 3:]
    i = pl.program_id(0)
    n_steps = pl.num_programs(0)

    def gather(step):
        slot = step % 2
        base = step * (2 * tc)
        for t in range(tc):
            for k in range(2):
                src = pl.multiple_of(pos_ref[base + 2 * t + k] * ROW_TILE, ROW_TILE)
                pltpu.make_async_copy(outs_ref.at[pl.ds(src, ROW_TILE)],
                                      g_ref.at[slot, k, pl.ds(t * ROW_TILE, ROW_TILE)],
                                      sem.at[slot]).start(priority=k)

    @pl.when(i == 0)
    def _():
        gather(0)

    @pl.when(i + 1 < n_steps)
    def _():
        gather(i + 1)

    slot = i % 2
    for _ in range(2 * tc):
        pltpu.make_async_copy(outs_ref.at[pl.ds(0, ROW_TILE)], g_ref.at[slot, 0, pl.ds(0, ROW_TILE)], sem.at[slot]).wait()
    for h1t_ref, rt_ref, y_ref, (first, steps) in zip(h1t_refs, rt_refs, y_refs, spans):
        @pl.when((i >= first) & (i < first + steps))
        def _():
            rt = rt_ref[...]
            w0, w1 = rt[:, RT_W:RT_W + 1], rt[:, RT_W + 1:RT_W + 2]
            cols = []
            for s in range(ROW_TILE):
                rows = pl.ds(s, tc, stride=ROW_TILE)
                cols.append(DEEPNORM_ALPHA * h1t_ref[rows, :] + (w0 * g_ref[slot, 0, rows, :] + w1 * g_ref[slot, 1, rows, :]))
            y_ref[...] = _layer_norm(jnp.concatenate(cols, axis=1), l2g_ref[...], l2b_ref[...])


def _combine(h1ts, rts, outs, pos, wp, *, tc):
    d = ROW_TILE * LANES
    spans = _group_spans([r.shape[0] for r in rts], tc)
    const = lambda: pl.BlockSpec((1, d), lambda i, *_: (0, 0))
    return pl.pallas_call(
        functools.partial(_combine_kernel, tc=tc, spans=spans),
        grid_spec=pltpu.PrefetchScalarGridSpec(
            num_scalar_prefetch=1,
            grid=(sum(steps for _, steps in spans),),
            in_specs=([_group_spec((tc * ROW_TILE, LANES), span) for span in spans]
                      + [_group_spec((tc, ROUTER_LANES), span) for span in spans]
                      + [pl.BlockSpec(memory_space=pl.ANY), const(), const()]),
            out_specs=[_group_spec((tc, d), span) for span in spans],
            scratch_shapes=[pltpu.VMEM((2, 2, tc * ROW_TILE, LANES), F32), pltpu.SemaphoreType.DMA((2,))]),
        out_shape=[jax.ShapeDtypeStruct((r.shape[0], d), F32) for r in rts],
        compiler_params=_cparams("arbitrary"),
        name="combine",
    )(pos, *h1ts, *rts, outs, wp["ln2_g"], wp["ln2_b"])


def _route_plan(rt, cnt):
    t = rt.shape[0]
    n_tiles = (2 * t) // EXPERT_ROWS + N_EXPERTS
    counts = cnt[0, :N_EXPERTS].astype(jnp.int32)
    padded = (counts + (EXPERT_ROWS - 1)) // EXPERT_ROWS * EXPERT_ROWS
    ends = jnp.cumsum(padded)
    starts = ends - padded
    eid = rt[:, RT_E:RT_E + 2].astype(jnp.int32)
    rank = rt[:, RT_RANK:RT_RANK + 2].astype(jnp.int32)
    first = jnp.sum(jnp.where(eid[..., None] == jnp.arange(N_EXPERTS, dtype=jnp.int32), starts, 0), axis=-1)
    pos = (first + rank).reshape(-1)
    n_used = ends[-1] // EXPERT_ROWS
    tile_blk = jnp.minimum(jnp.arange(n_tiles, dtype=jnp.int32), n_used - 1)
    tile_e = jnp.sum((tile_blk[:, None] * EXPERT_ROWS >= ends[None, :]).astype(jnp.int32), axis=1)
    zrow = jnp.where(padded > 0, ends - EXPERT_ROWS, -1)
    return pos, zrow.astype(jnp.int32), tile_e.astype(jnp.int32), tile_blk, n_used.reshape(1).astype(jnp.int32), n_tiles


def _rope_tables(pos):
    inv = ROPE_THETA ** (-jnp.arange(0, MLA_ROPE, 2, dtype=F32) / MLA_ROPE)
    ang = pos.astype(F32)[:, None] * inv[None, :]
    cs = jnp.concatenate([jnp.cos(ang), jnp.sin(ang)], axis=-1)
    return cs[None], cs.T[None]


def _prep_weights(ln_in_g, ln_in_b, w_in, w_gk2, b_gk, gla_norm_g, q_norm_g, kv_norm_g, w_uq, w_uk, w_uv,
                  w_br_gla, w_br_mla, w_mg, b_mg, w_out, ln1_g, ln1_b, w_rg, b_rg, w_re, b_re,
                  w_gate, w_up, w_down, ln2_g, ln2_b):
    d = w_in.shape[1]
    w = w_in[0]
    c0 = 0
    wk = w[:, c0:c0 + GLA_QK]; c0 += GLA_QK
    wv = w[:, c0:c0 + GLA_V]; c0 += GLA_V
    wgr = w[:, c0:c0 + GLA_RANK]; c0 += GLA_RANK
    wckv = w[:, c0:c0 + MLA_KV_RANK]; c0 += MLA_KV_RANK
    wkr = w[:, c0:c0 + MLA_ROPE]; c0 += MLA_ROPE
    wq = w[:, c0:c0 + GLA_QK]; c0 += GLA_QK
    wg = w[:, c0:c0 + GLA_V]; c0 += GLA_V
    wcq = w[:, c0:c0 + MLA_Q_RANK]
    w_tok = jnp.concatenate([wk, wv, wq, wckv, wkr, wgr, jnp.zeros((d, TOK_PAD - TOK_USED), F32)], axis=1)
    w_tr = jnp.concatenate([wckv, wcq], axis=1).T
    uq = w_uq[0].reshape(MLA_Q_RANK, MLA_HEADS, MLA_QK_DIM)
    uq_perm = jnp.concatenate([
        uq[:, :, :MLA_NOPE].reshape(MLA_Q_RANK, -1),
        uq[:, :, MLA_NOPE:MLA_NOPE + MLA_HALF].reshape(MLA_Q_RANK, -1),
        uq[:, :, MLA_NOPE + MLA_HALF:].reshape(MLA_Q_RANK, -1)], axis=1)
    uv = w_uv[0].transpose(1, 0, 2).reshape(MLA_HEADS // 2, 2, MLA_KV_RANK, MLA_DV)
    eye = jnp.eye(2, dtype=F32)
    w_uv_bd = (uv[:, :, :, None, :] * eye[None, :, None, :, None]).reshape(
        MLA_HEADS // 2, 2 * MLA_KV_RANK, 2 * MLA_DV)
    w_r = jnp.concatenate([w_rg[0], w_re[0].transpose(1, 0, 2).reshape(d, N_EXPERTS),
                           jnp.zeros((d, ROUTER_LANES - N_GROUPS - N_EXPERTS), F32)], axis=1)
    w_r_hi = w_r.astype(BF16)
    b_r = jnp.concatenate([b_rg[0], b_re[0].reshape(-1), jnp.zeros((ROUTER_LANES - N_GROUPS - N_EXPERTS,), F32)])
    row = lambda a: a.reshape(1, -1)
    return {
        "ln_g": row(ln_in_g), "ln_b": row(ln_in_b),
        "w_tok": w_tok.astype(BF16), "w_tr": w_tr.astype(BF16),
        "w_gk2": w_gk2[0].astype(BF16), "b_gk": row(b_gk[0]),
        "kv_g": row(kv_norm_g[0]), "kv_gt": kv_norm_g[0].reshape(-1, 1), "q_gt": q_norm_g[0].reshape(-1, 1),
        "w_uqt": uq_perm.T.astype(BF16), "w_uk": w_uk[0].transpose(1, 0, 2).astype(BF16),
        "w_g": wg.astype(BF16), "gla_g": row(gla_norm_g[0]),
        "w_br_gla": w_br_gla[0].astype(BF16), "w_uv_bd": w_uv_bd.astype(BF16), "w_br_mla": w_br_mla[0].astype(BF16),
        "w_mg": w_mg[0].astype(BF16), "b_mg": row(b_mg[0]), "w_out": w_out[0].astype(BF16),
        "ln1_g": row(ln1_g[0]), "ln1_b": row(ln1_b[0]),
        "w_r_hi": w_r_hi, "w_r_lo": (w_r - w_r_hi.astype(F32)).astype(BF16), "b_r": row(b_r),
        "w_gate": w_gate[0], "w_up": w_up[0], "w_down": w_down[0],
        "ln2_g": row(ln2_g[0]), "ln2_b": row(ln2_b[0]),
    }


def _value_rows(vt):
    lead, length = vt.shape[:-2], vt.shape[-1]
    return jnp.concatenate([vt, jnp.ones(lead + (1, length), vt.dtype),
                            jnp.zeros(lead + (V_ROWS - MLA_KV_RANK - 1, length), vt.dtype)], axis=-2)


def _key_tiles(kcat, vt):
    bsz, length, _ = kcat.shape
    n = -(-length // KEY_TILE)
    pad = n * KEY_TILE - length
    kcat = jnp.pad(kcat, ((0, 0), (0, pad), (0, 0)))
    vt = jnp.pad(vt, ((0, 0), (0, 0), (0, pad)))
    return (kcat.reshape(bsz, n, KEY_TILE, MLA_CAT),
            vt.reshape(bsz, V_ROWS, n, KEY_TILE).transpose(0, 2, 1, 3))


def _ffn(groups, wp):
    h1ts, rts = [], []
    cnt = jnp.zeros((1, ROUTER_LANES), F32)
    for x, og, ol in groups:
        t, d = x.shape[0] * x.shape[1], x.shape[2]
        h1t, rt, cnt = _merge(x.reshape(t, d), og.reshape(t, -1), ol.reshape(t, -1), wp, cnt, tt=min(MERGE_TILE, t))
        h1ts.append(h1t)
        rts.append(rt)
    n_fields = RT_RANK + 2
    pos, zrow, tile_e, tile_blk, n_used, n_tiles = _route_plan(
        jnp.concatenate([rt[:, :n_fields] for rt in rts], axis=0), cnt)
    xs = _dispatch(h1ts, pos, zrow, n_used, n_rows=n_tiles * EXPERT_ROWS, td=MOE_TOKEN_TILE)
    outs = _experts(xs, tile_e, tile_blk, n_used, wp)
    ys = _combine(h1ts, rts, outs, pos, wp, tc=MOE_TOKEN_TILE)
    return [y.reshape(x.shape) for y, (x, _, _) in zip(ys, groups)]


def kernel(x_prompt, x_sample, cache_mla_latent, cache_mla_krope, state_gla, meta_tokens, ln_in_g, ln_in_b, w_in, w_gk2, b_gk, gla_norm_g, q_norm_g, kv_norm_g, w_uq, w_uk, w_uv, w_br_gla, w_br_mla, w_mg, b_mg, w_out, ln1_g, ln1_b, w_rg, b_rg, w_re, b_re, w_gate, w_up, w_down, ln2_g, ln2_b):
    bp, sp, d = x_prompt.shape
    bs, ss, _ = x_sample.shape
    past = cache_mla_latent.shape[2]
    wp = _prep_weights(ln_in_g, ln_in_b, w_in, w_gk2, b_gk, gla_norm_g, q_norm_g, kv_norm_g, w_uq, w_uk, w_uv,
                       w_br_gla, w_br_mla, w_mg, b_mg, w_out, ln1_g, ln1_b, w_rg, b_rg, w_re, b_re,
                       w_gate, w_up, w_down, ln2_g, ln2_b)

    cs_m, cst_m = _rope_tables(jnp.arange(-N_META, 0, dtype=jnp.int32))
    m = _proj(meta_tokens[None], cs_m, cst_m, wp, cl=N_META, tt=N_META)
    _, _, m_kt, m_v, m_ebl, m_ckv, m_kr, m_kcat, m_vt, _ = m
    zero_state = jnp.zeros((1, GLA_HEADS, GLA_DK, GLA_DV), F32)
    _, m_state = _gla(m[0], m[1], m_kt, m_v, m_ebl, zero_state, cl=N_META, ts=N_META)

    cs_p, cst_p = _rope_tables(jnp.arange(sp, dtype=jnp.int32))
    p_qd, p_ki, p_kt, p_v, p_ebl, p_ckv, p_kr, p_kcat, p_vt, p_qt = _proj(
        x_prompt, cs_p, cst_p, wp, cl=CHUNK, tt=PROJ_TILE)
    p_o, p_state = _gla(p_qd, p_ki, p_kt, p_v, p_ebl, m_state, cl=CHUNK, ts=TOKEN_TILE)
    rep = lambda a, n: jnp.broadcast_to(a, (n,) + a.shape[1:])
    lat_p = jnp.concatenate([rep(m_ckv, bp), p_ckv], axis=1)
    kr_p = jnp.concatenate([rep(m_kr, bp), p_kr], axis=1)
    kcat_p = p_kcat.reshape(bp, sp // KEY_TILE, KEY_TILE, MLA_CAT)
    km, vm = m_kcat[0], m_vt[0, 0]
    p_ol = _attn(p_qt, kcat_p, p_vt, km, vm, bq=TOKEN_TILE, causal=True, n_valid_last=KEY_TILE, n_sets=2)

    ts_all = bs * ss
    cs_s, cst_s = _rope_tables(past + (jnp.arange(ts_all, dtype=jnp.int32) % ss))
    s_qd, s_ki, s_kt, s_v, s_ebl, s_ckv, s_kr, s_kcat, s_vt, s_qt = _proj(
        x_sample.reshape(1, ts_all, d), cs_s, cst_s, wp, cl=ss, tt=min(TOKEN_TILE, ts_all))
    per_stream = lambda a: a.reshape(bs, ss, a.shape[-1])
    s_o, s_state = _gla(per_stream(s_qd), per_stream(s_ki), per_stream(s_kt), per_stream(s_v),
                        s_ebl.reshape(bs, 1, 1, GLA_QK), state_gla[0].astype(F32), cl=ss, ts=ss)
    s_ckv, s_kr = per_stream(s_ckv), per_stream(s_kr)
    cache_kcat = jnp.concatenate([cache_mla_latent[0], cache_mla_krope[0]], axis=-1).astype(BF16)
    cache_vt = _value_rows(cache_mla_latent[0].astype(BF16).transpose(0, 2, 1))
    new_vt = s_vt[0].transpose(1, 0, 2).reshape(V_ROWS, bs, ss).transpose(1, 0, 2)
    kcat_s, vt_s = _key_tiles(jnp.concatenate([cache_kcat, per_stream(s_kcat)], axis=1),
                              jnp.concatenate([cache_vt, new_vt], axis=2))
    qt = s_qt.transpose(0, 2, 3, 1, 4).reshape(MLA_HEADS, MLA_CAT, bs, ss)
    qt = qt.transpose(2, 1, 0, 3).reshape(bs, MLA_CAT, MLA_HEADS * ss // KEY_TILE, KEY_TILE)
    qt = qt.transpose(0, 2, 1, 3)[:, None]
    s_ol = _attn(qt, kcat_s, vt_s, km, vm, bq=ss, causal=False, n_valid_last=(past + ss - 1) % KEY_TILE + 1, n_sets=1)
    y_prompt, y_sample = _ffn([(x_prompt, p_o, p_ol), (x_sample, s_o, s_ol)], wp)

    return (y_prompt, y_sample, lat_p[None], kr_p[None], p_state[None].astype(state_gla.dtype),
            s_ckv[None], s_kr[None], s_state[None].astype(state_gla.dtype))
```

```python
import functools

import jax
import jax.numpy as jnp
from jax import lax
from jax.experimental import pallas as pl
from jax.experimental.pallas import tpu as pltpu

F32 = jnp.float32
BF16 = jnp.bfloat16

CHUNK = 64
N_META = 16
GLA_HEADS = 4
GLA_DK = 128
GLA_DV = 256
GLA_RANK = 16
GLA_TAU = 16.0
GLA_QK = GLA_HEADS * GLA_DK
GLA_V = GLA_HEADS * GLA_DV
GLA_SCALE = GLA_DK ** -0.5
MLA_HEADS = 16
MLA_Q_RANK = 384
MLA_KV_RANK = 128
MLA_NOPE = 64
MLA_ROPE = 32
MLA_HALF = MLA_ROPE // 2
MLA_DV = 64
MLA_QK_DIM = MLA_NOPE + MLA_ROPE
MLA_CAT = MLA_KV_RANK + MLA_ROPE
MLA_SCALE = MLA_QK_DIM ** -0.5
LOG2_E = 1.4426950408889634
Q_SCALE = MLA_SCALE * LOG2_E
V_ROWS = MLA_KV_RANK + 16
ROPE_THETA = 10000.0
N_GROUPS = 4
EXPERTS_PER_GROUP = 8
N_EXPERTS = N_GROUPS * EXPERTS_PER_GROUP
D_EXPERT = 256
LN_EPS = 1e-5
RMS_EPS = 1e-6
DEEPNORM_ALPHA = 2.0 ** 0.25

LANES = 128
MXU_DIM = 256
VMEM_LIMIT_BYTES = 56 * 1024 * 1024

TOKEN_TILE = 256
MERGE_TILE = 2 * TOKEN_TILE
PROJ_TILE = 2 * TOKEN_TILE
KEY_TILE = MXU_DIM
DISPATCH_TILE = 512
COMBINE_TILE = 256
EXPERT_ROWS = 256
ROW_TILE = 8
ROUTER_LANES = LANES
RT_E, RT_W, RT_RANK = 0, 2, 4
NEG_BIG = -1e30

NT_DIMS = (((1,), (1,)), ((), ()))
TN_DIMS = (((0,), (0,)), ((), ()))


def _cparams(*sem):
    return pltpu.CompilerParams(dimension_semantics=sem, vmem_limit_bytes=VMEM_LIMIT_BYTES)


def _const_spec(shape):
    nd = len(shape)
    return pl.BlockSpec(shape, lambda *_: (0,) * nd, pipeline_mode=pl.Buffered(1))


def _layer_norm(x, g, b):
    mu = jnp.mean(x, axis=-1, keepdims=True)
    xc = x - mu
    var = jnp.mean(xc * xc, axis=-1, keepdims=True)
    return xc * lax.rsqrt(var + LN_EPS) * g + b


def _dot(a, b):
    return jnp.dot(a, b, preferred_element_type=F32)


TOK_COLS = (GLA_QK, GLA_V, GLA_QK, MLA_KV_RANK, MLA_ROPE, GLA_RANK)
TOK_USED = sum(TOK_COLS)
TOK_PAD = -(-TOK_USED // MXU_DIM) * MXU_DIM
TR_ROWS = MLA_KV_RANK + MLA_Q_RANK


def _proj_kernel(x_ref, cs_ref, cst_ref, lng_ref, lnb_ref, wtok_ref, wtr_ref, wgk2_ref, bgk_ref,
                 kvg_ref, kvgt_ref, qngt_ref, wuqt_ref, wuk_ref,
                 qd_ref, ki_ref, kt_ref, v_ref, ebl_ref, ckv_ref, kr_ref, kcat_ref, vt_ref, qt_ref, *, cl, sub):
    tt = x_ref.shape[1]
    n_chunks = sub // cl

    def project(r0):
        h = _layer_norm(x_ref[0, r0:r0 + sub, :], lng_ref[...], lnb_ref[...])
        hb = h.astype(BF16)
        z = _dot(hb, wtok_ref[...])
        zt = lax.dot_general(wtr_ref[...], hb, NT_DIMS, preferred_element_type=F32)
        return z, zt

    def finish(r0, z, zt):
        rows = slice(r0, r0 + sub)
        o0 = 0
        k = z[:, o0:o0 + GLA_QK]; o0 += GLA_QK
        v = z[:, o0:o0 + GLA_V]; o0 += GLA_V
        q = z[:, o0:o0 + GLA_QK]; o0 += GLA_QK
        ckv_raw = z[:, o0:o0 + MLA_KV_RANK]; o0 += MLA_KV_RANK
        kr = z[:, o0:o0 + MLA_ROPE]; o0 += MLA_ROPE
        gklr = z[:, o0:o0 + GLA_RANK]

        gz = _dot(gklr.astype(BF16), wgk2_ref[...]) + bgk_ref[...]
        gk = (jnp.minimum(gz, 0.0) - jnp.log(1.0 + jnp.exp(-jnp.abs(gz)))) * (1.0 / GLA_TAU)
        row_in_chunk = lax.broadcasted_iota(jnp.int32, gk.shape, 0) & (cl - 1)
        b = gk
        shift = 1
        while shift < cl:
            b = b + jnp.where(row_in_chunk >= shift, pltpu.roll(b, shift, 0), 0.0)
            shift *= 2
        b3 = b.reshape(n_chunks, cl, GLA_QK)
        bl = b3[:, cl - 1:cl, :]
        qd_ref[0, rows, :] = (q * GLA_SCALE * jnp.exp(b)).astype(BF16)
        ki_ref[0, rows, :] = (k * jnp.exp(-b)).astype(BF16)
        kt_ref[0, rows, :] = (k.reshape(n_chunks, cl, GLA_QK) * jnp.exp(bl - b3)).reshape(sub, GLA_QK).astype(BF16)
        v_ref[0, rows, :] = v.astype(BF16)
        ebl_ref[0, r0 // cl:r0 // cl + n_chunks] = jnp.exp(bl)

        ckv = ckv_raw * lax.rsqrt(jnp.mean(ckv_raw * ckv_raw, axis=-1, keepdims=True) + RMS_EPS) * kvg_ref[...]
        cs = cs_ref[0, rows, :]
        cos, sin = cs[:, :MLA_HALF], cs[:, MLA_HALF:]
        x1, x2 = kr[:, :MLA_HALF], kr[:, MLA_HALF:]
        kr_rot = jnp.concatenate([x1 * cos - x2 * sin, x2 * cos + x1 * sin], axis=-1)
        ckv_ref[0, rows, :] = ckv
        kr_ref[0, rows, :] = kr_rot
        kcat_ref[0, rows, :MLA_KV_RANK] = ckv.astype(BF16)
        kcat_ref[0, rows, MLA_KV_RANK:] = kr_rot.astype(BF16)

        ckvt = zt[:MLA_KV_RANK]
        ckvt = ckvt * lax.rsqrt(jnp.mean(ckvt * ckvt, axis=0, keepdims=True) + RMS_EPS) * kvgt_ref[...]
        blk = r0 // sub
        extra = lax.broadcasted_iota(jnp.int32, (V_ROWS - MLA_KV_RANK, sub), 0) == 0
        vt_ref[0, blk, :MLA_KV_RANK, :] = ckvt.astype(BF16)
        vt_ref[0, blk, MLA_KV_RANK:, :] = jnp.where(extra, 1.0, 0.0).astype(BF16)
        cqt = zt[MLA_KV_RANK:]
        cqt = cqt * lax.rsqrt(jnp.mean(cqt * cqt, axis=0, keepdims=True) + RMS_EPS) * qngt_ref[...]
        qmt = _dot(wuqt_ref[...], cqt.astype(BF16))
        n_nope = MLA_HEADS * MLA_NOPE
        n_half = MLA_HEADS * MLA_HALF
        cst = cst_ref[0, :, rows]
        cos_t = jnp.concatenate([cst[:MLA_HALF]] * MLA_HEADS, axis=0)
        sin_t = jnp.concatenate([cst[MLA_HALF:]] * MLA_HEADS, axis=0)
        r1 = qmt[n_nope:n_nope + n_half]
        r2 = qmt[n_nope + n_half:]
        rot1 = ((r1 * cos_t - r2 * sin_t) * Q_SCALE).astype(BF16)
        rot2 = ((r2 * cos_t + r1 * sin_t) * Q_SCALE).astype(BF16)
        for hd in range(MLA_HEADS):
            nope = qmt[hd * MLA_NOPE:(hd + 1) * MLA_NOPE].astype(BF16)
            qlat = _dot(wuk_ref[hd], nope) * Q_SCALE
            qt_ref[0, blk, hd, :MLA_KV_RANK, :] = qlat.astype(BF16)
            qt_ref[0, blk, hd, MLA_KV_RANK:MLA_KV_RANK + MLA_HALF, :] = rot1[hd * MLA_HALF:(hd + 1) * MLA_HALF]
            qt_ref[0, blk, hd, MLA_KV_RANK + MLA_HALF:, :] = rot2[hd * MLA_HALF:(hd + 1) * MLA_HALF]

    starts = list(range(0, tt, sub))
    projected = [project(r0) for r0 in starts]
    for r0, (z, zt) in zip(starts, projected):
        finish(r0, z, zt)


def _proj(x, cs, cst, wp, *, cl, tt):
    bsz, s, d = x.shape
    nt = s // tt
    n_chunks = tt // cl
    sub = min(TOKEN_TILE, tt)
    tok = lambda w: pl.BlockSpec((1, tt, w), lambda b, t: (b, t, 0))
    in_specs = [
        tok(d),
        pl.BlockSpec((1, tt, MLA_ROPE), lambda b, t: (0, t, 0)),
        pl.BlockSpec((1, MLA_ROPE, tt), lambda b, t: (0, 0, t)),
    ] + [_const_spec(wp[n].shape) for n in _PROJ_WEIGHTS]
    out_shape = [
        jax.ShapeDtypeStruct((bsz, s, GLA_QK), BF16),
        jax.ShapeDtypeStruct((bsz, s, GLA_QK), BF16),
        jax.ShapeDtypeStruct((bsz, s, GLA_QK), BF16),
        jax.ShapeDtypeStruct((bsz, s, GLA_V), BF16),
        jax.ShapeDtypeStruct((bsz, s // cl, 1, GLA_QK), F32),
        jax.ShapeDtypeStruct((bsz, s, MLA_KV_RANK), F32),
        jax.ShapeDtypeStruct((bsz, s, MLA_ROPE), F32),
        jax.ShapeDtypeStruct((bsz, s, MLA_CAT), BF16),
        jax.ShapeDtypeStruct((bsz, s // sub, V_ROWS, sub), BF16),
        jax.ShapeDtypeStruct((bsz, s // sub, MLA_HEADS, MLA_CAT, sub), BF16),
    ]
    out_specs = [
        tok(GLA_QK), tok(GLA_QK), tok(GLA_QK), tok(GLA_V),
        pl.BlockSpec((1, n_chunks, 1, GLA_QK), lambda b, t: (b, t, 0, 0)),
        tok(MLA_KV_RANK), tok(MLA_ROPE), tok(MLA_CAT),
        pl.BlockSpec((1, tt // sub, V_ROWS, sub), lambda b, t: (b, t, 0, 0)),
        pl.BlockSpec((1, tt // sub, MLA_HEADS, MLA_CAT, sub), lambda b, t: (b, t, 0, 0, 0)),
    ]
    return pl.pallas_call(
        functools.partial(_proj_kernel, cl=cl, sub=sub),
        grid=(bsz, nt),
        in_specs=in_specs,
        out_specs=out_specs,
        out_shape=out_shape,
        compiler_params=_cparams("parallel", "parallel"),
        name="proj",
    )(x, cs, cst, *[wp[n] for n in _PROJ_WEIGHTS])


_PROJ_WEIGHTS = ("ln_g", "ln_b", "w_tok", "w_tr", "w_gk2", "b_gk", "kv_g", "kv_gt", "q_gt", "w_uqt", "w_uk")


def _gla_kernel(qd_ref, ki_ref, kt_ref, v_ref, ebl_ref, s0_ref, o_ref, sfin_ref, st_ref, *, cl):
    t = pl.program_id(1)
    ts = qd_ref.shape[1]

    @pl.when(t == 0)
    def _():
        for hd in range(GLA_HEADS):
            st_ref[hd] = s0_ref[0, hd].T

    row = lax.broadcasted_iota(jnp.int32, (ts, ts), 0)
    col = lax.broadcasted_iota(jnp.int32, (ts, ts), 1)
    keep = (row >= col) & ((row & -cl) == (col & -cl))
    qk = lambda ref, hd: ref[0, :, hd * GLA_DK:(hd + 1) * GLA_DK]
    val = lambda hd: v_ref[0, :, hd * GLA_DV:(hd + 1) * GLA_DV]
    for hd in range(GLA_HEADS):
        a = lax.dot_general(qk(qd_ref, hd), qk(ki_ref, hd), NT_DIMS, preferred_element_type=F32)
        a = jnp.where(keep, a, 0.0).astype(BF16)
        o_ref[0, :, hd * GLA_DV:(hd + 1) * GLA_DV] = _dot(a, val(hd))
    for c in range(ts // cl):
        rows = slice(c * cl, (c + 1) * cl)
        for hd in range(GLA_HEADS):
            st = st_ref[hd]
            o_ref[0, rows, hd * GLA_DV:(hd + 1) * GLA_DV] += lax.dot_general(
                qk(qd_ref, hd)[rows], st.astype(BF16), NT_DIMS, preferred_element_type=F32)
            st_ref[hd] = (st * ebl_ref[0, c, :, hd * GLA_DK:(hd + 1) * GLA_DK]
                          + lax.dot_general(val(hd)[rows], qk(kt_ref, hd)[rows], TN_DIMS, preferred_element_type=F32))

    @pl.when(t == pl.num_programs(1) - 1)
    def _():
        for hd in range(GLA_HEADS):
            sfin_ref[0, hd] = st_ref[hd].T


def _gla(qd, ki, kt, v, ebl, s0, *, cl, ts):
    bsz, s, _ = qd.shape
    n_chunks = ts // cl
    s0_b = s0.shape[0]
    qk_spec = pl.BlockSpec((1, ts, GLA_QK), lambda b, t: (b, t, 0))
    v_spec = pl.BlockSpec((1, ts, GLA_V), lambda b, t: (b, t, 0))
    st_spec = pl.BlockSpec((1, GLA_HEADS, GLA_DK, GLA_DV), lambda b, t: (b, 0, 0, 0))
    s0_spec = st_spec if s0_b == bsz else pl.BlockSpec((1, GLA_HEADS, GLA_DK, GLA_DV), lambda b, t: (0, 0, 0, 0))
    return pl.pallas_call(
        functools.partial(_gla_kernel, cl=cl),
        grid=(bsz, s // ts),
        in_specs=[qk_spec, qk_spec, qk_spec, v_spec,
                  pl.BlockSpec((1, n_chunks, 1, GLA_QK), lambda b, t: (b, t, 0, 0)),
                  s0_spec],
        out_specs=[v_spec, st_spec],
        out_shape=[jax.ShapeDtypeStruct((bsz, s, GLA_V), F32),
                   jax.ShapeDtypeStruct((bsz, GLA_HEADS, GLA_DK, GLA_DV), F32)],
        scratch_shapes=[pltpu.VMEM((GLA_HEADS, GLA_DV, GLA_DK), F32)],
        compiler_params=_cparams("parallel", "arbitrary"),
        name="gla",
    )(qd, ki, kt, v, ebl, s0)


SCRATCH_PER_SET = 6
TILES_PER_TRIP = 4


def _attn_kernel(qt_ref, kcat_ref, vt_ref, km_ref, vm_ref, o_ref, *scratch, bq, causal, n_valid_last):
    n_sets = qt_ref.shape[1]
    n_cc = qt_ref.shape[3]
    n_kt = kcat_ref.shape[1]
    groups_per_cc = KEY_TILE // LANES

    def when(cond, fn):
        if isinstance(cond, bool):
            if cond:
                fn()
        else:
            pl.when(cond)(fn)

    def block(h):
        s_ref, p_ref, m_ref, a_ref, acc_ref, mx_ref = scratch[SCRATCH_PER_SET * h:SCRATCH_PER_SET * (h + 1)]
        blk = pl.program_id(1) + h * pl.num_programs(1)
        n_int = blk if causal else n_kt - 1
        q = lambda c: qt_ref[0, h, 0, c]
        col_max = lambda x: jnp.max(x, axis=0, keepdims=True)
        col_sum = lambda x: jnp.sum(x, axis=0, keepdims=True)

        def scores(c, kt):
            s = _dot(kt, q(c))
            s_ref[c] = s
            mx_ref[c] = col_max(s)

        def softmax(c, mask):
            s = s_ref[c]
            if mask is None:
                s_max = mx_ref[c]
            else:
                s = jnp.where(mask, s, NEG_BIG)
                s_max = col_max(s)
            m_prev = m_ref[c]
            m_new = jnp.maximum(m_prev, s_max)
            a_ref[c] = jnp.exp2(m_prev - m_new)
            m_ref[c] = m_new
            p_ref[c] = jnp.exp2(s - m_new).astype(BF16)

        def softmax_diagonal(c):
            half = KEY_TILE // 2
            chunks_per_half = half // CHUNK
            second = lax.broadcasted_iota(jnp.int32, (CHUNK, half), 1) >= CHUNK
            m_prev = m_ref[c]
            m_out, a_out = [], []
            for lh in range(KEY_TILE // half):
                lanes = slice(lh * half, (lh + 1) * half)
                n_full = chunks_per_half * lh + 1
                rows_full = slice(0, n_full * CHUNK)
                rows_part = slice(n_full * CHUNK, (n_full + 1) * CHUNK)
                full = s_ref[c, rows_full, lanes]
                part = jnp.where(second, s_ref[c, rows_part, lanes], NEG_BIG)
                m_new = jnp.maximum(m_prev[:, lanes], jnp.maximum(col_max(full), col_max(part)))
                p_ref[c, rows_full, lanes] = jnp.exp2(full - m_new).astype(BF16)
                p_ref[c, rows_part, lanes] = jnp.exp2(part - m_new).astype(BF16)
                if (n_full + 1) * CHUNK < KEY_TILE:
                    hidden = KEY_TILE - (n_full + 1) * CHUNK
                    p_ref[c, (n_full + 1) * CHUNK:, lanes] = jnp.zeros((hidden, half), BF16)
                m_out.append(m_new)
                a_out.append(jnp.exp2(m_prev[:, lanes] - m_new))
            m_ref[c] = jnp.concatenate(m_out, axis=1)
            a_ref[c] = jnp.concatenate(a_out, axis=1)

        def values(c, vt):
            acc_ref[c] = a_ref[c] * acc_ref[c] + _dot(vt, p_ref[c])

        def start():
            km, vm = km_ref[...], vm_ref[...]
            sm = [_dot(km, q(c)) for c in range(n_cc)]
            pm = []
            for c in range(n_cc):
                m0 = col_max(sm[c])
                m_ref[c] = m0
                pm.append(jnp.exp2(sm[c] - m0).astype(BF16))
            for c in range(n_cc):
                acc_ref[c] = _dot(vm, pm[c])
            kt = kcat_ref[0, 0]
            for c in range(n_cc):
                scores(c, kt)

        def first_step():
            kt_next = kcat_ref[0, 1]
            for c in range(n_cc):
                softmax(c, None)
                scores(c, kt_next)

        def no_step():
            a_ref[...] = jnp.ones(a_ref.shape, F32)
            p_ref[...] = jnp.zeros(p_ref.shape, BF16)

        def key_step(j):
            kt_next = kcat_ref[0, j + 1]
            vt_prev = vt_ref[0, j - 1]
            for c in range(n_cc):
                values(c, vt_prev)
                softmax(c, None)
                scores(c, kt_next)

        def key_step_group(jj, carry):
            for u in range(TILES_PER_TRIP):
                key_step(TILES_PER_TRIP * jj + 1 + u)
            return carry

        def unmasked_tiles():
            when(n_int >= 1, first_step)
            when(n_int == 0, no_step)
            n_rest = jnp.maximum(n_int - 1, 0) if causal else max(n_int - 1, 0)
            trips = n_rest // TILES_PER_TRIP
            lax.fori_loop(0, trips, key_step_group, 0)
            for u in range(TILES_PER_TRIP - 1):
                when(n_rest % TILES_PER_TRIP > u, lambda u=u: key_step(trips * TILES_PER_TRIP + 1 + u))

        def last_values_of_unmasked():
            vt_prev = vt_ref[0, jnp.maximum(n_int - 1, 0)]
            for c in range(n_cc):
                values(c, vt_prev)

        def masked_tile_and_emit():
            if causal:
                masked_softmax = softmax_diagonal
            else:
                mask = lax.broadcasted_iota(jnp.int32, (KEY_TILE, KEY_TILE), 0) < n_valid_last
                masked_softmax = lambda c: softmax(c, mask)
            vt_last = vt_ref[0, n_int]
            value_lag = 2
            for c in range(n_cc + value_lag):
                if c < n_cc:
                    masked_softmax(c)
                if c >= value_lag:
                    values(c - value_lag, vt_last)
            for c in range(n_cc):
                acc = acc_ref[c]
                o_t = acc[:MLA_KV_RANK] * (1.0 / acc[MLA_KV_RANK:MLA_KV_RANK + 1])
                for g in range(groups_per_cc):
                    tile = o_t[:, g * LANES:(g + 1) * LANES].T.astype(BF16)
                    col0 = c * KEY_TILE + g * LANES
                    if bq >= LANES:
                        hd, q0 = col0 // bq, col0 % bq
                        o_ref[0, h, q0:q0 + LANES, hd * MLA_KV_RANK:(hd + 1) * MLA_KV_RANK] = tile
                    else:
                        for hl in range(LANES // bq):
                            hd = col0 // bq + hl
                            o_ref[0, h, :, hd * MLA_KV_RANK:(hd + 1) * MLA_KV_RANK] = tile[hl * bq:(hl + 1) * bq]

        return start, unmasked_tiles, last_values_of_unmasked, masked_tile_and_emit

    blocks = [block(h) for h in range(n_sets)]
    blocks[0][0]()
    for h in range(n_sets):
        start, unmasked_tiles, last_values_of_unmasked, masked_tile_and_emit = blocks[h]
        unmasked_tiles()
        last_values_of_unmasked()
        if h + 1 < n_sets:
            blocks[h + 1][0]()
        masked_tile_and_emit()


def _attn(qt, kcat, vt, km, vm, *, bq, causal, n_valid_last, n_sets):
    bsz, nq, n_cc = qt.shape[:3]
    n_kt = kcat.shape[1]
    assert not causal or bq == KEY_TILE
    per_set = nq // n_sets
    width = MLA_HEADS * MLA_KV_RANK
    scratch_set = [pltpu.VMEM((n_cc, KEY_TILE, KEY_TILE), F32),
                   pltpu.VMEM((n_cc, KEY_TILE, KEY_TILE), BF16),
                   pltpu.VMEM((n_cc, 1, KEY_TILE), F32),
                   pltpu.VMEM((n_cc, 1, KEY_TILE), F32),
                   pltpu.VMEM((n_cc, V_ROWS, KEY_TILE), F32),
                   pltpu.VMEM((n_cc, 1, KEY_TILE), F32)]
    assert len(scratch_set) == SCRATCH_PER_SET
    out = pl.pallas_call(
        functools.partial(_attn_kernel, bq=bq, causal=causal, n_valid_last=n_valid_last),
        grid=(bsz, per_set),
        in_specs=[pl.BlockSpec((1, n_sets, 1, n_cc, MLA_CAT, KEY_TILE), lambda b, i: (b, 0, i, 0, 0, 0)),
                  pl.BlockSpec((1, n_kt, KEY_TILE, MLA_CAT), lambda b, i: (b, 0, 0, 0)),
                  pl.BlockSpec((1, n_kt, V_ROWS, KEY_TILE), lambda b, i: (b, 0, 0, 0)),
                  _const_spec(km.shape), _const_spec(vm.shape)],
        out_specs=pl.BlockSpec((1, n_sets, bq, width), lambda b, i: (b, 0, i, 0)),
        out_shape=jax.ShapeDtypeStruct((bsz, n_sets, per_set * bq, width), BF16),
        scratch_shapes=scratch_set * n_sets,
        compiler_params=_cparams("parallel", "arbitrary"),
        name="attn",
    )(qt.reshape(bsz, n_sets, per_set, *qt.shape[2:]), kcat, vt, km, vm)
    return out.reshape(bsz, nq * bq, width)


_MERGE_WEIGHTS = ("ln_g", "ln_b", "w_g", "gla_g", "w_br_gla", "w_uv_bd", "w_br_mla", "w_mg", "b_mg", "w_out",
                  "ln1_g", "ln1_b", "w_r_hi", "w_r_lo", "b_r")


def _merge_kernel(x_ref, og_ref, ol_ref, lng_ref, lnb_ref, wg_ref, glag_ref, wbg_ref, wuv_ref, wbm_ref,
                  wmg_ref, bmg_ref, wout_ref, l1g_ref, l1b_ref, wrh_ref, wrl_ref, br_ref, cin_ref,
                  h1t_ref, rt_ref, cnt_ref, carry_ref, *, sub):
    tt, d = x_ref.shape

    def branches(rows):
        h = _layer_norm(x_ref[rows, :], lng_ref[...], lnb_ref[...])
        hb = h.astype(BF16)
        g_out = _dot(hb, wg_ref[...])
        gate_pre = _dot(hb, wmg_ref[...])
        pair_in = 2 * MLA_KV_RANK
        y_heads = jnp.concatenate([_dot(ol_ref[rows, p * pair_in:(p + 1) * pair_in], wuv_ref[p])
                                   for p in range(MLA_HEADS // 2)], axis=-1)
        return h, g_out, gate_pre, y_heads

    def mix(rows, h, g_out, gate_pre, y_heads):
        og = og_ref[rows, :]
        parts = []
        for hd in range(GLA_HEADS):
            cols = slice(hd * GLA_DV, (hd + 1) * GLA_DV)
            o_h = og[:, cols]
            g_h = g_out[:, cols]
            o_n = o_h * lax.rsqrt(jnp.mean(o_h * o_h, axis=-1, keepdims=True) + RMS_EPS) * glag_ref[...]
            parts.append(o_n * (g_h * jax.nn.sigmoid(g_h)))
        y_a = _dot(jnp.concatenate(parts, axis=-1).astype(BF16), wbg_ref[...])
        y_b = _dot(y_heads.astype(BF16), wbm_ref[...])
        gates = jax.nn.sigmoid(gate_pre + bmg_ref[...])
        mix_in = gates[:, :d] * y_a + gates[:, d:] * y_b
        return DEEPNORM_ALPHA * h + _dot(mix_in.astype(BF16), wout_ref[...])

    def route(r0, pre):
        h1 = _layer_norm(pre, l1g_ref[...], l1b_ref[...])
        for s in range(ROW_TILE):
            h1t_ref[pl.ds(r0 * ROW_TILE + s, sub, stride=ROW_TILE), :] = h1[:, s * LANES:(s + 1) * LANES]
        h1_hi = h1.astype(BF16)
        h1_lo = (h1 - h1_hi.astype(F32)).astype(BF16)
        logits = (_dot(h1_hi, wrh_ref[...]) + (_dot(h1_hi, wrl_ref[...]) + _dot(h1_lo, wrh_ref[...]))) + br_ref[...]
        lane = lax.broadcasted_iota(jnp.int32, logits.shape, 1)
        is_grp = lane < N_GROUPS
        gl = jnp.where(is_grp, logits, NEG_BIG)
        g_max = jnp.max(gl, axis=-1, keepdims=True)
        g_sel = jnp.min(jnp.where(gl == g_max, lane, ROUTER_LANES), axis=-1, keepdims=True)
        p_grp = 1.0 / jnp.sum(jnp.where(is_grp, jnp.exp(gl - g_max), 0.0), axis=-1, keepdims=True)
        e_lo = N_GROUPS + g_sel * EXPERTS_PER_GROUP
        in_grp = (lane >= e_lo) & (lane < e_lo + EXPERTS_PER_GROUP)
        el = jnp.where(in_grp, logits, NEG_BIG)
        v1 = jnp.max(el, axis=-1, keepdims=True)
        i1 = jnp.min(jnp.where(el == v1, lane, ROUTER_LANES), axis=-1, keepdims=True)
        el2 = jnp.where(lane == i1, NEG_BIG, el)
        v2 = jnp.max(el2, axis=-1, keepdims=True)
        i2 = jnp.min(jnp.where(el2 == v2, lane, ROUTER_LANES), axis=-1, keepdims=True)
        e2 = jnp.exp(v2 - v1)
        w1 = p_grp / (1.0 + e2)
        w2 = p_grp * e2 / (1.0 + e2)
        e1, e2i = i1 - N_GROUPS, i2 - N_GROUPS
        hot = ((lane == e1) | (lane == e2i)).astype(F32)
        earlier = (lax.broadcasted_iota(jnp.int32, (sub, sub), 0) > lax.broadcasted_iota(jnp.int32, (sub, sub), 1))
        before = carry_ref[...] + _dot(earlier.astype(BF16), hot.astype(BF16))
        r1 = jnp.sum(jnp.where(lane == e1, before, 0.0), axis=-1, keepdims=True)
        r2 = jnp.sum(jnp.where(lane == e2i, before, 0.0), axis=-1, keepdims=True)
        carry_ref[...] += jnp.sum(hot, axis=0, keepdims=True)
        fields = (e1.astype(F32), e2i.astype(F32), w1, w2, r1, r2)
        rt = jnp.zeros(logits.shape, F32)
        for k, val in enumerate(fields):
            rt = jnp.where(lane == k, val, rt)
        rt_ref[pl.ds(r0, sub), :] = rt

    @pl.when(pl.program_id(0) == 0)
    def _():
        carry_ref[...] = cin_ref[...]

    starts = list(range(0, tt, sub))
    independent = [branches(pl.ds(r0, sub)) for r0 in starts]
    pending = None
    for r0, ind in zip(starts, independent):
        pre = mix(pl.ds(r0, sub), *ind)
        if pending is not None:
            route(*pending)
        pending = (r0, pre)
    route(*pending)
    cnt_ref[...] = carry_ref[...]


def _merge(x2, og2, ol2, wp, cnt_in, *, tt):
    t, d = x2.shape
    assert d == ROW_TILE * LANES
    row = lambda w: pl.BlockSpec((tt, w), lambda i: (i, 0))
    return pl.pallas_call(
        functools.partial(_merge_kernel, sub=min(TOKEN_TILE, tt)),
        grid=(t // tt,),
        in_specs=([row(d), row(GLA_V), row(MLA_HEADS * MLA_KV_RANK)] + [_const_spec(wp[n].shape) for n in _MERGE_WEIGHTS]
                  + [_const_spec(cnt_in.shape)]),
        out_specs=[pl.BlockSpec((tt * ROW_TILE, LANES), lambda i: (i, 0)), row(ROUTER_LANES),
                   pl.BlockSpec((1, ROUTER_LANES), lambda i: (0, 0))],
        out_shape=[jax.ShapeDtypeStruct((t * ROW_TILE, LANES), F32), jax.ShapeDtypeStruct((t, ROUTER_LANES), F32),
                   jax.ShapeDtypeStruct((1, ROUTER_LANES), F32)],
        scratch_shapes=[pltpu.VMEM((1, ROUTER_LANES), F32)],
        compiler_params=_cparams("arbitrary"),
        name="merge",
    )(x2, og2, ol2, *[wp[n] for n in _MERGE_WEIGHTS], cnt_in)


def _row_copy(src_ref, src_row, dst_ref, dst_row, sem):
    return pltpu.make_async_copy(src_ref.at[pl.ds(src_row * ROW_TILE, ROW_TILE)],
                                 dst_ref.at[pl.ds(dst_row * ROW_TILE, ROW_TILE)], sem)


def _group_spans(n_tokens, tile):
    spans, first = [], 0
    for n in n_tokens:
        spans.append((first, n // tile))
        first += n // tile
    return spans


def _group_spec(block, span):
    first, steps = span
    return pl.BlockSpec(block, lambda i, *_: (jnp.clip(i - first, 0, steps - 1), 0))


def _dispatch_kernel(pos_ref, zrow_ref, nu_ref, *refs, td, spans):
    h1t_refs, (xs_ref, zero_ref, sem, zsem) = refs[:len(spans)], refs[len(spans):]
    i = pl.program_id(0)

    @pl.when(i == 0)
    def _():
        zero_ref[...] = jnp.zeros(zero_ref.shape, F32)
        tile_rows = EXPERT_ROWS * ROW_TILE
        fill = lambda row: pltpu.make_async_copy(
            zero_ref, xs_ref.at[pl.ds(pl.multiple_of(row * ROW_TILE, ROW_TILE), tile_rows)], zsem)
        for e in range(N_EXPERTS):
            @pl.when(zrow_ref[e] >= 0)
            def _():
                fill(zrow_ref[e]).start()
        for e in range(N_EXPERTS):
            @pl.when(zrow_ref[e] >= 0)
            def _():
                fill(zrow_ref[e]).wait()

        def fill_tail(r, carry):
            fill(r * EXPERT_ROWS).start()
            fill(r * EXPERT_ROWS).wait()
            return carry

        lax.fori_loop(nu_ref[0], xs_ref.shape[0] // tile_rows, fill_tail, 0)

    base = i * (2 * td)
    for h1t_ref, (first, steps) in zip(h1t_refs, spans):
        @pl.when((i >= first) & (i < first + steps))
        def _():
            for t in range(td):
                for k in range(2):
                    dst = pl.multiple_of(pos_ref[base + 2 * t + k] * ROW_TILE, ROW_TILE)
                    pltpu.make_async_copy(h1t_ref.at[pl.ds(t * ROW_TILE, ROW_TILE)],
                                          xs_ref.at[pl.ds(dst, ROW_TILE)], sem).start(priority=k)
            for _ in range(2 * td):
                _row_copy(h1t_ref, 0, xs_ref, 0, sem).wait()


def _dispatch(h1ts, pos, zrow, n_used, *, n_rows, td):
    spans = _group_spans([h.shape[0] // ROW_TILE for h in h1ts], td)
    return pl.pallas_call(
        functools.partial(_dispatch_kernel, td=td, spans=spans),
        grid_spec=pltpu.PrefetchScalarGridSpec(
            num_scalar_prefetch=3,
            grid=(sum(steps for _, steps in spans),),
            in_specs=[_group_spec((td * ROW_TILE, LANES), span) for span in spans],
            out_specs=pl.BlockSpec(memory_space=pl.ANY),
            scratch_shapes=[pltpu.VMEM((EXPERT_ROWS * ROW_TILE, LANES), F32),
                            pltpu.SemaphoreType.DMA(()), pltpu.SemaphoreType.DMA(())]),
        out_shape=jax.ShapeDtypeStruct((n_rows * ROW_TILE, LANES), F32),
        compiler_params=_cparams("arbitrary"),
        name="dispatch",
    )(pos, zrow, n_used, *h1ts)


def _experts_kernel(te_ref, tb_ref, nu_ref, xs_ref, wg_ref, wu_ref, wd_ref, out_ref, wgb_ref, wub_ref, wdb_ref):
    del tb_ref
    r = pl.program_id(0)
    rows = xs_ref.shape[0] // ROW_TILE

    @pl.when((r == 0) | (te_ref[r] != te_ref[jnp.maximum(r - 1, 0)]))
    def _():
        wgb_ref[...] = wg_ref[0].astype(BF16)
        wub_ref[...] = wu_ref[0].astype(BF16)
        wdb_ref[...] = wd_ref[0].astype(BF16)

    @pl.when(r < nu_ref[0])
    def _():
        x = jnp.concatenate([xs_ref[pl.ds(s, rows, stride=ROW_TILE), :] for s in range(ROW_TILE)], axis=1).astype(BF16)
        gate = _dot(x, wgb_ref[...])
        up = _dot(x, wub_ref[...])
        hid = (gate * jax.nn.sigmoid(gate)) * up
        out = _dot(hid.astype(BF16), wdb_ref[...])
        for s in range(ROW_TILE):
            out_ref[pl.ds(s, rows, stride=ROW_TILE), :] = out[:, s * LANES:(s + 1) * LANES]

    @pl.when(pl.program_id(0) >= nu_ref[0])
    def _():
        out_ref[...] = jnp.zeros(out_ref.shape, F32)


def _experts(xs, tile_e, tile_blk, n_used, wp):
    n_rows = xs.shape[0] // ROW_TILE
    d = ROW_TILE * LANES
    blk = pl.BlockSpec((EXPERT_ROWS * ROW_TILE, LANES), lambda r, te, tb, nu: (tb[r], 0))
    return pl.pallas_call(
        _experts_kernel,
        grid_spec=pltpu.PrefetchScalarGridSpec(
            num_scalar_prefetch=3,
            grid=(n_rows // EXPERT_ROWS,),
            in_specs=[blk,
                      pl.BlockSpec((1, d, D_EXPERT), lambda r, te, tb, nu: (te[r], 0, 0)),
                      pl.BlockSpec((1, d, D_EXPERT), lambda r, te, tb, nu: (te[r], 0, 0)),
                      pl.BlockSpec((1, D_EXPERT, d), lambda r, te, tb, nu: (te[r], 0, 0))],
            out_specs=pl.BlockSpec((EXPERT_ROWS * ROW_TILE, LANES), lambda r, te, tb, nu: (r, 0)),
            scratch_shapes=[pltpu.VMEM((d, D_EXPERT), BF16), pltpu.VMEM((d, D_EXPERT), BF16),
                            pltpu.VMEM((D_EXPERT, d), BF16)]),
        out_shape=jax.ShapeDtypeStruct(xs.shape, F32),
        compiler_params=_cparams("arbitrary"),
        name="experts",
    )(tile_e, tile_blk, n_used, xs, wp["w_gate"], wp["w_up"], wp["w_down"])


def _combine_kernel(pos_ref, *refs, tc, spans):
    g = len(spans)
    h1t_refs, rt_refs = refs[:g], refs[g:2 * g]
    outs_ref, l2g_ref, l2b_ref = refs[2 * g:2 * g + 3]
    y_refs = refs[2 * g + 3:3 * g + 3]
    g_ref, sem = refs[3 * g + 3:]
    i = pl.program_id(0)
    n_steps = pl.num_programs(0)

    def gather(step):
        slot = step % 2
        base = step * (2 * tc)
        for t in range(tc):
            for k in range(2):
                src = pl.multiple_of(pos_ref[base + 2 * t + k] * ROW_TILE, ROW_TILE)
                pltpu.make_async_copy(outs_ref.at[pl.ds(src, ROW_TILE)],
                                      g_ref.at[slot, k, pl.ds(t * ROW_TILE, ROW_TILE)],
                                      sem.at[slot]).start(priority=k)

    @pl.when(i == 0)
    def _():
        gather(0)

    @pl.when(i + 1 < n_steps)
    def _():
        gather(i + 1)

    slot = i % 2
    for _ in range(2 * tc):
        pltpu.make_async_copy(outs_ref.at[pl.ds(0, ROW_TILE)], g_ref.at[slot, 0, pl.ds(0, ROW_TILE)], sem.at[slot]).wait()
    for h1t_ref, rt_ref, y_ref, (first, steps) in zip(h1t_refs, rt_refs, y_refs, spans):
        @pl.when((i >= first) & (i < first + steps))
        def _():
            rt = rt_ref[...]
            w0, w1 = rt[:, RT_W:RT_W + 1], rt[:, RT_W + 1:RT_W + 2]
            cols = []
            for s in range(ROW_TILE):
                rows = pl.ds(s, tc, stride=ROW_TILE)
                cols.append(DEEPNORM_ALPHA * h1t_ref[rows, :] + (w0 * g_ref[slot, 0, rows, :] + w1 * g_ref[slot, 1, rows, :]))
            y_ref[...] = _layer_norm(jnp.concatenate(cols, axis=1), l2g_ref[...], l2b_ref[...])


def _combine(h1ts, rts, outs, pos, wp, *, tc):
    d = ROW_TILE * LANES
    spans = _group_spans([r.shape[0] for r in rts], tc)
    const = lambda: pl.BlockSpec((1, d), lambda i, *_: (0, 0))
    return pl.pallas_call(
        functools.partial(_combine_kernel, tc=tc, spans=spans),
        grid_spec=pltpu.PrefetchScalarGridSpec(
            num_scalar_prefetch=1,
            grid=(sum(steps for _, steps in spans),),
            in_specs=([_group_spec((tc * ROW_TILE, LANES), span) for span in spans]
                      + [_group_spec((tc, ROUTER_LANES), span) for span in spans]
                      + [pl.BlockSpec(memory_space=pl.ANY), const(), const()]),
            out_specs=[_group_spec((tc, d), span) for span in spans],
            scratch_shapes=[pltpu.VMEM((2, 2, tc * ROW_TILE, LANES), F32), pltpu.SemaphoreType.DMA((2,))]),
        out_shape=[jax.ShapeDtypeStruct((r.shape[0], d), F32) for r in rts],
        compiler_params=_cparams("arbitrary"),
        name="combine",
    )(pos, *h1ts, *rts, outs, wp["ln2_g"], wp["ln2_b"])


def _route_plan(rt, cnt):
    t = rt.shape[0]
    n_tiles = (2 * t) // EXPERT_ROWS + N_EXPERTS
    counts = cnt[0, :N_EXPERTS].astype(jnp.int32)
    padded = (counts + (EXPERT_ROWS - 1)) // EXPERT_ROWS * EXPERT_ROWS
    ends = jnp.cumsum(padded)
    starts = ends - padded
    eid = rt[:, RT_E:RT_E + 2].astype(jnp.int32)
    rank = rt[:, RT_RANK:RT_RANK + 2].astype(jnp.int32)
    first = jnp.sum(jnp.where(eid[..., None] == jnp.arange(N_EXPERTS, dtype=jnp.int32), starts, 0), axis=-1)
    pos = (first + rank).reshape(-1)
    n_used = ends[-1] // EXPERT_ROWS
    tile_blk = jnp.minimum(jnp.arange(n_tiles, dtype=jnp.int32), n_used - 1)
    tile_e = jnp.sum((tile_blk[:, None] * EXPERT_ROWS >= ends[None, :]).astype(jnp.int32), axis=1)
    zrow = jnp.where(padded > 0, ends - EXPERT_ROWS, -1)
    return pos, zrow.astype(jnp.int32), tile_e.astype(jnp.int32), tile_blk, n_used.reshape(1).astype(jnp.int32), n_tiles


def _rope_tables(pos):
    inv = ROPE_THETA ** (-jnp.arange(0, MLA_ROPE, 2, dtype=F32) / MLA_ROPE)
    ang = pos.astype(F32)[:, None] * inv[None, :]
    cs = jnp.concatenate([jnp.cos(ang), jnp.sin(ang)], axis=-1)
    return cs[None], cs.T[None]


def _prep_weights(ln_in_g, ln_in_b, w_in, w_gk2, b_gk, gla_norm_g, q_norm_g, kv_norm_g, w_uq, w_uk, w_uv,
                  w_br_gla, w_br_mla, w_mg, b_mg, w_out, ln1_g, ln1_b, w_rg, b_rg, w_re, b_re,
                  w_gate, w_up, w_down, ln2_g, ln2_b):
    d = w_in.shape[1]
    w = w_in[0]
    c0 = 0
    wk = w[:, c0:c0 + GLA_QK]; c0 += GLA_QK
    wv = w[:, c0:c0 + GLA_V]; c0 += GLA_V
    wgr = w[:, c0:c0 + GLA_RANK]; c0 += GLA_RANK
    wckv = w[:, c0:c0 + MLA_KV_RANK]; c0 += MLA_KV_RANK
    wkr = w[:, c0:c0 + MLA_ROPE]; c0 += MLA_ROPE
    wq = w[:, c0:c0 + GLA_QK]; c0 += GLA_QK
    wg = w[:, c0:c0 + GLA_V]; c0 += GLA_V
    wcq = w[:, c0:c0 + MLA_Q_RANK]
    w_tok = jnp.concatenate([wk, wv, wq, wckv, wkr, wgr, jnp.zeros((d, TOK_PAD - TOK_USED), F32)], axis=1)
    w_tr = jnp.concatenate([wckv, wcq], axis=1).T
    uq = w_uq[0].reshape(MLA_Q_RANK, MLA_HEADS, MLA_QK_DIM)
    uq_perm = jnp.concatenate([
        uq[:, :, :MLA_NOPE].reshape(MLA_Q_RANK, -1),
        uq[:, :, MLA_NOPE:MLA_NOPE + MLA_HALF].reshape(MLA_Q_RANK, -1),
        uq[:, :, MLA_NOPE + MLA_HALF:].reshape(MLA_Q_RANK, -1)], axis=1)
    uv = w_uv[0].transpose(1, 0, 2).reshape(MLA_HEADS // 2, 2, MLA_KV_RANK, MLA_DV)
    eye = jnp.eye(2, dtype=F32)
    w_uv_bd = (uv[:, :, :, None, :] * eye[None, :, None, :, None]).reshape(
        MLA_HEADS // 2, 2 * MLA_KV_RANK, 2 * MLA_DV)
    w_r = jnp.concatenate([w_rg[0], w_re[0].transpose(1, 0, 2).reshape(d, N_EXPERTS),
                           jnp.zeros((d, ROUTER_LANES - N_GROUPS - N_EXPERTS), F32)], axis=1)
    w_r_hi = w_r.astype(BF16)
    b_r = jnp.concatenate([b_rg[0], b_re[0].reshape(-1), jnp.zeros((ROUTER_LANES - N_GROUPS - N_EXPERTS,), F32)])
    row = lambda a: a.reshape(1, -1)
    return {
        "ln_g": row(ln_in_g), "ln_b": row(ln_in_b),
        "w_tok": w_tok.astype(BF16), "w_tr": w_tr.astype(BF16),
        "w_gk2": w_gk2[0].astype(BF16), "b_gk": row(b_gk[0]),
        "kv_g": row(kv_norm_g[0]), "kv_gt": kv_norm_g[0].reshape(-1, 1), "q_gt": q_norm_g[0].reshape(-1, 1),
        "w_uqt": uq_perm.T.astype(BF16), "w_uk": w_uk[0].transpose(1, 0, 2).astype(BF16),
        "w_g": wg.astype(BF16), "gla_g": row(gla_norm_g[0]),
        "w_br_gla": w_br_gla[0].astype(BF16), "w_uv_bd": w_uv_bd.astype(BF16), "w_br_mla": w_br_mla[0].astype(BF16),
        "w_mg": w_mg[0].astype(BF16), "b_mg": row(b_mg[0]), "w_out": w_out[0].astype(BF16),
        "ln1_g": row(ln1_g[0]), "ln1_b": row(ln1_b[0]),
        "w_r_hi": w_r_hi, "w_r_lo": (w_r - w_r_hi.astype(F32)).astype(BF16), "b_r": row(b_r),
        "w_gate": w_gate[0], "w_up": w_up[0], "w_down": w_down[0],
        "ln2_g": row(ln2_g[0]), "ln2_b": row(ln2_b[0]),
    }


def _value_rows(vt):
    lead, length = vt.shape[:-2], vt.shape[-1]
    return jnp.concatenate([vt, jnp.ones(lead + (1, length), vt.dtype),
                            jnp.zeros(lead + (V_ROWS - MLA_KV_RANK - 1, length), vt.dtype)], axis=-2)


def _key_tiles(kcat, vt):
    bsz, length, _ = kcat.shape
    n = -(-length // KEY_TILE)
    pad = n * KEY_TILE - length
    kcat = jnp.pad(kcat, ((0, 0), (0, pad), (0, 0)))
    vt = jnp.pad(vt, ((0, 0), (0, 0), (0, pad)))
    return (kcat.reshape(bsz, n, KEY_TILE, MLA_CAT),
            vt.reshape(bsz, V_ROWS, n, KEY_TILE).transpose(0, 2, 1, 3))


def _ffn(groups, wp):
    h1ts, rts = [], []
    cnt = jnp.zeros((1, ROUTER_LANES), F32)
    for x, og, ol in groups:
        t, d = x.shape[0] * x.shape[1], x.shape[2]
        h1t, rt, cnt = _merge(x.reshape(t, d), og.reshape(t, -1), ol.reshape(t, -1), wp, cnt, tt=min(MERGE_TILE, t))
        h1ts.append(h1t)
        rts.append(rt)
    n_fields = RT_RANK + 2
    pos, zrow, tile_e, tile_blk, n_used, n_tiles = _route_plan(
        jnp.concatenate([rt[:, :n_fields] for rt in rts], axis=0), cnt)
    xs = _dispatch(h1ts, pos, zrow, n_used, n_rows=n_tiles * EXPERT_ROWS, td=DISPATCH_TILE)
    outs = _experts(xs, tile_e, tile_blk, n_used, wp)
    ys = _combine(h1ts, rts, outs, pos, wp, tc=COMBINE_TILE)
    return [y.reshape(x.shape) for y, (x, _, _) in zip(ys, groups)]


def kernel(x_prompt, x_sample, cache_mla_latent, cache_mla_krope, state_gla, meta_tokens, ln_in_g, ln_in_b, w_in, w_gk2, b_gk, gla_norm_g, q_norm_g, kv_norm_g, w_uq, w_uk, w_uv, w_br_gla, w_br_mla, w_mg, b_mg, w_out, ln1_g, ln1_b, w_rg, b_rg, w_re, b_re, w_gate, w_up, w_down, ln2_g, ln2_b):
    bp, sp, d = x_prompt.shape
    bs, ss, _ = x_sample.shape
    past = cache_mla_latent.shape[2]
    wp = _prep_weights(ln_in_g, ln_in_b, w_in, w_gk2, b_gk, gla_norm_g, q_norm_g, kv_norm_g, w_uq, w_uk, w_uv,
                       w_br_gla, w_br_mla, w_mg, b_mg, w_out, ln1_g, ln1_b, w_rg, b_rg, w_re, b_re,
                       w_gate, w_up, w_down, ln2_g, ln2_b)

    cs_m, cst_m = _rope_tables(jnp.arange(-N_META, 0, dtype=jnp.int32))
    m = _proj(meta_tokens[None], cs_m, cst_m, wp, cl=N_META, tt=N_META)
    _, _, m_kt, m_v, m_ebl, m_ckv, m_kr, m_kcat, m_vt, _ = m
    zero_state = jnp.zeros((1, GLA_HEADS, GLA_DK, GLA_DV), F32)
    _, m_state = _gla(m[0], m[1], m_kt, m_v, m_ebl, zero_state, cl=N_META, ts=N_META)

    cs_p, cst_p = _rope_tables(jnp.arange(sp, dtype=jnp.int32))
    p_qd, p_ki, p_kt, p_v, p_ebl, p_ckv, p_kr, p_kcat, p_vt, p_qt = _proj(
        x_prompt, cs_p, cst_p, wp, cl=CHUNK, tt=PROJ_TILE)
    p_o, p_state = _gla(p_qd, p_ki, p_kt, p_v, p_ebl, m_state, cl=CHUNK, ts=TOKEN_TILE)
    rep = lambda a, n: jnp.broadcast_to(a, (n,) + a.shape[1:])
    lat_p = jnp.concatenate([rep(m_ckv, bp), p_ckv], axis=1)
    kr_p = jnp.concatenate([rep(m_kr, bp), p_kr], axis=1)
    kcat_p = p_kcat.reshape(bp, sp // KEY_TILE, KEY_TILE, MLA_CAT)
    km, vm = m_kcat[0], m_vt[0, 0]
    p_ol = _attn(p_qt, kcat_p, p_vt, km, vm, bq=TOKEN_TILE, causal=True, n_valid_last=KEY_TILE, n_sets=2)

    ts_all = bs * ss
    cs_s, cst_s = _rope_tables(past + (jnp.arange(ts_all, dtype=jnp.int32) % ss))
    s_qd, s_ki, s_kt, s_v, s_ebl, s_ckv, s_kr, s_kcat, s_vt, s_qt = _proj(
        x_sample.reshape(1, ts_all, d), cs_s, cst_s, wp, cl=ss, tt=min(TOKEN_TILE, ts_all))
    per_stream = lambda a: a.reshape(bs, ss, a.shape[-1])
    s_o, s_state = _gla(per_stream(s_qd), per_stream(s_ki), per_stream(s_kt), per_stream(s_v),
                        s_ebl.reshape(bs, 1, 1, GLA_QK), state_gla[0].astype(F32), cl=ss, ts=ss)
    s_ckv, s_kr = per_stream(s_ckv), per_stream(s_kr)
    cache_kcat = jnp.concatenate([cache_mla_latent[0], cache_mla_krope[0]], axis=-1).astype(BF16)
    cache_vt = _value_rows(cache_mla_latent[0].astype(BF16).transpose(0, 2, 1))
    new_vt = s_vt[0].transpose(1, 0, 2).reshape(V_ROWS, bs, ss).transpose(1, 0, 2)
    kcat_s, vt_s = _key_tiles(jnp.concatenate([cache_kcat, per_stream(s_kcat)], axis=1),
                              jnp.concatenate([cache_vt, new_vt], axis=2))
    qt = s_qt.transpose(0, 2, 3, 1, 4).reshape(MLA_HEADS, MLA_CAT, bs, ss)
    qt = qt.transpose(2, 1, 0, 3).reshape(bs, MLA_CAT, MLA_HEADS * ss // KEY_TILE, KEY_TILE)
    qt = qt.transpose(0, 2, 1, 3)[:, None]
    s_ol = _attn(qt, kcat_s, vt_s, km, vm, bq=ss, causal=False, n_valid_last=(past + ss - 1) % KEY_TILE + 1, n_sets=1)
    y_prompt, y_sample = _ffn([(x_prompt, p_o, p_ol), (x_sample, s_o, s_ol)], wp)

    return (y_prompt, y_sample, lat_p[None], kr_p[None], p_state[None].astype(state_gla.dtype),
            s_ckv[None], s_kr[None], s_state[None].astype(state_gla.dtype))
```
